```python
import jax, jax.numpy as jnp
from jax import lax
import numpy as np

D_MODEL = 1024
BATCH = 8
SEQ = 2048
DEPTH = 2
DEC_BATCH = 8
DEC_SEQ = 64
PAST_LEN = 1024

CHUNK = 64
D_MIX = D_MODEL
D_CONV = 3 * D_MIX // 8
CONV_WIDTH = 3
D_POOL = D_MIX // 4
POOL_WINDOWS = (2, 4, 8, 16)
N_POOL_GROUPS = 4
POOL_GROUP = D_POOL // N_POOL_GROUPS
POOL_HIST = max(POOL_WINDOWS) - 1
D_SGU = D_MIX - D_CONV - D_POOL
SGU_HEADS = 4
SGU_HEAD_DIM = D_SGU // SGU_HEADS
SGU_LEN = 128
D_PROJ = 3 * D_CONV + D_POOL + 2 * D_SGU
SPLIT_POINTS = (D_CONV, 2 * D_CONV, 3 * D_CONV, 3 * D_CONV + D_POOL, 3 * D_CONV + D_POOL + D_SGU)
N_GROUPS = 4
EXPERTS_PER_GROUP = 8
N_EXPERTS = N_GROUPS * EXPERTS_PER_GROUP
TOP_K_FINE = 2
D_EXPERT = D_MODEL // 2
EPS = 1e-6

kernel_name = 'hybrid_conv_pool_sgu_hiermoe_stream_step'


def rmsnorm(x, g):
    xf = x.astype(jnp.float32)
    y = xf * lax.rsqrt(jnp.mean(xf * xf, axis=-1, keepdims=True) + EPS)
    return (y * g.astype(jnp.float32)).astype(x.dtype)


def short_conv(z, hist, w):
    L = z.shape[1]
    zc = jnp.concatenate([hist.astype(z.dtype), z], axis=1)
    y = w[0] * zc[:, 0:L]
    for k in range(1, CONV_WIDTH):
        y = y + w[k] * zc[:, k:k + L]
    return y, zc[:, -(CONV_WIDTH - 1):]


def multiscale_pool(z, hist, pos0, w, scale):
    B, L, _ = z.shape
    zp = jnp.concatenate([hist.astype(z.dtype), z], axis=1)
    cs = jnp.pad(jnp.cumsum(zp.astype(jnp.float32), axis=1), ((0, 0), (1, 0), (0, 0)))
    end = cs[:, POOL_HIST + 1:POOL_HIST + 1 + L]
    pos = pos0 + jnp.arange(L)
    means = []
    for g, win in enumerate(POOL_WINDOWS):
        sl = slice(g * POOL_GROUP, (g + 1) * POOL_GROUP)
        start = cs[:, POOL_HIST + 1 - win:POOL_HIST + 1 - win + L, sl]
        cnt = jnp.minimum(win, pos + 1).astype(jnp.float32)[None, :, None]
        means.append((end[..., sl] - start) / cnt)
    pooled = (jnp.concatenate(means, axis=-1) - z.astype(jnp.float32)).astype(z.dtype)
    pooled = pooled.reshape(B, L, N_POOL_GROUPS, POOL_GROUP)
    y = jnp.einsum('blgc,gcd->blgd', pooled, w).reshape(B, L, D_POOL) * scale
    return y, zp[:, -POOL_HIST:]


def spatial_gating(u, v, w_s, b_s):
    B, L, _ = v.shape
    pad = (-L) % SGU_LEN
    n = (L + pad) // SGU_LEN
    vp = jnp.pad(v, ((0, 0), (0, pad), (0, 0))).reshape(B, n, SGU_LEN, SGU_HEADS, SGU_HEAD_DIM)
    mask = jnp.tril(jnp.ones((SGU_LEN, SGU_LEN), dtype=bool))
    wm = jnp.where(mask[None], w_s, jnp.zeros_like(w_s))
    s = jnp.einsum('hij,bnjhc->bnihc', wm, vp) + b_s.T[None, None, :, :, None]
    s = s.reshape(B, n * SGU_LEN, D_SGU)[:, :L]
    return u * s


def hier_moe(x, rc_w, rc_b, rf_w, rf_b, w_gate, w_up, w_down):
    B, L, D = x.shape
    t = x.reshape(-1, D)
    lc = jnp.dot(t, rc_w).astype(jnp.float32) + rc_b.astype(jnp.float32)
    pc = jax.nn.softmax(lc, axis=-1)
    g_sel = jnp.argmax(lc, axis=-1)
    p_sel = jnp.take_along_axis(pc, g_sel[:, None], axis=-1)[:, 0]
    lf = (jnp.dot(t, rf_w).astype(jnp.float32) + rf_b.astype(jnp.float32)).reshape(-1, N_GROUPS, EXPERTS_PER_GROUP)
    lf_sel = jnp.take_along_axis(lf, g_sel[:, None, None], axis=1)[:, 0]
    top_v, top_i = lax.top_k(lf_sel, TOP_K_FINE)
    top_w = jax.nn.softmax(top_v, axis=-1)
    fine = jnp.einsum('tk,tke->te', top_w, jax.nn.one_hot(top_i, EXPERTS_PER_GROUP, dtype=jnp.float32))
    gates = (p_sel[:, None, None] * jax.nn.one_hot(g_sel, N_GROUPS, dtype=jnp.float32)[:, :, None]
             * fine[:, None, :]).astype(x.dtype)
    y = jnp.zeros_like(t)
    for g in range(N_GROUPS):
        e = slice(g * EXPERTS_PER_GROUP, (g + 1) * EXPERTS_PER_GROUP)
        hg = jnp.einsum('td,edf->tef', t, w_gate[e])
        hu = jnp.einsum('td,edf->tef', t, w_up[e])
        h = jax.nn.silu(hg) * hu * gates[:, g, :, None]
        y = y + jnp.einsum('tef,efd->td', h, w_down[e])
    return y.reshape(B, L, D)


def trunk_layer(x, conv_hist, pool_hist, pos0, g1, w_in, conv_w, pool_w, pool_scale, sgu_w, sgu_b,
                w_out, g2, rc_w, rc_b, rf_w, rf_b, w_gate, w_up, w_down):
    h = rmsnorm(x, g1)
    proj = jnp.einsum('bld,dp->blp', h, w_in)
    a_b, a_c, a_h, p_in, s_u, s_v = jnp.split(proj, SPLIT_POINTS, axis=-1)
    conv_y, conv_state = short_conv(a_c * a_h, conv_hist, conv_w)
    a_out = a_b * conv_y
    p_out, pool_state = multiscale_pool(p_in, pool_hist, pos0, pool_w, pool_scale)
    s_out = spatial_gating(s_u, s_v, sgu_w, sgu_b)
    mix = jnp.einsum('blm,md->bld', jnp.concatenate([a_out, p_out, s_out], axis=-1), w_out)
    x = x + mix
    x = x + hier_moe(rmsnorm(x, g2), rc_w, rc_b, rf_w, rf_b, w_gate, w_up, w_down)
    return x, conv_state, pool_state, s_v


def setup_inputs(seed: int = 0) -> dict:
    key = jax.random.key(seed)
    ks = jax.random.split(key, 24)
    f32 = jnp.float32
    nrm = lambda k, shape, s: (jax.random.normal(k, shape, f32) * s)
    return {
        'x_prompt': nrm(ks[0], (BATCH, SEQ, D_MODEL), 1.0),
        'x_sample': nrm(ks[1], (DEC_BATCH, DEC_SEQ, D_MODEL), 1.0),
        'state_conv': nrm(ks[2], (DEPTH, DEC_BATCH, CONV_WIDTH - 1, D_CONV), 1.0),
        'state_pool': nrm(ks[3], (DEPTH, DEC_BATCH, POOL_HIST, D_POOL), 1.0),
        'norm1_g': 1.0 + nrm(ks[4], (DEPTH, D_MODEL), 0.02),
        'w_in': nrm(ks[5], (DEPTH, D_MODEL, D_PROJ), D_MODEL ** -0.5),
        'conv_w': nrm(ks[6], (DEPTH, CONV_WIDTH, D_CONV), CONV_WIDTH ** -0.5),
        'pool_w': nrm(ks[7], (DEPTH, N_POOL_GROUPS, POOL_GROUP, POOL_GROUP), POOL_GROUP ** -0.5),
        'pool_scale': 1.0 + nrm(ks[8], (DEPTH, D_POOL), 0.02),
        'sgu_w': nrm(ks[9], (DEPTH, SGU_HEADS, SGU_LEN, SGU_LEN), SGU_LEN ** -0.5),
        'sgu_b': 1.0 + nrm(ks[10], (DEPTH, SGU_HEADS, SGU_LEN), 0.02),
        'w_out': nrm(ks[11], (DEPTH, D_MIX, D_MODEL), D_MIX ** -0.5),
        'norm2_g': 1.0 + nrm(ks[12], (DEPTH, D_MODEL), 0.02),
        'router_coarse_w': nrm(ks[13], (DEPTH, D_MODEL, N_GROUPS), D_MODEL ** -0.5),
        'router_coarse_b': nrm(ks[14], (DEPTH, N_GROUPS), 0.01),
        'router_fine_w': nrm(ks[15], (DEPTH, D_MODEL, N_EXPERTS), D_MODEL ** -0.5),
        'router_fine_b': nrm(ks[16], (DEPTH, N_EXPERTS), 0.01),
        'moe_w_gate': nrm(ks[17], (DEPTH, N_EXPERTS, D_MODEL, D_EXPERT), D_MODEL ** -0.5),
        'moe_w_up': nrm(ks[18], (DEPTH, N_EXPERTS, D_MODEL, D_EXPERT), D_MODEL ** -0.5),
        'moe_w_down': nrm(ks[19], (DEPTH, N_EXPERTS, D_EXPERT, D_MODEL), D_EXPERT ** -0.5),
        'final_norm_g': 1.0 + nrm(ks[20], (D_MODEL,), 0.02),
    }


def reference(x_prompt, x_sample, state_conv, state_pool, norm1_g, w_in, conv_w, pool_w, pool_scale,
              sgu_w, sgu_b, w_out, norm2_g, router_coarse_w, router_coarse_b, router_fine_w,
              router_fine_b, moe_w_gate, moe_w_up, moe_w_down, final_norm_g):
    def run_stack(x, conv_hist, pool_hist, pos0):
        conv_new, pool_new, v_new = [], [], []
        for l in range(DEPTH):
            x, c, p, v = trunk_layer(x, conv_hist[l], pool_hist[l], pos0, norm1_g[l], w_in[l], conv_w[l],
                                     pool_w[l], pool_scale[l], sgu_w[l], sgu_b[l], w_out[l], norm2_g[l],
                                     router_coarse_w[l], router_coarse_b[l], router_fine_w[l],
                                     router_fine_b[l], moe_w_gate[l], moe_w_up[l], moe_w_down[l])
            conv_new.append(c)
            pool_new.append(p)
            v_new.append(v)
        return rmsnorm(x, final_norm_g), jnp.stack(conv_new), jnp.stack(pool_new), jnp.stack(v_new)

    b_p = x_prompt.shape[0]
    zero_conv = jnp.zeros((DEPTH, b_p, CONV_WIDTH - 1, D_CONV), x_prompt.dtype)
    zero_pool = jnp.zeros((DEPTH, b_p, POOL_HIST, D_POOL), x_prompt.dtype)
    y_prompt, new_conv_prompt, new_pool_prompt, _ = run_stack(x_prompt, zero_conv, zero_pool, 0)
    y_sample, new_conv_sample, new_pool_sample, new_sgu_v_sample = run_stack(x_sample, state_conv, state_pool, PAST_LEN)
    return (y_prompt, y_sample, new_conv_prompt, new_pool_prompt, new_conv_sample, new_pool_sample, new_sgu_v_sample)
```

```python
import functools

import jax
import jax.numpy as jnp
from jax import lax
from jax.experimental import pallas as pl
from jax.experimental.pallas import tpu as pltpu

D_MODEL = 1024
BATCH = 8
SEQ = 2048
DEPTH = 2
DEC_BATCH = 8
DEC_SEQ = 64
PAST_LEN = 1024
D_CONV = 384
CONV_WIDTH = 3
D_POOL = 256
POOL_WINDOWS = (2, 4, 8, 16)
POOL_GROUP = 64
POOL_HIST = 15
D_SGU = 384
SGU_HEADS = 4
SGU_HEAD_DIM = 96
SGU_LEN = 128
D_PROJ = 2176
N_GROUPS = 4
EXPERTS_PER_GROUP = 8
N_EXPERTS = 32
D_EXPERT = 512
EPS = 1e-6

T_PROMPT = BATCH * SEQ
T_SAMPLE = DEC_BATCH * DEC_SEQ
T_ALL = T_PROMPT + T_SAMPLE
TL = 512
TILES_PER_SEQ = SEQ // TL
N_PROMPT_TILES = T_PROMPT // TL
N_TILES = N_PROMPT_TILES + T_SAMPLE // TL
HIST_ROWS = 16
ROUTER_ROWS = 8 + N_EXPERTS
TM = 256
N_ASSIGN = 2 * T_ALL
N_ETILES = N_ASSIGN // TM + N_EXPERTS
VMEM_LIMIT = 56 * 1024 * 1024


def _rms(x, g):
    return x * lax.rsqrt(jnp.mean(x * x, axis=-1, keepdims=True) + EPS) * g


def _mix_rows(proj, zhist, phist, pos0, n, conv_w, pool_bd, pool_scale, wm_all, bias_full):
    a_b = proj[:, 0:384]
    a_c = proj[:, 384:768]
    a_h = proj[:, 768:1152]
    p_in = proj[:, 1152:1408]
    s_u = proj[:, 1408:1792]
    s_v = proj[:, 1792:2176]

    z = a_c * a_h
    zext = jnp.concatenate([zhist, z], axis=0)
    conv_y = (conv_w[0:1, :] * pltpu.roll(zext, 2, 0)[HIST_ROWS:, :]
              + conv_w[1:2, :] * pltpu.roll(zext, 1, 0)[HIST_ROWS:, :]
              + conv_w[2:3, :] * z)
    a_out = a_b * conv_y

    pext = jnp.concatenate([phist, p_in], axis=0)
    s2 = pext + pltpu.roll(pext, 1, 0)
    s4 = s2 + pltpu.roll(s2, 2, 0)
    s8 = s4 + pltpu.roll(s4, 4, 0)
    s16 = s8 + pltpu.roll(s8, 8, 0)
    lane = lax.broadcasted_iota(jnp.int32, (1, D_POOL), 1)
    wsum = jnp.where(lane < 64, s2, jnp.where(lane < 128, s4, jnp.where(lane < 192, s8, s16)))
    wsum = wsum[HIST_ROWS:, :]
    win = jnp.where(lane < 64, 2.0, jnp.where(lane < 128, 4.0, jnp.where(lane < 192, 8.0, 16.0)))
    pos = (pos0 + lax.broadcasted_iota(jnp.int32, (n, 1), 0) + 1).astype(jnp.float32)
    cnt = jnp.minimum(win, pos)
    pooled = wsum / cnt - p_in
    p_out = jnp.dot(pooled.astype(jnp.bfloat16), pool_bd,
                    preferred_element_type=jnp.float32) * pool_scale

    lane_s = lax.broadcasted_iota(jnp.int32, (1, D_SGU), 1)
    chunk = min(n, SGU_LEN)
    if chunk == SGU_LEN:
        wm = wm_all
    else:
        wm = jnp.concatenate([wm_all[h * SGU_LEN:h * SGU_LEN + chunk, 0:chunk]
                              for h in range(SGU_HEADS)], axis=0)
    s_rows = []
    for c in range(n // chunk):
        v_c = s_v[c * chunk:(c + 1) * chunk, :].astype(jnp.bfloat16)
        r = jnp.dot(wm, v_c, preferred_element_type=jnp.float32)
        s_c = jnp.where(lane_s < 96, r[0:chunk],
                        jnp.where(lane_s < 192, r[chunk:2 * chunk],
                                  jnp.where(lane_s < 288, r[2 * chunk:3 * chunk], r[3 * chunk:4 * chunk])))
        s_rows.append(s_c + bias_full[0:chunk, :])
    s_gate = s_rows[0] if len(s_rows) == 1 else jnp.concatenate(s_rows, axis=0)
    s_out = s_u * s_gate

    mix = jnp.concatenate([a_out, p_out, s_out], axis=-1)
    return mix, zext[n:n + HIST_ROWS, :], pext[n:n + HIST_ROWS, :]


def _route(h2_bf16, wr_t, br_col, n):
    logits = lax.dot_general(wr_t, h2_bf16, (((1,), (1,)), ((), ())),
                             preferred_element_type=jnp.float32) + br_col
    row8 = lax.broadcasted_iota(jnp.int32, (8, n), 0)
    lc = jnp.where(row8 < N_GROUPS, logits[0:8, :], -jnp.inf)
    mc = jnp.max(lc, axis=0, keepdims=True)
    g_sel = jnp.min(jnp.where(lc == mc, row8, 8), axis=0, keepdims=True)
    p_sel = 1.0 / jnp.sum(jnp.exp(lc - mc), axis=0, keepdims=True)
    lf = logits[8 + 3 * EXPERTS_PER_GROUP:8 + 4 * EXPERTS_PER_GROUP, :]
    for g in (2, 1, 0):
        lf = jnp.where(g_sel == g, logits[8 + g * EXPERTS_PER_GROUP:8 + (g + 1) * EXPERTS_PER_GROUP, :], lf)
    m1 = jnp.max(lf, axis=0, keepdims=True)
    i1 = jnp.min(jnp.where(lf == m1, row8, 8), axis=0, keepdims=True)
    lf2 = jnp.where(row8 == i1, -jnp.inf, lf)
    m2 = jnp.max(lf2, axis=0, keepdims=True)
    i2 = jnp.min(jnp.where(lf2 == m2, row8, 8), axis=0, keepdims=True)
    t = jnp.exp(m2 - m1)
    wa = 1.0 / (1.0 + t)
    wb = t / (1.0 + t)
    e0 = g_sel * EXPERTS_PER_GROUP + i1
    e1 = g_sel * EXPERTS_PER_GROUP + i2
    return e0, e1, p_sel * wa, p_sel * wb


def _mixer_kernel(first_layer, *refs):
    if first_layer:
        xp_ref, xs_ref = refs[0:2]
        rest = refs[2:]
    else:
        xm_ref, y_ref, gc_ref = refs[0:3]
        rest = refs[3:]
    (sconv_ref, spool_ref, g1_ref, win_ref, convw_ref, poolbd_ref, pscale_ref, wm_ref, bias_ref,
     wout_ref, g2_ref, wr_ref, br_ref,
     xmid_ref, h2_ref, eidx_ref, gcol_ref, cpr_ref, ppr_ref, csm_ref, psm_ref, sv_ref,
     zh_ref, ph_ref) = rest

    i = pl.program_id(0)

    def load_x(prompt):
        if first_layer:
            return xp_ref[...] if prompt else xs_ref[...]
        g = gc_ref[...]
        return xm_ref[...] + g[:, 0:1] * y_ref[:, 0:D_MODEL] + g[:, 1:2] * y_ref[:, D_MODEL:2 * D_MODEL]

    row_m = lax.broadcasted_iota(jnp.int32, (SGU_HEADS * SGU_LEN, SGU_LEN), 0) % SGU_LEN
    col_m = lax.broadcasted_iota(jnp.int32, (SGU_HEADS * SGU_LEN, SGU_LEN), 1)

    def finish(x, mix):
        x_mid = x + jnp.dot(mix.astype(jnp.bfloat16), wout_ref[...], preferred_element_type=jnp.float32)
        xmid_ref[...] = x_mid
        h2 = _rms(x_mid, g2_ref[...])
        h2_ref[...] = h2
        e0, e1, g0, g1 = _route(h2.astype(jnp.bfloat16), wr_ref[...], br_ref[...], TL)
        row8 = lax.broadcasted_iota(jnp.int32, (8, TL), 0)
        eidx_ref[0] = jnp.where(row8 == 0, e0, jnp.where(row8 == 1, e1, 0))
        row128 = lax.broadcasted_iota(jnp.int32, (128, TL), 0)
        gcol_ref[...] = jnp.where(row128 == 0, g0, jnp.where(row128 == 1, g1, 0.0)).T

    def project(x):
        h = _rms(x, g1_ref[...]).astype(jnp.bfloat16)
        return jnp.dot(h, win_ref[...], preferred_element_type=jnp.float32)

    def masked_wm():
        return jnp.where(col_m <= row_m, wm_ref[...], jnp.zeros_like(wm_ref[...]))

    @pl.when(i == 0)
    def _init():
        zh_ref[...] = jnp.zeros_like(zh_ref)
        ph_ref[...] = jnp.zeros_like(ph_ref)

    @pl.when(i < N_PROMPT_TILES)
    def _prompt():
        s = i % TILES_PER_SEQ
        x = load_x(True)
        proj = project(x)
        zhist = jnp.where(s == 0, 0.0, zh_ref[...])
        phist = jnp.where(s == 0, 0.0, ph_ref[...])
        mix, znew, pnew = _mix_rows(proj, zhist, phist, s * TL, TL, convw_ref[...], poolbd_ref[...],
                                    pscale_ref[...], masked_wm(), bias_ref[...])
        zh_ref[...] = znew
        ph_ref[...] = pnew
        cpr_ref[0] = znew
        ppr_ref[0] = pnew
        finish(x, mix)

    @pl.when(i == N_PROMPT_TILES)
    def _sample():
        x = load_x(False)
        proj = project(x)
        wm = masked_wm()
        mixes = []
        for b in range(DEC_BATCH):
            rows = slice(b * DEC_SEQ, (b + 1) * DEC_SEQ)
            mix, znew, pnew = _mix_rows(proj[rows, :], sconv_ref[b], spool_ref[b], PAST_LEN, DEC_SEQ,
                                        convw_ref[...], poolbd_ref[...], pscale_ref[...], wm, bias_ref[...])
            csm_ref[b] = znew
            psm_ref[b] = pnew
            mixes.append(mix)
        sv_ref[...] = proj[:, 1792:2176]
        finish(x, jnp.concatenate(mixes, axis=0))


def _mixer_call(first_layer, xs, sconv_pad, spool_pad, g1, w_in, conv_w, pool_bd, pool_scale, wm_all,
                bias_full, w_out, g2, wr_t, br_col):
    tile = lambda i: (i, 0)
    prompt_tile = lambda i: (jnp.minimum(i, N_PROMPT_TILES - 1), 0)
    const2 = lambda i: (0, 0)
    const3 = lambda i: (0, 0, 0)
    if first_layer:
        x_specs = [pl.BlockSpec((TL, D_MODEL), prompt_tile), pl.BlockSpec((TL, D_MODEL), const2)]
    else:
        x_specs = [pl.BlockSpec((TL, D_MODEL), tile), pl.BlockSpec((TL, 2 * D_MODEL), tile),
                   pl.BlockSpec((TL, 128), tile)]
    full = lambda a: pl.BlockSpec(a.shape, const2 if a.ndim == 2 else const3)
    weights = [sconv_pad, spool_pad, g1, w_in, conv_w, pool_bd, pool_scale, wm_all, bias_full, w_out, g2,
               wr_t, br_col]
    in_specs = x_specs + [full(a) for a in weights]
    seq_of = lambda i: (jnp.minimum(i // TILES_PER_SEQ, BATCH - 1), 0, 0)
    out_shape = [
        jax.ShapeDtypeStruct((T_ALL, D_MODEL), jnp.float32),
        jax.ShapeDtypeStruct((T_ALL, D_MODEL), jnp.float32),
        jax.ShapeDtypeStruct((N_TILES, 8, TL), jnp.int32),
        jax.ShapeDtypeStruct((T_ALL, 128), jnp.float32),
        jax.ShapeDtypeStruct((BATCH, HIST_ROWS, D_CONV), jnp.float32),
        jax.ShapeDtypeStruct((BATCH, HIST_ROWS, D_POOL), jnp.float32),
        jax.ShapeDtypeStruct((DEC_BATCH, HIST_ROWS, D_CONV), jnp.float32),
        jax.ShapeDtypeStruct((DEC_BATCH, HIST_ROWS, D_POOL), jnp.float32),
        jax.ShapeDtypeStruct((T_SAMPLE, D_SGU), jnp.float32),
    ]
    out_specs = [
        pl.BlockSpec((TL, D_MODEL), tile),
        pl.BlockSpec((TL, D_MODEL), tile),
        pl.BlockSpec((1, 8, TL), lambda i: (i, 0, 0)),
        pl.BlockSpec((TL, 128), tile),
        pl.BlockSpec((1, HIST_ROWS, D_CONV), seq_of),
        pl.BlockSpec((1, HIST_ROWS, D_POOL), seq_of),
        pl.BlockSpec((DEC_BATCH, HIST_ROWS, D_CONV), const3),
        pl.BlockSpec((DEC_BATCH, HIST_ROWS, D_POOL), const3),
        pl.BlockSpec((T_SAMPLE, D_SGU), const2),
    ]
    return pl.pallas_call(
        functools.partial(_mixer_kernel, first_layer),
        grid=(N_TILES,),
        in_specs=in_specs,
        out_specs=out_specs,
        out_shape=out_shape,
        scratch_shapes=[pltpu.VMEM((HIST_ROWS, D_CONV), jnp.float32),
                        pltpu.VMEM((HIST_ROWS, D_POOL), jnp.float32)],
        compiler_params=pltpu.CompilerParams(dimension_semantics=("arbitrary",),
                                             vmem_limit_bytes=VMEM_LIMIT),
        name="mixer_first" if first_layer else "mixer_next",
    )(*xs, *weights)


def _expert_kernel(texp_ref, tnv_ref, aop_ref, h2_ref, wg_ref, wu_ref, wd_ref, y_ref,
                   xbuf, obuf, gsem, ssem):
    s = pl.program_id(0)
    nv = tnv_ref[s]

    @pl.when(s == 0)
    def _init():
        xbuf[...] = jnp.zeros_like(xbuf)

    def gather_copy(r):
        tok = aop_ref[s * TM + r] >> 1
        return pltpu.make_async_copy(h2_ref.at[pl.ds(tok, 1), :], xbuf.at[pl.ds(r, 1), :], gsem)

    def scatter_copy(r):
        a = aop_ref[s * TM + r]
        return pltpu.make_async_copy(obuf.at[pl.ds(r, 1), :], y_ref.at[pl.ds(a, 1), :], ssem)

    @pl.when(nv > 0)
    def _tile():
        lax.fori_loop(0, nv, lambda r, c: (gather_copy(r).start(), c)[1], 0)
        lax.fori_loop(0, nv, lambda r, c: (gather_copy(r).wait(), c)[1], 0)
        x = xbuf[...].astype(jnp.bfloat16)
        hg = jnp.dot(x, wg_ref[0].astype(jnp.bfloat16), preferred_element_type=jnp.float32)
        hu = jnp.dot(x, wu_ref[0].astype(jnp.bfloat16), preferred_element_type=jnp.float32)
        h = (hg * jax.nn.sigmoid(hg) * hu).astype(jnp.bfloat16)
        obuf[...] = jnp.dot(h, wd_ref[0].astype(jnp.bfloat16), preferred_element_type=jnp.float32)
        lax.fori_loop(0, nv, lambda r, c: (scatter_copy(r).start(), c)[1], 0)
        lax.fori_loop(0, nv, lambda r, c: (scatter_copy(r).wait(), c)[1], 0)


def _expert_call(tile_expert, tile_nvalid, assign_of_pos, h2, w_gate, w_up, w_down):
    grid_spec = pltpu.PrefetchScalarGridSpec(
        num_scalar_prefetch=3,
        grid=(N_ETILES,),
        in_specs=[
            pl.BlockSpec(memory_space=pl.ANY),
            pl.BlockSpec((1, D_MODEL, D_EXPERT), lambda s, te, tn, ap: (te[s], 0, 0)),
            pl.BlockSpec((1, D_MODEL, D_EXPERT), lambda s, te, tn, ap: (te[s], 0, 0)),
            pl.BlockSpec((1, D_EXPERT, D_MODEL), lambda s, te, tn, ap: (te[s], 0, 0)),
        ],
        out_specs=pl.BlockSpec(memory_space=pl.ANY),
        scratch_shapes=[pltpu.VMEM((TM, D_MODEL), jnp.float32),
                        pltpu.VMEM((TM, D_MODEL), jnp.float32),
                        pltpu.SemaphoreType.DMA(()),
                        pltpu.SemaphoreType.DMA(())],
    )
    return pl.pallas_call(
        _expert_kernel,
        grid_spec=grid_spec,
        out_shape=jax.ShapeDtypeStruct((N_ASSIGN, D_MODEL), jnp.float32),
        compiler_params=pltpu.CompilerParams(dimension_semantics=("arbitrary",),
                                             vmem_limit_bytes=VMEM_LIMIT),
        name="experts",
    )(tile_expert, tile_nvalid, assign_of_pos, h2, w_gate, w_up, w_down)


def _final_kernel(xm_ref, y_ref, gc_ref, g_ref, yp_ref, ys_ref):
    i = pl.program_id(0)
    g = gc_ref[...]
    x = xm_ref[...] + g[:, 0:1] * y_ref[:, 0:D_MODEL] + g[:, 1:2] * y_ref[:, D_MODEL:2 * D_MODEL]
    out = _rms(x, g_ref[...])

    @pl.when(i < N_PROMPT_TILES)
    def _():
        yp_ref[...] = out

    @pl.when(i == N_PROMPT_TILES)
    def _():
        ys_ref[...] = out


def _final_call(x_mid, y2, gcol, g):
    tile = lambda i: (i, 0)
    return pl.pallas_call(
        _final_kernel,
        grid=(N_TILES,),
        in_specs=[pl.BlockSpec((TL, D_MODEL), tile), pl.BlockSpec((TL, 2 * D_MODEL), tile),
                  pl.BlockSpec((TL, 128), tile), pl.BlockSpec((1, D_MODEL), lambda i: (0, 0))],
        out_specs=[pl.BlockSpec((TL, D_MODEL), lambda i: (jnp.minimum(i, N_PROMPT_TILES - 1), 0)),
                   pl.BlockSpec((TL, D_MODEL), lambda i: (0, 0))],
        out_shape=[jax.ShapeDtypeStruct((T_PROMPT, D_MODEL), jnp.float32),
                   jax.ShapeDtypeStruct((T_SAMPLE, D_MODEL), jnp.float32)],
        compiler_params=pltpu.CompilerParams(dimension_semantics=("arbitrary",),
                                             vmem_limit_bytes=VMEM_LIMIT),
        name="final_norm",
    )(x_mid, y2, gcol, g)


def _plan_tiles(eidx):
    e_flat = jnp.transpose(eidx[:, 0:2, :], (0, 2, 1)).reshape(N_ASSIGN)
    order = jnp.argsort(e_flat, stable=True).astype(jnp.int32)
    counts = jnp.sum((e_flat[:, None] == jnp.arange(N_EXPERTS)[None, :]).astype(jnp.int32), axis=0)
    starts = jnp.cumsum(counts) - counts
    tiles = (counts + TM - 1) // TM
    tile_end = jnp.cumsum(tiles)
    tile_start = tile_end - tiles
    sidx = jnp.arange(N_ETILES, dtype=jnp.int32)
    te = jnp.sum((sidx[:, None] >= tile_end[None, :]).astype(jnp.int32), axis=1)
    used = te < N_EXPERTS
    te_c = jnp.minimum(te, N_EXPERTS - 1)
    first_row = (sidx - tile_start[te_c]) * TM
    nvalid = jnp.where(used, jnp.clip(counts[te_c] - first_row, 0, TM), 0).astype(jnp.int32)
    src = starts[te_c][:, None] + first_row[:, None] + jnp.arange(TM, dtype=jnp.int32)[None, :]
    aop = order[jnp.clip(src, 0, N_ASSIGN - 1)].reshape(N_ETILES * TM)
    last_e = te_c[jnp.maximum(tile_end[-1] - 1, 0)]
    tile_expert = jnp.where(used, te_c, last_e).astype(jnp.int32)
    return tile_expert, nvalid, aop


def kernel(x_prompt, x_sample, state_conv, state_pool, norm1_g, w_in, conv_w, pool_w, pool_scale, sgu_w, sgu_b, w_out, norm2_g, router_coarse_w, router_coarse_b, router_fine_w, router_fine_b, moe_w_gate, moe_w_up, moe_w_down, final_norm_g):
    bf16 = jnp.bfloat16
    xs = (x_prompt.reshape(T_PROMPT, D_MODEL), x_sample.reshape(T_SAMPLE, D_MODEL))
    sconv_pad = jnp.pad(state_conv, ((0, 0), (0, 0), (HIST_ROWS - (CONV_WIDTH - 1), 0), (0, 0)))
    spool_pad = jnp.pad(state_pool, ((0, 0), (0, 0), (HIST_ROWS - POOL_HIST, 0), (0, 0)))
    conv_pr, pool_pr, conv_sm, pool_sm, sgu_v = [], [], [], [], []
    x_mid = y2 = gcol = None
    for l in range(DEPTH):
        pool_bd = jax.scipy.linalg.block_diag(*[pool_w[l, g] for g in range(4)]).astype(bf16)
        wm_all = sgu_w[l].reshape(SGU_HEADS * SGU_LEN, SGU_LEN).astype(bf16)
        bias_full = jnp.repeat(sgu_b[l].T, SGU_HEAD_DIM, axis=1)
        wr_t = jnp.zeros((ROUTER_ROWS, D_MODEL), jnp.float32)
        wr_t = wr_t.at[0:N_GROUPS].set(router_coarse_w[l].T).at[8:].set(router_fine_w[l].T).astype(bf16)
        br_col = jnp.zeros((ROUTER_ROWS, 1), jnp.float32)
        br_col = br_col.at[0:N_GROUPS, 0].set(router_coarse_b[l]).at[8:, 0].set(router_fine_b[l])
        outs = _mixer_call(l == 0, xs, sconv_pad[l], spool_pad[l], norm1_g[l].reshape(1, D_MODEL),
                           w_in[l].astype(bf16), conv_w[l], pool_bd, pool_scale[l].reshape(1, D_POOL),
                           wm_all, bias_full, w_out[l].astype(bf16), norm2_g[l].reshape(1, D_MODEL),
                           wr_t, br_col)
        x_mid, h2, eidx, gcol, cpr, ppr, csm, psm, sv = outs
        conv_pr.append(cpr[:, HIST_ROWS - (CONV_WIDTH - 1):, :])
        pool_pr.append(ppr[:, HIST_ROWS - POOL_HIST:, :])
        conv_sm.append(csm[:, HIST_ROWS - (CONV_WIDTH - 1):, :])
        pool_sm.append(psm[:, HIST_ROWS - POOL_HIST:, :])
        sgu_v.append(sv.reshape(DEC_BATCH, DEC_SEQ, D_SGU))
        tile_expert, tile_nvalid, aop = _plan_tiles(eidx)
        y = _expert_call(tile_expert, tile_nvalid, aop, h2, moe_w_gate[l], moe_w_up[l], moe_w_down[l])
        y2 = y.reshape(T_ALL, 2 * D_MODEL)
        xs = (x_mid, y2, gcol)
    y_prompt, y_sample = _final_call(x_mid, y2, gcol, final_norm_g.reshape(1, D_MODEL))
    return (y_prompt.reshape(BATCH, SEQ, D_MODEL), y_sample.reshape(DEC_BATCH, DEC_SEQ, D_MODEL),
            jnp.stack(conv_pr), jnp.stack(pool_pr), jnp.stack(conv_sm), jnp.stack(pool_sm),
            jnp.stack(sgu_v))
```

```python
import functools

import jax
import jax.numpy as jnp
from jax import lax
from jax.experimental import pallas as pl
from jax.experimental.pallas import tpu as pltpu

D_MODEL = 1024
BATCH = 8
SEQ = 2048
DEPTH = 2
DEC_BATCH = 8
DEC_SEQ = 64
PAST_LEN = 1024
D_CONV = 384
CONV_WIDTH = 3
D_POOL = 256
POOL_HIST = 15
D_SGU = 384
SGU_HEADS = 4
SGU_HEAD_DIM = 96
SGU_LEN = 128
D_PROJ = 2176
N_GROUPS = 4
EXPERTS_PER_GROUP = 8
N_EXPERTS = 32
D_EXPERT = 512
EPS = 1e-6

T_PROMPT = BATCH * SEQ
T_SAMPLE = DEC_BATCH * DEC_SEQ
T_ALL = T_PROMPT + T_SAMPLE
TL = 512
TILES_PER_SEQ = SEQ // TL
N_PROMPT_TILES = T_PROMPT // TL
N_TILES = N_PROMPT_TILES + T_SAMPLE // TL
HIST_ROWS = 16
ROUTER_ROWS = 8 + N_EXPERTS
TM = 256
N_ASSIGN = 2 * T_ALL
N_PAGES = N_ASSIGN // TM + N_EXPERTS
PAGE_LANES = 256
Y_ROWS = N_ASSIGN + TM
VMEM_LIMIT = 56 * 1024 * 1024

assert TL == 2 * TM and N_PAGES <= PAGE_LANES and T_ALL == N_TILES * TL


def _rms(x, g):
    return x * lax.rsqrt(jnp.mean(x * x, axis=-1, keepdims=True) + EPS) * g


def _mix_rows(proj, zhist, phist, pos0, n, conv_w, pool_bd, pool_scale, wm_all, bias_full):
    a_b = proj[:, 0:384]
    a_c = proj[:, 384:768]
    a_h = proj[:, 768:1152]
    p_in = proj[:, 1152:1408]
    s_u = proj[:, 1408:1792]
    s_v = proj[:, 1792:2176]

    z = a_c * a_h
    zext = jnp.concatenate([zhist, z], axis=0)
    conv_y = (conv_w[0:1, :] * pltpu.roll(zext, 2, 0)[HIST_ROWS:, :]
              + conv_w[1:2, :] * pltpu.roll(zext, 1, 0)[HIST_ROWS:, :]
              + conv_w[2:3, :] * z)
    a_out = a_b * conv_y

    pext = jnp.concatenate([phist, p_in], axis=0)
    s2 = pext + pltpu.roll(pext, 1, 0)
    s4 = s2 + pltpu.roll(s2, 2, 0)
    s8 = s4 + pltpu.roll(s4, 4, 0)
    s16 = s8 + pltpu.roll(s8, 8, 0)
    lane = lax.broadcasted_iota(jnp.int32, (1, D_POOL), 1)
    wsum = jnp.where(lane < 64, s2, jnp.where(lane < 128, s4, jnp.where(lane < 192, s8, s16)))
    wsum = wsum[HIST_ROWS:, :]
    win = jnp.where(lane < 64, 2.0, jnp.where(lane < 128, 4.0, jnp.where(lane < 192, 8.0, 16.0)))
    pos = (pos0 + lax.broadcasted_iota(jnp.int32, (n, 1), 0) + 1).astype(jnp.float32)
    cnt = jnp.minimum(win, pos)
    pooled = wsum / cnt - p_in
    p_out = jnp.dot(pooled.astype(jnp.bfloat16), pool_bd,
                    preferred_element_type=jnp.float32) * pool_scale

    lane_s = lax.broadcasted_iota(jnp.int32, (1, D_SGU), 1)
    chunk = min(n, SGU_LEN)
    if chunk == SGU_LEN:
        wm = wm_all
    else:
        wm = jnp.concatenate([wm_all[h * SGU_LEN:h * SGU_LEN + chunk, 0:chunk]
                              for h in range(SGU_HEADS)], axis=0)
    s_rows = []
    for c in range(n // chunk):
        v_c = s_v[c * chunk:(c + 1) * chunk, :].astype(jnp.bfloat16)
        r = jnp.dot(wm, v_c, preferred_element_type=jnp.float32)
        s_c = jnp.where(lane_s < 96, r[0:chunk],
                        jnp.where(lane_s < 192, r[chunk:2 * chunk],
                                  jnp.where(lane_s < 288, r[2 * chunk:3 * chunk], r[3 * chunk:4 * chunk])))
        s_rows.append(s_c + bias_full[0:chunk, :])
    s_gate = s_rows[0] if len(s_rows) == 1 else jnp.concatenate(s_rows, axis=0)
    s_out = s_u * s_gate

    mix = jnp.concatenate([a_out, p_out, s_out], axis=-1)
    return mix, zext[n:n + HIST_ROWS, :], pext[n:n + HIST_ROWS, :]


def _route(h2_bf16, wr_t, br_col, n):
    logits = lax.dot_general(wr_t, h2_bf16, (((1,), (1,)), ((), ())),
                             preferred_element_type=jnp.float32) + br_col
    row8 = lax.broadcasted_iota(jnp.int32, (8, n), 0)
    lc = jnp.where(row8 < N_GROUPS, logits[0:8, :], -jnp.inf)
    mc = jnp.max(lc, axis=0, keepdims=True)
    g_sel = jnp.min(jnp.where(lc == mc, row8, 8), axis=0, keepdims=True)
    p_sel = 1.0 / jnp.sum(jnp.exp(lc - mc), axis=0, keepdims=True)
    lf = logits[8 + 3 * EXPERTS_PER_GROUP:8 + 4 * EXPERTS_PER_GROUP, :]
    for g in (2, 1, 0):
        lf = jnp.where(g_sel == g, logits[8 + g * EXPERTS_PER_GROUP:8 + (g + 1) * EXPERTS_PER_GROUP, :], lf)
    m1 = jnp.max(lf, axis=0, keepdims=True)
    i1 = jnp.min(jnp.where(lf == m1, row8, 8), axis=0, keepdims=True)
    lf2 = jnp.where(row8 == i1, -jnp.inf, lf)
    m2 = jnp.max(lf2, axis=0, keepdims=True)
    i2 = jnp.min(jnp.where(lf2 == m2, row8, 8), axis=0, keepdims=True)
    t = jnp.exp(m2 - m1)
    wa = 1.0 / (1.0 + t)
    wb = t / (1.0 + t)
    e0 = g_sel * EXPERTS_PER_GROUP + i1
    e1 = g_sel * EXPERTS_PER_GROUP + i2
    return e0, e1, p_sel * wa, p_sel * wb


def _place_rows(e0, e1, triu, tril_e, fill_ref, base_ref, np_ref, pexp_ref):
    row_e = lax.broadcasted_iota(jnp.int32, (N_EXPERTS, TL), 0)
    oh0 = row_e == e0
    oh1 = row_e == e1
    oh = jnp.where(oh0 | oh1, 1.0, 0.0)
    rank = jnp.dot(oh.astype(jnp.bfloat16), triu, preferred_element_type=jnp.float32).astype(jnp.int32)
    cnt = jnp.sum(oh, axis=1, keepdims=True).astype(jnp.int32)
    fill = fill_ref[:, 0:1]
    base = base_ref[:, 0:1]
    npages = np_ref[0:1, 0:1]
    total = fill + cnt
    need = (total > TM).astype(jnp.int32) + (total > 2 * TM).astype(jnp.int32)
    need_b = jnp.broadcast_to(need.astype(jnp.float32), (N_EXPERTS, 128)).astype(jnp.bfloat16)
    before = jnp.dot(tril_e, need_b, preferred_element_type=jnp.float32)[:, 0:1].astype(jnp.int32)
    new_id = npages + before
    new_base = new_id * TM
    r = fill + rank
    pos_all = jnp.where(r < TM, base + r, new_base + (r - TM))
    pos0 = jnp.sum(jnp.where(oh0, pos_all, 0), axis=0, keepdims=True)
    pos1 = jnp.sum(jnp.where(oh1, pos_all, 0), axis=0, keepdims=True)
    fill_ref[...] = jnp.broadcast_to(total - need * TM, (N_EXPERTS, 128))
    base_ref[...] = jnp.broadcast_to(jnp.where(need > 0, new_base + (need - 1) * TM, base), (N_EXPERTS, 128))
    np_ref[...] = jnp.broadcast_to(npages + jnp.sum(need, axis=0, keepdims=True), (8, 128))
    page_lane = lax.broadcasted_iota(jnp.int32, (N_EXPERTS, PAGE_LANES), 1)
    expert_col = lax.broadcasted_iota(jnp.int32, (N_EXPERTS, 1), 0)
    owns = ((page_lane == new_id) & (need >= 1)) | ((page_lane == new_id + 1) & (need == 2))
    pexp_ref[...] = pexp_ref[...] + jnp.sum(jnp.where(owns, expert_col, 0), axis=0, keepdims=True)
    return pos0, pos1


def _mixer_kernel(first_layer, *refs):
    if first_layer:
        xp_ref, xs_ref = refs[0:2]
        rest = refs[2:]
    else:
        xm_ref, y0_ref, y1_ref, gc_ref = refs[0:4]
        rest = refs[4:]
    (sconv_ref, spool_ref, g1_ref, win_ref, convw_ref, poolbd_ref, pscale_ref, wm_ref, bias_ref,
     wout_ref, g2_ref, wr_ref, br_ref, triu_ref, trile_ref,
     xmid_ref, hs_ref, dest_ref, meta_ref, gcol_ref, cpr_ref, ppr_ref, csm_ref, psm_ref, sv_ref,
     zh_ref, ph_ref, h2buf, posv, poss, fill_ref, base_ref, np_ref, pexp_ref,
     fill_s, base_s, np_s, pexp_s, cnt_s, rsem, msem) = rest

    i = pl.program_id(0)

    def load_x(prompt):
        if first_layer:
            return xp_ref[...] if prompt else xs_ref[...]
        g = gc_ref[...]
        return xm_ref[...] + g[:, 0:1] * y0_ref[...] + g[:, 1:2] * y1_ref[...]

    row_m = lax.broadcasted_iota(jnp.int32, (SGU_HEADS * SGU_LEN, SGU_LEN), 0) % SGU_LEN
    col_m = lax.broadcasted_iota(jnp.int32, (SGU_HEADS * SGU_LEN, SGU_LEN), 1)

    def row_copy(r, pos):
        return pltpu.make_async_copy(h2buf.at[pl.ds(r, 1), :], hs_ref.at[pl.ds(pos, 1), :], rsem)

    def finish(x, mix):
        x_mid = x + jnp.dot(mix.astype(jnp.bfloat16), wout_ref[...], preferred_element_type=jnp.float32)
        xmid_ref[...] = x_mid
        h2 = _rms(x_mid, g2_ref[...])
        h2buf[...] = h2
        e0, e1, g0, g1 = _route(h2.astype(jnp.bfloat16), wr_ref[...], br_ref[...], TL)
        row128 = lax.broadcasted_iota(jnp.int32, (128, TL), 0)
        gcol_ref[...] = jnp.where(row128 == 0, g0, jnp.where(row128 == 1, g1, 0.0)).T
        pos0, pos1 = _place_rows(e0, e1, triu_ref[...], trile_ref[...], fill_ref, base_ref, np_ref, pexp_ref)
        row8 = lax.broadcasted_iota(jnp.int32, (8, TL), 0)
        posv[...] = jnp.where(row8 == 0, pos0, jnp.where(row8 == 1, pos1, 0))
        to_smem = pltpu.make_async_copy(posv, poss, msem)
        to_smem.start()
        to_smem.wait()
        tok0 = i * TL
        for r in range(TL):
            for k in range(2):
                p = poss[k, r]
                row_copy(r, p).start()
                dest_ref[p] = k * T_ALL + tok0 + r
        for k in range(2):
            pltpu.make_async_copy(h2buf, h2buf, rsem).wait()

    def project(x):
        h = _rms(x, g1_ref[...]).astype(jnp.bfloat16)
        return jnp.dot(h, win_ref[...], preferred_element_type=jnp.float32)

    def masked_wm():
        return jnp.where(col_m <= row_m, wm_ref[...], jnp.zeros_like(wm_ref[...]))

    @pl.when(i == 0)
    def _init():
        zh_ref[...] = jnp.zeros_like(zh_ref)
        ph_ref[...] = jnp.zeros_like(ph_ref)
        fill_ref[...] = jnp.full_like(fill_ref, TM)
        base_ref[...] = jnp.zeros_like(base_ref)
        np_ref[...] = jnp.zeros_like(np_ref)
        pexp_ref[...] = jnp.zeros_like(pexp_ref)

    def _close_pages():
        copies = [pltpu.make_async_copy(fill_ref, fill_s, msem), pltpu.make_async_copy(base_ref, base_s, msem),
                  pltpu.make_async_copy(np_ref, np_s, msem), pltpu.make_async_copy(pexp_ref, pexp_s, msem)]
        for c in copies:
            c.start()
        for c in copies:
            c.wait()
        npages = np_s[0, 0]

        def zero_cnt(e, c):
            cnt_s[e] = 0
            return c
        lax.fori_loop(0, N_EXPERTS, zero_cnt, 0)

        def count(p, c):
            e = pexp_s[0, p]
            cnt_s[e] = cnt_s[e] + 1
            return c
        lax.fori_loop(0, npages, count, 0)

        def prefix(e, run):
            n = cnt_s[e]
            cnt_s[e] = run
            return run + n
        lax.fori_loop(0, N_EXPERTS, prefix, 0)

        def emit(p, c):
            e = pexp_s[0, p]
            q = cnt_s[e]
            cnt_s[e] = q + 1
            meta_ref[0, q] = p
            meta_ref[1, q] = e
            return c
        lax.fori_loop(0, npages, emit, 0)

        def pad_meta(q, c):
            meta_ref[0, q] = meta_ref[0, npages - 1]
            meta_ref[1, q] = meta_ref[1, npages - 1]
            return c
        lax.fori_loop(npages, PAGE_LANES, pad_meta, 0)

        def fill_row2(q, c):
            meta_ref[2, q] = npages
            return c
        lax.fori_loop(0, PAGE_LANES, fill_row2, 0)

        h2buf[...] = jnp.zeros_like(h2buf)

        def tail_single(e, start):
            f = fill_s[e, 0]
            b = base_s[e, 0]
            f8 = jnp.minimum(((f + 7) // 8) * 8, TM)

            def one(r, c):
                cp = row_copy(0, b + r)
                cp.start() if start else cp.wait()
                return c
            lax.fori_loop(f, f8, one, 0)

            for nrows in (128, 64, 32, 16, 8):
                off = ((TM - f8) // (2 * nrows)) * (2 * nrows)

                @pl.when(((TM - f8) // nrows) % 2 == 1)
                def _blk():
                    row0 = pl.multiple_of(b + f8 + off, 8)
                    cp = pltpu.make_async_copy(h2buf.at[pl.ds(0, nrows), :], hs_ref.at[pl.ds(row0, nrows), :], rsem)
                    cp.start() if start else cp.wait()

            if start:
                def spare(r, c):
                    dest_ref[b + r] = N_ASSIGN + r
                    return c
                lax.fori_loop(f, TM, spare, 0)

        def tails(start):
            def body(e, c):
                tail_single(e, start)
                return c
            lax.fori_loop(0, N_EXPERTS, body, 0)

            def unused(p, c):
                row0 = pl.multiple_of(p * TM, TM)
                cp = pltpu.make_async_copy(h2buf.at[pl.ds(0, TM), :], hs_ref.at[pl.ds(row0, TM), :], rsem)
                cp.start() if start else cp.wait()
                if start:
                    def spare(r, c2):
                        dest_ref[p * TM + r] = N_ASSIGN + r
                        return c2
                    lax.fori_loop(0, TM, spare, 0)
                return c
            lax.fori_loop(npages, N_PAGES, unused, 0)

        tails(True)
        tails(False)

    @pl.when(i < N_PROMPT_TILES)
    def _prompt():
        s = i % TILES_PER_SEQ
        x = load_x(True)
        proj = project(x)
        zhist = jnp.where(s == 0, 0.0, zh_ref[...])
        phist = jnp.where(s == 0, 0.0, ph_ref[...])
        mix, znew, pnew = _mix_rows(proj, zhist, phist, s * TL, TL, convw_ref[...], poolbd_ref[...],
                                    pscale_ref[...], masked_wm(), bias_ref[...])
        zh_ref[...] = znew
        ph_ref[...] = pnew
        cpr_ref[0] = znew
        ppr_ref[0] = pnew
        finish(x, mix)

    @pl.when(i == N_PROMPT_TILES)
    def _sample():
        x = load_x(False)
        proj = project(x)
        wm = masked_wm()
        mixes = []
        for b in range(DEC_BATCH):
            rows = slice(b * DEC_SEQ, (b + 1) * DEC_SEQ)
            mix, znew, pnew = _mix_rows(proj[rows, :], sconv_ref[b], spool_ref[b], PAST_LEN, DEC_SEQ,
                                        convw_ref[...], poolbd_ref[...], pscale_ref[...], wm, bias_ref[...])
            csm_ref[b] = znew
            psm_ref[b] = pnew
            mixes.append(mix)
        sv_ref[...] = proj[:, 1792:2176]
        finish(x, jnp.concatenate(mixes, axis=0))
        _close_pages()


def _mixer_call(first_layer, xs, sconv_pad, spool_pad, g1, w_in, conv_w, pool_bd, pool_scale, wm_all,
                bias_full, w_out, g2, wr_t, br_col, triu, tril_e):
    tile = lambda i: (i, 0)
    prompt_tile = lambda i: (jnp.minimum(i, N_PROMPT_TILES - 1), 0)
    const2 = lambda i: (0, 0)
    const3 = lambda i: (0, 0, 0)
    if first_layer:
        x_specs = [pl.BlockSpec((TL, D_MODEL), prompt_tile), pl.BlockSpec((TL, D_MODEL), const2)]
    else:
        x_specs = [pl.BlockSpec((TL, D_MODEL), tile), pl.BlockSpec((TL, D_MODEL), tile),
                   pl.BlockSpec((TL, D_MODEL), lambda i: (i + N_TILES, 0)), pl.BlockSpec((TL, 128), tile)]
    full = lambda a: pl.BlockSpec(a.shape, const2 if a.ndim == 2 else const3)
    weights = [sconv_pad, spool_pad, g1, w_in, conv_w, pool_bd, pool_scale, wm_all, bias_full, w_out, g2,
               wr_t, br_col, triu, tril_e]
    in_specs = x_specs + [full(a) for a in weights]
    seq_of = lambda i: (jnp.minimum(i // TILES_PER_SEQ, BATCH - 1), 0, 0)
    out_shape = [
        jax.ShapeDtypeStruct((T_ALL, D_MODEL), jnp.float32),
        jax.ShapeDtypeStruct((N_PAGES * TM, D_MODEL), jnp.float32),
        jax.ShapeDtypeStruct((N_PAGES * TM,), jnp.int32),
        jax.ShapeDtypeStruct((3, PAGE_LANES), jnp.int32),
        jax.ShapeDtypeStruct((T_ALL, 128), jnp.float32),
        jax.ShapeDtypeStruct((BATCH, HIST_ROWS, D_CONV), jnp.float32),
        jax.ShapeDtypeStruct((BATCH, HIST_ROWS, D_POOL), jnp.float32),
        jax.ShapeDtypeStruct((DEC_BATCH, HIST_ROWS, D_CONV), jnp.float32),
        jax.ShapeDtypeStruct((DEC_BATCH, HIST_ROWS, D_POOL), jnp.float32),
        jax.ShapeDtypeStruct((T_SAMPLE, D_SGU), jnp.float32),
    ]
    out_specs = [
        pl.BlockSpec((TL, D_MODEL), tile),
        pl.BlockSpec(memory_space=pl.ANY),
        pl.BlockSpec(memory_space=pltpu.SMEM),
        pl.BlockSpec(memory_space=pltpu.SMEM),
        pl.BlockSpec((TL, 128), tile),
        pl.BlockSpec((1, HIST_ROWS, D_CONV), seq_of),
        pl.BlockSpec((1, HIST_ROWS, D_POOL), seq_of),
        pl.BlockSpec((DEC_BATCH, HIST_ROWS, D_CONV), const3),
        pl.BlockSpec((DEC_BATCH, HIST_ROWS, D_POOL), const3),
        pl.BlockSpec((T_SAMPLE, D_SGU), const2),
    ]
    scratch = [
        pltpu.VMEM((HIST_ROWS, D_CONV), jnp.float32),
        pltpu.VMEM((HIST_ROWS, D_POOL), jnp.float32),
        pltpu.VMEM((TL, D_MODEL), jnp.float32),
        pltpu.VMEM((8, TL), jnp.int32),
        pltpu.SMEM((8, TL), jnp.int32),
        pltpu.VMEM((N_EXPERTS, 128), jnp.int32),
        pltpu.VMEM((N_EXPERTS, 128), jnp.int32),
        pltpu.VMEM((8, 128), jnp.int32),
        pltpu.VMEM((8, PAGE_LANES), jnp.int32),
        pltpu.SMEM((N_EXPERTS, 128), jnp.int32),
        pltpu.SMEM((N_EXPERTS, 128), jnp.int32),
        pltpu.SMEM((8, 128), jnp.int32),
        pltpu.SMEM((8, PAGE_LANES), jnp.int32),
        pltpu.SMEM((N_EXPERTS,), jnp.int32),
        pltpu.SemaphoreType.DMA(()),
        pltpu.SemaphoreType.DMA(()),
    ]
    return pl.pallas_call(
        functools.partial(_mixer_kernel, first_layer),
        grid=(N_TILES,),
        in_specs=in_specs,
        out_specs=out_specs,
        out_shape=out_shape,
        scratch_shapes=scratch,
        compiler_params=pltpu.CompilerParams(dimension_semantics=("arbitrary",),
                                             vmem_limit_bytes=VMEM_LIMIT),
        name="mixer_first" if first_layer else "mixer_next",
    )(*xs, *weights)


def _expert_kernel(meta_ref, dest_ref, hs_ref, wg_ref, wu_ref, wd_ref, y_ref, obuf, sem):
    s = pl.program_id(0)
    npages = meta_ref[2, 0]

    @pl.when(s == 0)
    def _spare_rows():
        obuf[...] = jnp.zeros_like(obuf)
        cp = pltpu.make_async_copy(obuf, y_ref.at[pl.ds(N_ASSIGN, TM), :], sem)
        cp.start()
        cp.wait()

    @pl.when(s < npages)
    def _page():
        x = hs_ref[...].astype(jnp.bfloat16)
        hg = jnp.dot(x, wg_ref[0, 0].astype(jnp.bfloat16), preferred_element_type=jnp.float32)
        hu = jnp.dot(x, wu_ref[0, 0].astype(jnp.bfloat16), preferred_element_type=jnp.float32)
        h = (hg * jax.nn.sigmoid(hg) * hu).astype(jnp.bfloat16)
        obuf[...] = jnp.dot(h, wd_ref[0, 0].astype(jnp.bfloat16), preferred_element_type=jnp.float32)
        row0 = meta_ref[0, s] * TM
        for r in range(TM):
            d = dest_ref[row0 + r]
            pltpu.make_async_copy(obuf.at[pl.ds(r, 1), :], y_ref.at[pl.ds(d, 1), :], sem).start()
        pltpu.make_async_copy(obuf, obuf, sem).wait()


def _expert_call(layer, meta, dest, hs, w_gate, w_up, w_down):
    w_idx = lambda s, meta, dest: (layer, meta[1, s], 0, 0)
    grid_spec = pltpu.PrefetchScalarGridSpec(
        num_scalar_prefetch=2,
        grid=(N_PAGES,),
        in_specs=[
            pl.BlockSpec((TM, D_MODEL), lambda s, meta, dest: (meta[0, s], 0)),
            pl.BlockSpec((1, 1, D_MODEL, D_EXPERT), w_idx),
            pl.BlockSpec((1, 1, D_MODEL, D_EXPERT), w_idx),
            pl.BlockSpec((1, 1, D_EXPERT, D_MODEL), w_idx),
        ],
        out_specs=pl.BlockSpec(memory_space=pl.ANY),
        scratch_shapes=[pltpu.VMEM((TM, D_MODEL), jnp.float32), pltpu.SemaphoreType.DMA(())],
    )
    return pl.pallas_call(
        _expert_kernel,
        grid_spec=grid_spec,
        out_shape=jax.ShapeDtypeStruct((Y_ROWS, D_MODEL), jnp.float32),
        compiler_params=pltpu.CompilerParams(dimension_semantics=("arbitrary",),
                                             vmem_limit_bytes=VMEM_LIMIT),
        name="experts",
    )(meta, dest, hs, w_gate, w_up, w_down)


def _final_kernel(xm_ref, y0_ref, y1_ref, gc_ref, g_ref, yp_ref, ys_ref):
    i = pl.program_id(0)
    g = gc_ref[...]
    x = xm_ref[...] + g[:, 0:1] * y0_ref[...] + g[:, 1:2] * y1_ref[...]
    out = _rms(x, g_ref[...])

    @pl.when(i < N_PROMPT_TILES)
    def _():
        yp_ref[...] = out

    @pl.when(i == N_PROMPT_TILES)
    def _():
        ys_ref[...] = out


def _final_call(x_mid, y, gcol, g):
    tile = lambda i: (i, 0)
    return pl.pallas_call(
        _final_kernel,
        grid=(N_TILES,),
        in_specs=[pl.BlockSpec((TL, D_MODEL), tile), pl.BlockSpec((TL, D_MODEL), tile),
                  pl.BlockSpec((TL, D_MODEL), lambda i: (i + N_TILES, 0)),
                  pl.BlockSpec((TL, 128), tile), pl.BlockSpec((1, D_MODEL), lambda i: (0, 0))],
        out_specs=[pl.BlockSpec((TL, D_MODEL), lambda i: (jnp.minimum(i, N_PROMPT_TILES - 1), 0)),
                   pl.BlockSpec((TL, D_MODEL), lambda i: (0, 0))],
        out_shape=[jax.ShapeDtypeStruct((T_PROMPT, D_MODEL), jnp.float32),
                   jax.ShapeDtypeStruct((T_SAMPLE, D_MODEL), jnp.float32)],
        compiler_params=pltpu.CompilerParams(dimension_semantics=("arbitrary",),
                                             vmem_limit_bytes=VMEM_LIMIT),
        name="final_norm",
    )(x_mid, y, y, gcol, g)


def kernel(x_prompt, x_sample, state_conv, state_pool, norm1_g, w_in, conv_w, pool_w, pool_scale, sgu_w, sgu_b, w_out, norm2_g, router_coarse_w, router_coarse_b, router_fine_w, router_fine_b, moe_w_gate, moe_w_up, moe_w_down, final_norm_g):
    bf16 = jnp.bfloat16
    xs = (x_prompt.reshape(T_PROMPT, D_MODEL), x_sample.reshape(T_SAMPLE, D_MODEL))
    sconv_pad = jnp.pad(state_conv, ((0, 0), (0, 0), (HIST_ROWS - (CONV_WIDTH - 1), 0), (0, 0)))
    spool_pad = jnp.pad(state_pool, ((0, 0), (0, 0), (HIST_ROWS - POOL_HIST, 0), (0, 0)))
    idx = jnp.arange(TL, dtype=jnp.int32)
    triu = (idx[:, None] < idx[None, :]).astype(bf16)
    ide = jnp.arange(N_EXPERTS, dtype=jnp.int32)
    tril_e = (ide[None, :] < ide[:, None]).astype(bf16)
    conv_pr, pool_pr, conv_sm, pool_sm, sgu_v = [], [], [], [], []
    x_mid = y = gcol = None
    for l in range(DEPTH):
        pool_bd = jax.scipy.linalg.block_diag(*[pool_w[l, g] for g in range(4)]).astype(bf16)
        wm_all = sgu_w[l].reshape(SGU_HEADS * SGU_LEN, SGU_LEN).astype(bf16)
        bias_full = jnp.repeat(sgu_b[l].T, SGU_HEAD_DIM, axis=1)
        wr_t = jnp.zeros((ROUTER_ROWS, D_MODEL), jnp.float32)
        wr_t = wr_t.at[0:N_GROUPS].set(router_coarse_w[l].T).at[8:].set(router_fine_w[l].T).astype(bf16)
        br_col = jnp.zeros((ROUTER_ROWS, 1), jnp.float32)
        br_col = br_col.at[0:N_GROUPS, 0].set(router_coarse_b[l]).at[8:, 0].set(router_fine_b[l])
        outs = _mixer_call(l == 0, xs, sconv_pad[l], spool_pad[l], norm1_g[l].reshape(1, D_MODEL),
                           w_in[l].astype(bf16), conv_w[l], pool_bd, pool_scale[l].reshape(1, D_POOL),
                           wm_all, bias_full, w_out[l].astype(bf16), norm2_g[l].reshape(1, D_MODEL),
                           wr_t, br_col, triu, tril_e)
        x_mid, hs, dest, meta, gcol, cpr, ppr, csm, psm, sv = outs
        conv_pr.append(cpr[:, HIST_ROWS - (CONV_WIDTH - 1):, :])
        pool_pr.append(ppr[:, HIST_ROWS - POOL_HIST:, :])
        conv_sm.append(csm[:, HIST_ROWS - (CONV_WIDTH - 1):, :])
        pool_sm.append(psm[:, HIST_ROWS - POOL_HIST:, :])
        sgu_v.append(sv.reshape(DEC_BATCH, DEC_SEQ, D_SGU))
        y = _expert_call(l, meta, dest, hs, moe_w_gate, moe_w_up, moe_w_down)
        xs = (x_mid, y, y, gcol)
    y_prompt, y_sample = _final_call(x_mid, y, gcol, final_norm_g.reshape(1, D_MODEL))
    return (y_prompt.reshape(BATCH, SEQ, D_MODEL), y_sample.reshape(DEC_BATCH, DEC_SEQ, D_MODEL),
            jnp.stack(conv_pr), jnp.stack(pool_pr), jnp.stack(conv_sm), jnp.stack(pool_sm),
            jnp.stack(sgu_v))
```

```python
import functools

import jax
import jax.numpy as jnp
from jax import lax
from jax.experimental import pallas as pl
from jax.experimental.pallas import tpu as pltpu

D_MODEL = 1024
BATCH = 8
SEQ = 2048
DEPTH = 2
DEC_BATCH = 8
DEC_SEQ = 64
PAST_LEN = 1024
D_CONV = 384
CONV_WIDTH = 3
D_POOL = 256
POOL_HIST = 15
D_SGU = 384
SGU_HEADS = 4
SGU_HEAD_DIM = 96
SGU_LEN = 128
D_PROJ = 2176
N_GROUPS = 4
EXPERTS_PER_GROUP = 8
N_EXPERTS = 32
D_EXPERT = 512
EPS = 1e-6

T_PROMPT = BATCH * SEQ
T_SAMPLE = DEC_BATCH * DEC_SEQ
T_ALL = T_PROMPT + T_SAMPLE
TL = 512
TILES_PER_SEQ = SEQ // TL
N_PROMPT_TILES = T_PROMPT // TL
N_TILES = N_PROMPT_TILES + T_SAMPLE // TL
HIST_ROWS = 16
ROUTER_ROWS = 8 + N_EXPERTS
TM = 256
N_ASSIGN = 2 * T_ALL
N_PAGES = N_ASSIGN // TM + N_EXPERTS
PAGE_LANES = 256
Y_ROWS = N_ASSIGN + 2 * TM
HS_ROWS = N_PAGES * TM + 2 * TL
VMEM_LIMIT = 56 * 1024 * 1024

assert TL == 2 * TM and N_PAGES <= PAGE_LANES and T_ALL == N_TILES * TL


def _rms(x, g):
    return x * lax.rsqrt(jnp.mean(x * x, axis=-1, keepdims=True) + EPS) * g


def _mix_rows(proj, zhist, phist, pos0, n, conv_w, pool_bd, pool_scale, wm_all, bias_full):
    a_b = proj[:, 0:384]
    a_c = proj[:, 384:768]
    a_h = proj[:, 768:1152]
    p_in = proj[:, 1152:1408]
    s_u = proj[:, 1408:1792]
    s_v = proj[:, 1792:2176]

    z = a_c * a_h
    zext = jnp.concatenate([zhist, z], axis=0)
    conv_y = (conv_w[0:1, :] * pltpu.roll(zext, 2, 0)[HIST_ROWS:, :]
              + conv_w[1:2, :] * pltpu.roll(zext, 1, 0)[HIST_ROWS:, :]
              + conv_w[2:3, :] * z)
    a_out = a_b * conv_y

    pext = jnp.concatenate([phist, p_in], axis=0)
    s2 = pext + pltpu.roll(pext, 1, 0)
    s4 = s2 + pltpu.roll(s2, 2, 0)
    s8 = s4 + pltpu.roll(s4, 4, 0)
    s16 = s8 + pltpu.roll(s8, 8, 0)
    lane = lax.broadcasted_iota(jnp.int32, (1, D_POOL), 1)
    wsum = jnp.where(lane < 64, s2, jnp.where(lane < 128, s4, jnp.where(lane < 192, s8, s16)))
    wsum = wsum[HIST_ROWS:, :]
    win = jnp.where(lane < 64, 2.0, jnp.where(lane < 128, 4.0, jnp.where(lane < 192, 8.0, 16.0)))
    pos = (pos0 + lax.broadcasted_iota(jnp.int32, (n, 1), 0) + 1).astype(jnp.float32)
    cnt = jnp.minimum(win, pos)
    pooled = wsum / cnt - p_in
    p_out = jnp.dot(pooled.astype(jnp.bfloat16), pool_bd,
                    preferred_element_type=jnp.float32) * pool_scale

    lane_s = lax.broadcasted_iota(jnp.int32, (1, D_SGU), 1)
    chunk = min(n, SGU_LEN)
    if chunk == SGU_LEN:
        wm = wm_all
    else:
        wm = jnp.concatenate([wm_all[h * SGU_LEN:h * SGU_LEN + chunk, 0:chunk]
                              for h in range(SGU_HEADS)], axis=0)
    s_rows = []
    for c in range(n // chunk):
        v_c = s_v[c * chunk:(c + 1) * chunk, :].astype(jnp.bfloat16)
        r = jnp.dot(wm, v_c, preferred_element_type=jnp.float32)
        s_c = jnp.where(lane_s < 96, r[0:chunk],
                        jnp.where(lane_s < 192, r[chunk:2 * chunk],
                                  jnp.where(lane_s < 288, r[2 * chunk:3 * chunk], r[3 * chunk:4 * chunk])))
        s_rows.append(s_c + bias_full[0:chunk, :])
    s_gate = s_rows[0] if len(s_rows) == 1 else jnp.concatenate(s_rows, axis=0)
    s_out = s_u * s_gate

    mix = jnp.concatenate([a_out, p_out, s_out], axis=-1)
    return mix, zext[n:n + HIST_ROWS, :], pext[n:n + HIST_ROWS, :]


def _route(h2_bf16, wr_t, br_col, n):
    logits = lax.dot_general(wr_t, h2_bf16, (((1,), (1,)), ((), ())),
                             preferred_element_type=jnp.float32) + br_col
    row8 = lax.broadcasted_iota(jnp.int32, (8, n), 0)
    lc = jnp.where(row8 < N_GROUPS, logits[0:8, :], -jnp.inf)
    mc = jnp.max(lc, axis=0, keepdims=True)
    g_sel = jnp.min(jnp.where(lc == mc, row8, 8), axis=0, keepdims=True)
    p_sel = 1.0 / jnp.sum(jnp.exp(lc - mc), axis=0, keepdims=True)
    lf = logits[8 + 3 * EXPERTS_PER_GROUP:8 + 4 * EXPERTS_PER_GROUP, :]
    for g in (2, 1, 0):
        lf = jnp.where(g_sel == g, logits[8 + g * EXPERTS_PER_GROUP:8 + (g + 1) * EXPERTS_PER_GROUP, :], lf)
    m1 = jnp.max(lf, axis=0, keepdims=True)
    i1 = jnp.min(jnp.where(lf == m1, row8, 8), axis=0, keepdims=True)
    lf2 = jnp.where(row8 == i1, -jnp.inf, lf)
    m2 = jnp.max(lf2, axis=0, keepdims=True)
    i2 = jnp.min(jnp.where(lf2 == m2, row8, 8), axis=0, keepdims=True)
    t = jnp.exp(m2 - m1)
    wa = 1.0 / (1.0 + t)
    wb = t / (1.0 + t)
    e0 = g_sel * EXPERTS_PER_GROUP + i1
    e1 = g_sel * EXPERTS_PER_GROUP + i2
    return e0, e1, p_sel * wa, p_sel * wb


def _place_rows(e0, e1, triu, tril_e, fill_ref, base_ref, np_ref, pexp_ref):
    row_e = lax.broadcasted_iota(jnp.int32, (N_EXPERTS, TL), 0)
    oh0 = row_e == e0
    oh1 = row_e == e1
    oh = jnp.where(oh0 | oh1, 1.0, 0.0)
    rank = jnp.dot(oh.astype(jnp.bfloat16), triu, preferred_element_type=jnp.float32).astype(jnp.int32)
    cnt = jnp.sum(oh, axis=1, keepdims=True).astype(jnp.int32)
    fill = fill_ref[:, 0:1]
    base = base_ref[:, 0:1]
    npages = np_ref[0:1, 0:1]
    total = fill + cnt
    need = (total > TM).astype(jnp.int32) + (total > 2 * TM).astype(jnp.int32)
    need_b = jnp.broadcast_to(need.astype(jnp.float32), (N_EXPERTS, 128)).astype(jnp.bfloat16)
    before = jnp.dot(tril_e, need_b, preferred_element_type=jnp.float32)[:, 0:1].astype(jnp.int32)
    new_id = npages + before
    new_base = new_id * TM
    r = fill + rank
    pos_all = jnp.where(r < TM, base + r, new_base + (r - TM))
    pos0 = jnp.sum(jnp.where(oh0, pos_all, 0), axis=0, keepdims=True)
    pos1 = jnp.sum(jnp.where(oh1, pos_all, 0), axis=0, keepdims=True)
    fill_ref[...] = jnp.broadcast_to(total - need * TM, (N_EXPERTS, 128))
    base_ref[...] = jnp.broadcast_to(jnp.where(need > 0, new_base + (need - 1) * TM, base), (N_EXPERTS, 128))
    np_ref[...] = jnp.broadcast_to(npages + jnp.sum(need, axis=0, keepdims=True), (8, 128))
    page_lane = lax.broadcasted_iota(jnp.int32, (N_EXPERTS, PAGE_LANES), 1)
    expert_col = lax.broadcasted_iota(jnp.int32, (N_EXPERTS, 1), 0)
    owns = ((page_lane == new_id) & (need >= 1)) | ((page_lane == new_id + 1) & (need == 2))
    pexp_ref[...] = pexp_ref[...] + jnp.sum(jnp.where(owns, expert_col, 0), axis=0, keepdims=True)
    return pos0, pos1


def _mixer_kernel(first_layer, *refs):
    if first_layer:
        xp_ref, xs_ref = refs[0:2]
        rest = refs[2:]
    else:
        xm_ref, y0_ref, y1_ref, gc_ref = refs[0:4]
        rest = refs[4:]
    (sconv_ref, spool_ref, g1_ref, win_ref, convw_ref, poolbd_ref, pscale_ref, wm_ref, bias_ref,
     wout_ref, g2_ref, wr_ref, br_ref, triu_ref, trile_ref,
     xmid_ref, hs_ref, dest_ref, meta_ref, gcol_ref, cpr_ref, ppr_ref, csm_ref, psm_ref, sv_ref,
     zh_ref, ph_ref, h2buf, posv, poss, fill_ref, base_ref, np_ref, pexp_ref,
     fill_s, base_s, np_s, pexp_s, cnt_s, ppos_s, rsem, msem) = rest

    i = pl.program_id(0)

    def load_x(prompt):
        if first_layer:
            return xp_ref[...] if prompt else xs_ref[...]
        g = gc_ref[...]
        return xm_ref[...] + g[:, 0:1] * y0_ref[...] + g[:, 1:2] * y1_ref[...]

    row_m = lax.broadcasted_iota(jnp.int32, (SGU_HEADS * SGU_LEN, SGU_LEN), 0) % SGU_LEN
    col_m = lax.broadcasted_iota(jnp.int32, (SGU_HEADS * SGU_LEN, SGU_LEN), 1)

    def row_copy(slot, r, pos):
        return pltpu.make_async_copy(h2buf.at[slot, pl.ds(r, 1), :], hs_ref.at[pl.ds(pos, 1), :], rsem)

    def scatter_rows(tile, slot):
        pltpu.make_async_copy(posv, poss.at[slot], msem).wait()
        tok0 = tile * TL
        for r in range(TL):
            for k in range(2):
                p = poss[slot, k, r]
                row_copy(slot, r, p).start()
                dest_ref[p] = k * T_ALL + tok0 + r

    def wait_rows():
        for k in range(2):
            pltpu.make_async_copy(h2buf.at[0], h2buf.at[0], rsem).wait()

    def finish(x, mix):
        x_mid = x + jnp.dot(mix.astype(jnp.bfloat16), wout_ref[...], preferred_element_type=jnp.float32)
        xmid_ref[...] = x_mid
        h2 = _rms(x_mid, g2_ref[...])
        h2buf[i % 2] = h2
        e0, e1, g0, g1 = _route(h2.astype(jnp.bfloat16), wr_ref[...], br_ref[...], TL)
        row128 = lax.broadcasted_iota(jnp.int32, (128, TL), 0)
        gcol_ref[...] = jnp.where(row128 == 0, g0, jnp.where(row128 == 1, g1, 0.0)).T
        pos0, pos1 = _place_rows(e0, e1, triu_ref[...], trile_ref[...], fill_ref, base_ref, np_ref, pexp_ref)
        row8 = lax.broadcasted_iota(jnp.int32, (8, TL), 0)
        posv[...] = jnp.where(row8 == 0, pos0, jnp.where(row8 == 1, pos1, 0))
        pltpu.make_async_copy(posv, poss.at[i % 2], msem).start()

    def project(x):
        h = _rms(x, g1_ref[...]).astype(jnp.bfloat16)
        return jnp.dot(h, win_ref[...], preferred_element_type=jnp.float32)

    def masked_wm():
        return jnp.where(col_m <= row_m, wm_ref[...], jnp.zeros_like(wm_ref[...]))

    @pl.when(i == 0)
    def _init():
        zh_ref[...] = jnp.zeros_like(zh_ref)
        ph_ref[...] = jnp.zeros_like(ph_ref)
        fill_ref[...] = jnp.full_like(fill_ref, TM)
        base_ref[...] = jnp.zeros_like(base_ref)
        np_ref[...] = jnp.zeros_like(np_ref)
        pexp_ref[...] = jnp.zeros_like(pexp_ref)
        h2buf[1] = jnp.zeros((TL, D_MODEL), jnp.float32)
        row8 = lax.broadcasted_iota(jnp.int32, (8, TL), 0)
        lane = lax.broadcasted_iota(jnp.int32, (8, TL), 1)
        posv[...] = N_PAGES * TM + jnp.minimum(row8, 1) * TL + lane
        pltpu.make_async_copy(posv, poss.at[1], msem).start()

    def _close_pages():
        copies = [pltpu.make_async_copy(fill_ref, fill_s, msem), pltpu.make_async_copy(base_ref, base_s, msem),
                  pltpu.make_async_copy(np_ref, np_s, msem), pltpu.make_async_copy(pexp_ref, pexp_s, msem)]
        for c in copies:
            c.start()
        for c in copies:
            c.wait()
        npages = np_s[0, 0]

        def zero_cnt(e, c):
            cnt_s[e] = 0
            return c
        lax.fori_loop(0, N_EXPERTS, zero_cnt, 0)

        def count(p, c):
            e = pexp_s[0, p]
            cnt_s[e] = cnt_s[e] + 1
            return c
        lax.fori_loop(0, npages, count, 0)

        def prefix(e, run):
            n = cnt_s[e]
            cnt_s[e] = run
            return run + n
        lax.fori_loop(0, N_EXPERTS, prefix, 0)

        def emit(p, c):
            e = pexp_s[0, p]
            q = cnt_s[e]
            cnt_s[e] = q + 1
            meta_ref[0, q] = p
            meta_ref[1, q] = e
            ppos_s[p] = q
            return c
        lax.fori_loop(0, npages, emit, 0)

        def pad_meta(q, c):
            meta_ref[0, q] = meta_ref[0, npages - 1]
            meta_ref[1, q] = meta_ref[1, npages - 1]
            return c
        lax.fori_loop(npages, PAGE_LANES, pad_meta, 0)
        meta_ref[0, PAGE_LANES - 1] = N_PAGES

        def warm_up(r, c):
            dest_ref[N_PAGES * TM + r] = N_ASSIGN + TM + r
            return c
        lax.fori_loop(0, TM, warm_up, 0)

        def fill_row2(q, c):
            meta_ref[2, q] = npages
            return c
        lax.fori_loop(0, PAGE_LANES, fill_row2, 0)

        h2buf[0] = jnp.zeros((TL, D_MODEL), jnp.float32)

        def tail_single(e, start):
            f = fill_s[e, 0]
            b = base_s[e, 0]
            f8 = jnp.minimum(((f + 7) // 8) * 8, TM)
            spare0 = N_ASSIGN + (ppos_s[b // TM] % 2) * TM

            def one(r, c):
                cp = row_copy(0, 0, b + r)
                cp.start() if start else cp.wait()
                return c
            lax.fori_loop(f, f8, one, 0)

            for nrows in (128, 64, 32, 16, 8):
                off = ((TM - f8) // (2 * nrows)) * (2 * nrows)

                @pl.when(((TM - f8) // nrows) % 2 == 1)
                def _blk():
                    row0 = pl.multiple_of(b + f8 + off, 8)
                    cp = pltpu.make_async_copy(h2buf.at[0, pl.ds(0, nrows), :], hs_ref.at[pl.ds(row0, nrows), :], rsem)
                    cp.start() if start else cp.wait()

            if start:
                def spare(r, c):
                    dest_ref[b + r] = spare0 + r
                    return c
                lax.fori_loop(f, TM, spare, 0)

        def tails(start):
            def body(e, c):
                tail_single(e, start)
                return c
            lax.fori_loop(0, N_EXPERTS, body, 0)

            def unused(p, c):
                row0 = pl.multiple_of(p * TM, TM)
                cp = pltpu.make_async_copy(h2buf.at[0, pl.ds(0, TM), :], hs_ref.at[pl.ds(row0, TM), :], rsem)
                cp.start() if start else cp.wait()
                if start:
                    def spare(r, c2):
                        dest_ref[p * TM + r] = N_ASSIGN + r
                        return c2
                    lax.fori_loop(0, TM, spare, 0)
                return c
            lax.fori_loop(npages, N_PAGES, unused, 0)

        tails(True)
        tails(False)

    @pl.when(i < N_PROMPT_TILES)
    def _prompt():
        s = i % TILES_PER_SEQ
        x = load_x(True)
        proj = project(x)
        scatter_rows(i - 1, (i + 1) % 2)
        zhist = jnp.where(s == 0, 0.0, zh_ref[...])
        phist = jnp.where(s == 0, 0.0, ph_ref[...])
        mix, znew, pnew = _mix_rows(proj, zhist, phist, s * TL, TL, convw_ref[...], poolbd_ref[...],
                                    pscale_ref[...], masked_wm(), bias_ref[...])
        zh_ref[...] = znew
        ph_ref[...] = pnew
        cpr_ref[0] = znew
        ppr_ref[0] = pnew
        finish(x, mix)
        wait_rows()

    @pl.when(i == N_PROMPT_TILES)
    def _sample():
        x = load_x(False)
        proj = project(x)
        scatter_rows(i - 1, (i + 1) % 2)
        wm = masked_wm()
        mixes = []
        for b in range(DEC_BATCH):
            rows = slice(b * DEC_SEQ, (b + 1) * DEC_SEQ)
            mix, znew, pnew = _mix_rows(proj[rows, :], sconv_ref[b], spool_ref[b], PAST_LEN, DEC_SEQ,
                                        convw_ref[...], poolbd_ref[...], pscale_ref[...], wm, bias_ref[...])
            csm_ref[b] = znew
            psm_ref[b] = pnew
            mixes.append(mix)
        sv_ref[...] = proj[:, 1792:2176]
        finish(x, jnp.concatenate(mixes, axis=0))
        wait_rows()
        scatter_rows(i, i % 2)
        wait_rows()
        _close_pages()


def _mixer_call(first_layer, xs, sconv_pad, spool_pad, g1, w_in, conv_w, pool_bd, pool_scale, wm_all,
                bias_full, w_out, g2, wr_t, br_col, triu, tril_e):
    tile = lambda i: (i, 0)
    prompt_tile = lambda i: (jnp.minimum(i, N_PROMPT_TILES - 1), 0)
    const2 = lambda i: (0, 0)
    const3 = lambda i: (0, 0, 0)
    if first_layer:
        x_specs = [pl.BlockSpec((TL, D_MODEL), prompt_tile), pl.BlockSpec((TL, D_MODEL), const2)]
    else:
        x_specs = [pl.BlockSpec((TL, D_MODEL), tile), pl.BlockSpec((TL, D_MODEL), tile),
                   pl.BlockSpec((TL, D_MODEL), lambda i: (i + N_TILES, 0)), pl.BlockSpec((TL, 128), tile)]
    full = lambda a: pl.BlockSpec(a.shape, const2 if a.ndim == 2 else const3)
    weights = [sconv_pad, spool_pad, g1, w_in, conv_w, pool_bd, pool_scale, wm_all, bias_full, w_out, g2,
               wr_t, br_col, triu, tril_e]
    in_specs = x_specs + [full(a) for a in weights]
    seq_of = lambda i: (jnp.minimum(i // TILES_PER_SEQ, BATCH - 1), 0, 0)
    out_shape = [
        jax.ShapeDtypeStruct((T_ALL, D_MODEL), jnp.float32),
        jax.ShapeDtypeStruct((HS_ROWS, D_MODEL), jnp.float32),
        jax.ShapeDtypeStruct((HS_ROWS,), jnp.int32),
        jax.ShapeDtypeStruct((3, PAGE_LANES), jnp.int32),
        jax.ShapeDtypeStruct((T_ALL, 128), jnp.float32),
        jax.ShapeDtypeStruct((BATCH, HIST_ROWS, D_CONV), jnp.float32),
        jax.ShapeDtypeStruct((BATCH, HIST_ROWS, D_POOL), jnp.float32),
        jax.ShapeDtypeStruct((DEC_BATCH, HIST_ROWS, D_CONV), jnp.float32),
        jax.ShapeDtypeStruct((DEC_BATCH, HIST_ROWS, D_POOL), jnp.float32),
        jax.ShapeDtypeStruct((T_SAMPLE, D_SGU), jnp.float32),
    ]
    out_specs = [
        pl.BlockSpec((TL, D_MODEL), tile),
        pl.BlockSpec(memory_space=pl.ANY),
        pl.BlockSpec(memory_space=pltpu.SMEM),
        pl.BlockSpec(memory_space=pltpu.SMEM),
        pl.BlockSpec((TL, 128), tile),
        pl.BlockSpec((1, HIST_ROWS, D_CONV), seq_of),
        pl.BlockSpec((1, HIST_ROWS, D_POOL), seq_of),
        pl.BlockSpec((DEC_BATCH, HIST_ROWS, D_CONV), const3),
        pl.BlockSpec((DEC_BATCH, HIST_ROWS, D_POOL), const3),
        pl.BlockSpec((T_SAMPLE, D_SGU), const2),
    ]
    scratch = [
        pltpu.VMEM((HIST_ROWS, D_CONV), jnp.float32),
        pltpu.VMEM((HIST_ROWS, D_POOL), jnp.float32),
        pltpu.VMEM((2, TL, D_MODEL), jnp.float32),
        pltpu.VMEM((8, TL), jnp.int32),
        pltpu.SMEM((2, 8, TL), jnp.int32),
        pltpu.VMEM((N_EXPERTS, 128), jnp.int32),
        pltpu.VMEM((N_EXPERTS, 128), jnp.int32),
        pltpu.VMEM((8, 128), jnp.int32),
        pltpu.VMEM((8, PAGE_LANES), jnp.int32),
        pltpu.SMEM((N_EXPERTS, 128), jnp.int32),
        pltpu.SMEM((N_EXPERTS, 128), jnp.int32),
        pltpu.SMEM((8, 128), jnp.int32),
        pltpu.SMEM((8, PAGE_LANES), jnp.int32),
        pltpu.SMEM((N_EXPERTS,), jnp.int32),
        pltpu.SMEM((PAGE_LANES,), jnp.int32),
        pltpu.SemaphoreType.DMA(()),
        pltpu.SemaphoreType.DMA(()),
    ]
    return pl.pallas_call(
        functools.partial(_mixer_kernel, first_layer),
        grid=(N_TILES,),
        in_specs=in_specs,
        out_specs=out_specs,
        out_shape=out_shape,
        scratch_shapes=scratch,
        compiler_params=pltpu.CompilerParams(dimension_semantics=("arbitrary",),
                                             vmem_limit_bytes=VMEM_LIMIT),
        name="mixer_first" if first_layer else "mixer_next",
    )(*xs, *weights)


def _expert_kernel(meta_ref, dest_ref, hs_ref, wg_ref, wu_ref, wd_ref, y_ref, obuf, sem):
    s = pl.program_id(0)
    npages = meta_ref[2, 0]

    @pl.when(s == 0)
    def _spare_rows():
        obuf[1] = jnp.zeros((TM, D_MODEL), jnp.float32)
        cp = pltpu.make_async_copy(obuf.at[1], y_ref.at[pl.ds(N_ASSIGN, TM), :], sem)
        cp.start()
        cp.wait()

    def scatter_previous():
        slot = (s + 1) % 2
        row0 = meta_ref[0, (s + PAGE_LANES - 1) % PAGE_LANES] * TM
        for r in range(TM):
            d = dest_ref[row0 + r]
            pltpu.make_async_copy(obuf.at[slot, pl.ds(r, 1), :], y_ref.at[pl.ds(d, 1), :], sem).start()

    def wait_previous():
        pltpu.make_async_copy(obuf.at[0], obuf.at[0], sem).wait()

    @pl.when(s < npages)
    def _page():
        scatter_previous()
        x = hs_ref[...].astype(jnp.bfloat16)
        hg = jnp.dot(x, wg_ref[0, 0].astype(jnp.bfloat16), preferred_element_type=jnp.float32)
        hu = jnp.dot(x, wu_ref[0, 0].astype(jnp.bfloat16), preferred_element_type=jnp.float32)
        h = (hg * jax.nn.sigmoid(hg) * hu).astype(jnp.bfloat16)
        obuf[s % 2] = jnp.dot(h, wd_ref[0, 0].astype(jnp.bfloat16), preferred_element_type=jnp.float32)
        wait_previous()

    @pl.when(s == npages)
    def _last():
        scatter_previous()
        wait_previous()


def _expert_call(layer, meta, dest, hs, w_gate, w_up, w_down):
    w_idx = lambda s, meta, dest: (layer, meta[1, s], 0, 0)
    grid_spec = pltpu.PrefetchScalarGridSpec(
        num_scalar_prefetch=2,
        grid=(N_PAGES + 1,),
        in_specs=[
            pl.BlockSpec((TM, D_MODEL), lambda s, meta, dest: (meta[0, s], 0)),
            pl.BlockSpec((1, 1, D_MODEL, D_EXPERT), w_idx),
            pl.BlockSpec((1, 1, D_MODEL, D_EXPERT), w_idx),
            pl.BlockSpec((1, 1, D_EXPERT, D_MODEL), w_idx),
        ],
        out_specs=pl.BlockSpec(memory_space=pl.ANY),
        scratch_shapes=[pltpu.VMEM((2, TM, D_MODEL), jnp.float32), pltpu.SemaphoreType.DMA(())],
    )
    return pl.pallas_call(
        _expert_kernel,
        grid_spec=grid_spec,
        out_shape=jax.ShapeDtypeStruct((Y_ROWS, D_MODEL), jnp.float32),
        compiler_params=pltpu.CompilerParams(dimension_semantics=("arbitrary",),
                                             vmem_limit_bytes=VMEM_LIMIT),
        name="experts",
    )(meta, dest, hs, w_gate, w_up, w_down)


def _final_kernel(xm_ref, y0_ref, y1_ref, gc_ref, g_ref, yp_ref, ys_ref):
    i = pl.program_id(0)
    g = gc_ref[...]
    x = xm_ref[...] + g[:, 0:1] * y0_ref[...] + g[:, 1:2] * y1_ref[...]
    out = _rms(x, g_ref[...])

    @pl.when(i < N_PROMPT_TILES)
    def _():
        yp_ref[...] = out

    @pl.when(i == N_PROMPT_TILES)
    def _():
        ys_ref[...] = out


def _final_call(x_mid, y, gcol, g):
    tile = lambda i: (i, 0)
    return pl.pallas_call(
        _final_kernel,
        grid=(N_TILES,),
        in_specs=[pl.BlockSpec((TL, D_MODEL), tile), pl.BlockSpec((TL, D_MODEL), tile),
                  pl.BlockSpec((TL, D_MODEL), lambda i: (i + N_TILES, 0)),
                  pl.BlockSpec((TL, 128), tile), pl.BlockSpec((1, D_MODEL), lambda i: (0, 0))],
        out_specs=[pl.BlockSpec((TL, D_MODEL), lambda i: (jnp.minimum(i, N_PROMPT_TILES - 1), 0)),
                   pl.BlockSpec((TL, D_MODEL), lambda i: (0, 0))],
        out_shape=[jax.ShapeDtypeStruct((T_PROMPT, D_MODEL), jnp.float32),
                   jax.ShapeDtypeStruct((T_SAMPLE, D_MODEL), jnp.float32)],
        compiler_params=pltpu.CompilerParams(dimension_semantics=("arbitrary",),
                                             vmem_limit_bytes=VMEM_LIMIT),
        name="final_norm",
    )(x_mid, y, y, gcol, g)


def kernel(x_prompt, x_sample, state_conv, state_pool, norm1_g, w_in, conv_w, pool_w, pool_scale, sgu_w, sgu_b, w_out, norm2_g, router_coarse_w, router_coarse_b, router_fine_w, router_fine_b, moe_w_gate, moe_w_up, moe_w_down, final_norm_g):
    bf16 = jnp.bfloat16
    xs = (x_prompt.reshape(T_PROMPT, D_MODEL), x_sample.reshape(T_SAMPLE, D_MODEL))
    sconv_pad = jnp.pad(state_conv, ((0, 0), (0, 0), (HIST_ROWS - (CONV_WIDTH - 1), 0), (0, 0)))
    spool_pad = jnp.pad(state_pool, ((0, 0), (0, 0), (HIST_ROWS - POOL_HIST, 0), (0, 0)))
    idx = jnp.arange(TL, dtype=jnp.int32)
    triu = (idx[:, None] < idx[None, :]).astype(bf16)
    ide = jnp.arange(N_EXPERTS, dtype=jnp.int32)
    tril_e = (ide[None, :] < ide[:, None]).astype(bf16)
    conv_pr, pool_pr, conv_sm, pool_sm, sgu_v = [], [], [], [], []
    x_mid = y = gcol = None
    for l in range(DEPTH):
        pool_bd = jax.scipy.linalg.block_diag(*[pool_w[l, g] for g in range(4)]).astype(bf16)
        wm_all = sgu_w[l].reshape(SGU_HEADS * SGU_LEN, SGU_LEN).astype(bf16)
        bias_full = jnp.repeat(sgu_b[l].T, SGU_HEAD_DIM, axis=1)
        wr_t = jnp.zeros((ROUTER_ROWS, D_MODEL), jnp.float32)
        wr_t = wr_t.at[0:N_GROUPS].set(router_coarse_w[l].T).at[8:].set(router_fine_w[l].T).astype(bf16)
        br_col = jnp.zeros((ROUTER_ROWS, 1), jnp.float32)
        br_col = br_col.at[0:N_GROUPS, 0].set(router_coarse_b[l]).at[8:, 0].set(router_fine_b[l])
        outs = _mixer_call(l == 0, xs, sconv_pad[l], spool_pad[l], norm1_g[l].reshape(1, D_MODEL),
                           w_in[l].astype(bf16), conv_w[l], pool_bd, pool_scale[l].reshape(1, D_POOL),
                           wm_all, bias_full, w_out[l].astype(bf16), norm2_g[l].reshape(1, D_MODEL),
                           wr_t, br_col, triu, tril_e)
        x_mid, hs, dest, meta, gcol, cpr, ppr, csm, psm, sv = outs
        conv_pr.append(cpr[:, HIST_ROWS - (CONV_WIDTH - 1):, :])
        pool_pr.append(ppr[:, HIST_ROWS - POOL_HIST:, :])
        conv_sm.append(csm[:, HIST_ROWS - (CONV_WIDTH - 1):, :])
        pool_sm.append(psm[:, HIST_ROWS - POOL_HIST:, :])
        sgu_v.append(sv.reshape(DEC_BATCH, DEC_SEQ, D_SGU))
        y = _expert_call(l, meta, dest, hs, moe_w_gate, moe_w_up, moe_w_down)
        xs = (x_mid, y, y, gcol)
    y_prompt, y_sample = _final_call(x_mid, y, gcol, final_norm_g.reshape(1, D_MODEL))
    return (y_prompt.reshape(BATCH, SEQ, D_MODEL), y_sample.reshape(DEC_BATCH, DEC_SEQ, D_MODEL),
            jnp.stack(conv_pr), jnp.stack(pool_pr), jnp.stack(conv_sm), jnp.stack(pool_sm),
            jnp.stack(sgu_v))
```

```python
import functools

import jax
import jax.numpy as jnp
from jax import lax
from jax.experimental import pallas as pl
from jax.experimental.pallas import tpu as pltpu

D_MODEL = 1024
BATCH = 8
SEQ = 2048
DEPTH = 2
DEC_BATCH = 8
DEC_SEQ = 64
PAST_LEN = 1024
D_CONV = 384
CONV_WIDTH = 3
D_POOL = 256
POOL_HIST = 15
D_SGU = 384
SGU_HEADS = 4
SGU_HEAD_DIM = 96
SGU_LEN = 128
D_PROJ = 2176
N_GROUPS = 4
EXPERTS_PER_GROUP = 8
N_EXPERTS = 32
D_EXPERT = 512
EPS = 1e-6

T_PROMPT = BATCH * SEQ
T_SAMPLE = DEC_BATCH * DEC_SEQ
T_ALL = T_PROMPT + T_SAMPLE
TL = 512
TILES_PER_SEQ = SEQ // TL
N_PROMPT_TILES = T_PROMPT // TL
N_TILES = N_PROMPT_TILES + T_SAMPLE // TL
HIST_ROWS = 16
ROUTER_ROWS = 8 + N_EXPERTS
TM = 256
N_ASSIGN = 2 * T_ALL
N_PAGES = N_ASSIGN // TM + N_EXPERTS
PAGE_LANES = 256
Y_ROWS = N_ASSIGN + 2 * TM
HS_ROWS = N_PAGES * TM + 2 * TL
EXPERT_ROWS_STAGE1 = 208
ROWS_STAGE1 = 320
VMEM_LIMIT = 56 * 1024 * 1024

assert TL == 2 * TM and N_PAGES <= PAGE_LANES and T_ALL == N_TILES * TL


def _rms(x, g):
    return x * lax.rsqrt(jnp.mean(x * x, axis=-1, keepdims=True) + EPS) * g


def _mix_rows(proj, zhist, phist, pos0, n, conv_w, pool_bd, pool_scale, wm_all, bias_full):
    a_b = proj[:, 0:384]
    a_c = proj[:, 384:768]
    a_h = proj[:, 768:1152]
    p_in = proj[:, 1152:1408]
    s_u = proj[:, 1408:1792]
    s_v = proj[:, 1792:2176]

    z = a_c * a_h
    zext = jnp.concatenate([zhist, z], axis=0)
    conv_y = (conv_w[0:1, :] * pltpu.roll(zext, 2, 0)[HIST_ROWS:, :]
              + conv_w[1:2, :] * pltpu.roll(zext, 1, 0)[HIST_ROWS:, :]
              + conv_w[2:3, :] * z)
    a_out = a_b * conv_y

    pext = jnp.concatenate([phist, p_in], axis=0)
    s2 = pext + pltpu.roll(pext, 1, 0)
    s4 = s2 + pltpu.roll(s2, 2, 0)
    s8 = s4 + pltpu.roll(s4, 4, 0)
    s16 = s8 + pltpu.roll(s8, 8, 0)
    lane = lax.broadcasted_iota(jnp.int32, (1, D_POOL), 1)
    wsum = jnp.where(lane < 64, s2, jnp.where(lane < 128, s4, jnp.where(lane < 192, s8, s16)))
    wsum = wsum[HIST_ROWS:, :]
    win = jnp.where(lane < 64, 2.0, jnp.where(lane < 128, 4.0, jnp.where(lane < 192, 8.0, 16.0)))
    pos = (pos0 + lax.broadcasted_iota(jnp.int32, (n, 1), 0) + 1).astype(jnp.float32)
    cnt = jnp.minimum(win, pos)
    pooled = wsum / cnt - p_in
    p_out = jnp.dot(pooled.astype(jnp.bfloat16), pool_bd,
                    preferred_element_type=jnp.float32) * pool_scale

    lane_s = lax.broadcasted_iota(jnp.int32, (1, D_SGU), 1)
    chunk = min(n, SGU_LEN)
    if chunk == SGU_LEN:
        wm = wm_all
    else:
        wm = jnp.concatenate([wm_all[h * SGU_LEN:h * SGU_LEN + chunk, 0:chunk]
                              for h in range(SGU_HEADS)], axis=0)
    s_rows = []
    for c in range(n // chunk):
        v_c = s_v[c * chunk:(c + 1) * chunk, :].astype(jnp.bfloat16)
        r = jnp.dot(wm, v_c, preferred_element_type=jnp.float32)
        s_c = jnp.where(lane_s < 96, r[0:chunk],
                        jnp.where(lane_s < 192, r[chunk:2 * chunk],
                                  jnp.where(lane_s < 288, r[2 * chunk:3 * chunk], r[3 * chunk:4 * chunk])))
        s_rows.append(s_c + bias_full[0:chunk, :])
    s_gate = s_rows[0] if len(s_rows) == 1 else jnp.concatenate(s_rows, axis=0)
    s_out = s_u * s_gate

    mix = jnp.concatenate([a_out, p_out, s_out], axis=-1)
    return mix, zext[n:n + HIST_ROWS, :], pext[n:n + HIST_ROWS, :]


def _route(h2_bf16, wr_t, br_col, n):
    logits = lax.dot_general(wr_t, h2_bf16, (((1,), (1,)), ((), ())),
                             preferred_element_type=jnp.float32) + br_col
    row8 = lax.broadcasted_iota(jnp.int32, (8, n), 0)
    lc = jnp.where(row8 < N_GROUPS, logits[0:8, :], -jnp.inf)
    mc = jnp.max(lc, axis=0, keepdims=True)
    g_sel = jnp.min(jnp.where(lc == mc, row8, 8), axis=0, keepdims=True)
    p_sel = 1.0 / jnp.sum(jnp.exp(lc - mc), axis=0, keepdims=True)
    lf = logits[8 + 3 * EXPERTS_PER_GROUP:8 + 4 * EXPERTS_PER_GROUP, :]
    for g in (2, 1, 0):
        lf = jnp.where(g_sel == g, logits[8 + g * EXPERTS_PER_GROUP:8 + (g + 1) * EXPERTS_PER_GROUP, :], lf)
    m1 = jnp.max(lf, axis=0, keepdims=True)
    i1 = jnp.min(jnp.where(lf == m1, row8, 8), axis=0, keepdims=True)
    lf2 = jnp.where(row8 == i1, -jnp.inf, lf)
    m2 = jnp.max(lf2, axis=0, keepdims=True)
    i2 = jnp.min(jnp.where(lf2 == m2, row8, 8), axis=0, keepdims=True)
    t = jnp.exp(m2 - m1)
    wa = 1.0 / (1.0 + t)
    wb = t / (1.0 + t)
    e0 = g_sel * EXPERTS_PER_GROUP + i1
    e1 = g_sel * EXPERTS_PER_GROUP + i2
    return e0, e1, p_sel * wa, p_sel * wb


def _place_rows(e0, e1, triu, tril_e, fill_ref, base_ref, np_ref, pexp_ref):
    row_e = lax.broadcasted_iota(jnp.int32, (N_EXPERTS, TL), 0)
    oh0 = row_e == e0
    oh1 = row_e == e1
    oh = jnp.where(oh0 | oh1, 1.0, 0.0)
    rank = jnp.dot(oh.astype(jnp.bfloat16), triu, preferred_element_type=jnp.float32).astype(jnp.int32)
    cnt = jnp.sum(oh, axis=1, keepdims=True).astype(jnp.int32)
    fill = fill_ref[:, 0:1]
    base = base_ref[:, 0:1]
    npages = np_ref[0:1, 0:1]
    total = fill + cnt
    need = (total > TM).astype(jnp.int32) + (total > 2 * TM).astype(jnp.int32)
    need_b = jnp.broadcast_to(need.astype(jnp.float32), (N_EXPERTS, 128)).astype(jnp.bfloat16)
    before = jnp.dot(tril_e, need_b, preferred_element_type=jnp.float32)[:, 0:1].astype(jnp.int32)
    new_id = npages + before
    new_base = new_id * TM
    r = fill + rank
    pos_all = jnp.where(r < TM, base + r, new_base + (r - TM))
    pos0 = jnp.sum(jnp.where(oh0, pos_all, 0), axis=0, keepdims=True)
    pos1 = jnp.sum(jnp.where(oh1, pos_all, 0), axis=0, keepdims=True)
    fill_ref[...] = jnp.broadcast_to(total - need * TM, (N_EXPERTS, 128))
    base_ref[...] = jnp.broadcast_to(jnp.where(need > 0, new_base + (need - 1) * TM, base), (N_EXPERTS, 128))
    np_ref[...] = jnp.broadcast_to(npages + jnp.sum(need, axis=0, keepdims=True), (8, 128))
    page_lane = lax.broadcasted_iota(jnp.int32, (N_EXPERTS, PAGE_LANES), 1)
    expert_col = lax.broadcasted_iota(jnp.int32, (N_EXPERTS, 1), 0)
    owns = ((page_lane == new_id) & (need >= 1)) | ((page_lane == new_id + 1) & (need == 2))
    pexp_ref[...] = pexp_ref[...] + jnp.sum(jnp.where(owns, expert_col, 0), axis=0, keepdims=True)
    return pos0, pos1


def _mixer_kernel(first_layer, *refs):
    if first_layer:
        xp_ref, xs_ref = refs[0:2]
        rest = refs[2:]
    else:
        xm_ref, y0_ref, y1_ref, gc_ref = refs[0:4]
        rest = refs[4:]
    (sconv_ref, spool_ref, g1_ref, win_ref, convw_ref, poolbd_ref, pscale_ref, wm_ref, bias_ref,
     wout_ref, g2_ref, wr_ref, br_ref, triu_ref, trile_ref,
     xmid_ref, hs_ref, dest_ref, meta_ref, gcol_ref, cpr_ref, ppr_ref, csm_ref, psm_ref, sv_ref,
     zh_ref, ph_ref, h2buf, proj_ref, mix_ref, posv, poss, fill_ref, base_ref, np_ref, pexp_ref,
     fill_s, base_s, np_s, pexp_s, cnt_s, ppos_s, rsem, msem) = rest

    i = pl.program_id(0)

    def load_x(prompt):
        if first_layer:
            return xp_ref[...] if prompt else xs_ref[...]
        g = gc_ref[...]
        return xm_ref[...] + g[:, 0:1] * y0_ref[...] + g[:, 1:2] * y1_ref[...]

    row_m = lax.broadcasted_iota(jnp.int32, (SGU_HEADS * SGU_LEN, SGU_LEN), 0) % SGU_LEN
    col_m = lax.broadcasted_iota(jnp.int32, (SGU_HEADS * SGU_LEN, SGU_LEN), 1)

    def row_copy(slot, r, pos):
        return pltpu.make_async_copy(h2buf.at[slot, pl.ds(r, 1), :], hs_ref.at[pl.ds(pos, 1), :], rsem)

    def wait_positions(slot):
        pltpu.make_async_copy(posv, poss.at[slot], msem).wait()

    def scatter_rows(tile, slot, rows):
        tok0 = tile * TL
        for r in rows:
            for k in range(2):
                p = poss[slot, k, r]
                row_copy(slot, r, p).start(priority=k)
                dest_ref[p] = k * T_ALL + tok0 + r

    def wait_rows():
        for k in range(2):
            pltpu.make_async_copy(h2buf.at[0], h2buf.at[0], rsem).wait()

    def finish(x, mix):
        x_mid = x + jnp.dot(mix.astype(jnp.bfloat16), wout_ref[...], preferred_element_type=jnp.float32)
        xmid_ref[...] = x_mid
        h2 = _rms(x_mid, g2_ref[...])
        h2buf[i % 2] = h2
        e0, e1, g0, g1 = _route(h2.astype(jnp.bfloat16), wr_ref[...], br_ref[...], TL)
        row128 = lax.broadcasted_iota(jnp.int32, (128, TL), 0)
        gcol_ref[...] = jnp.where(row128 == 0, g0, jnp.where(row128 == 1, g1, 0.0)).T
        pos0, pos1 = _place_rows(e0, e1, triu_ref[...], trile_ref[...], fill_ref, base_ref, np_ref, pexp_ref)
        row8 = lax.broadcasted_iota(jnp.int32, (8, TL), 0)
        posv[...] = jnp.where(row8 == 0, pos0, jnp.where(row8 == 1, pos1, 0))
        pltpu.make_async_copy(posv, poss.at[i % 2], msem).start()

    def project(x):
        h = _rms(x, g1_ref[...]).astype(jnp.bfloat16)
        return jnp.dot(h, win_ref[...], preferred_element_type=jnp.float32)

    def masked_wm():
        return jnp.where(col_m <= row_m, wm_ref[...], jnp.zeros_like(wm_ref[...]))

    @pl.when(i == 0)
    def _init():
        zh_ref[...] = jnp.zeros_like(zh_ref)
        ph_ref[...] = jnp.zeros_like(ph_ref)
        fill_ref[...] = jnp.full_like(fill_ref, TM)
        base_ref[...] = jnp.zeros_like(base_ref)
        np_ref[...] = jnp.zeros_like(np_ref)
        pexp_ref[...] = jnp.zeros_like(pexp_ref)
        h2buf[1] = jnp.zeros((TL, D_MODEL), jnp.float32)
        row8 = lax.broadcasted_iota(jnp.int32, (8, TL), 0)
        lane = lax.broadcasted_iota(jnp.int32, (8, TL), 1)
        posv[...] = N_PAGES * TM + jnp.minimum(row8, 1) * TL + lane
        pltpu.make_async_copy(posv, poss.at[1], msem).start()

    def _close_pages():
        copies = [pltpu.make_async_copy(fill_ref, fill_s, msem), pltpu.make_async_copy(base_ref, base_s, msem),
                  pltpu.make_async_copy(np_ref, np_s, msem), pltpu.make_async_copy(pexp_ref, pexp_s, msem)]
        for c in copies:
            c.start()
        for c in copies:
            c.wait()
        npages = np_s[0, 0]

        def zero_cnt(e, c):
            cnt_s[e] = 0
            return c
        lax.fori_loop(0, N_EXPERTS, zero_cnt, 0)

        def count(p, c):
            e = pexp_s[0, p]
            cnt_s[e] = cnt_s[e] + 1
            return c
        lax.fori_loop(0, npages, count, 0)

        def prefix(e, run):
            n = cnt_s[e]
            cnt_s[e] = run
            return run + n
        lax.fori_loop(0, N_EXPERTS, prefix, 0)

        def emit(p, c):
            e = pexp_s[0, p]
            q = cnt_s[e]
            cnt_s[e] = q + 1
            meta_ref[0, q] = p
            meta_ref[1, q] = e
            ppos_s[p] = q
            return c
        lax.fori_loop(0, npages, emit, 0)

        def pad_meta(q, c):
            meta_ref[0, q] = meta_ref[0, npages - 1]
            meta_ref[1, q] = meta_ref[1, npages - 1]
            return c
        lax.fori_loop(npages, PAGE_LANES, pad_meta, 0)
        meta_ref[0, PAGE_LANES - 1] = N_PAGES

        def warm_up(r, c):
            dest_ref[N_PAGES * TM + r] = N_ASSIGN + TM + r
            return c
        lax.fori_loop(0, TM, warm_up, 0)

        def fill_row2(q, c):
            meta_ref[2, q] = npages
            return c
        lax.fori_loop(0, PAGE_LANES, fill_row2, 0)

        h2buf[0] = jnp.zeros((TL, D_MODEL), jnp.float32)

        def tail_single(e, start):
            f = fill_s[e, 0]
            b = base_s[e, 0]
            f8 = jnp.minimum(((f + 7) // 8) * 8, TM)
            spare0 = N_ASSIGN + (ppos_s[b // TM] % 2) * TM

            def one(r, c):
                cp = row_copy(0, 0, b + r)
                cp.start() if start else cp.wait()
                return c
            lax.fori_loop(f, f8, one, 0)

            for nrows in (128, 64, 32, 16, 8):
                off = ((TM - f8) // (2 * nrows)) * (2 * nrows)

                @pl.when(((TM - f8) // nrows) % 2 == 1)
                def _blk():
                    row0 = pl.multiple_of(b + f8 + off, 8)
                    cp = pltpu.make_async_copy(h2buf.at[0, pl.ds(0, nrows), :], hs_ref.at[pl.ds(row0, nrows), :], rsem)
                    cp.start() if start else cp.wait()

            if start:
                def spare(r, c):
                    dest_ref[b + r] = spare0 + r
                    return c
                lax.fori_loop(f, TM, spare, 0)

        def tails(start):
            def body(e, c):
                tail_single(e, start)
                return c
            lax.fori_loop(0, N_EXPERTS, body, 0)

            def unused(p, c):
                row0 = pl.multiple_of(p * TM, TM)
                cp = pltpu.make_async_copy(h2buf.at[0, pl.ds(0, TM), :], hs_ref.at[pl.ds(row0, TM), :], rsem)
                cp.start() if start else cp.wait()
                if start:
                    def spare(r, c2):
                        dest_ref[p * TM + r] = N_ASSIGN + r
                        return c2
                    lax.fori_loop(0, TM, spare, 0)
                return c
            lax.fori_loop(npages, N_PAGES, unused, 0)

        tails(True)
        tails(False)

    prev_slot = (i + 1) % 2

    @pl.when(i < N_PROMPT_TILES)
    def _prompt_project():
        wait_positions(prev_slot)
        x = load_x(True)
        xmid_ref[...] = x
        proj_ref[...] = project(x)
        scatter_rows(i - 1, prev_slot, range(0, ROWS_STAGE1))

    @pl.when(i <= N_PROMPT_TILES - 1)
    def _prompt_mix():
        s = i % TILES_PER_SEQ
        zhist = jnp.where(s == 0, 0.0, zh_ref[...])
        phist = jnp.where(s == 0, 0.0, ph_ref[...])
        mix, znew, pnew = _mix_rows(proj_ref[...], zhist, phist, s * TL, TL, convw_ref[...], poolbd_ref[...],
                                    pscale_ref[...], masked_wm(), bias_ref[...])
        zh_ref[...] = znew
        ph_ref[...] = pnew
        cpr_ref[0] = znew
        ppr_ref[0] = pnew
        mix_ref[...] = mix.astype(jnp.bfloat16)
        scatter_rows(i - 1, prev_slot, range(ROWS_STAGE1, TL))

    @pl.when(i < N_PROMPT_TILES)
    def _prompt_finish():
        finish(xmid_ref[...], mix_ref[...])
        wait_rows()

    @pl.when(i == N_PROMPT_TILES)
    def _sample():
        wait_positions(prev_slot)
        x = load_x(False)
        proj = project(x)
        scatter_rows(i - 1, prev_slot, range(TL))
        wm = masked_wm()
        mixes = []
        for b in range(DEC_BATCH):
            rows = slice(b * DEC_SEQ, (b + 1) * DEC_SEQ)
            mix, znew, pnew = _mix_rows(proj[rows, :], sconv_ref[b], spool_ref[b], PAST_LEN, DEC_SEQ,
                                        convw_ref[...], poolbd_ref[...], pscale_ref[...], wm, bias_ref[...])
            csm_ref[b] = znew
            psm_ref[b] = pnew
            mixes.append(mix)
        sv_ref[...] = proj[:, 1792:2176]
        finish(x, jnp.concatenate(mixes, axis=0))
        wait_rows()
        wait_positions(i % 2)
        scatter_rows(i, i % 2, range(TL))
        wait_rows()
        _close_pages()


def _mixer_call(first_layer, xs, sconv_pad, spool_pad, g1, w_in, conv_w, pool_bd, pool_scale, wm_all,
                bias_full, w_out, g2, wr_t, br_col, triu, tril_e):
    tile = lambda i: (i, 0)
    prompt_tile = lambda i: (jnp.minimum(i, N_PROMPT_TILES - 1), 0)
    const2 = lambda i: (0, 0)
    const3 = lambda i: (0, 0, 0)
    if first_layer:
        x_specs = [pl.BlockSpec((TL, D_MODEL), prompt_tile), pl.BlockSpec((TL, D_MODEL), const2)]
    else:
        x_specs = [pl.BlockSpec((TL, D_MODEL), tile), pl.BlockSpec((TL, D_MODEL), tile),
                   pl.BlockSpec((TL, D_MODEL), lambda i: (i + N_TILES, 0)), pl.BlockSpec((TL, 128), tile)]
    full = lambda a: pl.BlockSpec(a.shape, const2 if a.ndim == 2 else const3)
    weights = [sconv_pad, spool_pad, g1, w_in, conv_w, pool_bd, pool_scale, wm_all, bias_full, w_out, g2,
               wr_t, br_col, triu, tril_e]
    in_specs = x_specs + [full(a) for a in weights]
    seq_of = lambda i: (jnp.minimum(i // TILES_PER_SEQ, BATCH - 1), 0, 0)
    out_shape = [
        jax.ShapeDtypeStruct((T_ALL, D_MODEL), jnp.float32),
        jax.ShapeDtypeStruct((HS_ROWS, D_MODEL), jnp.float32),
        jax.ShapeDtypeStruct((HS_ROWS,), jnp.int32),
        jax.ShapeDtypeStruct((3, PAGE_LANES), jnp.int32),
        jax.ShapeDtypeStruct((T_ALL, 128), jnp.float32),
        jax.ShapeDtypeStruct((BATCH, HIST_ROWS, D_CONV), jnp.float32),
        jax.ShapeDtypeStruct((BATCH, HIST_ROWS, D_POOL), jnp.float32),
        jax.ShapeDtypeStruct((DEC_BATCH, HIST_ROWS, D_CONV), jnp.float32),
        jax.ShapeDtypeStruct((DEC_BATCH, HIST_ROWS, D_POOL), jnp.float32),
        jax.ShapeDtypeStruct((T_SAMPLE, D_SGU), jnp.float32),
    ]
    out_specs = [
        pl.BlockSpec((TL, D_MODEL), tile),
        pl.BlockSpec(memory_space=pl.ANY),
        pl.BlockSpec(memory_space=pltpu.SMEM),
        pl.BlockSpec(memory_space=pltpu.SMEM),
        pl.BlockSpec((TL, 128), tile),
        pl.BlockSpec((1, HIST_ROWS, D_CONV), seq_of),
        pl.BlockSpec((1, HIST_ROWS, D_POOL), seq_of),
        pl.BlockSpec((DEC_BATCH, HIST_ROWS, D_CONV), const3),
        pl.BlockSpec((DEC_BATCH, HIST_ROWS, D_POOL), const3),
        pl.BlockSpec((T_SAMPLE, D_SGU), const2),
    ]
    scratch = [
        pltpu.VMEM((HIST_ROWS, D_CONV), jnp.float32),
        pltpu.VMEM((HIST_ROWS, D_POOL), jnp.float32),
        pltpu.VMEM((2, TL, D_MODEL), jnp.float32),
        pltpu.VMEM((TL, D_PROJ), jnp.float32),
        pltpu.VMEM((TL, D_MODEL), jnp.bfloat16),
        pltpu.VMEM((8, TL), jnp.int32),
        pltpu.SMEM((2, 8, TL), jnp.int32),
        pltpu.VMEM((N_EXPERTS, 128), jnp.int32),
        pltpu.VMEM((N_EXPERTS, 128), jnp.int32),
        pltpu.VMEM((8, 128), jnp.int32),
        pltpu.VMEM((8, PAGE_LANES), jnp.int32),
        pltpu.SMEM((N_EXPERTS, 128), jnp.int32),
        pltpu.SMEM((N_EXPERTS, 128), jnp.int32),
        pltpu.SMEM((8, 128), jnp.int32),
        pltpu.SMEM((8, PAGE_LANES), jnp.int32),
        pltpu.SMEM((N_EXPERTS,), jnp.int32),
        pltpu.SMEM((PAGE_LANES,), jnp.int32),
        pltpu.SemaphoreType.DMA(()),
        pltpu.SemaphoreType.DMA(()),
    ]
    return pl.pallas_call(
        functools.partial(_mixer_kernel, first_layer),
        grid=(N_TILES,),
        in_specs=in_specs,
        out_specs=out_specs,
        out_shape=out_shape,
        scratch_shapes=scratch,
        compiler_params=pltpu.CompilerParams(dimension_semantics=("arbitrary",),
                                             vmem_limit_bytes=VMEM_LIMIT),
        name="mixer_first" if first_layer else "mixer_next",
    )(*xs, *weights)


def _expert_kernel(meta_ref, dest_ref, hs_ref, wg_ref, wu_ref, wd_ref, y_ref, obuf, hbuf, sem):
    s = pl.program_id(0)
    npages = meta_ref[2, 0]

    @pl.when(s == 0)
    def _spare_rows():
        obuf[1] = jnp.zeros((TM, D_MODEL), jnp.float32)
        cp = pltpu.make_async_copy(obuf.at[1], y_ref.at[pl.ds(N_ASSIGN, TM), :], sem)
        cp.start()
        cp.wait()

    prev_row0 = meta_ref[0, (s + PAGE_LANES - 1) % PAGE_LANES] * TM

    def scatter_previous(slot, rows):
        for r in rows:
            d = dest_ref[prev_row0 + r]
            pltpu.make_async_copy(obuf.at[slot, pl.ds(r, 1), :], y_ref.at[pl.ds(d, 1), :],
                                  sem).start(priority=r % 2)

    def wait_previous():
        pltpu.make_async_copy(obuf.at[0], obuf.at[0], sem).wait()

    for slot in range(2):
        @pl.when((s < npages) & (s % 2 == slot))
        def _gate_up():
            x = hs_ref[...].astype(jnp.bfloat16)
            hg = jnp.dot(x, wg_ref[0, 0].astype(jnp.bfloat16), preferred_element_type=jnp.float32)
            hu = jnp.dot(x, wu_ref[0, 0].astype(jnp.bfloat16), preferred_element_type=jnp.float32)
            hbuf[...] = (hg * jax.nn.sigmoid(hg) * hu).astype(jnp.bfloat16)
            scatter_previous(1 - slot, range(0, EXPERT_ROWS_STAGE1))

        @pl.when((s % 2 == slot) & (s < npages))
        def _down():
            obuf[slot] = jnp.dot(hbuf[...], wd_ref[0, 0].astype(jnp.bfloat16),
                                 preferred_element_type=jnp.float32)
            scatter_previous(1 - slot, range(EXPERT_ROWS_STAGE1, TM))
            wait_previous()

    @pl.when(s == npages)
    def _last():
        scatter_previous((s + 1) % 2, range(TM))
        wait_previous()


def _expert_call(layer, meta, dest, hs, w_gate, w_up, w_down):
    w_idx = lambda s, meta, dest: (layer, meta[1, s], 0, 0)
    grid_spec = pltpu.PrefetchScalarGridSpec(
        num_scalar_prefetch=2,
        grid=(N_PAGES + 1,),
        in_specs=[
            pl.BlockSpec((TM, D_MODEL), lambda s, meta, dest: (meta[0, s], 0)),
            pl.BlockSpec((1, 1, D_MODEL, D_EXPERT), w_idx),
            pl.BlockSpec((1, 1, D_MODEL, D_EXPERT), w_idx),
            pl.BlockSpec((1, 1, D_EXPERT, D_MODEL), w_idx),
        ],
        out_specs=pl.BlockSpec(memory_space=pl.ANY),
        scratch_shapes=[pltpu.VMEM((2, TM, D_MODEL), jnp.float32), pltpu.VMEM((TM, D_EXPERT), jnp.bfloat16),
                        pltpu.SemaphoreType.DMA(())],
    )
    return pl.pallas_call(
        _expert_kernel,
        grid_spec=grid_spec,
        out_shape=jax.ShapeDtypeStruct((Y_ROWS, D_MODEL), jnp.float32),
        compiler_params=pltpu.CompilerParams(dimension_semantics=("arbitrary",),
                                             vmem_limit_bytes=VMEM_LIMIT),
        name="experts",
    )(meta, dest, hs, w_gate, w_up, w_down)


def _final_kernel(xm_ref, y0_ref, y1_ref, gc_ref, g_ref, yp_ref, ys_ref):
    i = pl.program_id(0)
    g = gc_ref[...]
    x = xm_ref[...] + g[:, 0:1] * y0_ref[...] + g[:, 1:2] * y1_ref[...]
    out = _rms(x, g_ref[...])

    @pl.when(i < N_PROMPT_TILES)
    def _():
        yp_ref[...] = out

    @pl.when(i == N_PROMPT_TILES)
    def _():
        ys_ref[...] = out


def _final_call(x_mid, y, gcol, g):
    tile = lambda i: (i, 0)
    return pl.pallas_call(
        _final_kernel,
        grid=(N_TILES,),
        in_specs=[pl.BlockSpec((TL, D_MODEL), tile), pl.BlockSpec((TL, D_MODEL), tile),
                  pl.BlockSpec((TL, D_MODEL), lambda i: (i + N_TILES, 0)),
                  pl.BlockSpec((TL, 128), tile), pl.BlockSpec((1, D_MODEL), lambda i: (0, 0))],
        out_specs=[pl.BlockSpec((TL, D_MODEL), lambda i: (jnp.minimum(i, N_PROMPT_TILES - 1), 0)),
                   pl.BlockSpec((TL, D_MODEL), lambda i: (0, 0))],
        out_shape=[jax.ShapeDtypeStruct((T_PROMPT, D_MODEL), jnp.float32),
                   jax.ShapeDtypeStruct((T_SAMPLE, D_MODEL), jnp.float32)],
        compiler_params=pltpu.CompilerParams(dimension_semantics=("arbitrary",),
                                             vmem_limit_bytes=VMEM_LIMIT),
        name="final_norm",
    )(x_mid, y, y, gcol, g)


def kernel(x_prompt, x_sample, state_conv, state_pool, norm1_g, w_in, conv_w, pool_w, pool_scale, sgu_w, sgu_b, w_out, norm2_g, router_coarse_w, router_coarse_b, router_fine_w, router_fine_b, moe_w_gate, moe_w_up, moe_w_down, final_norm_g):
    bf16 = jnp.bfloat16
    xs = (x_prompt.reshape(T_PROMPT, D_MODEL), x_sample.reshape(T_SAMPLE, D_MODEL))
    sconv_pad = jnp.pad(state_conv, ((0, 0), (0, 0), (HIST_ROWS - (CONV_WIDTH - 1), 0), (0, 0)))
    spool_pad = jnp.pad(state_pool, ((0, 0), (0, 0), (HIST_ROWS - POOL_HIST, 0), (0, 0)))
    idx = jnp.arange(TL, dtype=jnp.int32)
    triu = (idx[:, None] < idx[None, :]).astype(bf16)
    ide = jnp.arange(N_EXPERTS, dtype=jnp.int32)
    tril_e = (ide[None, :] < ide[:, None]).astype(bf16)
    conv_pr, pool_pr, conv_sm, pool_sm, sgu_v = [], [], [], [], []
    x_mid = y = gcol = None
    for l in range(DEPTH):
        pool_bd = jax.scipy.linalg.block_diag(*[pool_w[l, g] for g in range(4)]).astype(bf16)
        wm_all = sgu_w[l].reshape(SGU_HEADS * SGU_LEN, SGU_LEN).astype(bf16)
        bias_full = jnp.repeat(sgu_b[l].T, SGU_HEAD_DIM, axis=1)
        wr_t = jnp.zeros((ROUTER_ROWS, D_MODEL), jnp.float32)
        wr_t = wr_t.at[0:N_GROUPS].set(router_coarse_w[l].T).at[8:].set(router_fine_w[l].T).astype(bf16)
        br_col = jnp.zeros((ROUTER_ROWS, 1), jnp.float32)
        br_col = br_col.at[0:N_GROUPS, 0].set(router_coarse_b[l]).at[8:, 0].set(router_fine_b[l])
        outs = _mixer_call(l == 0, xs, sconv_pad[l], spool_pad[l], norm1_g[l].reshape(1, D_MODEL),
                           w_in[l].astype(bf16), conv_w[l], pool_bd, pool_scale[l].reshape(1, D_POOL),
                           wm_all, bias_full, w_out[l].astype(bf16), norm2_g[l].reshape(1, D_MODEL),
                           wr_t, br_col, triu, tril_e)
        x_mid, hs, dest, meta, gcol, cpr, ppr, csm, psm, sv = outs
        conv_pr.append(cpr[:, HIST_ROWS - (CONV_WIDTH - 1):, :])
        pool_pr.append(ppr[:, HIST_ROWS - POOL_HIST:, :])
        conv_sm.append(csm[:, HIST_ROWS - (CONV_WIDTH - 1):, :])
        pool_sm.append(psm[:, HIST_ROWS - POOL_HIST:, :])
        sgu_v.append(sv.reshape(DEC_BATCH, DEC_SEQ, D_SGU))
        y = _expert_call(l, meta, dest, hs, moe_w_gate, moe_w_up, moe_w_down)
        xs = (x_mid, y, y, gcol)
    y_prompt, y_sample = _final_call(x_mid, y, gcol, final_norm_g.reshape(1, D_MODEL))
    return (y_prompt.reshape(BATCH, SEQ, D_MODEL), y_sample.reshape(DEC_BATCH, DEC_SEQ, D_MODEL),
            jnp.stack(conv_pr), jnp.stack(pool_pr), jnp.stack(conv_sm), jnp.stack(pool_sm),
            jnp.stack(sgu_v))
```

```python
import functools

import jax
import jax.numpy as jnp
from jax import lax
from jax.experimental import pallas as pl
from jax.experimental.pallas import tpu as pltpu

D_MODEL = 1024
BATCH = 8
SEQ = 2048
DEPTH = 2
DEC_BATCH = 8
DEC_SEQ = 64
PAST_LEN = 1024
D_CONV = 384
CONV_WIDTH = 3
D_POOL = 256
POOL_HIST = 15
D_SGU = 384
SGU_HEADS = 4
SGU_HEAD_DIM = 96
SGU_LEN = 128
D_PROJ = 2176
N_GROUPS = 4
EXPERTS_PER_GROUP = 8
N_EXPERTS = 32
D_EXPERT = 512
EPS = 1e-6

T_PROMPT = BATCH * SEQ
T_SAMPLE = DEC_BATCH * DEC_SEQ
T_ALL = T_PROMPT + T_SAMPLE
TL = 512
TILES_PER_SEQ = SEQ // TL
N_PROMPT_TILES = T_PROMPT // TL
N_TILES = N_PROMPT_TILES + T_SAMPLE // TL
HIST_ROWS = 16
ROUTER_ROWS = 8 + N_EXPERTS
TM = 256
N_ASSIGN = 2 * T_ALL
N_PAGES = N_ASSIGN // TM + N_EXPERTS
PAGE_LANES = 256
Y_ROWS = N_ASSIGN + 2 * TM
HS_ROWS = N_PAGES * TM + 2 * TL
EXPERT_ROWS_STAGE1 = 208
ROWS_STAGE1 = 320
VMEM_LIMIT = 56 * 1024 * 1024

assert TL == 2 * TM and N_PAGES <= PAGE_LANES and T_ALL == N_TILES * TL


def _rms(x, g):
    return x * lax.rsqrt(jnp.mean(x * x, axis=-1, keepdims=True) + EPS) * g


def _mix_rows(proj, zhist, phist, pos0, n, conv_w, pool_bd, pool_scale, wm_all, bias_full):
    a_b = proj[:, 0:384]
    a_c = proj[:, 384:768]
    a_h = proj[:, 768:1152]
    p_in = proj[:, 1152:1408]
    s_u = proj[:, 1408:1792]
    s_v = proj[:, 1792:2176]

    z = a_c * a_h
    zext = jnp.concatenate([zhist, z], axis=0)
    conv_y = (conv_w[0:1, :] * pltpu.roll(zext, 2, 0)[HIST_ROWS:, :]
              + conv_w[1:2, :] * pltpu.roll(zext, 1, 0)[HIST_ROWS:, :]
              + conv_w[2:3, :] * z)
    a_out = a_b * conv_y

    pext = jnp.concatenate([phist, p_in], axis=0)
    s2 = pext + pltpu.roll(pext, 1, 0)
    s4 = s2 + pltpu.roll(s2, 2, 0)
    s8 = s4 + pltpu.roll(s4, 4, 0)
    s16 = s8 + pltpu.roll(s8, 8, 0)
    lane = lax.broadcasted_iota(jnp.int32, (1, D_POOL), 1)
    wsum = jnp.where(lane < 64, s2, jnp.where(lane < 128, s4, jnp.where(lane < 192, s8, s16)))
    wsum = wsum[HIST_ROWS:, :]
    win = jnp.where(lane < 64, 2.0, jnp.where(lane < 128, 4.0, jnp.where(lane < 192, 8.0, 16.0)))
    pos = (pos0 + lax.broadcasted_iota(jnp.int32, (n, 1), 0) + 1).astype(jnp.float32)
    cnt = jnp.minimum(win, pos)
    pooled = wsum / cnt - p_in
    p_out = jnp.dot(pooled.astype(jnp.bfloat16), pool_bd,
                    preferred_element_type=jnp.float32) * pool_scale

    lane_s = lax.broadcasted_iota(jnp.int32, (1, D_SGU), 1)
    chunk = min(n, SGU_LEN)
    if chunk == SGU_LEN:
        wm = wm_all
    else:
        wm = jnp.concatenate([wm_all[h * SGU_LEN:h * SGU_LEN + chunk, 0:chunk]
                              for h in range(SGU_HEADS)], axis=0)
    s_rows = []
    for c in range(n // chunk):
        v_c = s_v[c * chunk:(c + 1) * chunk, :].astype(jnp.bfloat16)
        r = jnp.dot(wm, v_c, preferred_element_type=jnp.float32)
        s_c = jnp.where(lane_s < 96, r[0:chunk],
                        jnp.where(lane_s < 192, r[chunk:2 * chunk],
                                  jnp.where(lane_s < 288, r[2 * chunk:3 * chunk], r[3 * chunk:4 * chunk])))
        s_rows.append(s_c + bias_full[0:chunk, :])
    s_gate = s_rows[0] if len(s_rows) == 1 else jnp.concatenate(s_rows, axis=0)
    s_out = s_u * s_gate

    mix = jnp.concatenate([a_out, p_out, s_out], axis=-1)
    return mix, zext[n:n + HIST_ROWS, :], pext[n:n + HIST_ROWS, :]


def _route(h2_bf16, wr_t, br_col, n):
    logits = lax.dot_general(wr_t, h2_bf16, (((1,), (1,)), ((), ())),
                             preferred_element_type=jnp.float32) + br_col
    row8 = lax.broadcasted_iota(jnp.int32, (8, n), 0)
    lc = jnp.where(row8 < N_GROUPS, logits[0:8, :], -jnp.inf)
    mc = jnp.max(lc, axis=0, keepdims=True)
    g_sel = jnp.min(jnp.where(lc == mc, row8, 8), axis=0, keepdims=True)
    p_sel = 1.0 / jnp.sum(jnp.exp(lc - mc), axis=0, keepdims=True)
    lf = logits[8 + 3 * EXPERTS_PER_GROUP:8 + 4 * EXPERTS_PER_GROUP, :]
    for g in (2, 1, 0):
        lf = jnp.where(g_sel == g, logits[8 + g * EXPERTS_PER_GROUP:8 + (g + 1) * EXPERTS_PER_GROUP, :], lf)
    m1 = jnp.max(lf, axis=0, keepdims=True)
    i1 = jnp.min(jnp.where(lf == m1, row8, 8), axis=0, keepdims=True)
    lf2 = jnp.where(row8 == i1, -jnp.inf, lf)
    m2 = jnp.max(lf2, axis=0, keepdims=True)
    i2 = jnp.min(jnp.where(lf2 == m2, row8, 8), axis=0, keepdims=True)
    t = jnp.exp(m2 - m1)
    wa = 1.0 / (1.0 + t)
    wb = t / (1.0 + t)
    e0 = g_sel * EXPERTS_PER_GROUP + i1
    e1 = g_sel * EXPERTS_PER_GROUP + i2
    return e0, e1, p_sel * wa, p_sel * wb


def _place_rows(e0, e1, triu, tril_e, fill_ref, base_ref, np_ref, pexp_ref):
    row_e = lax.broadcasted_iota(jnp.int32, (N_EXPERTS, TL), 0)
    oh0 = row_e == e0
    oh1 = row_e == e1
    oh = jnp.where(oh0 | oh1, 1.0, 0.0)
    rank = jnp.dot(oh.astype(jnp.bfloat16), triu, preferred_element_type=jnp.float32).astype(jnp.int32)
    cnt = jnp.sum(oh, axis=1, keepdims=True).astype(jnp.int32)
    fill = fill_ref[:, 0:1]
    base = base_ref[:, 0:1]
    npages = np_ref[0:1, 0:1]
    total = fill + cnt
    need = (total > TM).astype(jnp.int32) + (total > 2 * TM).astype(jnp.int32)
    need_b = jnp.broadcast_to(need.astype(jnp.float32), (N_EXPERTS, 128)).astype(jnp.bfloat16)
    before = jnp.dot(tril_e, need_b, preferred_element_type=jnp.float32)[:, 0:1].astype(jnp.int32)
    new_id = npages + before
    new_base = new_id * TM
    r = fill + rank
    pos_all = jnp.where(r < TM, base + r, new_base + (r - TM))
    pos0 = jnp.sum(jnp.where(oh0, pos_all, 0), axis=0, keepdims=True)
    pos1 = jnp.sum(jnp.where(oh1, pos_all, 0), axis=0, keepdims=True)
    fill_ref[...] = jnp.broadcast_to(total - need * TM, (N_EXPERTS, 128))
    base_ref[...] = jnp.broadcast_to(jnp.where(need > 0, new_base + (need - 1) * TM, base), (N_EXPERTS, 128))
    np_ref[...] = jnp.broadcast_to(npages + jnp.sum(need, axis=0, keepdims=True), (8, 128))
    page_lane = lax.broadcasted_iota(jnp.int32, (N_EXPERTS, PAGE_LANES), 1)
    expert_col = lax.broadcasted_iota(jnp.int32, (N_EXPERTS, 1), 0)
    owns = ((page_lane == new_id) & (need >= 1)) | ((page_lane == new_id + 1) & (need == 2))
    pexp_ref[...] = pexp_ref[...] + jnp.sum(jnp.where(owns, expert_col, 0), axis=0, keepdims=True)
    return pos0, pos1


def _mixer_kernel(first_layer, *refs):
    if first_layer:
        xp_ref, xs_ref = refs[0:2]
        rest = refs[2:]
    else:
        xm_ref, y0_ref, y1_ref, gc_ref = refs[0:4]
        rest = refs[4:]
    (sconv_ref, spool_ref, g1_ref, win_ref, convw_ref, poolbd_ref, pscale_ref, wm_ref, bias_ref,
     wout_ref, g2_ref, wr_ref, br_ref, triu_ref, trile_ref,
     xmid_ref, hs_ref, dest_ref, meta_ref, gcol_ref, cpr_ref, ppr_ref, csm_ref, psm_ref, sv_ref,
     zh_ref, ph_ref, h2buf, proj_ref, mix_ref, posv, poss, fill_ref, base_ref, np_ref, pexp_ref,
     fill_s, base_s, np_s, pexp_s, cnt_s, ppos_s, rsem, msem) = rest

    i = pl.program_id(0)

    def load_x(prompt):
        if first_layer:
            return xp_ref[...] if prompt else xs_ref[...]
        g = gc_ref[...]
        y0 = y0_ref[...].reshape(TL, D_MODEL)
        y1 = y1_ref[...].reshape(TL, D_MODEL)
        return xm_ref[...] + g[:, 0:1] * y0 + g[:, 1:2] * y1

    row_m = lax.broadcasted_iota(jnp.int32, (SGU_HEADS * SGU_LEN, SGU_LEN), 0) % SGU_LEN
    col_m = lax.broadcasted_iota(jnp.int32, (SGU_HEADS * SGU_LEN, SGU_LEN), 1)

    def row_copy(slot, r, pos):
        return pltpu.make_async_copy(h2buf.at[slot, r], hs_ref.at[pos], rsem)

    def wait_positions(slot):
        pltpu.make_async_copy(posv, poss.at[slot], msem).wait()

    def scatter_rows(tile, slot, rows):
        tok0 = tile * TL
        for r in rows:
            for k in range(2):
                p = poss[slot, k, r]
                row_copy(slot, r, p).start(priority=k)
                dest_ref[p] = k * T_ALL + tok0 + r

    def wait_rows():
        for k in range(2):
            pltpu.make_async_copy(h2buf.at[0], h2buf.at[0], rsem).wait()

    def finish(x, mix):
        x_mid = x + jnp.dot(mix.astype(jnp.bfloat16), wout_ref[...], preferred_element_type=jnp.float32)
        xmid_ref[...] = x_mid
        h2 = _rms(x_mid, g2_ref[...])
        h2buf[i % 2] = h2.reshape(TL, 1, D_MODEL)
        e0, e1, g0, g1 = _route(h2.astype(jnp.bfloat16), wr_ref[...], br_ref[...], TL)
        row128 = lax.broadcasted_iota(jnp.int32, (128, TL), 0)
        gcol_ref[...] = jnp.where(row128 == 0, g0, jnp.where(row128 == 1, g1, 0.0)).T
        pos0, pos1 = _place_rows(e0, e1, triu_ref[...], trile_ref[...], fill_ref, base_ref, np_ref, pexp_ref)
        row8 = lax.broadcasted_iota(jnp.int32, (8, TL), 0)
        posv[...] = jnp.where(row8 == 0, pos0, jnp.where(row8 == 1, pos1, 0))
        pltpu.make_async_copy(posv, poss.at[i % 2], msem).start()

    def project(x):
        h = _rms(x, g1_ref[...]).astype(jnp.bfloat16)
        return jnp.dot(h, win_ref[...], preferred_element_type=jnp.float32)

    def masked_wm():
        return jnp.where(col_m <= row_m, wm_ref[...], jnp.zeros_like(wm_ref[...]))

    @pl.when(i == 0)
    def _init():
        zh_ref[...] = jnp.zeros_like(zh_ref)
        ph_ref[...] = jnp.zeros_like(ph_ref)
        fill_ref[...] = jnp.full_like(fill_ref, TM)
        base_ref[...] = jnp.zeros_like(base_ref)
        np_ref[...] = jnp.zeros_like(np_ref)
        pexp_ref[...] = jnp.zeros_like(pexp_ref)
        h2buf[1] = jnp.zeros((TL, 1, D_MODEL), jnp.float32)
        row8 = lax.broadcasted_iota(jnp.int32, (8, TL), 0)
        lane = lax.broadcasted_iota(jnp.int32, (8, TL), 1)
        posv[...] = N_PAGES * TM + jnp.minimum(row8, 1) * TL + lane
        pltpu.make_async_copy(posv, poss.at[1], msem).start()

    def _close_pages():
        copies = [pltpu.make_async_copy(fill_ref, fill_s, msem), pltpu.make_async_copy(base_ref, base_s, msem),
                  pltpu.make_async_copy(np_ref, np_s, msem), pltpu.make_async_copy(pexp_ref, pexp_s, msem)]
        for c in copies:
            c.start()
        for c in copies:
            c.wait()
        npages = np_s[0, 0]

        def zero_cnt(e, c):
            cnt_s[e] = 0
            return c
        lax.fori_loop(0, N_EXPERTS, zero_cnt, 0)

        def count(p, c):
            e = pexp_s[0, p]
            cnt_s[e] = cnt_s[e] + 1
            return c
        lax.fori_loop(0, npages, count, 0)

        def prefix(e, run):
            n = cnt_s[e]
            cnt_s[e] = run
            return run + n
        lax.fori_loop(0, N_EXPERTS, prefix, 0)

        def emit(p, c):
            e = pexp_s[0, p]
            q = cnt_s[e]
            cnt_s[e] = q + 1
            meta_ref[0, q] = p
            meta_ref[1, q] = e
            ppos_s[p] = q
            return c
        lax.fori_loop(0, npages, emit, 0)

        def pad_meta(q, c):
            meta_ref[0, q] = meta_ref[0, npages - 1]
            meta_ref[1, q] = meta_ref[1, npages - 1]
            return c
        lax.fori_loop(npages, PAGE_LANES, pad_meta, 0)
        meta_ref[0, PAGE_LANES - 1] = N_PAGES

        def warm_up(r, c):
            dest_ref[N_PAGES * TM + r] = N_ASSIGN + TM + r
            return c
        lax.fori_loop(0, TM, warm_up, 0)

        def fill_row2(q, c):
            meta_ref[2, q] = npages
            return c
        lax.fori_loop(0, PAGE_LANES, fill_row2, 0)

        h2buf[0] = jnp.zeros((TL, 1, D_MODEL), jnp.float32)

        def tail_single(e, start):
            f = fill_s[e, 0]
            b = base_s[e, 0]
            spare0 = N_ASSIGN + (ppos_s[b // TM] % 2) * TM

            @pl.when(f < TM)
            def _zero_tail():
                cp = pltpu.make_async_copy(h2buf.at[0, pl.ds(0, TM - f)], hs_ref.at[pl.ds(b + f, TM - f)], rsem)
                cp.start() if start else cp.wait()

            if start:
                def spare(r, c):
                    dest_ref[b + r] = spare0 + r
                    return c
                lax.fori_loop(f, TM, spare, 0)

        def tails(start):
            def body(e, c):
                tail_single(e, start)
                return c
            lax.fori_loop(0, N_EXPERTS, body, 0)

            def unused(p, c):
                cp = pltpu.make_async_copy(h2buf.at[0, pl.ds(0, TM)], hs_ref.at[pl.ds(p * TM, TM)], rsem)
                cp.start() if start else cp.wait()
                if start:
                    def spare(r, c2):
                        dest_ref[p * TM + r] = N_ASSIGN + r
                        return c2
                    lax.fori_loop(0, TM, spare, 0)
                return c
            lax.fori_loop(npages, N_PAGES, unused, 0)

        tails(True)
        tails(False)

    prev_slot = (i + 1) % 2

    @pl.when(i < N_PROMPT_TILES)
    def _prompt_project():
        wait_positions(prev_slot)
        x = load_x(True)
        xmid_ref[...] = x
        proj_ref[...] = project(x)
        scatter_rows(i - 1, prev_slot, range(0, ROWS_STAGE1))

    @pl.when(i <= N_PROMPT_TILES - 1)
    def _prompt_mix():
        s = i % TILES_PER_SEQ
        zhist = jnp.where(s == 0, 0.0, zh_ref[...])
        phist = jnp.where(s == 0, 0.0, ph_ref[...])
        mix, znew, pnew = _mix_rows(proj_ref[...], zhist, phist, s * TL, TL, convw_ref[...], poolbd_ref[...],
                                    pscale_ref[...], masked_wm(), bias_ref[...])
        zh_ref[...] = znew
        ph_ref[...] = pnew
        cpr_ref[0] = znew
        ppr_ref[0] = pnew
        mix_ref[...] = mix.astype(jnp.bfloat16)
        scatter_rows(i - 1, prev_slot, range(ROWS_STAGE1, TL))

    @pl.when(i < N_PROMPT_TILES)
    def _prompt_finish():
        finish(xmid_ref[...], mix_ref[...])
        wait_rows()

    @pl.when(i == N_PROMPT_TILES)
    def _sample():
        wait_positions(prev_slot)
        x = load_x(False)
        proj = project(x)
        scatter_rows(i - 1, prev_slot, range(TL))
        wm = masked_wm()
        mixes = []
        for b in range(DEC_BATCH):
            rows = slice(b * DEC_SEQ, (b + 1) * DEC_SEQ)
            mix, znew, pnew = _mix_rows(proj[rows, :], sconv_ref[b], spool_ref[b], PAST_LEN, DEC_SEQ,
                                        convw_ref[...], poolbd_ref[...], pscale_ref[...], wm, bias_ref[...])
            csm_ref[b] = znew
            psm_ref[b] = pnew
            mixes.append(mix)
        sv_ref[...] = proj[:, 1792:2176]
        finish(x, jnp.concatenate(mixes, axis=0))
        wait_rows()
        wait_positions(i % 2)
        scatter_rows(i, i % 2, range(TL))
        wait_rows()
        _close_pages()


def _mixer_call(first_layer, xs, sconv_pad, spool_pad, g1, w_in, conv_w, pool_bd, pool_scale, wm_all,
                bias_full, w_out, g2, wr_t, br_col, triu, tril_e):
    tile = lambda i: (i, 0)
    prompt_tile = lambda i: (jnp.minimum(i, N_PROMPT_TILES - 1), 0)
    const2 = lambda i: (0, 0)
    const3 = lambda i: (0, 0, 0)
    if first_layer:
        x_specs = [pl.BlockSpec((TL, D_MODEL), prompt_tile), pl.BlockSpec((TL, D_MODEL), const2)]
    else:
        x_specs = [pl.BlockSpec((TL, D_MODEL), tile), pl.BlockSpec((TL, 1, D_MODEL), lambda i: (i, 0, 0)),
                   pl.BlockSpec((TL, 1, D_MODEL), lambda i: (i + N_TILES, 0, 0)), pl.BlockSpec((TL, 128), tile)]
    full = lambda a: pl.BlockSpec(a.shape, const2 if a.ndim == 2 else const3)
    weights = [sconv_pad, spool_pad, g1, w_in, conv_w, pool_bd, pool_scale, wm_all, bias_full, w_out, g2,
               wr_t, br_col, triu, tril_e]
    in_specs = x_specs + [full(a) for a in weights]
    seq_of = lambda i: (jnp.minimum(i // TILES_PER_SEQ, BATCH - 1), 0, 0)
    out_shape = [
        jax.ShapeDtypeStruct((T_ALL, D_MODEL), jnp.float32),
        jax.ShapeDtypeStruct((HS_ROWS, 1, D_MODEL), jnp.float32),
        jax.ShapeDtypeStruct((HS_ROWS,), jnp.int32),
        jax.ShapeDtypeStruct((3, PAGE_LANES), jnp.int32),
        jax.ShapeDtypeStruct((T_ALL, 128), jnp.float32),
        jax.ShapeDtypeStruct((BATCH, HIST_ROWS, D_CONV), jnp.float32),
        jax.ShapeDtypeStruct((BATCH, HIST_ROWS, D_POOL), jnp.float32),
        jax.ShapeDtypeStruct((DEC_BATCH, HIST_ROWS, D_CONV), jnp.float32),
        jax.ShapeDtypeStruct((DEC_BATCH, HIST_ROWS, D_POOL), jnp.float32),
        jax.ShapeDtypeStruct((T_SAMPLE, D_SGU), jnp.float32),
    ]
    out_specs = [
        pl.BlockSpec((TL, D_MODEL), tile),
        pl.BlockSpec(memory_space=pl.ANY),
        pl.BlockSpec(memory_space=pltpu.SMEM),
        pl.BlockSpec(memory_space=pltpu.SMEM),
        pl.BlockSpec((TL, 128), tile),
        pl.BlockSpec((1, HIST_ROWS, D_CONV), seq_of),
        pl.BlockSpec((1, HIST_ROWS, D_POOL), seq_of),
        pl.BlockSpec((DEC_BATCH, HIST_ROWS, D_CONV), const3),
        pl.BlockSpec((DEC_BATCH, HIST_ROWS, D_POOL), const3),
        pl.BlockSpec((T_SAMPLE, D_SGU), const2),
    ]
    scratch = [
        pltpu.VMEM((HIST_ROWS, D_CONV), jnp.float32),
        pltpu.VMEM((HIST_ROWS, D_POOL), jnp.float32),
        pltpu.VMEM((2, TL, 1, D_MODEL), jnp.float32),
        pltpu.VMEM((TL, D_PROJ), jnp.float32),
        pltpu.VMEM((TL, D_MODEL), jnp.bfloat16),
        pltpu.VMEM((8, TL), jnp.int32),
        pltpu.SMEM((2, 8, TL), jnp.int32),
        pltpu.VMEM((N_EXPERTS, 128), jnp.int32),
        pltpu.VMEM((N_EXPERTS, 128), jnp.int32),
        pltpu.VMEM((8, 128), jnp.int32),
        pltpu.VMEM((8, PAGE_LANES), jnp.int32),
        pltpu.SMEM((N_EXPERTS, 128), jnp.int32),
        pltpu.SMEM((N_EXPERTS, 128), jnp.int32),
        pltpu.SMEM((8, 128), jnp.int32),
        pltpu.SMEM((8, PAGE_LANES), jnp.int32),
        pltpu.SMEM((N_EXPERTS,), jnp.int32),
        pltpu.SMEM((PAGE_LANES,), jnp.int32),
        pltpu.SemaphoreType.DMA(()),
        pltpu.SemaphoreType.DMA(()),
    ]
    return pl.pallas_call(
        functools.partial(_mixer_kernel, first_layer),
        grid=(N_TILES,),
        in_specs=in_specs,
        out_specs=out_specs,
        out_shape=out_shape,
        scratch_shapes=scratch,
        compiler_params=pltpu.CompilerParams(dimension_semantics=("arbitrary",),
                                             vmem_limit_bytes=VMEM_LIMIT),
        name="mixer_first" if first_layer else "mixer_next",
    )(*xs, *weights)


def _expert_kernel(meta_ref, dest_ref, hs_ref, wg_ref, wu_ref, wd_ref, y_ref, obuf, hbuf, xbuf, sem):
    s = pl.program_id(0)
    npages = meta_ref[2, 0]

    @pl.when(s == 0)
    def _spare_rows():
        obuf[1] = jnp.zeros((TM, 1, D_MODEL), jnp.float32)
        cp = pltpu.make_async_copy(obuf.at[1], y_ref.at[pl.ds(N_ASSIGN, TM)], sem)
        cp.start()
        cp.wait()

    prev_row0 = meta_ref[0, (s + PAGE_LANES - 1) % PAGE_LANES] * TM

    def scatter_previous(slot, rows):
        for r in rows:
            d = dest_ref[prev_row0 + r]
            pltpu.make_async_copy(obuf.at[slot, r], y_ref.at[d], sem).start(priority=r % 2)

    def wait_previous():
        pltpu.make_async_copy(obuf.at[0], obuf.at[0], sem).wait()

    for slot in range(2):
        @pl.when((s < npages) & (s % 2 == slot))
        def _gate_up():
            xbuf[...] = hs_ref[...].reshape(TM, D_MODEL)
            x = xbuf[...].astype(jnp.bfloat16)
            hg = jnp.dot(x, wg_ref[0, 0].astype(jnp.bfloat16), preferred_element_type=jnp.float32)
            hu = jnp.dot(x, wu_ref[0, 0].astype(jnp.bfloat16), preferred_element_type=jnp.float32)
            hbuf[...] = (hg * jax.nn.sigmoid(hg) * hu).astype(jnp.bfloat16)
            scatter_previous(1 - slot, range(0, EXPERT_ROWS_STAGE1))

        @pl.when((s % 2 == slot) & (s < npages))
        def _down():
            obuf[slot] = jnp.dot(hbuf[...], wd_ref[0, 0].astype(jnp.bfloat16),
                                 preferred_element_type=jnp.float32).reshape(TM, 1, D_MODEL)
            scatter_previous(1 - slot, range(EXPERT_ROWS_STAGE1, TM))
            wait_previous()

    @pl.when(s == npages)
    def _last():
        scatter_previous((s + 1) % 2, range(TM))
        wait_previous()


def _expert_call(layer, meta, dest, hs, w_gate, w_up, w_down):
    w_idx = lambda s, meta, dest: (layer, meta[1, s], 0, 0)
    grid_spec = pltpu.PrefetchScalarGridSpec(
        num_scalar_prefetch=2,
        grid=(N_PAGES + 1,),
        in_specs=[
            pl.BlockSpec((TM, 1, D_MODEL), lambda s, meta, dest: (meta[0, s], 0, 0)),
            pl.BlockSpec((1, 1, D_MODEL, D_EXPERT), w_idx),
            pl.BlockSpec((1, 1, D_MODEL, D_EXPERT), w_idx),
            pl.BlockSpec((1, 1, D_EXPERT, D_MODEL), w_idx),
        ],
        out_specs=pl.BlockSpec(memory_space=pl.ANY),
        scratch_shapes=[pltpu.VMEM((2, TM, 1, D_MODEL), jnp.float32), pltpu.VMEM((TM, D_EXPERT), jnp.bfloat16),
                        pltpu.VMEM((TM, D_MODEL), jnp.float32),
                        pltpu.SemaphoreType.DMA(())],
    )
    return pl.pallas_call(
        _expert_kernel,
        grid_spec=grid_spec,
        out_shape=jax.ShapeDtypeStruct((Y_ROWS, 1, D_MODEL), jnp.float32),
        compiler_params=pltpu.CompilerParams(dimension_semantics=("arbitrary",),
                                             vmem_limit_bytes=VMEM_LIMIT),
        name="experts",
    )(meta, dest, hs, w_gate, w_up, w_down)


def _final_kernel(xm_ref, y0_ref, y1_ref, gc_ref, g_ref, yp_ref, ys_ref):
    i = pl.program_id(0)
    g = gc_ref[...]
    y0 = y0_ref[...].reshape(TL, D_MODEL)
    y1 = y1_ref[...].reshape(TL, D_MODEL)
    x = xm_ref[...] + g[:, 0:1] * y0 + g[:, 1:2] * y1
    out = _rms(x, g_ref[...])

    @pl.when(i < N_PROMPT_TILES)
    def _():
        yp_ref[...] = out

    @pl.when(i == N_PROMPT_TILES)
    def _():
        ys_ref[...] = out


def _final_call(x_mid, y, gcol, g):
    tile = lambda i: (i, 0)
    return pl.pallas_call(
        _final_kernel,
        grid=(N_TILES,),
        in_specs=[pl.BlockSpec((TL, D_MODEL), tile), pl.BlockSpec((TL, 1, D_MODEL), lambda i: (i, 0, 0)),
                  pl.BlockSpec((TL, 1, D_MODEL), lambda i: (i + N_TILES, 0, 0)),
                  pl.BlockSpec((TL, 128), tile), pl.BlockSpec((1, D_MODEL), lambda i: (0, 0))],
        out_specs=[pl.BlockSpec((TL, D_MODEL), lambda i: (jnp.minimum(i, N_PROMPT_TILES - 1), 0)),
                   pl.BlockSpec((TL, D_MODEL), lambda i: (0, 0))],
        out_shape=[jax.ShapeDtypeStruct((T_PROMPT, D_MODEL), jnp.float32),
                   jax.ShapeDtypeStruct((T_SAMPLE, D_MODEL), jnp.float32)],
        compiler_params=pltpu.CompilerParams(dimension_semantics=("arbitrary",),
                                             vmem_limit_bytes=VMEM_LIMIT),
        name="final_norm",
    )(x_mid, y, y, gcol, g)


def kernel(x_prompt, x_sample, state_conv, state_pool, norm1_g, w_in, conv_w, pool_w, pool_scale, sgu_w, sgu_b, w_out, norm2_g, router_coarse_w, router_coarse_b, router_fine_w, router_fine_b, moe_w_gate, moe_w_up, moe_w_down, final_norm_g):
    bf16 = jnp.bfloat16
    xs = (x_prompt.reshape(T_PROMPT, D_MODEL), x_sample.reshape(T_SAMPLE, D_MODEL))
    sconv_pad = jnp.pad(state_conv, ((0, 0), (0, 0), (HIST_ROWS - (CONV_WIDTH - 1), 0), (0, 0)))
    spool_pad = jnp.pad(state_pool, ((0, 0), (0, 0), (HIST_ROWS - POOL_HIST, 0), (0, 0)))
    idx = jnp.arange(TL, dtype=jnp.int32)
    triu = (idx[:, None] < idx[None, :]).astype(bf16)
    ide = jnp.arange(N_EXPERTS, dtype=jnp.int32)
    tril_e = (ide[None, :] < ide[:, None]).astype(bf16)
    conv_pr, pool_pr, conv_sm, pool_sm, sgu_v = [], [], [], [], []
    x_mid = y = gcol = None
    for l in range(DEPTH):
        pool_bd = jax.scipy.linalg.block_diag(*[pool_w[l, g] for g in range(4)]).astype(bf16)
        wm_all = sgu_w[l].reshape(SGU_HEADS * SGU_LEN, SGU_LEN).astype(bf16)
        bias_full = jnp.repeat(sgu_b[l].T, SGU_HEAD_DIM, axis=1)
        wr_t = jnp.zeros((ROUTER_ROWS, D_MODEL), jnp.float32)
        wr_t = wr_t.at[0:N_GROUPS].set(router_coarse_w[l].T).at[8:].set(router_fine_w[l].T).astype(bf16)
        br_col = jnp.zeros((ROUTER_ROWS, 1), jnp.float32)
        br_col = br_col.at[0:N_GROUPS, 0].set(router_coarse_b[l]).at[8:, 0].set(router_fine_b[l])
        outs = _mixer_call(l == 0, xs, sconv_pad[l], spool_pad[l], norm1_g[l].reshape(1, D_MODEL),
                           w_in[l].astype(bf16), conv_w[l], pool_bd, pool_scale[l].reshape(1, D_POOL),
                           wm_all, bias_full, w_out[l].astype(bf16), norm2_g[l].reshape(1, D_MODEL),
                           wr_t, br_col, triu, tril_e)
        x_mid, hs, dest, meta, gcol, cpr, ppr, csm, psm, sv = outs
        conv_pr.append(cpr[:, HIST_ROWS - (CONV_WIDTH - 1):, :])
        pool_pr.append(ppr[:, HIST_ROWS - POOL_HIST:, :])
        conv_sm.append(csm[:, HIST_ROWS - (CONV_WIDTH - 1):, :])
        pool_sm.append(psm[:, HIST_ROWS - POOL_HIST:, :])
        sgu_v.append(sv.reshape(DEC_BATCH, DEC_SEQ, D_SGU))
        y = _expert_call(l, meta, dest, hs, moe_w_gate, moe_w_up, moe_w_down)
        xs = (x_mid, y, y, gcol)
    y_prompt, y_sample = _final_call(x_mid, y, gcol, final_norm_g.reshape(1, D_MODEL))
    return (y_prompt.reshape(BATCH, SEQ, D_MODEL), y_sample.reshape(DEC_BATCH, DEC_SEQ, D_MODEL),
            jnp.stack(conv_pr), jnp.stack(pool_pr), jnp.stack(conv_sm), jnp.stack(pool_sm),
            jnp.stack(sgu_v))
```

```python
import functools

import jax
import jax.numpy as jnp
from jax import lax
from jax.experimental import pallas as pl
from jax.experimental.pallas import tpu as pltpu

D_MODEL = 1024
BATCH = 8
SEQ = 2048
DEPTH = 2
DEC_BATCH = 8
DEC_SEQ = 64
PAST_LEN = 1024
D_CONV = 384
CONV_WIDTH = 3
D_POOL = 256
POOL_HIST = 15
D_SGU = 384
SGU_HEADS = 4
SGU_HEAD_DIM = 96
SGU_LEN = 128
D_PROJ = 2176
N_GROUPS = 4
EXPERTS_PER_GROUP = 8
N_EXPERTS = 32
D_EXPERT = 512
EPS = 1e-6

T_PROMPT = BATCH * SEQ
T_SAMPLE = DEC_BATCH * DEC_SEQ
T_ALL = T_PROMPT + T_SAMPLE
TL = 512
TILES_PER_SEQ = SEQ // TL
N_PROMPT_TILES = T_PROMPT // TL
N_TILES = N_PROMPT_TILES + T_SAMPLE // TL
HIST_ROWS = 16
ROUTER_ROWS = 8 + N_EXPERTS
TM = 256
N_ASSIGN = 2 * T_ALL
N_PAGES = N_ASSIGN // TM + N_EXPERTS
PAGE_LANES = 256
PAGE_ROWS = N_PAGES * TM
RUN_FIELDS = 9
RUNS_PER_TILE = N_EXPERTS * RUN_FIELDS
VMEM_LIMIT = 56 * 1024 * 1024

assert TL == 2 * TM and N_PAGES <= PAGE_LANES and T_ALL == N_TILES * TL


def _rms(x, g):
    return x * lax.rsqrt(jnp.mean(x * x, axis=-1, keepdims=True) + EPS) * g


def _mix_rows(proj, zhist, phist, pos0, n, conv_w, pool_bd, pool_scale, wm_all, bias_full):
    a_b = proj[:, 0:384]
    a_c = proj[:, 384:768]
    a_h = proj[:, 768:1152]
    p_in = proj[:, 1152:1408]
    s_u = proj[:, 1408:1792]
    s_v = proj[:, 1792:2176]

    z = a_c * a_h
    zext = jnp.concatenate([zhist, z], axis=0)
    conv_y = (conv_w[0:1, :] * pltpu.roll(zext, 2, 0)[HIST_ROWS:, :]
              + conv_w[1:2, :] * pltpu.roll(zext, 1, 0)[HIST_ROWS:, :]
              + conv_w[2:3, :] * z)
    a_out = a_b * conv_y

    pext = jnp.concatenate([phist, p_in], axis=0)
    s2 = pext + pltpu.roll(pext, 1, 0)
    s4 = s2 + pltpu.roll(s2, 2, 0)
    s8 = s4 + pltpu.roll(s4, 4, 0)
    s16 = s8 + pltpu.roll(s8, 8, 0)
    lane = lax.broadcasted_iota(jnp.int32, (1, D_POOL), 1)
    wsum = jnp.where(lane < 64, s2, jnp.where(lane < 128, s4, jnp.where(lane < 192, s8, s16)))
    wsum = wsum[HIST_ROWS:, :]
    win = jnp.where(lane < 64, 2.0, jnp.where(lane < 128, 4.0, jnp.where(lane < 192, 8.0, 16.0)))
    pos = (pos0 + lax.broadcasted_iota(jnp.int32, (n, 1), 0) + 1).astype(jnp.float32)
    cnt = jnp.minimum(win, pos)
    pooled = wsum / cnt - p_in
    p_out = jnp.dot(pooled.astype(jnp.bfloat16), pool_bd,
                    preferred_element_type=jnp.float32) * pool_scale

    lane_s = lax.broadcasted_iota(jnp.int32, (1, D_SGU), 1)
    chunk = min(n, SGU_LEN)
    if chunk == SGU_LEN:
        wm = wm_all
    else:
        wm = jnp.concatenate([wm_all[h * SGU_LEN:h * SGU_LEN + chunk, 0:chunk]
                              for h in range(SGU_HEADS)], axis=0)
    s_rows = []
    for c in range(n // chunk):
        v_c = s_v[c * chunk:(c + 1) * chunk, :].astype(jnp.bfloat16)
        r = jnp.dot(wm, v_c, preferred_element_type=jnp.float32)
        s_c = jnp.where(lane_s < 96, r[0:chunk],
                        jnp.where(lane_s < 192, r[chunk:2 * chunk],
                                  jnp.where(lane_s < 288, r[2 * chunk:3 * chunk], r[3 * chunk:4 * chunk])))
        s_rows.append(s_c + bias_full[0:chunk, :])
    s_gate = s_rows[0] if len(s_rows) == 1 else jnp.concatenate(s_rows, axis=0)
    s_out = s_u * s_gate

    mix = jnp.concatenate([a_out, p_out, s_out], axis=-1)
    return mix, zext[n:n + HIST_ROWS, :], pext[n:n + HIST_ROWS, :]


def _route(h2_bf16, wr_t, br_col, n):
    logits = lax.dot_general(wr_t, h2_bf16, (((1,), (1,)), ((), ())),
                             preferred_element_type=jnp.float32) + br_col
    row8 = lax.broadcasted_iota(jnp.int32, (8, n), 0)
    lc = jnp.where(row8 < N_GROUPS, logits[0:8, :], -jnp.inf)
    mc = jnp.max(lc, axis=0, keepdims=True)
    g_sel = jnp.min(jnp.where(lc == mc, row8, 8), axis=0, keepdims=True)
    p_sel = 1.0 / jnp.sum(jnp.exp(lc - mc), axis=0, keepdims=True)
    lf = logits[8 + 3 * EXPERTS_PER_GROUP:8 + 4 * EXPERTS_PER_GROUP, :]
    for g in (2, 1, 0):
        lf = jnp.where(g_sel == g, logits[8 + g * EXPERTS_PER_GROUP:8 + (g + 1) * EXPERTS_PER_GROUP, :], lf)
    m1 = jnp.max(lf, axis=0, keepdims=True)
    i1 = jnp.min(jnp.where(lf == m1, row8, 8), axis=0, keepdims=True)
    lf2 = jnp.where(row8 == i1, -jnp.inf, lf)
    m2 = jnp.max(lf2, axis=0, keepdims=True)
    i2 = jnp.min(jnp.where(lf2 == m2, row8, 8), axis=0, keepdims=True)
    t = jnp.exp(m2 - m1)
    wa = 1.0 / (1.0 + t)
    wb = t / (1.0 + t)
    e0 = g_sel * EXPERTS_PER_GROUP + i1
    e1 = g_sel * EXPERTS_PER_GROUP + i2
    return e0, e1, p_sel * wa, p_sel * wb


def _place_rows(e0, e1, triu, tril_e, fill_ref, base_ref, np_ref, pexp_ref):
    row_e = lax.broadcasted_iota(jnp.int32, (N_EXPERTS, TL), 0)
    oh0 = row_e == e0
    oh1 = row_e == e1
    oh = jnp.where(oh0 | oh1, 1.0, 0.0)
    rank = jnp.dot(oh.astype(jnp.bfloat16), triu, preferred_element_type=jnp.float32).astype(jnp.int32)
    cnt = jnp.sum(oh, axis=1, keepdims=True).astype(jnp.int32)
    lower = jnp.where(e0 < row_e, 1.0, 0.0) + jnp.where(e1 < row_e, 1.0, 0.0)
    first = jnp.sum(lower, axis=1, keepdims=True).astype(jnp.int32)
    sorted_all = first + rank
    lpos0 = jnp.sum(jnp.where(oh0, sorted_all, 0), axis=0, keepdims=True)
    lpos1 = jnp.sum(jnp.where(oh1, sorted_all, 0), axis=0, keepdims=True)

    fill = fill_ref[:, 0:1]
    base = base_ref[:, 0:1]
    npages = np_ref[0:1, 0:1]
    total = fill + cnt
    need = (total > TM).astype(jnp.int32) + (total > 2 * TM).astype(jnp.int32)
    need_b = jnp.broadcast_to(need.astype(jnp.float32), (N_EXPERTS, 128)).astype(jnp.bfloat16)
    before = jnp.dot(tril_e, need_b, preferred_element_type=jnp.float32)[:, 0:1].astype(jnp.int32)
    new_id = npages + before
    new_base = new_id * TM
    fill_ref[...] = jnp.broadcast_to(total - need * TM, (N_EXPERTS, 128))
    base_ref[...] = jnp.broadcast_to(jnp.where(need > 0, new_base + (need - 1) * TM, base), (N_EXPERTS, 128))
    np_ref[...] = jnp.broadcast_to(npages + jnp.sum(need, axis=0, keepdims=True), (8, 128))
    page_lane = lax.broadcasted_iota(jnp.int32, (N_EXPERTS, PAGE_LANES), 1)
    expert_col = lax.broadcasted_iota(jnp.int32, (N_EXPERTS, 1), 0)
    owns = ((page_lane == new_id) & (need >= 1)) | ((page_lane == new_id + 1) & (need == 2))
    pexp_ref[...] = pexp_ref[...] + jnp.sum(jnp.where(owns, expert_col, 0), axis=0, keepdims=True)

    lane = lax.broadcasted_iota(jnp.int32, (N_EXPERTS, 128), 1)
    cols = jnp.where(lane == 0, cnt, jnp.where(lane == 1, first, jnp.where(lane == 2, fill,
                     jnp.where(lane == 3, base, jnp.where(lane == 4, new_base, 0)))))
    square = jnp.concatenate([cols, jnp.zeros((128 - N_EXPERTS, 128), jnp.int32)], axis=0)
    per_expert = square.astype(jnp.float32).T.astype(jnp.int32)[0:8, :]
    return lpos0, lpos1, per_expert


def _run_pieces(cnt, first, fill, base, new_base):
    n0 = jnp.minimum(cnt, TM - fill)
    n1 = jnp.minimum(cnt - n0, TM)
    n2 = cnt - n0 - n1
    return ((first, base + fill, n0), (first + n0, new_base, n1), (first + n0 + n1, new_base + TM, n2))


def _start_run_gather(runs_ref, tile, pages_ref, gbuf, gsem):
    def per_expert(e, c):
        k0 = (tile * N_EXPERTS + e) * RUN_FIELDS
        for j in range(3):
            src = runs_ref[k0 + 3 * j]
            dst = runs_ref[k0 + 3 * j + 1]
            n = runs_ref[k0 + 3 * j + 2]

            @pl.when(n > 0)
            def _():
                pltpu.make_async_copy(pages_ref.at[pl.ds(dst, n)], gbuf.at[pl.ds(src, n)], gsem).start()
        return c
    lax.fori_loop(0, N_EXPERTS, per_expert, 0)


def _unsort_results(lpos_ref, tile, gbuf, y0buf, y1buf, gsem):
    pltpu.make_async_copy(gbuf, gbuf, gsem).wait()
    k0 = tile * 2 * TL
    for r in range(TL):
        y0buf[r] = gbuf[lpos_ref[k0 + r]]
        y1buf[r] = gbuf[lpos_ref[k0 + TL + r]]


def _mixer_kernel(first_layer, *refs):
    if first_layer:
        xp_ref, xs_ref = refs[0:2]
        rest = refs[2:]
    else:
        lposp_ref, runsp_ref, xm_ref, yp_ref, gc_ref = refs[0:5]
        rest = refs[5:]
    (sconv_ref, spool_ref, g1_ref, win_ref, convw_ref, poolbd_ref, pscale_ref, wm_ref, bias_ref,
     wout_ref, g2_ref, wr_ref, br_ref, triu_ref, trile_ref,
     xmid_ref, hs_ref, lpos_ref, runs_ref, meta_ref, gcol_ref, cpr_ref, ppr_ref, csm_ref, psm_ref, sv_ref,
     zh_ref, ph_ref, h2buf, sbuf, proj_ref, mix_ref, posv, pexv, poss, pexs, fill_ref, base_ref, np_ref,
     pexp_ref, fill_s, base_s, np_s, pexp_s, cnt_s, rsem, msem) = rest[:47]
    if not first_layer:
        gbuf, y0buf, y1buf, gsem = rest[47:]

    i = pl.program_id(0)

    def load_x(prompt):
        if first_layer:
            return xp_ref[...] if prompt else xs_ref[...]
        g = gc_ref[...]
        y0 = y0buf[...].reshape(TL, D_MODEL)
        y1 = y1buf[...].reshape(TL, D_MODEL)
        return xm_ref[...] + g[:, 0:1] * y0 + g[:, 1:2] * y1

    row_m = lax.broadcasted_iota(jnp.int32, (SGU_HEADS * SGU_LEN, SGU_LEN), 0) % SGU_LEN
    col_m = lax.broadcasted_iota(jnp.int32, (SGU_HEADS * SGU_LEN, SGU_LEN), 1)

    def wait_rows():
        pltpu.make_async_copy(sbuf, sbuf, rsem).wait()

    def flush_tile(tile, slot):
        pltpu.make_async_copy(posv, poss.at[slot], msem).wait()
        pltpu.make_async_copy(pexv, pexs.at[slot], msem).wait()

        @pl.when(tile >= 1)
        def _():
            wait_rows()
        for r in range(TL):
            row = h2buf[r]
            sbuf[poss[slot, 0, r]] = row
            sbuf[poss[slot, 1, r]] = row

        def per_expert(e, c):
            pieces = _run_pieces(pexs[slot, 0, e], pexs[slot, 1, e], pexs[slot, 2, e], pexs[slot, 3, e],
                                 pexs[slot, 4, e])
            k0 = (tile * N_EXPERTS + e) * RUN_FIELDS
            for j, (src, dst, n) in enumerate(pieces):
                runs_ref[k0 + 3 * j] = src
                runs_ref[k0 + 3 * j + 1] = dst
                runs_ref[k0 + 3 * j + 2] = n

                @pl.when(n > 0)
                def _():
                    pltpu.make_async_copy(sbuf.at[pl.ds(src, n)], hs_ref.at[pl.ds(dst, n)], rsem).start()
            return c
        lax.fori_loop(0, N_EXPERTS, per_expert, 0)

    def finish(x, mix):
        x_mid = x + jnp.dot(mix.astype(jnp.bfloat16), wout_ref[...], preferred_element_type=jnp.float32)
        xmid_ref[...] = x_mid
        h2 = _rms(x_mid, g2_ref[...])
        h2buf[...] = h2.reshape(TL, 1, D_MODEL)
        e0, e1, g0, g1 = _route(h2.astype(jnp.bfloat16), wr_ref[...], br_ref[...], TL)
        row128 = lax.broadcasted_iota(jnp.int32, (128, TL), 0)
        gcol_ref[...] = jnp.where(row128 == 0, g0, jnp.where(row128 == 1, g1, 0.0)).T
        lpos0, lpos1, per_expert = _place_rows(e0, e1, triu_ref[...], trile_ref[...], fill_ref, base_ref,
                                               np_ref, pexp_ref)
        row8 = lax.broadcasted_iota(jnp.int32, (8, TL), 0)
        posv[...] = jnp.where(row8 == 0, lpos0, jnp.where(row8 == 1, lpos1, 0))
        pexv[...] = per_expert
        row2 = lax.broadcasted_iota(jnp.int32, (2, TL), 0)
        lpos_ref[0] = jnp.where(row2 == 0, lpos0, lpos1)
        pltpu.make_async_copy(posv, poss.at[i % 2], msem).start()
        pltpu.make_async_copy(pexv, pexs.at[i % 2], msem).start()

    def project(x):
        h = _rms(x, g1_ref[...]).astype(jnp.bfloat16)
        return jnp.dot(h, win_ref[...], preferred_element_type=jnp.float32)

    def masked_wm():
        return jnp.where(col_m <= row_m, wm_ref[...], jnp.zeros_like(wm_ref[...]))

    @pl.when(i == 0)
    def _init():
        zh_ref[...] = jnp.zeros_like(zh_ref)
        ph_ref[...] = jnp.zeros_like(ph_ref)
        fill_ref[...] = jnp.full_like(fill_ref, TM)
        base_ref[...] = jnp.zeros_like(base_ref)
        np_ref[...] = jnp.zeros_like(np_ref)
        pexp_ref[...] = jnp.zeros_like(pexp_ref)
        if not first_layer:
            _start_run_gather(runsp_ref, 0, yp_ref, gbuf, gsem)

    def _close_pages():
        copies = [pltpu.make_async_copy(fill_ref, fill_s, msem), pltpu.make_async_copy(base_ref, base_s, msem),
                  pltpu.make_async_copy(np_ref, np_s, msem), pltpu.make_async_copy(pexp_ref, pexp_s, msem)]
        for c in copies:
            c.start()
        for c in copies:
            c.wait()
        npages = np_s[0, 0]

        def zero_cnt(e, c):
            cnt_s[e] = 0
            return c
        lax.fori_loop(0, N_EXPERTS, zero_cnt, 0)

        def count(p, c):
            e = pexp_s[0, p]
            cnt_s[e] = cnt_s[e] + 1
            return c
        lax.fori_loop(0, npages, count, 0)

        def prefix(e, run):
            n = cnt_s[e]
            cnt_s[e] = run
            return run + n
        lax.fori_loop(0, N_EXPERTS, prefix, 0)

        def emit(p, c):
            e = pexp_s[0, p]
            q = cnt_s[e]
            cnt_s[e] = q + 1
            meta_ref[0, q] = p
            meta_ref[1, q] = e
            return c
        lax.fori_loop(0, npages, emit, 0)

        def pad_meta(q, c):
            meta_ref[0, q] = jnp.minimum(q, N_PAGES - 1)
            meta_ref[1, q] = meta_ref[1, npages - 1]
            return c
        lax.fori_loop(npages, PAGE_LANES, pad_meta, 0)

        def fill_row2(q, c):
            meta_ref[2, q] = npages
            return c
        lax.fori_loop(0, PAGE_LANES, fill_row2, 0)

        h2buf[...] = jnp.zeros((TL, 1, D_MODEL), jnp.float32)

        def tails(start):
            def tail(e, c):
                f = fill_s[e, 0]
                b = base_s[e, 0]

                @pl.when(f < TM)
                def _zero_tail():
                    cp = pltpu.make_async_copy(h2buf.at[pl.ds(0, TM - f)], hs_ref.at[pl.ds(b + f, TM - f)], rsem)
                    cp.start() if start else cp.wait()
                return c
            lax.fori_loop(0, N_EXPERTS, tail, 0)

            def unused(p, c):
                cp = pltpu.make_async_copy(h2buf.at[pl.ds(0, TM)], hs_ref.at[pl.ds(p * TM, TM)], rsem)
                cp.start() if start else cp.wait()
                return c
            lax.fori_loop(npages, N_PAGES, unused, 0)

        tails(True)
        tails(False)

    prev_slot = (i + 1) % 2

    @pl.when(i >= 1)
    def _flush_previous():
        flush_tile(i - 1, prev_slot)

    if not first_layer:
        @pl.when(i >= 0)
        def _results():
            _unsort_results(lposp_ref, i, gbuf, y0buf, y1buf, gsem)

            @pl.when(i + 1 < N_TILES)
            def _():
                _start_run_gather(runsp_ref, i + 1, yp_ref, gbuf, gsem)

    @pl.when(i < N_PROMPT_TILES)
    def _prompt_project():
        x = load_x(True)
        xmid_ref[...] = x
        proj_ref[...] = project(x)

    @pl.when(i <= N_PROMPT_TILES - 1)
    def _prompt_mix():
        s = i % TILES_PER_SEQ
        zhist = jnp.where(s == 0, 0.0, zh_ref[...])
        phist = jnp.where(s == 0, 0.0, ph_ref[...])
        mix, znew, pnew = _mix_rows(proj_ref[...], zhist, phist, s * TL, TL, convw_ref[...], poolbd_ref[...],
                                    pscale_ref[...], masked_wm(), bias_ref[...])
        zh_ref[...] = znew
        ph_ref[...] = pnew
        cpr_ref[0] = znew
        ppr_ref[0] = pnew
        mix_ref[...] = mix.astype(jnp.bfloat16)

    @pl.when(i < N_PROMPT_TILES)
    def _prompt_finish():
        finish(xmid_ref[...], mix_ref[...])

    @pl.when(i == N_PROMPT_TILES)
    def _sample():
        x = load_x(False)
        proj = project(x)
        wm = masked_wm()
        mixes = []
        for b in range(DEC_BATCH):
            rows = slice(b * DEC_SEQ, (b + 1) * DEC_SEQ)
            mix, znew, pnew = _mix_rows(proj[rows, :], sconv_ref[b], spool_ref[b], PAST_LEN, DEC_SEQ,
                                        convw_ref[...], poolbd_ref[...], pscale_ref[...], wm, bias_ref[...])
            csm_ref[b] = znew
            psm_ref[b] = pnew
            mixes.append(mix)
        sv_ref[...] = proj[:, 1792:2176]
        finish(x, jnp.concatenate(mixes, axis=0))
        flush_tile(i, i % 2)
        wait_rows()
        _close_pages()


def _mixer_call(first_layer, xs, sconv_pad, spool_pad, g1, w_in, conv_w, pool_bd, pool_scale, wm_all,
                bias_full, w_out, g2, wr_t, br_col, triu, tril_e):
    tile = lambda i, *_: (i, 0)
    prompt_tile = lambda i, *_: (jnp.minimum(i, N_PROMPT_TILES - 1), 0)
    const2 = lambda i, *_: (0, 0)
    const3 = lambda i, *_: (0, 0, 0)
    if first_layer:
        prefetch = ()
        x_specs = [pl.BlockSpec((TL, D_MODEL), prompt_tile), pl.BlockSpec((TL, D_MODEL), const2)]
    else:
        prefetch = xs[0:2]
        xs = xs[2:]
        x_specs = [pl.BlockSpec((TL, D_MODEL), tile), pl.BlockSpec(memory_space=pl.ANY),
                   pl.BlockSpec((TL, 128), tile)]
    full = lambda a: pl.BlockSpec(a.shape, const2 if a.ndim == 2 else const3)
    weights = [sconv_pad, spool_pad, g1, w_in, conv_w, pool_bd, pool_scale, wm_all, bias_full, w_out, g2,
               wr_t, br_col, triu, tril_e]
    in_specs = x_specs + [full(a) for a in weights]
    seq_of = lambda i, *_: (jnp.minimum(i // TILES_PER_SEQ, BATCH - 1), 0, 0)
    out_shape = [
        jax.ShapeDtypeStruct((T_ALL, D_MODEL), jnp.float32),
        jax.ShapeDtypeStruct((PAGE_ROWS, 1, D_MODEL), jnp.float32),
        jax.ShapeDtypeStruct((N_TILES, 2, TL), jnp.int32),
        jax.ShapeDtypeStruct((N_TILES * RUNS_PER_TILE,), jnp.int32),
        jax.ShapeDtypeStruct((3, PAGE_LANES), jnp.int32),
        jax.ShapeDtypeStruct((T_ALL, 128), jnp.float32),
        jax.ShapeDtypeStruct((BATCH, HIST_ROWS, D_CONV), jnp.float32),
        jax.ShapeDtypeStruct((BATCH, HIST_ROWS, D_POOL), jnp.float32),
        jax.ShapeDtypeStruct((DEC_BATCH, HIST_ROWS, D_CONV), jnp.float32),
        jax.ShapeDtypeStruct((DEC_BATCH, HIST_ROWS, D_POOL), jnp.float32),
        jax.ShapeDtypeStruct((T_SAMPLE, D_SGU), jnp.float32),
    ]
    out_specs = [
        pl.BlockSpec((TL, D_MODEL), tile),
        pl.BlockSpec(memory_space=pl.ANY),
        pl.BlockSpec((1, 2, TL), lambda i, *_: (i, 0, 0)),
        pl.BlockSpec(memory_space=pltpu.SMEM),
        pl.BlockSpec(memory_space=pltpu.SMEM),
        pl.BlockSpec((TL, 128), tile),
        pl.BlockSpec((1, HIST_ROWS, D_CONV), seq_of),
        pl.BlockSpec((1, HIST_ROWS, D_POOL), seq_of),
        pl.BlockSpec((DEC_BATCH, HIST_ROWS, D_CONV), const3),
        pl.BlockSpec((DEC_BATCH, HIST_ROWS, D_POOL), const3),
        pl.BlockSpec((T_SAMPLE, D_SGU), const2),
    ]
    scratch = [
        pltpu.VMEM((HIST_ROWS, D_CONV), jnp.float32),
        pltpu.VMEM((HIST_ROWS, D_POOL), jnp.float32),
        pltpu.VMEM((TL, 1, D_MODEL), jnp.float32),
        pltpu.VMEM((2 * TL, 1, D_MODEL), jnp.float32),
        pltpu.VMEM((TL, D_PROJ), jnp.float32),
        pltpu.VMEM((TL, D_MODEL), jnp.bfloat16),
        pltpu.VMEM((8, TL), jnp.int32),
        pltpu.VMEM((8, 128), jnp.int32),
        pltpu.SMEM((2, 8, TL), jnp.int32),
        pltpu.SMEM((2, 8, 128), jnp.int32),
        pltpu.VMEM((N_EXPERTS, 128), jnp.int32),
        pltpu.VMEM((N_EXPERTS, 128), jnp.int32),
        pltpu.VMEM((8, 128), jnp.int32),
        pltpu.VMEM((8, PAGE_LANES), jnp.int32),
        pltpu.SMEM((N_EXPERTS, 128), jnp.int32),
        pltpu.SMEM((N_EXPERTS, 128), jnp.int32),
        pltpu.SMEM((8, 128), jnp.int32),
        pltpu.SMEM((8, PAGE_LANES), jnp.int32),
        pltpu.SMEM((N_EXPERTS,), jnp.int32),
        pltpu.SemaphoreType.DMA(()),
        pltpu.SemaphoreType.DMA(()),
    ]
    if not first_layer:
        scratch += [
            pltpu.VMEM((2 * TL, 1, D_MODEL), jnp.float32),
            pltpu.VMEM((TL, 1, D_MODEL), jnp.float32),
            pltpu.VMEM((TL, 1, D_MODEL), jnp.float32),
            pltpu.SemaphoreType.DMA(()),
        ]
    grid_spec = pltpu.PrefetchScalarGridSpec(num_scalar_prefetch=len(prefetch), grid=(N_TILES,),
                                             in_specs=in_specs, out_specs=out_specs, scratch_shapes=scratch)
    return pl.pallas_call(
        functools.partial(_mixer_kernel, first_layer),
        grid_spec=grid_spec,
        out_shape=out_shape,
        compiler_params=pltpu.CompilerParams(dimension_semantics=("arbitrary",),
                                             vmem_limit_bytes=VMEM_LIMIT),
        name="mixer_first" if first_layer else "mixer_next",
    )(*prefetch, *xs, *weights)


def _expert_kernel(meta_ref, hs_ref, wg_ref, wu_ref, wd_ref, y_ref, xbuf):
    s = pl.program_id(0)
    npages = meta_ref[2, 0]

    @pl.when(s < npages)
    def _page():
        xbuf[...] = hs_ref[...].reshape(TM, D_MODEL)
        x = xbuf[...].astype(jnp.bfloat16)
        hg = jnp.dot(x, wg_ref[0, 0].astype(jnp.bfloat16), preferred_element_type=jnp.float32)
        hu = jnp.dot(x, wu_ref[0, 0].astype(jnp.bfloat16), preferred_element_type=jnp.float32)
        h = (hg * jax.nn.sigmoid(hg) * hu).astype(jnp.bfloat16)
        out = jnp.dot(h, wd_ref[0, 0].astype(jnp.bfloat16), preferred_element_type=jnp.float32)
        y_ref[...] = out.reshape(TM, 1, D_MODEL)

    @pl.when(s >= npages)
    def _unused_page():
        y_ref[...] = jnp.zeros((TM, 1, D_MODEL), jnp.float32)


def _expert_call(layer, meta, hs, w_gate, w_up, w_down):
    w_idx = lambda s, meta: (layer, meta[1, s], 0, 0)
    page = lambda s, meta: (meta[0, s], 0, 0)
    grid_spec = pltpu.PrefetchScalarGridSpec(
        num_scalar_prefetch=1,
        grid=(N_PAGES,),
        in_specs=[
            pl.BlockSpec((TM, 1, D_MODEL), page),
            pl.BlockSpec((1, 1, D_MODEL, D_EXPERT), w_idx),
            pl.BlockSpec((1, 1, D_MODEL, D_EXPERT), w_idx),
            pl.BlockSpec((1, 1, D_EXPERT, D_MODEL), w_idx),
        ],
        out_specs=pl.BlockSpec((TM, 1, D_MODEL), page),
        scratch_shapes=[pltpu.VMEM((TM, D_MODEL), jnp.float32)],
    )
    return pl.pallas_call(
        _expert_kernel,
        grid_spec=grid_spec,
        out_shape=jax.ShapeDtypeStruct((PAGE_ROWS, 1, D_MODEL), jnp.float32),
        compiler_params=pltpu.CompilerParams(dimension_semantics=("arbitrary",),
                                             vmem_limit_bytes=VMEM_LIMIT),
        name="experts",
    )(meta, hs, w_gate, w_up, w_down)


def _final_kernel(lposp_ref, runsp_ref, xm_ref, yp_ref, gc_ref, g_ref, yp_out, ys_out, gbuf, y0buf, y1buf, gsem):
    i = pl.program_id(0)

    @pl.when(i == 0)
    def _():
        _start_run_gather(runsp_ref, 0, yp_ref, gbuf, gsem)

    _unsort_results(lposp_ref, i, gbuf, y0buf, y1buf, gsem)

    @pl.when(i + 1 < N_TILES)
    def _():
        _start_run_gather(runsp_ref, i + 1, yp_ref, gbuf, gsem)

    g = gc_ref[...]
    y0 = y0buf[...].reshape(TL, D_MODEL)
    y1 = y1buf[...].reshape(TL, D_MODEL)
    x = xm_ref[...] + g[:, 0:1] * y0 + g[:, 1:2] * y1
    out = _rms(x, g_ref[...])

    @pl.when(i < N_PROMPT_TILES)
    def _():
        yp_out[...] = out

    @pl.when(i == N_PROMPT_TILES)
    def _():
        ys_out[...] = out


def _final_call(lpos, runs, x_mid, y_pages, gcol, g):
    tile = lambda i, *_: (i, 0)
    grid_spec = pltpu.PrefetchScalarGridSpec(
        num_scalar_prefetch=2,
        grid=(N_TILES,),
        in_specs=[pl.BlockSpec((TL, D_MODEL), tile), pl.BlockSpec(memory_space=pl.ANY),
                  pl.BlockSpec((TL, 128), tile), pl.BlockSpec((1, D_MODEL), lambda i, *_: (0, 0))],
        out_specs=[pl.BlockSpec((TL, D_MODEL), lambda i, *_: (jnp.minimum(i, N_PROMPT_TILES - 1), 0)),
                   pl.BlockSpec((TL, D_MODEL), lambda i, *_: (0, 0))],
        scratch_shapes=[pltpu.VMEM((2 * TL, 1, D_MODEL), jnp.float32), pltpu.VMEM((TL, 1, D_MODEL), jnp.float32),
                        pltpu.VMEM((TL, 1, D_MODEL), jnp.float32), pltpu.SemaphoreType.DMA(())],
    )
    return pl.pallas_call(
        _final_kernel,
        grid_spec=grid_spec,
        out_shape=[jax.ShapeDtypeStruct((T_PROMPT, D_MODEL), jnp.float32),
                   jax.ShapeDtypeStruct((T_SAMPLE, D_MODEL), jnp.float32)],
        compiler_params=pltpu.CompilerParams(dimension_semantics=("arbitrary",),
                                             vmem_limit_bytes=VMEM_LIMIT),
        name="final_norm",
    )(lpos, runs, x_mid, y_pages, gcol, g)


def kernel(x_prompt, x_sample, state_conv, state_pool, norm1_g, w_in, conv_w, pool_w, pool_scale, sgu_w, sgu_b, w_out, norm2_g, router_coarse_w, router_coarse_b, router_fine_w, router_fine_b, moe_w_gate, moe_w_up, moe_w_down, final_norm_g):
    bf16 = jnp.bfloat16
    xs = (x_prompt.reshape(T_PROMPT, D_MODEL), x_sample.reshape(T_SAMPLE, D_MODEL))
    sconv_pad = jnp.pad(state_conv, ((0, 0), (0, 0), (HIST_ROWS - (CONV_WIDTH - 1), 0), (0, 0)))
    spool_pad = jnp.pad(state_pool, ((0, 0), (0, 0), (HIST_ROWS - POOL_HIST, 0), (0, 0)))
    idx = jnp.arange(TL, dtype=jnp.int32)
    triu = (idx[:, None] < idx[None, :]).astype(bf16)
    ide = jnp.arange(N_EXPERTS, dtype=jnp.int32)
    tril_e = (ide[None, :] < ide[:, None]).astype(bf16)
    conv_pr, pool_pr, conv_sm, pool_sm, sgu_v = [], [], [], [], []
    x_mid = y_pages = gcol = lpos = runs = None
    for l in range(DEPTH):
        pool_bd = jax.scipy.linalg.block_diag(*[pool_w[l, g] for g in range(4)]).astype(bf16)
        wm_all = sgu_w[l].reshape(SGU_HEADS * SGU_LEN, SGU_LEN).astype(bf16)
        bias_full = jnp.repeat(sgu_b[l].T, SGU_HEAD_DIM, axis=1)
        wr_t = jnp.zeros((ROUTER_ROWS, D_MODEL), jnp.float32)
        wr_t = wr_t.at[0:N_GROUPS].set(router_coarse_w[l].T).at[8:].set(router_fine_w[l].T).astype(bf16)
        br_col = jnp.zeros((ROUTER_ROWS, 1), jnp.float32)
        br_col = br_col.at[0:N_GROUPS, 0].set(router_coarse_b[l]).at[8:, 0].set(router_fine_b[l])
        outs = _mixer_call(l == 0, xs, sconv_pad[l], spool_pad[l], norm1_g[l].reshape(1, D_MODEL),
                           w_in[l].astype(bf16), conv_w[l], pool_bd, pool_scale[l].reshape(1, D_POOL),
                           wm_all, bias_full, w_out[l].astype(bf16), norm2_g[l].reshape(1, D_MODEL),
                           wr_t, br_col, triu, tril_e)
        x_mid, hs, lpos, runs, meta, gcol, cpr, ppr, csm, psm, sv = outs
        lpos = lpos.reshape(N_TILES * 2 * TL)
        conv_pr.append(cpr[:, HIST_ROWS - (CONV_WIDTH - 1):, :])
        pool_pr.append(ppr[:, HIST_ROWS - POOL_HIST:, :])
        conv_sm.append(csm[:, HIST_ROWS - (CONV_WIDTH - 1):, :])
        pool_sm.append(psm[:, HIST_ROWS - POOL_HIST:, :])
        sgu_v.append(sv.reshape(DEC_BATCH, DEC_SEQ, D_SGU))
        y_pages = _expert_call(l, meta, hs, moe_w_gate, moe_w_up, moe_w_down)
        xs = (lpos, runs, x_mid, y_pages, gcol)
    y_prompt, y_sample = _final_call(lpos, runs, x_mid, y_pages, gcol, final_norm_g.reshape(1, D_MODEL))
    return (y_prompt.reshape(BATCH, SEQ, D_MODEL), y_sample.reshape(DEC_BATCH, DEC_SEQ, D_MODEL),
            jnp.stack(conv_pr), jnp.stack(pool_pr), jnp.stack(conv_sm), jnp.stack(pool_sm),
            jnp.stack(sgu_v))
```

```python
import functools

import jax
import jax.numpy as jnp
from jax import lax
from jax.experimental import pallas as pl
from jax.experimental.pallas import tpu as pltpu

D_MODEL = 1024
BATCH = 8
SEQ = 2048
DEPTH = 2
DEC_BATCH = 8
DEC_SEQ = 64
PAST_LEN = 1024
D_CONV = 384
CONV_WIDTH = 3
D_POOL = 256
POOL_HIST = 15
D_SGU = 384
SGU_HEADS = 4
SGU_HEAD_DIM = 96
SGU_LEN = 128
D_PROJ = 2176
N_GROUPS = 4
EXPERTS_PER_GROUP = 8
N_EXPERTS = 32
D_EXPERT = 512
EPS = 1e-6

T_PROMPT = BATCH * SEQ
T_SAMPLE = DEC_BATCH * DEC_SEQ
T_ALL = T_PROMPT + T_SAMPLE
TL = 512
TILES_PER_SEQ = SEQ // TL
N_PROMPT_TILES = T_PROMPT // TL
N_TILES = N_PROMPT_TILES + T_SAMPLE // TL
HIST_ROWS = 16
ROUTER_ROWS = 8 + N_EXPERTS
TM = 256
N_ASSIGN = 2 * T_ALL
N_PAGES = N_ASSIGN // TM + N_EXPERTS
PAGE_LANES = 256
PAGE_ROWS = N_PAGES * TM
RUN_FIELDS = 9
RUNS_PER_TILE = N_EXPERTS * RUN_FIELDS
VMEM_LIMIT = 56 * 1024 * 1024

assert TL == 2 * TM and N_PAGES <= PAGE_LANES and T_ALL == N_TILES * TL


def _rms(x, g):
    return x * lax.rsqrt(jnp.mean(x * x, axis=-1, keepdims=True) + EPS) * g


def _mix_rows(proj, zhist, phist, pos0, n, conv_w, pool_bd, pool_scale, wm_all, bias_full):
    a_b = proj[:, 0:384]
    a_c = proj[:, 384:768]
    a_h = proj[:, 768:1152]
    p_in = proj[:, 1152:1408]
    s_u = proj[:, 1408:1792]
    s_v = proj[:, 1792:2176]

    z = a_c * a_h
    zext = jnp.concatenate([zhist, z], axis=0)
    conv_y = (conv_w[0:1, :] * pltpu.roll(zext, 2, 0)[HIST_ROWS:, :]
              + conv_w[1:2, :] * pltpu.roll(zext, 1, 0)[HIST_ROWS:, :]
              + conv_w[2:3, :] * z)
    a_out = a_b * conv_y

    pext = jnp.concatenate([phist, p_in], axis=0)
    s2 = pext + pltpu.roll(pext, 1, 0)
    s4 = s2 + pltpu.roll(s2, 2, 0)
    s8 = s4 + pltpu.roll(s4, 4, 0)
    s16 = s8 + pltpu.roll(s8, 8, 0)
    lane = lax.broadcasted_iota(jnp.int32, (1, D_POOL), 1)
    wsum = jnp.where(lane < 64, s2, jnp.where(lane < 128, s4, jnp.where(lane < 192, s8, s16)))
    wsum = wsum[HIST_ROWS:, :]
    win = jnp.where(lane < 64, 2.0, jnp.where(lane < 128, 4.0, jnp.where(lane < 192, 8.0, 16.0)))
    pos = (pos0 + lax.broadcasted_iota(jnp.int32, (n, 1), 0) + 1).astype(jnp.float32)
    cnt = jnp.minimum(win, pos)
    pooled = wsum / cnt - p_in
    p_out = jnp.dot(pooled.astype(jnp.bfloat16), pool_bd,
                    preferred_element_type=jnp.float32) * pool_scale

    lane_s = lax.broadcasted_iota(jnp.int32, (1, D_SGU), 1)
    chunk = min(n, SGU_LEN)
    if chunk == SGU_LEN:
        wm = wm_all
    else:
        wm = jnp.concatenate([wm_all[h * SGU_LEN:h * SGU_LEN + chunk, 0:chunk]
                              for h in range(SGU_HEADS)], axis=0)
    s_rows = []
    for c in range(n // chunk):
        v_c = s_v[c * chunk:(c + 1) * chunk, :].astype(jnp.bfloat16)
        r = jnp.dot(wm, v_c, preferred_element_type=jnp.float32)
        s_c = jnp.where(lane_s < 96, r[0:chunk],
                        jnp.where(lane_s < 192, r[chunk:2 * chunk],
                                  jnp.where(lane_s < 288, r[2 * chunk:3 * chunk], r[3 * chunk:4 * chunk])))
        s_rows.append(s_c + bias_full[0:chunk, :])
    s_gate = s_rows[0] if len(s_rows) == 1 else jnp.concatenate(s_rows, axis=0)
    s_out = s_u * s_gate

    mix = jnp.concatenate([a_out, p_out, s_out], axis=-1)
    return mix, zext[n:n + HIST_ROWS, :], pext[n:n + HIST_ROWS, :]


def _route(h2_bf16, wr_t, br_col, n):
    logits = lax.dot_general(wr_t, h2_bf16, (((1,), (1,)), ((), ())),
                             preferred_element_type=jnp.float32) + br_col
    row8 = lax.broadcasted_iota(jnp.int32, (8, n), 0)
    lc = jnp.where(row8 < N_GROUPS, logits[0:8, :], -jnp.inf)
    mc = jnp.max(lc, axis=0, keepdims=True)
    g_sel = jnp.min(jnp.where(lc == mc, row8, 8), axis=0, keepdims=True)
    p_sel = 1.0 / jnp.sum(jnp.exp(lc - mc), axis=0, keepdims=True)
    lf = logits[8 + 3 * EXPERTS_PER_GROUP:8 + 4 * EXPERTS_PER_GROUP, :]
    for g in (2, 1, 0):
        lf = jnp.where(g_sel == g, logits[8 + g * EXPERTS_PER_GROUP:8 + (g + 1) * EXPERTS_PER_GROUP, :], lf)
    m1 = jnp.max(lf, axis=0, keepdims=True)
    i1 = jnp.min(jnp.where(lf == m1, row8, 8), axis=0, keepdims=True)
    lf2 = jnp.where(row8 == i1, -jnp.inf, lf)
    m2 = jnp.max(lf2, axis=0, keepdims=True)
    i2 = jnp.min(jnp.where(lf2 == m2, row8, 8), axis=0, keepdims=True)
    t = jnp.exp(m2 - m1)
    wa = 1.0 / (1.0 + t)
    wb = t / (1.0 + t)
    e0 = g_sel * EXPERTS_PER_GROUP + i1
    e1 = g_sel * EXPERTS_PER_GROUP + i2
    return e0, e1, p_sel * wa, p_sel * wb


def _place_rows(e0, e1, triu, tril_e, fill_ref, base_ref, np_ref, pexp_ref):
    row_e = lax.broadcasted_iota(jnp.int32, (N_EXPERTS, TL), 0)
    oh0 = row_e == e0
    oh1 = row_e == e1
    oh = jnp.where(oh0 | oh1, 1.0, 0.0)
    rank = jnp.dot(oh.astype(jnp.bfloat16), triu, preferred_element_type=jnp.float32).astype(jnp.int32)
    cnt = jnp.sum(oh, axis=1, keepdims=True).astype(jnp.int32)
    lower = jnp.where(e0 < row_e, 1.0, 0.0) + jnp.where(e1 < row_e, 1.0, 0.0)
    first = jnp.sum(lower, axis=1, keepdims=True).astype(jnp.int32)
    sorted_all = first + rank
    lpos0 = jnp.sum(jnp.where(oh0, sorted_all, 0), axis=0, keepdims=True)
    lpos1 = jnp.sum(jnp.where(oh1, sorted_all, 0), axis=0, keepdims=True)

    fill = fill_ref[:, 0:1]
    base = base_ref[:, 0:1]
    npages = np_ref[0:1, 0:1]
    total = fill + cnt
    need = (total > TM).astype(jnp.int32) + (total > 2 * TM).astype(jnp.int32)
    need_b = jnp.broadcast_to(need.astype(jnp.float32), (N_EXPERTS, 128)).astype(jnp.bfloat16)
    before = jnp.dot(tril_e, need_b, preferred_element_type=jnp.float32)[:, 0:1].astype(jnp.int32)
    new_id = npages + before
    new_base = new_id * TM
    fill_ref[...] = jnp.broadcast_to(total - need * TM, (N_EXPERTS, 128))
    base_ref[...] = jnp.broadcast_to(jnp.where(need > 0, new_base + (need - 1) * TM, base), (N_EXPERTS, 128))
    np_ref[...] = jnp.broadcast_to(npages + jnp.sum(need, axis=0, keepdims=True), (8, 128))
    page_lane = lax.broadcasted_iota(jnp.int32, (N_EXPERTS, PAGE_LANES), 1)
    expert_col = lax.broadcasted_iota(jnp.int32, (N_EXPERTS, 1), 0)
    owns = ((page_lane == new_id) & (need >= 1)) | ((page_lane == new_id + 1) & (need == 2))
    pexp_ref[...] = pexp_ref[...] + jnp.sum(jnp.where(owns, expert_col, 0), axis=0, keepdims=True)

    lane = lax.broadcasted_iota(jnp.int32, (N_EXPERTS, 128), 1)
    cols = jnp.where(lane == 0, cnt, jnp.where(lane == 1, first, jnp.where(lane == 2, fill,
                     jnp.where(lane == 3, base, jnp.where(lane == 4, new_base, 0)))))
    square = jnp.concatenate([cols, jnp.zeros((128 - N_EXPERTS, 128), jnp.int32)], axis=0)
    per_expert = square.astype(jnp.float32).T.astype(jnp.int32)[0:8, :]
    return lpos0, lpos1, per_expert


def _run_pieces(cnt, first, fill, base, new_base):
    n0 = jnp.minimum(cnt, TM - fill)
    n1 = jnp.minimum(cnt - n0, TM)
    n2 = cnt - n0 - n1
    return ((first, base + fill, n0), (first + n0, new_base, n1), (first + n0 + n1, new_base + TM, n2))


def _start_run_gather(runs_ref, tile, pages_ref, gbuf, gsem):
    def per_expert(e, c):
        k0 = (tile * N_EXPERTS + e) * RUN_FIELDS
        for j in range(3):
            src = runs_ref[k0 + 3 * j]
            dst = runs_ref[k0 + 3 * j + 1]
            n = runs_ref[k0 + 3 * j + 2]

            @pl.when(n > 0)
            def _():
                pltpu.make_async_copy(pages_ref.at[pl.ds(dst, n)], gbuf.at[pl.ds(src, n)], gsem).start()
        return c
    lax.fori_loop(0, N_EXPERTS, per_expert, 0)


def _unsort_results(lpos_ref, tile, gbuf, y0buf, y1buf, gsem):
    pltpu.make_async_copy(gbuf, gbuf, gsem).wait()
    k0 = tile * 2 * TL
    for r in range(TL):
        y0buf[r] = gbuf[lpos_ref[k0 + r]]
        y1buf[r] = gbuf[lpos_ref[k0 + TL + r]]


def _mixer_kernel(first_layer, *refs):
    if first_layer:
        xp_ref, xs_ref = refs[0:2]
        rest = refs[2:]
    else:
        lposp_ref, runsp_ref, xm_ref, yp_ref, gc_ref = refs[0:5]
        rest = refs[5:]
    (sconv_ref, spool_ref, g1_ref, win_ref, convw_ref, poolbd_ref, pscale_ref, wm_ref, bias_ref,
     wout_ref, g2_ref, wr_ref, br_ref, triu_ref, trile_ref,
     xmid_ref, hs_ref, lpos_ref, runs_ref, meta_ref, gcol_ref, cpr_ref, ppr_ref, csm_ref, psm_ref, sv_ref,
     zh_ref, ph_ref, h2buf, sbuf, proj_ref, mix_ref, posv, pexv, poss, pexs, fill_ref, base_ref, np_ref,
     pexp_ref, fill_s, base_s, np_s, pexp_s, cnt_s, rsem, msem) = rest[:47]
    if not first_layer:
        gbuf, y0buf, y1buf, gsem = rest[47:]

    i = pl.program_id(0)

    def load_x(prompt):
        if first_layer:
            return xp_ref[...] if prompt else xs_ref[...]
        g = gc_ref[...]
        y0 = y0buf[...].reshape(TL, D_MODEL)
        y1 = y1buf[...].reshape(TL, D_MODEL)
        return xm_ref[...] + g[:, 0:1] * y0 + g[:, 1:2] * y1

    row_m = lax.broadcasted_iota(jnp.int32, (SGU_HEADS * SGU_LEN, SGU_LEN), 0) % SGU_LEN
    col_m = lax.broadcasted_iota(jnp.int32, (SGU_HEADS * SGU_LEN, SGU_LEN), 1)

    def wait_rows():
        pltpu.make_async_copy(sbuf, sbuf, rsem).wait()

    def flush_tile(tile, slot):
        pltpu.make_async_copy(posv, poss.at[slot], msem).wait()
        pltpu.make_async_copy(pexv, pexs.at[slot], msem).wait()

        @pl.when(tile >= 1)
        def _():
            wait_rows()
        for r in range(TL):
            row = h2buf[r]
            sbuf[poss[slot, 0, r]] = row
            sbuf[poss[slot, 1, r]] = row

        def per_expert(e, c):
            pieces = _run_pieces(pexs[slot, 0, e], pexs[slot, 1, e], pexs[slot, 2, e], pexs[slot, 3, e],
                                 pexs[slot, 4, e])
            k0 = (tile * N_EXPERTS + e) * RUN_FIELDS
            for j, (src, dst, n) in enumerate(pieces):
                runs_ref[k0 + 3 * j] = src
                runs_ref[k0 + 3 * j + 1] = dst
                runs_ref[k0 + 3 * j + 2] = n

                @pl.when(n > 0)
                def _():
                    pltpu.make_async_copy(sbuf.at[pl.ds(src, n)], hs_ref.at[pl.ds(dst, n)], rsem).start()
            return c
        lax.fori_loop(0, N_EXPERTS, per_expert, 0)

    def finish(x, mix):
        x_mid = x + jnp.dot(mix.astype(jnp.bfloat16), wout_ref[...], preferred_element_type=jnp.float32)
        xmid_ref[...] = x_mid
        h2 = _rms(x_mid, g2_ref[...])
        h2buf[...] = h2.reshape(TL, 1, D_MODEL)
        e0, e1, g0, g1 = _route(h2.astype(jnp.bfloat16), wr_ref[...], br_ref[...], TL)
        row128 = lax.broadcasted_iota(jnp.int32, (128, TL), 0)
        gcol_ref[...] = jnp.where(row128 == 0, g0, jnp.where(row128 == 1, g1, 0.0)).T
        lpos0, lpos1, per_expert = _place_rows(e0, e1, triu_ref[...], trile_ref[...], fill_ref, base_ref,
                                               np_ref, pexp_ref)
        row8 = lax.broadcasted_iota(jnp.int32, (8, TL), 0)
        posv[...] = jnp.where(row8 == 0, lpos0, jnp.where(row8 == 1, lpos1, 0))
        pexv[...] = per_expert
        row2 = lax.broadcasted_iota(jnp.int32, (2, TL), 0)
        lpos_ref[0] = jnp.where(row2 == 0, lpos0, lpos1)
        pltpu.make_async_copy(posv, poss.at[i % 2], msem).start()
        pltpu.make_async_copy(pexv, pexs.at[i % 2], msem).start()

    def project(x):
        h = _rms(x, g1_ref[...]).astype(jnp.bfloat16)
        return jnp.dot(h, win_ref[...], preferred_element_type=jnp.float32)

    def masked_wm():
        return jnp.where(col_m <= row_m, wm_ref[...], jnp.zeros_like(wm_ref[...]))

    @pl.when(i == 0)
    def _init():
        zh_ref[...] = jnp.zeros_like(zh_ref)
        ph_ref[...] = jnp.zeros_like(ph_ref)
        fill_ref[...] = jnp.full_like(fill_ref, TM)
        base_ref[...] = jnp.zeros_like(base_ref)
        np_ref[...] = jnp.zeros_like(np_ref)
        pexp_ref[...] = jnp.zeros_like(pexp_ref)
        if not first_layer:
            _start_run_gather(runsp_ref, 0, yp_ref, gbuf, gsem)

    def _close_pages():
        copies = [pltpu.make_async_copy(fill_ref, fill_s, msem), pltpu.make_async_copy(base_ref, base_s, msem),
                  pltpu.make_async_copy(np_ref, np_s, msem), pltpu.make_async_copy(pexp_ref, pexp_s, msem)]
        for c in copies:
            c.start()
        for c in copies:
            c.wait()
        npages = np_s[0, 0]

        def zero_cnt(e, c):
            cnt_s[e] = 0
            return c
        lax.fori_loop(0, N_EXPERTS, zero_cnt, 0)

        def count(p, c):
            e = pexp_s[0, p]
            cnt_s[e] = cnt_s[e] + 1
            return c
        lax.fori_loop(0, npages, count, 0)

        def prefix(e, run):
            n = cnt_s[e]
            cnt_s[e] = run
            return run + n
        lax.fori_loop(0, N_EXPERTS, prefix, 0)

        def emit(p, c):
            e = pexp_s[0, p]
            q = cnt_s[e]
            cnt_s[e] = q + 1
            meta_ref[0, q] = p
            meta_ref[1, q] = e
            return c
        lax.fori_loop(0, npages, emit, 0)

        def pad_meta(q, c):
            meta_ref[0, q] = jnp.minimum(q, N_PAGES - 1)
            meta_ref[1, q] = meta_ref[1, npages - 1]
            return c
        lax.fori_loop(npages, PAGE_LANES, pad_meta, 0)

        def fill_row2(q, c):
            meta_ref[2, q] = npages
            return c
        lax.fori_loop(0, PAGE_LANES, fill_row2, 0)

        h2buf[...] = jnp.zeros((TL, 1, D_MODEL), jnp.float32)

        def tails(start):
            def tail(e, c):
                f = fill_s[e, 0]
                b = base_s[e, 0]

                @pl.when(f < TM)
                def _zero_tail():
                    cp = pltpu.make_async_copy(h2buf.at[pl.ds(0, TM - f)], hs_ref.at[pl.ds(b + f, TM - f)], rsem)
                    cp.start() if start else cp.wait()
                return c
            lax.fori_loop(0, N_EXPERTS, tail, 0)

            def unused(p, c):
                cp = pltpu.make_async_copy(h2buf.at[pl.ds(0, TM)], hs_ref.at[pl.ds(p * TM, TM)], rsem)
                cp.start() if start else cp.wait()
                return c
            lax.fori_loop(npages, N_PAGES, unused, 0)

        tails(True)
        tails(False)

    prev_slot = (i + 1) % 2

    if not first_layer:
        @pl.when(i >= 0)
        def _results():
            _unsort_results(lposp_ref, i, gbuf, y0buf, y1buf, gsem)

            @pl.when(i + 1 < N_TILES)
            def _():
                _start_run_gather(runsp_ref, i + 1, yp_ref, gbuf, gsem)

    @pl.when(i < N_PROMPT_TILES)
    def _prompt_project():
        x = load_x(True)
        xmid_ref[...] = x
        proj_ref[...] = project(x)

    @pl.when(i <= N_PROMPT_TILES - 1)
    def _prompt_mix():
        s = i % TILES_PER_SEQ
        zhist = jnp.where(s == 0, 0.0, zh_ref[...])
        phist = jnp.where(s == 0, 0.0, ph_ref[...])
        mix, znew, pnew = _mix_rows(proj_ref[...], zhist, phist, s * TL, TL, convw_ref[...], poolbd_ref[...],
                                    pscale_ref[...], masked_wm(), bias_ref[...])
        zh_ref[...] = znew
        ph_ref[...] = pnew
        cpr_ref[0] = znew
        ppr_ref[0] = pnew
        mix_ref[...] = mix.astype(jnp.bfloat16)

    @pl.when((i >= 1) & (i < N_PROMPT_TILES))
    def _flush_previous():
        flush_tile(i - 1, prev_slot)

    @pl.when(i < N_PROMPT_TILES)
    def _prompt_finish():
        finish(xmid_ref[...], mix_ref[...])

    @pl.when(i == N_PROMPT_TILES)
    def _sample():
        x = load_x(False)
        proj = project(x)
        wm = masked_wm()
        mixes = []
        for b in range(DEC_BATCH):
            rows = slice(b * DEC_SEQ, (b + 1) * DEC_SEQ)
            mix, znew, pnew = _mix_rows(proj[rows, :], sconv_ref[b], spool_ref[b], PAST_LEN, DEC_SEQ,
                                        convw_ref[...], poolbd_ref[...], pscale_ref[...], wm, bias_ref[...])
            csm_ref[b] = znew
            psm_ref[b] = pnew
            mixes.append(mix)
        sv_ref[...] = proj[:, 1792:2176]
        mix_ref[...] = jnp.concatenate(mixes, axis=0).astype(jnp.bfloat16)
        xmid_ref[...] = x
        flush_tile(i - 1, prev_slot)
        finish(xmid_ref[...], mix_ref[...])
        flush_tile(i, i % 2)
        wait_rows()
        _close_pages()


def _mixer_call(first_layer, xs, sconv_pad, spool_pad, g1, w_in, conv_w, pool_bd, pool_scale, wm_all,
                bias_full, w_out, g2, wr_t, br_col, triu, tril_e):
    tile = lambda i, *_: (i, 0)
    prompt_tile = lambda i, *_: (jnp.minimum(i, N_PROMPT_TILES - 1), 0)
    const2 = lambda i, *_: (0, 0)
    const3 = lambda i, *_: (0, 0, 0)
    if first_layer:
        prefetch = ()
        x_specs = [pl.BlockSpec((TL, D_MODEL), prompt_tile), pl.BlockSpec((TL, D_MODEL), const2)]
    else:
        prefetch = xs[0:2]
        xs = xs[2:]
        x_specs = [pl.BlockSpec((TL, D_MODEL), tile), pl.BlockSpec(memory_space=pl.ANY),
                   pl.BlockSpec((TL, 128), tile)]
    full = lambda a: pl.BlockSpec(a.shape, const2 if a.ndim == 2 else const3)
    weights = [sconv_pad, spool_pad, g1, w_in, conv_w, pool_bd, pool_scale, wm_all, bias_full, w_out, g2,
               wr_t, br_col, triu, tril_e]
    in_specs = x_specs + [full(a) for a in weights]
    seq_of = lambda i, *_: (jnp.minimum(i // TILES_PER_SEQ, BATCH - 1), 0, 0)
    out_shape = [
        jax.ShapeDtypeStruct((T_ALL, D_MODEL), jnp.float32),
        jax.ShapeDtypeStruct((PAGE_ROWS, 1, D_MODEL), jnp.float32),
        jax.ShapeDtypeStruct((N_TILES, 2, TL), jnp.int32),
        jax.ShapeDtypeStruct((N_TILES * RUNS_PER_TILE,), jnp.int32),
        jax.ShapeDtypeStruct((3, PAGE_LANES), jnp.int32),
        jax.ShapeDtypeStruct((T_ALL, 128), jnp.float32),
        jax.ShapeDtypeStruct((BATCH, HIST_ROWS, D_CONV), jnp.float32),
        jax.ShapeDtypeStruct((BATCH, HIST_ROWS, D_POOL), jnp.float32),
        jax.ShapeDtypeStruct((DEC_BATCH, HIST_ROWS, D_CONV), jnp.float32),
        jax.ShapeDtypeStruct((DEC_BATCH, HIST_ROWS, D_POOL), jnp.float32),
        jax.ShapeDtypeStruct((T_SAMPLE, D_SGU), jnp.float32),
    ]
    out_specs = [
        pl.BlockSpec((TL, D_MODEL), tile),
        pl.BlockSpec(memory_space=pl.ANY),
        pl.BlockSpec((1, 2, TL), lambda i, *_: (i, 0, 0)),
        pl.BlockSpec(memory_space=pltpu.SMEM),
        pl.BlockSpec(memory_space=pltpu.SMEM),
        pl.BlockSpec((TL, 128), tile),
        pl.BlockSpec((1, HIST_ROWS, D_CONV), seq_of),
        pl.BlockSpec((1, HIST_ROWS, D_POOL), seq_of),
        pl.BlockSpec((DEC_BATCH, HIST_ROWS, D_CONV), const3),
        pl.BlockSpec((DEC_BATCH, HIST_ROWS, D_POOL), const3),
        pl.BlockSpec((T_SAMPLE, D_SGU), const2),
    ]
    scratch = [
        pltpu.VMEM((HIST_ROWS, D_CONV), jnp.float32),
        pltpu.VMEM((HIST_ROWS, D_POOL), jnp.float32),
        pltpu.VMEM((TL, 1, D_MODEL), jnp.float32),
        pltpu.VMEM((2 * TL, 1, D_MODEL), jnp.float32),
        pltpu.VMEM((TL, D_PROJ), jnp.float32),
        pltpu.VMEM((TL, D_MODEL), jnp.bfloat16),
        pltpu.VMEM((8, TL), jnp.int32),
        pltpu.VMEM((8, 128), jnp.int32),
        pltpu.SMEM((2, 8, TL), jnp.int32),
        pltpu.SMEM((2, 8, 128), jnp.int32),
        pltpu.VMEM((N_EXPERTS, 128), jnp.int32),
        pltpu.VMEM((N_EXPERTS, 128), jnp.int32),
        pltpu.VMEM((8, 128), jnp.int32),
        pltpu.VMEM((8, PAGE_LANES), jnp.int32),
        pltpu.SMEM((N_EXPERTS, 128), jnp.int32),
        pltpu.SMEM((N_EXPERTS, 128), jnp.int32),
        pltpu.SMEM((8, 128), jnp.int32),
        pltpu.SMEM((8, PAGE_LANES), jnp.int32),
        pltpu.SMEM((N_EXPERTS,), jnp.int32),
        pltpu.SemaphoreType.DMA(()),
        pltpu.SemaphoreType.DMA(()),
    ]
    if not first_layer:
        scratch += [
            pltpu.VMEM((2 * TL, 1, D_MODEL), jnp.float32),
            pltpu.VMEM((TL, 1, D_MODEL), jnp.float32),
            pltpu.VMEM((TL, 1, D_MODEL), jnp.float32),
            pltpu.SemaphoreType.DMA(()),
        ]
    grid_spec = pltpu.PrefetchScalarGridSpec(num_scalar_prefetch=len(prefetch), grid=(N_TILES,),
                                             in_specs=in_specs, out_specs=out_specs, scratch_shapes=scratch)
    return pl.pallas_call(
        functools.partial(_mixer_kernel, first_layer),
        grid_spec=grid_spec,
        out_shape=out_shape,
        compiler_params=pltpu.CompilerParams(dimension_semantics=("arbitrary",),
                                             vmem_limit_bytes=VMEM_LIMIT),
        name="mixer_first" if first_layer else "mixer_next",
    )(*prefetch, *xs, *weights)


def _expert_kernel(meta_ref, hs_ref, wg_ref, wu_ref, wd_ref, y_ref, xbuf):
    s = pl.program_id(0)
    npages = meta_ref[2, 0]

    @pl.when(s < npages)
    def _page():
        xbuf[...] = hs_ref[...].reshape(TM, D_MODEL)
        x = xbuf[...].astype(jnp.bfloat16)
        hg = jnp.dot(x, wg_ref[0, 0].astype(jnp.bfloat16), preferred_element_type=jnp.float32)
        hu = jnp.dot(x, wu_ref[0, 0].astype(jnp.bfloat16), preferred_element_type=jnp.float32)
        h = (hg * jax.nn.sigmoid(hg) * hu).astype(jnp.bfloat16)
        out = jnp.dot(h, wd_ref[0, 0].astype(jnp.bfloat16), preferred_element_type=jnp.float32)
        y_ref[...] = out.reshape(TM, 1, D_MODEL)

    @pl.when(s >= npages)
    def _unused_page():
        y_ref[...] = jnp.zeros((TM, 1, D_MODEL), jnp.float32)


def _expert_call(layer, meta, hs, w_gate, w_up, w_down):
    w_idx = lambda s, meta: (layer, meta[1, s], 0, 0)
    page = lambda s, meta: (meta[0, s], 0, 0)
    grid_spec = pltpu.PrefetchScalarGridSpec(
        num_scalar_prefetch=1,
        grid=(N_PAGES,),
        in_specs=[
            pl.BlockSpec((TM, 1, D_MODEL), page),
            pl.BlockSpec((1, 1, D_MODEL, D_EXPERT), w_idx),
            pl.BlockSpec((1, 1, D_MODEL, D_EXPERT), w_idx),
            pl.BlockSpec((1, 1, D_EXPERT, D_MODEL), w_idx),
        ],
        out_specs=pl.BlockSpec((TM, 1, D_MODEL), page),
        scratch_shapes=[pltpu.VMEM((TM, D_MODEL), jnp.float32)],
    )
    return pl.pallas_call(
        _expert_kernel,
        grid_spec=grid_spec,
        out_shape=jax.ShapeDtypeStruct((PAGE_ROWS, 1, D_MODEL), jnp.float32),
        compiler_params=pltpu.CompilerParams(dimension_semantics=("arbitrary",),
                                             vmem_limit_bytes=VMEM_LIMIT),
        name="experts",
    )(meta, hs, w_gate, w_up, w_down)


def _final_kernel(lposp_ref, runsp_ref, xm_ref, yp_ref, gc_ref, g_ref, yp_out, ys_out, gbuf, y0buf, y1buf, gsem):
    i = pl.program_id(0)

    @pl.when(i == 0)
    def _():
        _start_run_gather(runsp_ref, 0, yp_ref, gbuf, gsem)

    _unsort_results(lposp_ref, i, gbuf, y0buf, y1buf, gsem)

    @pl.when(i + 1 < N_TILES)
    def _():
        _start_run_gather(runsp_ref, i + 1, yp_ref, gbuf, gsem)

    g = gc_ref[...]
    y0 = y0buf[...].reshape(TL, D_MODEL)
    y1 = y1buf[...].reshape(TL, D_MODEL)
    x = xm_ref[...] + g[:, 0:1] * y0 + g[:, 1:2] * y1
    out = _rms(x, g_ref[...])

    @pl.when(i < N_PROMPT_TILES)
    def _():
        yp_out[...] = out

    @pl.when(i == N_PROMPT_TILES)
    def _():
        ys_out[...] = out


def _final_call(lpos, runs, x_mid, y_pages, gcol, g):
    tile = lambda i, *_: (i, 0)
    grid_spec = pltpu.PrefetchScalarGridSpec(
        num_scalar_prefetch=2,
        grid=(N_TILES,),
        in_specs=[pl.BlockSpec((TL, D_MODEL), tile), pl.BlockSpec(memory_space=pl.ANY),
                  pl.BlockSpec((TL, 128), tile), pl.BlockSpec((1, D_MODEL), lambda i, *_: (0, 0))],
        out_specs=[pl.BlockSpec((TL, D_MODEL), lambda i, *_: (jnp.minimum(i, N_PROMPT_TILES - 1), 0)),
                   pl.BlockSpec((TL, D_MODEL), lambda i, *_: (0, 0))],
        scratch_shapes=[pltpu.VMEM((2 * TL, 1, D_MODEL), jnp.float32), pltpu.VMEM((TL, 1, D_MODEL), jnp.float32),
                        pltpu.VMEM((TL, 1, D_MODEL), jnp.float32), pltpu.SemaphoreType.DMA(())],
    )
    return pl.pallas_call(
        _final_kernel,
        grid_spec=grid_spec,
        out_shape=[jax.ShapeDtypeStruct((T_PROMPT, D_MODEL), jnp.float32),
                   jax.ShapeDtypeStruct((T_SAMPLE, D_MODEL), jnp.float32)],
        compiler_params=pltpu.CompilerParams(dimension_semantics=("arbitrary",),
                                             vmem_limit_bytes=VMEM_LIMIT),
        name="final_norm",
    )(lpos, runs, x_mid, y_pages, gcol, g)


def kernel(x_prompt, x_sample, state_conv, state_pool, norm1_g, w_in, conv_w, pool_w, pool_scale, sgu_w, sgu_b, w_out, norm2_g, router_coarse_w, router_coarse_b, router_fine_w, router_fine_b, moe_w_gate, moe_w_up, moe_w_down, final_norm_g):
    bf16 = jnp.bfloat16
    xs = (x_prompt.reshape(T_PROMPT, D_MODEL), x_sample.reshape(T_SAMPLE, D_MODEL))
    sconv_pad = jnp.pad(state_conv, ((0, 0), (0, 0), (HIST_ROWS - (CONV_WIDTH - 1), 0), (0, 0)))
    spool_pad = jnp.pad(state_pool, ((0, 0), (0, 0), (HIST_ROWS - POOL_HIST, 0), (0, 0)))
    idx = jnp.arange(TL, dtype=jnp.int32)
    triu = (idx[:, None] < idx[None, :]).astype(bf16)
    ide = jnp.arange(N_EXPERTS, dtype=jnp.int32)
    tril_e = (ide[None, :] < ide[:, None]).astype(bf16)
    conv_pr, pool_pr, conv_sm, pool_sm, sgu_v = [], [], [], [], []
    x_mid = y_pages = gcol = lpos = runs = None
    for l in range(DEPTH):
        pool_bd = jax.scipy.linalg.block_diag(*[pool_w[l, g] for g in range(4)]).astype(bf16)
        wm_all = sgu_w[l].reshape(SGU_HEADS * SGU_LEN, SGU_LEN).astype(bf16)
        bias_full = jnp.repeat(sgu_b[l].T, SGU_HEAD_DIM, axis=1)
        wr_t = jnp.zeros((ROUTER_ROWS, D_MODEL), jnp.float32)
        wr_t = wr_t.at[0:N_GROUPS].set(router_coarse_w[l].T).at[8:].set(router_fine_w[l].T).astype(bf16)
        br_col = jnp.zeros((ROUTER_ROWS, 1), jnp.float32)
        br_col = br_col.at[0:N_GROUPS, 0].set(router_coarse_b[l]).at[8:, 0].set(router_fine_b[l])
        outs = _mixer_call(l == 0, xs, sconv_pad[l], spool_pad[l], norm1_g[l].reshape(1, D_MODEL),
                           w_in[l].astype(bf16), conv_w[l], pool_bd, pool_scale[l].reshape(1, D_POOL),
                           wm_all, bias_full, w_out[l].astype(bf16), norm2_g[l].reshape(1, D_MODEL),
                           wr_t, br_col, triu, tril_e)
        x_mid, hs, lpos, runs, meta, gcol, cpr, ppr, csm, psm, sv = outs
        lpos = lpos.reshape(N_TILES * 2 * TL)
        conv_pr.append(cpr[:, HIST_ROWS - (CONV_WIDTH - 1):, :])
        pool_pr.append(ppr[:, HIST_ROWS - POOL_HIST:, :])
        conv_sm.append(csm[:, HIST_ROWS - (CONV_WIDTH - 1):, :])
        pool_sm.append(psm[:, HIST_ROWS - POOL_HIST:, :])
        sgu_v.append(sv.reshape(DEC_BATCH, DEC_SEQ, D_SGU))
        y_pages = _expert_call(l, meta, hs, moe_w_gate, moe_w_up, moe_w_down)
        xs = (lpos, runs, x_mid, y_pages, gcol)
    y_prompt, y_sample = _final_call(lpos, runs, x_mid, y_pages, gcol, final_norm_g.reshape(1, D_MODEL))
    return (y_prompt.reshape(BATCH, SEQ, D_MODEL), y_sample.reshape(DEC_BATCH, DEC_SEQ, D_MODEL),
            jnp.stack(conv_pr), jnp.stack(pool_pr), jnp.stack(conv_sm), jnp.stack(pool_sm),
            jnp.stack(sgu_v))
```

```python
import functools

import jax
import jax.numpy as jnp
from jax import lax
from jax.experimental import pallas as pl
from jax.experimental.pallas import tpu as pltpu

D_MODEL = 1024
BATCH = 8
SEQ = 2048
DEPTH = 2
DEC_BATCH = 8
DEC_SEQ = 64
PAST_LEN = 1024
D_CONV = 384
CONV_WIDTH = 3
D_POOL = 256
POOL_HIST = 15
D_SGU = 384
SGU_HEADS = 4
SGU_HEAD_DIM = 96
SGU_LEN = 128
D_PROJ = 2176
N_GROUPS = 4
EXPERTS_PER_GROUP = 8
N_EXPERTS = 32
D_EXPERT = 512
EPS = 1e-6

T_PROMPT = BATCH * SEQ
T_SAMPLE = DEC_BATCH * DEC_SEQ
T_ALL = T_PROMPT + T_SAMPLE
TL = 512
TILES_PER_SEQ = SEQ // TL
N_PROMPT_TILES = T_PROMPT // TL
N_TILES = N_PROMPT_TILES + T_SAMPLE // TL
HIST_ROWS = 16
ROUTER_ROWS = 8 + N_EXPERTS
TM = 256
N_ASSIGN = 2 * T_ALL
N_PAGES = N_ASSIGN // TM + N_EXPERTS
PAGE_LANES = 256
PAGE_ROWS = N_PAGES * TM
RUN_FIELDS = 9
RUNS_PER_TILE = N_EXPERTS * RUN_FIELDS
VMEM_LIMIT = 56 * 1024 * 1024

assert TL == 2 * TM and N_PAGES <= PAGE_LANES and T_ALL == N_TILES * TL


def _rms(x, g):
    return x * lax.rsqrt(jnp.mean(x * x, axis=-1, keepdims=True) + EPS) * g


def _mix_rows(proj, zhist, phist, pos0, n, conv_w, pool_bd, pool_scale, wm_all, bias_full):
    a_b = proj[:, 0:384]
    a_c = proj[:, 384:768]
    a_h = proj[:, 768:1152]
    p_in = proj[:, 1152:1408]
    s_u = proj[:, 1408:1792]
    s_v = proj[:, 1792:2176]

    z = a_c * a_h
    zext = jnp.concatenate([zhist, z], axis=0)
    conv_y = (conv_w[0:1, :] * pltpu.roll(zext, 2, 0)[HIST_ROWS:, :]
              + conv_w[1:2, :] * pltpu.roll(zext, 1, 0)[HIST_ROWS:, :]
              + conv_w[2:3, :] * z)
    a_out = a_b * conv_y

    pext = jnp.concatenate([phist, p_in], axis=0)
    s2 = pext + pltpu.roll(pext, 1, 0)
    s4 = s2 + pltpu.roll(s2, 2, 0)
    s8 = s4 + pltpu.roll(s4, 4, 0)
    s16 = s8 + pltpu.roll(s8, 8, 0)
    lane = lax.broadcasted_iota(jnp.int32, (1, D_POOL), 1)
    wsum = jnp.where(lane < 64, s2, jnp.where(lane < 128, s4, jnp.where(lane < 192, s8, s16)))
    wsum = wsum[HIST_ROWS:, :]
    win = jnp.where(lane < 64, 2.0, jnp.where(lane < 128, 4.0, jnp.where(lane < 192, 8.0, 16.0)))
    pos = (pos0 + lax.broadcasted_iota(jnp.int32, (n, 1), 0) + 1).astype(jnp.float32)
    cnt = jnp.minimum(win, pos)
    pooled = wsum / cnt - p_in
    p_out = jnp.dot(pooled.astype(jnp.bfloat16), pool_bd,
                    preferred_element_type=jnp.float32) * pool_scale

    lane_s = lax.broadcasted_iota(jnp.int32, (1, D_SGU), 1)
    chunk = min(n, SGU_LEN)
    if chunk == SGU_LEN:
        wm = wm_all
    else:
        wm = jnp.concatenate([wm_all[h * SGU_LEN:h * SGU_LEN + chunk, 0:chunk]
                              for h in range(SGU_HEADS)], axis=0)
    s_rows = []
    for c in range(n // chunk):
        v_c = s_v[c * chunk:(c + 1) * chunk, :].astype(jnp.bfloat16)
        r = jnp.dot(wm, v_c, preferred_element_type=jnp.float32)
        s_c = jnp.where(lane_s < 96, r[0:chunk],
                        jnp.where(lane_s < 192, r[chunk:2 * chunk],
                                  jnp.where(lane_s < 288, r[2 * chunk:3 * chunk], r[3 * chunk:4 * chunk])))
        s_rows.append(s_c + bias_full[0:chunk, :])
    s_gate = s_rows[0] if len(s_rows) == 1 else jnp.concatenate(s_rows, axis=0)
    s_out = s_u * s_gate

    mix = jnp.concatenate([a_out, p_out, s_out], axis=-1)
    return mix, zext[n:n + HIST_ROWS, :], pext[n:n + HIST_ROWS, :]


def _route(h2_bf16, wr_t, br_col, n):
    logits = lax.dot_general(wr_t, h2_bf16, (((1,), (1,)), ((), ())),
                             preferred_element_type=jnp.float32) + br_col
    row8 = lax.broadcasted_iota(jnp.int32, (8, n), 0)
    lc = jnp.where(row8 < N_GROUPS, logits[0:8, :], -jnp.inf)
    mc = jnp.max(lc, axis=0, keepdims=True)
    g_sel = jnp.min(jnp.where(lc == mc, row8, 8), axis=0, keepdims=True)
    p_sel = 1.0 / jnp.sum(jnp.exp(lc - mc), axis=0, keepdims=True)
    lf = logits[8 + 3 * EXPERTS_PER_GROUP:8 + 4 * EXPERTS_PER_GROUP, :]
    for g in (2, 1, 0):
        lf = jnp.where(g_sel == g, logits[8 + g * EXPERTS_PER_GROUP:8 + (g + 1) * EXPERTS_PER_GROUP, :], lf)
    m1 = jnp.max(lf, axis=0, keepdims=True)
    i1 = jnp.min(jnp.where(lf == m1, row8, 8), axis=0, keepdims=True)
    lf2 = jnp.where(row8 == i1, -jnp.inf, lf)
    m2 = jnp.max(lf2, axis=0, keepdims=True)
    i2 = jnp.min(jnp.where(lf2 == m2, row8, 8), axis=0, keepdims=True)
    t = jnp.exp(m2 - m1)
    wa = 1.0 / (1.0 + t)
    wb = t / (1.0 + t)
    e0 = g_sel * EXPERTS_PER_GROUP + i1
    e1 = g_sel * EXPERTS_PER_GROUP + i2
    return e0, e1, p_sel * wa, p_sel * wb


def _place_rows(e0, e1, triu, tril_e, fill_ref, base_ref, np_ref, pexp_ref):
    row_e = lax.broadcasted_iota(jnp.int32, (N_EXPERTS, TL), 0)
    oh0 = row_e == e0
    oh1 = row_e == e1
    oh = jnp.where(oh0 | oh1, 1.0, 0.0)
    rank = jnp.dot(oh.astype(jnp.bfloat16), triu, preferred_element_type=jnp.float32).astype(jnp.int32)
    cnt = jnp.sum(oh, axis=1, keepdims=True).astype(jnp.int32)
    lower = jnp.where(e0 < row_e, 1.0, 0.0) + jnp.where(e1 < row_e, 1.0, 0.0)
    first = jnp.sum(lower, axis=1, keepdims=True).astype(jnp.int32)
    sorted_all = first + rank
    lpos0 = jnp.sum(jnp.where(oh0, sorted_all, 0), axis=0, keepdims=True)
    lpos1 = jnp.sum(jnp.where(oh1, sorted_all, 0), axis=0, keepdims=True)

    fill = fill_ref[:, 0:1]
    base = base_ref[:, 0:1]
    npages = np_ref[0:1, 0:1]
    total = fill + cnt
    need = (total > TM).astype(jnp.int32) + (total > 2 * TM).astype(jnp.int32)
    need_b = jnp.broadcast_to(need.astype(jnp.float32), (N_EXPERTS, 128)).astype(jnp.bfloat16)
    before = jnp.dot(tril_e, need_b, preferred_element_type=jnp.float32)[:, 0:1].astype(jnp.int32)
    new_id = npages + before
    new_base = new_id * TM
    fill_ref[...] = jnp.broadcast_to(total - need * TM, (N_EXPERTS, 128))
    base_ref[...] = jnp.broadcast_to(jnp.where(need > 0, new_base + (need - 1) * TM, base), (N_EXPERTS, 128))
    np_ref[...] = jnp.broadcast_to(npages + jnp.sum(need, axis=0, keepdims=True), (8, 128))
    page_lane = lax.broadcasted_iota(jnp.int32, (N_EXPERTS, PAGE_LANES), 1)
    expert_col = lax.broadcasted_iota(jnp.int32, (N_EXPERTS, 1), 0)
    owns = ((page_lane == new_id) & (need >= 1)) | ((page_lane == new_id + 1) & (need == 2))
    pexp_ref[...] = pexp_ref[...] + jnp.sum(jnp.where(owns, expert_col, 0), axis=0, keepdims=True)

    lane = lax.broadcasted_iota(jnp.int32, (N_EXPERTS, 128), 1)
    cols = jnp.where(lane == 0, cnt, jnp.where(lane == 1, first, jnp.where(lane == 2, fill,
                     jnp.where(lane == 3, base, jnp.where(lane == 4, new_base, 0)))))
    square = jnp.concatenate([cols, jnp.zeros((128 - N_EXPERTS, 128), jnp.int32)], axis=0)
    per_expert = square.astype(jnp.float32).T.astype(jnp.int32)[0:8, :]
    return lpos0, lpos1, per_expert


def _run_pieces(cnt, first, fill, base, new_base):
    n0 = jnp.minimum(cnt, TM - fill)
    n1 = jnp.minimum(cnt - n0, TM)
    n2 = cnt - n0 - n1
    return ((first, base + fill, n0), (first + n0, new_base, n1), (first + n0 + n1, new_base + TM, n2))


def _start_run_gather(runs_ref, tile, pages_ref, gbuf, gsem):
    def per_expert(e, c):
        k0 = (tile * N_EXPERTS + e) * RUN_FIELDS
        for j in range(3):
            src = runs_ref[k0 + 3 * j]
            dst = runs_ref[k0 + 3 * j + 1]
            n = runs_ref[k0 + 3 * j + 2]

            @pl.when(n > 0)
            def _():
                pltpu.make_async_copy(pages_ref.at[pl.ds(dst, n)], gbuf.at[pl.ds(src, n)], gsem).start()
        return c
    lax.fori_loop(0, N_EXPERTS, per_expert, 0)


def _unsort_results(lpos_ref, tile, gbuf, y0buf, y1buf, gsem):
    pltpu.make_async_copy(gbuf, gbuf, gsem).wait()
    k0 = tile * 2 * TL
    for r in range(TL):
        y0buf[r] = gbuf[lpos_ref[k0 + r]]
        y1buf[r] = gbuf[lpos_ref[k0 + TL + r]]


def _mixer_kernel(first_layer, *refs):
    if first_layer:
        xp_ref, xs_ref = refs[0:2]
        rest = refs[2:]
    else:
        lposp_ref, runsp_ref, xm_ref, yp_ref, gc_ref = refs[0:5]
        rest = refs[5:]
    (sconv_ref, spool_ref, g1_ref, win_ref, convw_ref, poolbd_ref, pscale_ref, wm_ref, bias_ref,
     wout_ref, g2_ref, wr_ref, br_ref, triu_ref, trile_ref,
     xmid_ref, hs_ref, lpos_ref, runs_ref, meta_ref, gcol_ref, cpr_ref, ppr_ref, csm_ref, psm_ref, sv_ref,
     zh_ref, ph_ref, h2buf, sbuf, proj_ref, mix_ref, posv, pexv, poss, pexs, fill_ref, base_ref, np_ref,
     pexp_ref, fill_s, base_s, np_s, pexp_s, cnt_s, rsem, msem) = rest[:47]
    if not first_layer:
        gbuf, y0buf, y1buf, gsem = rest[47:]

    i = pl.program_id(0)

    def load_x(prompt):
        if first_layer:
            return xp_ref[...] if prompt else xs_ref[...]
        g = gc_ref[...]
        y0 = y0buf[...].reshape(TL, D_MODEL)
        y1 = y1buf[...].reshape(TL, D_MODEL)
        return xm_ref[...] + g[:, 0:1] * y0 + g[:, 1:2] * y1

    row_m = lax.broadcasted_iota(jnp.int32, (SGU_HEADS * SGU_LEN, SGU_LEN), 0) % SGU_LEN
    col_m = lax.broadcasted_iota(jnp.int32, (SGU_HEADS * SGU_LEN, SGU_LEN), 1)

    def wait_rows():
        pltpu.make_async_copy(sbuf, sbuf, rsem).wait()

    def flush_tile(tile, slot):
        pltpu.make_async_copy(posv, poss.at[slot], msem).wait()
        pltpu.make_async_copy(pexv, pexs.at[slot], msem).wait()

        @pl.when(tile >= 1)
        def _():
            wait_rows()
        for r in range(TL):
            row = h2buf[r]
            sbuf[poss[slot, 0, r]] = row
            sbuf[poss[slot, 1, r]] = row

        def per_expert(e, c):
            pieces = _run_pieces(pexs[slot, 0, e], pexs[slot, 1, e], pexs[slot, 2, e], pexs[slot, 3, e],
                                 pexs[slot, 4, e])
            k0 = (tile * N_EXPERTS + e) * RUN_FIELDS
            for j, (src, dst, n) in enumerate(pieces):
                runs_ref[k0 + 3 * j] = src
                runs_ref[k0 + 3 * j + 1] = dst
                runs_ref[k0 + 3 * j + 2] = n

                @pl.when(n > 0)
                def _():
                    pltpu.make_async_copy(sbuf.at[pl.ds(src, n)], hs_ref.at[pl.ds(dst, n)], rsem).start()
            return c
        lax.fori_loop(0, N_EXPERTS, per_expert, 0)

    def finish(x, mix):
        x_mid = x + jnp.dot(mix.astype(jnp.bfloat16), wout_ref[...], preferred_element_type=jnp.float32)
        xmid_ref[...] = x_mid
        h2 = _rms(x_mid, g2_ref[...])
        h2buf[...] = h2.reshape(TL, 1, D_MODEL)
        e0, e1, g0, g1 = _route(h2.astype(jnp.bfloat16), wr_ref[...], br_ref[...], TL)
        row128 = lax.broadcasted_iota(jnp.int32, (128, TL), 0)
        gcol_ref[...] = jnp.where(row128 == 0, g0, jnp.where(row128 == 1, g1, 0.0)).T
        lpos0, lpos1, per_expert = _place_rows(e0, e1, triu_ref[...], trile_ref[...], fill_ref, base_ref,
                                               np_ref, pexp_ref)
        row8 = lax.broadcasted_iota(jnp.int32, (8, TL), 0)
        posv[...] = jnp.where(row8 == 0, lpos0, jnp.where(row8 == 1, lpos1, 0))
        pexv[...] = per_expert
        row2 = lax.broadcasted_iota(jnp.int32, (2, TL), 0)
        lpos_ref[0] = jnp.where(row2 == 0, lpos0, lpos1)
        pltpu.make_async_copy(posv, poss.at[i % 2], msem).start()
        pltpu.make_async_copy(pexv, pexs.at[i % 2], msem).start()

    def project(x):
        h = _rms(x, g1_ref[...]).astype(jnp.bfloat16)
        return jnp.dot(h, win_ref[...], preferred_element_type=jnp.float32)

    def masked_wm():
        return jnp.where(col_m <= row_m, wm_ref[...], jnp.zeros_like(wm_ref[...]))

    @pl.when(i == 0)
    def _init():
        zh_ref[...] = jnp.zeros_like(zh_ref)
        ph_ref[...] = jnp.zeros_like(ph_ref)
        fill_ref[...] = jnp.full_like(fill_ref, TM)
        base_ref[...] = jnp.zeros_like(base_ref)
        np_ref[...] = jnp.zeros_like(np_ref)
        pexp_ref[...] = jnp.zeros_like(pexp_ref)
        if not first_layer:
            _start_run_gather(runsp_ref, 0, yp_ref, gbuf, gsem)

    def _close_pages():
        copies = [pltpu.make_async_copy(fill_ref, fill_s, msem), pltpu.make_async_copy(base_ref, base_s, msem),
                  pltpu.make_async_copy(np_ref, np_s, msem), pltpu.make_async_copy(pexp_ref, pexp_s, msem)]
        for c in copies:
            c.start()
        for c in copies:
            c.wait()
        npages = np_s[0, 0]

        def zero_cnt(e, c):
            cnt_s[e] = 0
            return c
        lax.fori_loop(0, N_EXPERTS, zero_cnt, 0)

        def count(p, c):
            e = pexp_s[0, p]
            cnt_s[e] = cnt_s[e] + 1
            return c
        lax.fori_loop(0, npages, count, 0)

        def prefix(e, run):
            n = cnt_s[e]
            cnt_s[e] = run
            return run + n
        lax.fori_loop(0, N_EXPERTS, prefix, 0)

        def emit(p, c):
            e = pexp_s[0, p]
            q = cnt_s[e]
            cnt_s[e] = q + 1
            meta_ref[0, q] = p
            meta_ref[1, q] = e
            return c
        lax.fori_loop(0, npages, emit, 0)

        def pad_meta(q, c):
            meta_ref[0, q] = jnp.minimum(q, N_PAGES - 1)
            meta_ref[1, q] = meta_ref[1, npages - 1]
            return c
        lax.fori_loop(npages, PAGE_LANES, pad_meta, 0)

        def fill_row2(q, c):
            meta_ref[2, q] = npages
            return c
        lax.fori_loop(0, PAGE_LANES, fill_row2, 0)

        def mark_first(q, seen):
            is_first = (q < npages) & ((q == 0) | (meta_ref[1, q] != meta_ref[1, jnp.maximum(q - 1, 0)]))
            seen = seen + jnp.where(is_first, 1, 0)
            meta_ref[3, q] = jnp.where(is_first, 1, 0)
            meta_ref[4, q] = jnp.maximum(seen - 1, 0)
            meta_ref[5, q] = -1
            return seen
        lax.fori_loop(0, PAGE_LANES, mark_first, 0)

        def mark_next(t, carry):
            cur, nxt = carry
            q = npages - 1 - t
            e = meta_ref[1, q]
            nxt = jnp.where(e != cur, cur, nxt)
            meta_ref[5, q] = nxt
            return e, nxt
        lax.fori_loop(0, npages, mark_next, (jnp.int32(-1), jnp.int32(-1)))

        h2buf[...] = jnp.zeros((TL, 1, D_MODEL), jnp.float32)

        def tails(start):
            def tail(e, c):
                f = fill_s[e, 0]
                b = base_s[e, 0]

                @pl.when(f < TM)
                def _zero_tail():
                    cp = pltpu.make_async_copy(h2buf.at[pl.ds(0, TM - f)], hs_ref.at[pl.ds(b + f, TM - f)], rsem)
                    cp.start() if start else cp.wait()
                return c
            lax.fori_loop(0, N_EXPERTS, tail, 0)

            def unused(p, c):
                cp = pltpu.make_async_copy(h2buf.at[pl.ds(0, TM)], hs_ref.at[pl.ds(p * TM, TM)], rsem)
                cp.start() if start else cp.wait()
                return c
            lax.fori_loop(npages, N_PAGES, unused, 0)

        tails(True)
        tails(False)

    prev_slot = (i + 1) % 2

    if not first_layer:
        @pl.when(i >= 0)
        def _results():
            _unsort_results(lposp_ref, i, gbuf, y0buf, y1buf, gsem)

            @pl.when(i + 1 < N_TILES)
            def _():
                _start_run_gather(runsp_ref, i + 1, yp_ref, gbuf, gsem)

    @pl.when(i < N_PROMPT_TILES)
    def _prompt_project():
        x = load_x(True)
        xmid_ref[...] = x
        proj_ref[...] = project(x)

    @pl.when(i <= N_PROMPT_TILES - 1)
    def _prompt_mix():
        s = i % TILES_PER_SEQ
        zhist = jnp.where(s == 0, 0.0, zh_ref[...])
        phist = jnp.where(s == 0, 0.0, ph_ref[...])
        mix, znew, pnew = _mix_rows(proj_ref[...], zhist, phist, s * TL, TL, convw_ref[...], poolbd_ref[...],
                                    pscale_ref[...], masked_wm(), bias_ref[...])
        zh_ref[...] = znew
        ph_ref[...] = pnew
        cpr_ref[0] = znew
        ppr_ref[0] = pnew
        mix_ref[...] = mix.astype(jnp.bfloat16)

    @pl.when((i >= 1) & (i < N_PROMPT_TILES))
    def _flush_previous():
        flush_tile(i - 1, prev_slot)

    @pl.when(i < N_PROMPT_TILES)
    def _prompt_finish():
        finish(xmid_ref[...], mix_ref[...])

    @pl.when(i == N_PROMPT_TILES)
    def _sample():
        x = load_x(False)
        proj = project(x)
        wm = masked_wm()
        mixes = []
        for b in range(DEC_BATCH):
            rows = slice(b * DEC_SEQ, (b + 1) * DEC_SEQ)
            mix, znew, pnew = _mix_rows(proj[rows, :], sconv_ref[b], spool_ref[b], PAST_LEN, DEC_SEQ,
                                        convw_ref[...], poolbd_ref[...], pscale_ref[...], wm, bias_ref[...])
            csm_ref[b] = znew
            psm_ref[b] = pnew
            mixes.append(mix)
        sv_ref[...] = proj[:, 1792:2176]
        mix_ref[...] = jnp.concatenate(mixes, axis=0).astype(jnp.bfloat16)
        xmid_ref[...] = x
        flush_tile(i - 1, prev_slot)
        finish(xmid_ref[...], mix_ref[...])
        flush_tile(i, i % 2)
        wait_rows()
        _close_pages()


def _mixer_call(first_layer, xs, sconv_pad, spool_pad, g1, w_in, conv_w, pool_bd, pool_scale, wm_all,
                bias_full, w_out, g2, wr_t, br_col, triu, tril_e):
    tile = lambda i, *_: (i, 0)
    prompt_tile = lambda i, *_: (jnp.minimum(i, N_PROMPT_TILES - 1), 0)
    const2 = lambda i, *_: (0, 0)
    const3 = lambda i, *_: (0, 0, 0)
    if first_layer:
        prefetch = ()
        x_specs = [pl.BlockSpec((TL, D_MODEL), prompt_tile), pl.BlockSpec((TL, D_MODEL), const2)]
    else:
        prefetch = xs[0:2]
        xs = xs[2:]
        x_specs = [pl.BlockSpec((TL, D_MODEL), tile), pl.BlockSpec(memory_space=pl.ANY),
                   pl.BlockSpec((TL, 128), tile)]
    full = lambda a: pl.BlockSpec(a.shape, const2 if a.ndim == 2 else const3)
    weights = [sconv_pad, spool_pad, g1, w_in, conv_w, pool_bd, pool_scale, wm_all, bias_full, w_out, g2,
               wr_t, br_col, triu, tril_e]
    in_specs = x_specs + [full(a) for a in weights]
    seq_of = lambda i, *_: (jnp.minimum(i // TILES_PER_SEQ, BATCH - 1), 0, 0)
    out_shape = [
        jax.ShapeDtypeStruct((T_ALL, D_MODEL), jnp.float32),
        jax.ShapeDtypeStruct((PAGE_ROWS, 1, D_MODEL), jnp.float32),
        jax.ShapeDtypeStruct((N_TILES, 2, TL), jnp.int32),
        jax.ShapeDtypeStruct((N_TILES * RUNS_PER_TILE,), jnp.int32),
        jax.ShapeDtypeStruct((6, PAGE_LANES), jnp.int32),
        jax.ShapeDtypeStruct((T_ALL, 128), jnp.float32),
        jax.ShapeDtypeStruct((BATCH, HIST_ROWS, D_CONV), jnp.float32),
        jax.ShapeDtypeStruct((BATCH, HIST_ROWS, D_POOL), jnp.float32),
        jax.ShapeDtypeStruct((DEC_BATCH, HIST_ROWS, D_CONV), jnp.float32),
        jax.ShapeDtypeStruct((DEC_BATCH, HIST_ROWS, D_POOL), jnp.float32),
        jax.ShapeDtypeStruct((T_SAMPLE, D_SGU), jnp.float32),
    ]
    out_specs = [
        pl.BlockSpec((TL, D_MODEL), tile),
        pl.BlockSpec(memory_space=pl.ANY),
        pl.BlockSpec((1, 2, TL), lambda i, *_: (i, 0, 0)),
        pl.BlockSpec(memory_space=pltpu.SMEM),
        pl.BlockSpec(memory_space=pltpu.SMEM),
        pl.BlockSpec((TL, 128), tile),
        pl.BlockSpec((1, HIST_ROWS, D_CONV), seq_of),
        pl.BlockSpec((1, HIST_ROWS, D_POOL), seq_of),
        pl.BlockSpec((DEC_BATCH, HIST_ROWS, D_CONV), const3),
        pl.BlockSpec((DEC_BATCH, HIST_ROWS, D_POOL), const3),
        pl.BlockSpec((T_SAMPLE, D_SGU), const2),
    ]
    scratch = [
        pltpu.VMEM((HIST_ROWS, D_CONV), jnp.float32),
        pltpu.VMEM((HIST_ROWS, D_POOL), jnp.float32),
        pltpu.VMEM((TL, 1, D_MODEL), jnp.float32),
        pltpu.VMEM((2 * TL, 1, D_MODEL), jnp.float32),
        pltpu.VMEM((TL, D_PROJ), jnp.float32),
        pltpu.VMEM((TL, D_MODEL), jnp.bfloat16),
        pltpu.VMEM((8, TL), jnp.int32),
        pltpu.VMEM((8, 128), jnp.int32),
        pltpu.SMEM((2, 8, TL), jnp.int32),
        pltpu.SMEM((2, 8, 128), jnp.int32),
        pltpu.VMEM((N_EXPERTS, 128), jnp.int32),
        pltpu.VMEM((N_EXPERTS, 128), jnp.int32),
        pltpu.VMEM((8, 128), jnp.int32),
        pltpu.VMEM((8, PAGE_LANES), jnp.int32),
        pltpu.SMEM((N_EXPERTS, 128), jnp.int32),
        pltpu.SMEM((N_EXPERTS, 128), jnp.int32),
        pltpu.SMEM((8, 128), jnp.int32),
        pltpu.SMEM((8, PAGE_LANES), jnp.int32),
        pltpu.SMEM((N_EXPERTS,), jnp.int32),
        pltpu.SemaphoreType.DMA(()),
        pltpu.SemaphoreType.DMA(()),
    ]
    if not first_layer:
        scratch += [
            pltpu.VMEM((2 * TL, 1, D_MODEL), jnp.float32),
            pltpu.VMEM((TL, 1, D_MODEL), jnp.float32),
            pltpu.VMEM((TL, 1, D_MODEL), jnp.float32),
            pltpu.SemaphoreType.DMA(()),
        ]
    grid_spec = pltpu.PrefetchScalarGridSpec(num_scalar_prefetch=len(prefetch), grid=(N_TILES,),
                                             in_specs=in_specs, out_specs=out_specs, scratch_shapes=scratch)
    return pl.pallas_call(
        functools.partial(_mixer_kernel, first_layer),
        grid_spec=grid_spec,
        out_shape=out_shape,
        compiler_params=pltpu.CompilerParams(dimension_semantics=("arbitrary",),
                                             vmem_limit_bytes=VMEM_LIMIT),
        name="mixer_first" if first_layer else "mixer_next",
    )(*prefetch, *xs, *weights)


def _expert_kernel(layer, meta_ref, hs_ref, wg_hbm, wu_hbm, wd_hbm, y_ref, xbuf, wg32, wu32, wd32, wg16, wu16,
                   wd16, wsem):
    s = pl.program_id(0)
    npages = meta_ref[2, 0]

    def weight_copies(expert, slot):
        return [pltpu.make_async_copy(wg_hbm.at[layer, expert], wg32.at[slot], wsem.at[slot]),
                pltpu.make_async_copy(wu_hbm.at[layer, expert], wu32.at[slot], wsem.at[slot]),
                pltpu.make_async_copy(wd_hbm.at[layer, expert], wd32.at[slot], wsem.at[slot])]

    @pl.when(s == 0)
    def _first_weights():
        for c in weight_copies(meta_ref[1, 0], 0):
            c.start()

    @pl.when((s < npages) & (meta_ref[3, s] == 1))
    def _switch_expert():
        slot = meta_ref[4, s] % 2
        for c in weight_copies(meta_ref[1, s], slot):
            c.wait()
        nxt = meta_ref[5, s]

        @pl.when(nxt >= 0)
        def _():
            for c in weight_copies(nxt, 1 - slot):
                c.start()
        wg16[...] = wg32[slot].astype(jnp.bfloat16)
        wu16[...] = wu32[slot].astype(jnp.bfloat16)
        wd16[...] = wd32[slot].astype(jnp.bfloat16)

    @pl.when(s < npages)
    def _page():
        xbuf[...] = hs_ref[...].reshape(TM, D_MODEL)
        x = xbuf[...].astype(jnp.bfloat16)
        hg = jnp.dot(x, wg16[...], preferred_element_type=jnp.float32)
        hu = jnp.dot(x, wu16[...], preferred_element_type=jnp.float32)
        h = (hg * jax.nn.sigmoid(hg) * hu).astype(jnp.bfloat16)
        out = jnp.dot(h, wd16[...], preferred_element_type=jnp.float32)
        y_ref[...] = out.reshape(TM, 1, D_MODEL)

    @pl.when(s >= npages)
    def _unused_page():
        y_ref[...] = jnp.zeros((TM, 1, D_MODEL), jnp.float32)


def _expert_call(layer, meta, hs, w_gate, w_up, w_down):
    page = lambda s, meta: (meta[0, s], 0, 0)
    grid_spec = pltpu.PrefetchScalarGridSpec(
        num_scalar_prefetch=1,
        grid=(N_PAGES,),
        in_specs=[
            pl.BlockSpec((TM, 1, D_MODEL), page),
            pl.BlockSpec(memory_space=pl.ANY),
            pl.BlockSpec(memory_space=pl.ANY),
            pl.BlockSpec(memory_space=pl.ANY),
        ],
        out_specs=pl.BlockSpec((TM, 1, D_MODEL), page),
        scratch_shapes=[
            pltpu.VMEM((TM, D_MODEL), jnp.float32),
            pltpu.VMEM((2, D_MODEL, D_EXPERT), jnp.float32),
            pltpu.VMEM((2, D_MODEL, D_EXPERT), jnp.float32),
            pltpu.VMEM((2, D_EXPERT, D_MODEL), jnp.float32),
            pltpu.VMEM((D_MODEL, D_EXPERT), jnp.bfloat16),
            pltpu.VMEM((D_MODEL, D_EXPERT), jnp.bfloat16),
            pltpu.VMEM((D_EXPERT, D_MODEL), jnp.bfloat16),
            pltpu.SemaphoreType.DMA((2,)),
        ],
    )
    return pl.pallas_call(
        functools.partial(_expert_kernel, layer),
        grid_spec=grid_spec,
        out_shape=jax.ShapeDtypeStruct((PAGE_ROWS, 1, D_MODEL), jnp.float32),
        compiler_params=pltpu.CompilerParams(dimension_semantics=("arbitrary",),
                                             vmem_limit_bytes=VMEM_LIMIT),
        name="experts",
    )(meta, hs, w_gate, w_up, w_down)


def _final_kernel(lposp_ref, runsp_ref, xm_ref, yp_ref, gc_ref, g_ref, yp_out, ys_out, gbuf, y0buf, y1buf, gsem):
    i = pl.program_id(0)

    @pl.when(i == 0)
    def _():
        _start_run_gather(runsp_ref, 0, yp_ref, gbuf, gsem)

    _unsort_results(lposp_ref, i, gbuf, y0buf, y1buf, gsem)

    @pl.when(i + 1 < N_TILES)
    def _():
        _start_run_gather(runsp_ref, i + 1, yp_ref, gbuf, gsem)

    g = gc_ref[...]
    y0 = y0buf[...].reshape(TL, D_MODEL)
    y1 = y1buf[...].reshape(TL, D_MODEL)
    x = xm_ref[...] + g[:, 0:1] * y0 + g[:, 1:2] * y1
    out = _rms(x, g_ref[...])

    @pl.when(i < N_PROMPT_TILES)
    def _():
        yp_out[...] = out

    @pl.when(i == N_PROMPT_TILES)
    def _():
        ys_out[...] = out


def _final_call(lpos, runs, x_mid, y_pages, gcol, g):
    tile = lambda i, *_: (i, 0)
    grid_spec = pltpu.PrefetchScalarGridSpec(
        num_scalar_prefetch=2,
        grid=(N_TILES,),
        in_specs=[pl.BlockSpec((TL, D_MODEL), tile), pl.BlockSpec(memory_space=pl.ANY),
                  pl.BlockSpec((TL, 128), tile), pl.BlockSpec((1, D_MODEL), lambda i, *_: (0, 0))],
        out_specs=[pl.BlockSpec((TL, D_MODEL), lambda i, *_: (jnp.minimum(i, N_PROMPT_TILES - 1), 0)),
                   pl.BlockSpec((TL, D_MODEL), lambda i, *_: (0, 0))],
        scratch_shapes=[pltpu.VMEM((2 * TL, 1, D_MODEL), jnp.float32), pltpu.VMEM((TL, 1, D_MODEL), jnp.float32),
                        pltpu.VMEM((TL, 1, D_MODEL), jnp.float32), pltpu.SemaphoreType.DMA(())],
    )
    return pl.pallas_call(
        _final_kernel,
        grid_spec=grid_spec,
        out_shape=[jax.ShapeDtypeStruct((T_PROMPT, D_MODEL), jnp.float32),
                   jax.ShapeDtypeStruct((T_SAMPLE, D_MODEL), jnp.float32)],
        compiler_params=pltpu.CompilerParams(dimension_semantics=("arbitrary",),
                                             vmem_limit_bytes=VMEM_LIMIT),
        name="final_norm",
    )(lpos, runs, x_mid, y_pages, gcol, g)


def kernel(x_prompt, x_sample, state_conv, state_pool, norm1_g, w_in, conv_w, pool_w, pool_scale, sgu_w, sgu_b, w_out, norm2_g, router_coarse_w, router_coarse_b, router_fine_w, router_fine_b, moe_w_gate, moe_w_up, moe_w_down, final_norm_g):
    bf16 = jnp.bfloat16
    xs = (x_prompt.reshape(T_PROMPT, D_MODEL), x_sample.reshape(T_SAMPLE, D_MODEL))
    sconv_pad = jnp.pad(state_conv, ((0, 0), (0, 0), (HIST_ROWS - (CONV_WIDTH - 1), 0), (0, 0)))
    spool_pad = jnp.pad(state_pool, ((0, 0), (0, 0), (HIST_ROWS - POOL_HIST, 0), (0, 0)))
    idx = jnp.arange(TL, dtype=jnp.int32)
    triu = (idx[:, None] < idx[None, :]).astype(bf16)
    ide = jnp.arange(N_EXPERTS, dtype=jnp.int32)
    tril_e = (ide[None, :] < ide[:, None]).astype(bf16)
    conv_pr, pool_pr, conv_sm, pool_sm, sgu_v = [], [], [], [], []
    x_mid = y_pages = gcol = lpos = runs = None
    for l in range(DEPTH):
        pool_bd = jax.scipy.linalg.block_diag(*[pool_w[l, g] for g in range(4)]).astype(bf16)
        wm_all = sgu_w[l].reshape(SGU_HEADS * SGU_LEN, SGU_LEN).astype(bf16)
        bias_full = jnp.repeat(sgu_b[l].T, SGU_HEAD_DIM, axis=1)
        wr_t = jnp.zeros((ROUTER_ROWS, D_MODEL), jnp.float32)
        wr_t = wr_t.at[0:N_GROUPS].set(router_coarse_w[l].T).at[8:].set(router_fine_w[l].T).astype(bf16)
        br_col = jnp.zeros((ROUTER_ROWS, 1), jnp.float32)
        br_col = br_col.at[0:N_GROUPS, 0].set(router_coarse_b[l]).at[8:, 0].set(router_fine_b[l])
        outs = _mixer_call(l == 0, xs, sconv_pad[l], spool_pad[l], norm1_g[l].reshape(1, D_MODEL),
                           w_in[l].astype(bf16), conv_w[l], pool_bd, pool_scale[l].reshape(1, D_POOL),
                           wm_all, bias_full, w_out[l].astype(bf16), norm2_g[l].reshape(1, D_MODEL),
                           wr_t, br_col, triu, tril_e)
        x_mid, hs, lpos, runs, meta, gcol, cpr, ppr, csm, psm, sv = outs
        lpos = lpos.reshape(N_TILES * 2 * TL)
        conv_pr.append(cpr[:, HIST_ROWS - (CONV_WIDTH - 1):, :])
        pool_pr.append(ppr[:, HIST_ROWS - POOL_HIST:, :])
        conv_sm.append(csm[:, HIST_ROWS - (CONV_WIDTH - 1):, :])
        pool_sm.append(psm[:, HIST_ROWS - POOL_HIST:, :])
        sgu_v.append(sv.reshape(DEC_BATCH, DEC_SEQ, D_SGU))
        y_pages = _expert_call(l, meta, hs, moe_w_gate, moe_w_up, moe_w_down)
        xs = (lpos, runs, x_mid, y_pages, gcol)
    y_prompt, y_sample = _final_call(lpos, runs, x_mid, y_pages, gcol, final_norm_g.reshape(1, D_MODEL))
    return (y_prompt.reshape(BATCH, SEQ, D_MODEL), y_sample.reshape(DEC_BATCH, DEC_SEQ, D_MODEL),
            jnp.stack(conv_pr), jnp.stack(pool_pr), jnp.stack(conv_sm), jnp.stack(pool_sm),
            jnp.stack(sgu_v))
```

```python
import functools

import jax
import jax.numpy as jnp
from jax import lax
from jax.experimental import pallas as pl
from jax.experimental.pallas import tpu as pltpu

D_MODEL = 1024
BATCH = 8
SEQ = 2048
DEPTH = 2
DEC_BATCH = 8
DEC_SEQ = 64
PAST_LEN = 1024
D_CONV = 384
CONV_WIDTH = 3
D_POOL = 256
POOL_HIST = 15
D_SGU = 384
SGU_HEADS = 4
SGU_HEAD_DIM = 96
SGU_LEN = 128
D_PROJ = 2176
N_GROUPS = 4
EXPERTS_PER_GROUP = 8
N_EXPERTS = 32
D_EXPERT = 512
EPS = 1e-6

T_PROMPT = BATCH * SEQ
T_SAMPLE = DEC_BATCH * DEC_SEQ
T_ALL = T_PROMPT + T_SAMPLE
TL = 512
TILES_PER_SEQ = SEQ // TL
N_PROMPT_TILES = T_PROMPT // TL
N_TILES = N_PROMPT_TILES + T_SAMPLE // TL
HIST_ROWS = 16
ROUTER_ROWS = 8 + N_EXPERTS
TM = 256
N_ASSIGN = 2 * T_ALL
N_PAGES = N_ASSIGN // TM + N_EXPERTS
PAGE_LANES = 256
PAGE_ROWS = N_PAGES * TM
RUN_FIELDS = 9
RUNS_PER_TILE = N_EXPERTS * RUN_FIELDS
VMEM_LIMIT = 56 * 1024 * 1024

assert TL == 2 * TM and N_PAGES <= PAGE_LANES and T_ALL == N_TILES * TL


def _rms(x, g):
    return x * lax.rsqrt(jnp.mean(x * x, axis=-1, keepdims=True) + EPS) * g


def _mix_rows(proj, zhist, phist, pos0, n, conv_w, pool_bd, pool_scale, wm_all, bias_full):
    a_b = proj[:, 0:384]
    a_c = proj[:, 384:768]
    a_h = proj[:, 768:1152]
    p_in = proj[:, 1152:1408]
    s_u = proj[:, 1408:1792]
    s_v = proj[:, 1792:2176]

    z = a_c * a_h
    zext = jnp.concatenate([zhist, z], axis=0)
    conv_y = (conv_w[0:1, :] * pltpu.roll(zext, 2, 0)[HIST_ROWS:, :]
              + conv_w[1:2, :] * pltpu.roll(zext, 1, 0)[HIST_ROWS:, :]
              + conv_w[2:3, :] * z)
    a_out = a_b * conv_y

    pext = jnp.concatenate([phist, p_in], axis=0)
    s2 = pext + pltpu.roll(pext, 1, 0)
    s4 = s2 + pltpu.roll(s2, 2, 0)
    s8 = s4 + pltpu.roll(s4, 4, 0)
    s16 = s8 + pltpu.roll(s8, 8, 0)
    lane = lax.broadcasted_iota(jnp.int32, (1, D_POOL), 1)
    wsum = jnp.where(lane < 64, s2, jnp.where(lane < 128, s4, jnp.where(lane < 192, s8, s16)))
    wsum = wsum[HIST_ROWS:, :]
    win = jnp.where(lane < 64, 2.0, jnp.where(lane < 128, 4.0, jnp.where(lane < 192, 8.0, 16.0)))
    pos = (pos0 + lax.broadcasted_iota(jnp.int32, (n, 1), 0) + 1).astype(jnp.float32)
    cnt = jnp.minimum(win, pos)
    pooled = wsum / cnt - p_in
    p_out = jnp.dot(pooled.astype(jnp.bfloat16), pool_bd,
                    preferred_element_type=jnp.float32) * pool_scale

    lane_s = lax.broadcasted_iota(jnp.int32, (1, D_SGU), 1)
    chunk = min(n, SGU_LEN)
    if chunk == SGU_LEN:
        wm = wm_all
    else:
        wm = jnp.concatenate([wm_all[h * SGU_LEN:h * SGU_LEN + chunk, 0:chunk]
                              for h in range(SGU_HEADS)], axis=0)
    s_rows = []
    for c in range(n // chunk):
        v_c = s_v[c * chunk:(c + 1) * chunk, :].astype(jnp.bfloat16)
        r = jnp.dot(wm, v_c, preferred_element_type=jnp.float32)
        s_c = jnp.where(lane_s < 96, r[0:chunk],
                        jnp.where(lane_s < 192, r[chunk:2 * chunk],
                                  jnp.where(lane_s < 288, r[2 * chunk:3 * chunk], r[3 * chunk:4 * chunk])))
        s_rows.append(s_c + bias_full[0:chunk, :])
    s_gate = s_rows[0] if len(s_rows) == 1 else jnp.concatenate(s_rows, axis=0)
    s_out = s_u * s_gate

    mix = jnp.concatenate([a_out, p_out, s_out], axis=-1)
    return mix, zext[n:n + HIST_ROWS, :], pext[n:n + HIST_ROWS, :]


def _route(h2_bf16, wr_t, br_col, n):
    logits = lax.dot_general(wr_t, h2_bf16, (((1,), (1,)), ((), ())),
                             preferred_element_type=jnp.float32) + br_col
    row8 = lax.broadcasted_iota(jnp.int32, (8, n), 0)
    lc = jnp.where(row8 < N_GROUPS, logits[0:8, :], -jnp.inf)
    mc = jnp.max(lc, axis=0, keepdims=True)
    g_sel = jnp.min(jnp.where(lc == mc, row8, 8), axis=0, keepdims=True)
    p_sel = 1.0 / jnp.sum(jnp.exp(lc - mc), axis=0, keepdims=True)
    lf = logits[8 + 3 * EXPERTS_PER_GROUP:8 + 4 * EXPERTS_PER_GROUP, :]
    for g in (2, 1, 0):
        lf = jnp.where(g_sel == g, logits[8 + g * EXPERTS_PER_GROUP:8 + (g + 1) * EXPERTS_PER_GROUP, :], lf)
    m1 = jnp.max(lf, axis=0, keepdims=True)
    i1 = jnp.min(jnp.where(lf == m1, row8, 8), axis=0, keepdims=True)
    lf2 = jnp.where(row8 == i1, -jnp.inf, lf)
    m2 = jnp.max(lf2, axis=0, keepdims=True)
    i2 = jnp.min(jnp.where(lf2 == m2, row8, 8), axis=0, keepdims=True)
    t = jnp.exp(m2 - m1)
    wa = 1.0 / (1.0 + t)
    wb = t / (1.0 + t)
    e0 = g_sel * EXPERTS_PER_GROUP + i1
    e1 = g_sel * EXPERTS_PER_GROUP + i2
    return e0, e1, p_sel * wa, p_sel * wb


def _place_rows(e0, e1, triu, tril_e, fill_ref, base_ref, np_ref, pexp_ref):
    row_e = lax.broadcasted_iota(jnp.int32, (N_EXPERTS, TL), 0)
    oh0 = row_e == e0
    oh1 = row_e == e1
    oh = jnp.where(oh0 | oh1, 1.0, 0.0)
    rank = jnp.dot(oh.astype(jnp.bfloat16), triu, preferred_element_type=jnp.float32).astype(jnp.int32)
    cnt = jnp.sum(oh, axis=1, keepdims=True).astype(jnp.int32)
    lower = jnp.where(e0 < row_e, 1.0, 0.0) + jnp.where(e1 < row_e, 1.0, 0.0)
    first = jnp.sum(lower, axis=1, keepdims=True).astype(jnp.int32)
    sorted_all = first + rank
    lpos0 = jnp.sum(jnp.where(oh0, sorted_all, 0), axis=0, keepdims=True)
    lpos1 = jnp.sum(jnp.where(oh1, sorted_all, 0), axis=0, keepdims=True)

    fill = fill_ref[:, 0:1]
    base = base_ref[:, 0:1]
    npages = np_ref[0:1, 0:1]
    total = fill + cnt
    need = (total > TM).astype(jnp.int32) + (total > 2 * TM).astype(jnp.int32)
    need_b = jnp.broadcast_to(need.astype(jnp.float32), (N_EXPERTS, 128)).astype(jnp.bfloat16)
    before = jnp.dot(tril_e, need_b, preferred_element_type=jnp.float32)[:, 0:1].astype(jnp.int32)
    new_id = npages + before
    new_base = new_id * TM
    fill_ref[...] = jnp.broadcast_to(total - need * TM, (N_EXPERTS, 128))
    base_ref[...] = jnp.broadcast_to(jnp.where(need > 0, new_base + (need - 1) * TM, base), (N_EXPERTS, 128))
    np_ref[...] = jnp.broadcast_to(npages + jnp.sum(need, axis=0, keepdims=True), (8, 128))
    page_lane = lax.broadcasted_iota(jnp.int32, (N_EXPERTS, PAGE_LANES), 1)
    expert_col = lax.broadcasted_iota(jnp.int32, (N_EXPERTS, 1), 0)
    owns = ((page_lane == new_id) & (need >= 1)) | ((page_lane == new_id + 1) & (need == 2))
    pexp_ref[...] = pexp_ref[...] + jnp.sum(jnp.where(owns, expert_col, 0), axis=0, keepdims=True)

    lane = lax.broadcasted_iota(jnp.int32, (N_EXPERTS, 128), 1)
    cols = jnp.where(lane == 0, cnt, jnp.where(lane == 1, first, jnp.where(lane == 2, fill,
                     jnp.where(lane == 3, base, jnp.where(lane == 4, new_base, 0)))))
    square = jnp.concatenate([cols, jnp.zeros((128 - N_EXPERTS, 128), jnp.int32)], axis=0)
    per_expert = square.astype(jnp.float32).T.astype(jnp.int32)[0:8, :]
    return lpos0, lpos1, per_expert


def _run_pieces(cnt, first, fill, base, new_base):
    n0 = jnp.minimum(cnt, TM - fill)
    n1 = jnp.minimum(cnt - n0, TM)
    n2 = cnt - n0 - n1
    return ((first, base + fill, n0), (first + n0, new_base, n1), (first + n0 + n1, new_base + TM, n2))


def _start_run_gather(runs_ref, tile, pages_ref, gbuf, gsem):
    def per_expert(e, c):
        k0 = (tile * N_EXPERTS + e) * RUN_FIELDS
        for j in range(3):
            src = runs_ref[k0 + 3 * j]
            dst = runs_ref[k0 + 3 * j + 1]
            n = runs_ref[k0 + 3 * j + 2]

            @pl.when(n > 0)
            def _():
                pltpu.make_async_copy(pages_ref.at[pl.ds(dst, n)], gbuf.at[pl.ds(src, n)], gsem).start()
        return c
    lax.fori_loop(0, N_EXPERTS, per_expert, 0)


def _unsort_results(lpos_ref, tile, gbuf, y0buf, y1buf, gsem):
    pltpu.make_async_copy(gbuf, gbuf, gsem).wait()
    k0 = tile * 2 * TL
    for r in range(TL):
        y0buf[r] = gbuf[lpos_ref[k0 + r]]
        y1buf[r] = gbuf[lpos_ref[k0 + TL + r]]


def _mixer_kernel(first_layer, *refs):
    if first_layer:
        xp_ref, xs_ref = refs[0:2]
        rest = refs[2:]
    else:
        lposp_ref, runsp_ref, xm_ref, yp_ref, gc_ref = refs[0:5]
        rest = refs[5:]
    (sconv_ref, spool_ref, g1_ref, win_ref, convw_ref, poolbd_ref, pscale_ref, wm_ref, bias_ref,
     wout_ref, g2_ref, wr_ref, br_ref, triu_ref, trile_ref,
     xmid_ref, hs_ref, lpos_ref, runs_ref, meta_ref, gcol_ref, cpr_ref, ppr_ref, csm_ref, psm_ref, sv_ref,
     zh_ref, ph_ref, h2buf, sbuf, posv, pexv, poss, pexs, fill_ref, base_ref, np_ref,
     pexp_ref, fill_s, base_s, np_s, pexp_s, cnt_s, rsem, msem) = rest[:45]
    if not first_layer:
        gbuf, y0buf, y1buf, gsem = rest[45:]

    i = pl.program_id(0)

    def load_x(prompt):
        if first_layer:
            return xp_ref[...] if prompt else xs_ref[...]
        g = gc_ref[...]
        y0 = y0buf[...].reshape(TL, D_MODEL)
        y1 = y1buf[...].reshape(TL, D_MODEL)
        return xm_ref[...] + g[:, 0:1] * y0 + g[:, 1:2] * y1

    row_m = lax.broadcasted_iota(jnp.int32, (SGU_HEADS * SGU_LEN, SGU_LEN), 0) % SGU_LEN
    col_m = lax.broadcasted_iota(jnp.int32, (SGU_HEADS * SGU_LEN, SGU_LEN), 1)

    def wait_rows(slot):
        pltpu.make_async_copy(sbuf.at[0], sbuf.at[0], rsem.at[slot]).wait()

    def sort_rows(slot):
        pltpu.make_async_copy(posv, poss.at[slot], msem).wait()
        pltpu.make_async_copy(pexv, pexs.at[slot], msem).wait()
        for r in range(TL):
            row = h2buf[r]
            sbuf[slot, poss[slot, 0, r]] = row
            sbuf[slot, poss[slot, 1, r]] = row

    def start_runs(tile, slot):
        def per_expert(e, c):
            pieces = _run_pieces(pexs[slot, 0, e], pexs[slot, 1, e], pexs[slot, 2, e], pexs[slot, 3, e],
                                 pexs[slot, 4, e])
            k0 = (tile * N_EXPERTS + e) * RUN_FIELDS
            for j, (src, dst, n) in enumerate(pieces):
                runs_ref[k0 + 3 * j] = src
                runs_ref[k0 + 3 * j + 1] = dst
                runs_ref[k0 + 3 * j + 2] = n

                @pl.when(n > 0)
                def _():
                    pltpu.make_async_copy(sbuf.at[slot, pl.ds(src, n)], hs_ref.at[pl.ds(dst, n)],
                                          rsem.at[slot]).start()
            return c
        lax.fori_loop(0, N_EXPERTS, per_expert, 0)

        @pl.when(tile >= 1)
        def _():
            wait_rows(1 - slot)

    def finish(x, mix):
        x_mid = x + jnp.dot(mix.astype(jnp.bfloat16), wout_ref[...], preferred_element_type=jnp.float32)
        xmid_ref[...] = x_mid
        h2 = _rms(x_mid, g2_ref[...])
        h2buf[...] = h2.reshape(TL, 1, D_MODEL)
        e0, e1, g0, g1 = _route(h2.astype(jnp.bfloat16), wr_ref[...], br_ref[...], TL)
        row128 = lax.broadcasted_iota(jnp.int32, (128, TL), 0)
        gcol_ref[...] = jnp.where(row128 == 0, g0, jnp.where(row128 == 1, g1, 0.0)).T
        lpos0, lpos1, per_expert = _place_rows(e0, e1, triu_ref[...], trile_ref[...], fill_ref, base_ref,
                                               np_ref, pexp_ref)
        row8 = lax.broadcasted_iota(jnp.int32, (8, TL), 0)
        posv[...] = jnp.where(row8 == 0, lpos0, jnp.where(row8 == 1, lpos1, 0))
        pexv[...] = per_expert
        row2 = lax.broadcasted_iota(jnp.int32, (2, TL), 0)
        lpos_ref[0] = jnp.where(row2 == 0, lpos0, lpos1)
        pltpu.make_async_copy(posv, poss.at[i % 2], msem).start()
        pltpu.make_async_copy(pexv, pexs.at[i % 2], msem).start()

    def project(x):
        h = _rms(x, g1_ref[...]).astype(jnp.bfloat16)
        return jnp.dot(h, win_ref[...], preferred_element_type=jnp.float32)

    def masked_wm():
        return jnp.where(col_m <= row_m, wm_ref[...], jnp.zeros_like(wm_ref[...]))

    @pl.when(i == 0)
    def _init():
        zh_ref[...] = jnp.zeros_like(zh_ref)
        ph_ref[...] = jnp.zeros_like(ph_ref)
        fill_ref[...] = jnp.full_like(fill_ref, TM)
        base_ref[...] = jnp.zeros_like(base_ref)
        np_ref[...] = jnp.zeros_like(np_ref)
        pexp_ref[...] = jnp.zeros_like(pexp_ref)
        h2buf[...] = jnp.zeros((TL, 1, D_MODEL), jnp.float32)
        row8 = lax.broadcasted_iota(jnp.int32, (8, TL), 0)
        posv[...] = jnp.minimum(row8, 1) * TL + lax.broadcasted_iota(jnp.int32, (8, TL), 1)
        pexv[...] = jnp.zeros((8, 128), jnp.int32)
        pltpu.make_async_copy(posv, poss.at[1], msem).start()
        pltpu.make_async_copy(pexv, pexs.at[1], msem).start()
        if not first_layer:
            _start_run_gather(runsp_ref, 0, yp_ref, gbuf, gsem)

    def _close_pages():
        copies = [pltpu.make_async_copy(fill_ref, fill_s, msem), pltpu.make_async_copy(base_ref, base_s, msem),
                  pltpu.make_async_copy(np_ref, np_s, msem), pltpu.make_async_copy(pexp_ref, pexp_s, msem)]
        for c in copies:
            c.start()
        for c in copies:
            c.wait()
        npages = np_s[0, 0]

        def zero_cnt(e, c):
            cnt_s[e] = 0
            return c
        lax.fori_loop(0, N_EXPERTS, zero_cnt, 0)

        def count(p, c):
            e = pexp_s[0, p]
            cnt_s[e] = cnt_s[e] + 1
            return c
        lax.fori_loop(0, npages, count, 0)

        def prefix(e, run):
            n = cnt_s[e]
            cnt_s[e] = run
            return run + n
        lax.fori_loop(0, N_EXPERTS, prefix, 0)

        def emit(p, c):
            e = pexp_s[0, p]
            q = cnt_s[e]
            cnt_s[e] = q + 1
            meta_ref[0, q] = p
            meta_ref[1, q] = e
            return c
        lax.fori_loop(0, npages, emit, 0)

        def pad_meta(q, c):
            meta_ref[0, q] = jnp.minimum(q, N_PAGES - 1)
            meta_ref[1, q] = meta_ref[1, npages - 1]
            return c
        lax.fori_loop(npages, PAGE_LANES, pad_meta, 0)

        def fill_row2(q, c):
            meta_ref[2, q] = npages
            return c
        lax.fori_loop(0, PAGE_LANES, fill_row2, 0)

        def mark_first(q, seen):
            is_first = (q < npages) & ((q == 0) | (meta_ref[1, q] != meta_ref[1, jnp.maximum(q - 1, 0)]))
            seen = seen + jnp.where(is_first, 1, 0)
            meta_ref[3, q] = jnp.where(is_first, 1, 0)
            meta_ref[4, q] = jnp.maximum(seen - 1, 0)
            meta_ref[5, q] = -1
            return seen
        lax.fori_loop(0, PAGE_LANES, mark_first, 0)

        def mark_next(t, carry):
            cur, nxt = carry
            q = npages - 1 - t
            e = meta_ref[1, q]
            nxt = jnp.where(e != cur, cur, nxt)
            meta_ref[5, q] = nxt
            return e, nxt
        lax.fori_loop(0, npages, mark_next, (jnp.int32(-1), jnp.int32(-1)))

        h2buf[...] = jnp.zeros((TL, 1, D_MODEL), jnp.float32)

        def tails(start):
            def tail(e, c):
                f = fill_s[e, 0]
                b = base_s[e, 0]

                @pl.when(f < TM)
                def _zero_tail():
                    cp = pltpu.make_async_copy(h2buf.at[pl.ds(0, TM - f)], hs_ref.at[pl.ds(b + f, TM - f)],
                                               rsem.at[0])
                    cp.start() if start else cp.wait()
                return c
            lax.fori_loop(0, N_EXPERTS, tail, 0)

            def unused(p, c):
                cp = pltpu.make_async_copy(h2buf.at[pl.ds(0, TM)], hs_ref.at[pl.ds(p * TM, TM)], rsem.at[0])
                cp.start() if start else cp.wait()
                return c
            lax.fori_loop(npages, N_PAGES, unused, 0)

        tails(True)
        tails(False)

    prev_slot = (i + 1) % 2

    if not first_layer:
        @pl.when(i >= 0)
        def _results():
            _unsort_results(lposp_ref, i, gbuf, y0buf, y1buf, gsem)

            @pl.when(i + 1 < N_TILES)
            def _():
                _start_run_gather(runsp_ref, i + 1, yp_ref, gbuf, gsem)

    @pl.when(i < N_PROMPT_TILES)
    def _prompt():
        s = i % TILES_PER_SEQ
        x = load_x(True)
        proj = project(x)
        zhist = jnp.where(s == 0, 0.0, zh_ref[...])
        phist = jnp.where(s == 0, 0.0, ph_ref[...])
        mix, znew, pnew = _mix_rows(proj, zhist, phist, s * TL, TL, convw_ref[...], poolbd_ref[...],
                                    pscale_ref[...], masked_wm(), bias_ref[...])
        zh_ref[...] = znew
        ph_ref[...] = pnew
        cpr_ref[0] = znew
        ppr_ref[0] = pnew
        sort_rows(prev_slot)
        finish(x, mix)

    @pl.when(i <= N_PROMPT_TILES - 1)
    def _prompt_runs():
        start_runs(jnp.maximum(i - 1, 0), prev_slot)

    @pl.when(i == N_PROMPT_TILES)
    def _sample():
        x = load_x(False)
        proj = project(x)
        wm = masked_wm()
        mixes = []
        for b in range(DEC_BATCH):
            rows = slice(b * DEC_SEQ, (b + 1) * DEC_SEQ)
            mix, znew, pnew = _mix_rows(proj[rows, :], sconv_ref[b], spool_ref[b], PAST_LEN, DEC_SEQ,
                                        convw_ref[...], poolbd_ref[...], pscale_ref[...], wm, bias_ref[...])
            csm_ref[b] = znew
            psm_ref[b] = pnew
            mixes.append(mix)
        sv_ref[...] = proj[:, 1792:2176]
        sort_rows(prev_slot)
        finish(x, jnp.concatenate(mixes, axis=0))
        start_runs(i - 1, prev_slot)
        sort_rows(i % 2)
        start_runs(i, i % 2)
        wait_rows(i % 2)
        _close_pages()


def _mixer_call(first_layer, xs, sconv_pad, spool_pad, g1, w_in, conv_w, pool_bd, pool_scale, wm_all,
                bias_full, w_out, g2, wr_t, br_col, triu, tril_e):
    tile = lambda i, *_: (i, 0)
    prompt_tile = lambda i, *_: (jnp.minimum(i, N_PROMPT_TILES - 1), 0)
    const2 = lambda i, *_: (0, 0)
    const3 = lambda i, *_: (0, 0, 0)
    if first_layer:
        prefetch = ()
        x_specs = [pl.BlockSpec((TL, D_MODEL), prompt_tile), pl.BlockSpec((TL, D_MODEL), const2)]
    else:
        prefetch = xs[0:2]
        xs = xs[2:]
        x_specs = [pl.BlockSpec((TL, D_MODEL), tile), pl.BlockSpec(memory_space=pl.ANY),
                   pl.BlockSpec((TL, 128), tile)]
    full = lambda a: pl.BlockSpec(a.shape, const2 if a.ndim == 2 else const3, pipeline_mode=pl.Buffered(1))
    weights = [sconv_pad, spool_pad, g1, w_in, conv_w, pool_bd, pool_scale, wm_all, bias_full, w_out, g2,
               wr_t, br_col, triu, tril_e]
    in_specs = x_specs + [full(a) for a in weights]
    seq_of = lambda i, *_: (jnp.minimum(i // TILES_PER_SEQ, BATCH - 1), 0, 0)
    out_shape = [
        jax.ShapeDtypeStruct((T_ALL, D_MODEL), jnp.float32),
        jax.ShapeDtypeStruct((PAGE_ROWS, 1, D_MODEL), jnp.float32),
        jax.ShapeDtypeStruct((N_TILES, 2, TL), jnp.int32),
        jax.ShapeDtypeStruct((N_TILES * RUNS_PER_TILE,), jnp.int32),
        jax.ShapeDtypeStruct((6, PAGE_LANES), jnp.int32),
        jax.ShapeDtypeStruct((T_ALL, 128), jnp.float32),
        jax.ShapeDtypeStruct((BATCH, HIST_ROWS, D_CONV), jnp.float32),
        jax.ShapeDtypeStruct((BATCH, HIST_ROWS, D_POOL), jnp.float32),
        jax.ShapeDtypeStruct((DEC_BATCH, HIST_ROWS, D_CONV), jnp.float32),
        jax.ShapeDtypeStruct((DEC_BATCH, HIST_ROWS, D_POOL), jnp.float32),
        jax.ShapeDtypeStruct((T_SAMPLE, D_SGU), jnp.float32),
    ]
    out_specs = [
        pl.BlockSpec((TL, D_MODEL), tile),
        pl.BlockSpec(memory_space=pl.ANY),
        pl.BlockSpec((1, 2, TL), lambda i, *_: (i, 0, 0)),
        pl.BlockSpec(memory_space=pltpu.SMEM),
        pl.BlockSpec(memory_space=pltpu.SMEM),
        pl.BlockSpec((TL, 128), tile),
        pl.BlockSpec((1, HIST_ROWS, D_CONV), seq_of),
        pl.BlockSpec((1, HIST_ROWS, D_POOL), seq_of),
        pl.BlockSpec((DEC_BATCH, HIST_ROWS, D_CONV), const3),
        pl.BlockSpec((DEC_BATCH, HIST_ROWS, D_POOL), const3),
        pl.BlockSpec((T_SAMPLE, D_SGU), const2),
    ]
    scratch = [
        pltpu.VMEM((HIST_ROWS, D_CONV), jnp.float32),
        pltpu.VMEM((HIST_ROWS, D_POOL), jnp.float32),
        pltpu.VMEM((TL, 1, D_MODEL), jnp.float32),
        pltpu.VMEM((2, 2 * TL, 1, D_MODEL), jnp.float32),
        pltpu.VMEM((8, TL), jnp.int32),
        pltpu.VMEM((8, 128), jnp.int32),
        pltpu.SMEM((2, 8, TL), jnp.int32),
        pltpu.SMEM((2, 8, 128), jnp.int32),
        pltpu.VMEM((N_EXPERTS, 128), jnp.int32),
        pltpu.VMEM((N_EXPERTS, 128), jnp.int32),
        pltpu.VMEM((8, 128), jnp.int32),
        pltpu.VMEM((8, PAGE_LANES), jnp.int32),
        pltpu.SMEM((N_EXPERTS, 128), jnp.int32),
        pltpu.SMEM((N_EXPERTS, 128), jnp.int32),
        pltpu.SMEM((8, 128), jnp.int32),
        pltpu.SMEM((8, PAGE_LANES), jnp.int32),
        pltpu.SMEM((N_EXPERTS,), jnp.int32),
        pltpu.SemaphoreType.DMA((2,)),
        pltpu.SemaphoreType.DMA(()),
    ]
    if not first_layer:
        scratch += [
            pltpu.VMEM((2 * TL, 1, D_MODEL), jnp.float32),
            pltpu.VMEM((TL, 1, D_MODEL), jnp.float32),
            pltpu.VMEM((TL, 1, D_MODEL), jnp.float32),
            pltpu.SemaphoreType.DMA(()),
        ]
    grid_spec = pltpu.PrefetchScalarGridSpec(num_scalar_prefetch=len(prefetch), grid=(N_TILES,),
                                             in_specs=in_specs, out_specs=out_specs, scratch_shapes=scratch)
    return pl.pallas_call(
        functools.partial(_mixer_kernel, first_layer),
        grid_spec=grid_spec,
        out_shape=out_shape,
        compiler_params=pltpu.CompilerParams(dimension_semantics=("arbitrary",),
                                             vmem_limit_bytes=VMEM_LIMIT),
        name="mixer_first" if first_layer else "mixer_next",
    )(*prefetch, *xs, *weights)


def _expert_kernel(layer, meta_ref, hs_ref, wg_hbm, wu_hbm, wd_hbm, y_ref, xbuf, wg32, wu32, wd32, wg16, wu16,
                   wd16, wsem):
    s = pl.program_id(0)
    npages = meta_ref[2, 0]

    def weight_copies(expert, slot):
        return [pltpu.make_async_copy(wg_hbm.at[layer, expert], wg32.at[slot], wsem.at[slot]),
                pltpu.make_async_copy(wu_hbm.at[layer, expert], wu32.at[slot], wsem.at[slot]),
                pltpu.make_async_copy(wd_hbm.at[layer, expert], wd32.at[slot], wsem.at[slot])]

    @pl.when(s == 0)
    def _first_weights():
        for c in weight_copies(meta_ref[1, 0], 0):
            c.start()

    @pl.when((s < npages) & (meta_ref[3, s] == 1))
    def _switch_expert():
        slot = meta_ref[4, s] % 2
        for c in weight_copies(meta_ref[1, s], slot):
            c.wait()
        nxt = meta_ref[5, s]

        @pl.when(nxt >= 0)
        def _():
            for c in weight_copies(nxt, 1 - slot):
                c.start()
        wg16[...] = wg32[slot].astype(jnp.bfloat16)
        wu16[...] = wu32[slot].astype(jnp.bfloat16)
        wd16[...] = wd32[slot].astype(jnp.bfloat16)

    @pl.when(s < npages)
    def _page():
        xbuf[...] = hs_ref[...].reshape(TM, D_MODEL)
        x = xbuf[...].astype(jnp.bfloat16)
        hg = jnp.dot(x, wg16[...], preferred_element_type=jnp.float32)
        hu = jnp.dot(x, wu16[...], preferred_element_type=jnp.float32)
        h = (hg * jax.nn.sigmoid(hg) * hu).astype(jnp.bfloat16)
        out = jnp.dot(h, wd16[...], preferred_element_type=jnp.float32)
        y_ref[...] = out.reshape(TM, 1, D_MODEL)

    @pl.when(s >= npages)
    def _unused_page():
        y_ref[...] = jnp.zeros((TM, 1, D_MODEL), jnp.float32)


def _expert_call(layer, meta, hs, w_gate, w_up, w_down):
    page = lambda s, meta: (meta[0, s], 0, 0)
    grid_spec = pltpu.PrefetchScalarGridSpec(
        num_scalar_prefetch=1,
        grid=(N_PAGES,),
        in_specs=[
            pl.BlockSpec((TM, 1, D_MODEL), page),
            pl.BlockSpec(memory_space=pl.ANY),
            pl.BlockSpec(memory_space=pl.ANY),
            pl.BlockSpec(memory_space=pl.ANY),
        ],
        out_specs=pl.BlockSpec((TM, 1, D_MODEL), page),
        scratch_shapes=[
            pltpu.VMEM((TM, D_MODEL), jnp.float32),
            pltpu.VMEM((2, D_MODEL, D_EXPERT), jnp.float32),
            pltpu.VMEM((2, D_MODEL, D_EXPERT), jnp.float32),
            pltpu.VMEM((2, D_EXPERT, D_MODEL), jnp.float32),
            pltpu.VMEM((D_MODEL, D_EXPERT), jnp.bfloat16),
            pltpu.VMEM((D_MODEL, D_EXPERT), jnp.bfloat16),
            pltpu.VMEM((D_EXPERT, D_MODEL), jnp.bfloat16),
            pltpu.SemaphoreType.DMA((2,)),
        ],
    )
    return pl.pallas_call(
        functools.partial(_expert_kernel, layer),
        grid_spec=grid_spec,
        out_shape=jax.ShapeDtypeStruct((PAGE_ROWS, 1, D_MODEL), jnp.float32),
        compiler_params=pltpu.CompilerParams(dimension_semantics=("arbitrary",),
                                             vmem_limit_bytes=VMEM_LIMIT),
        name="experts",
    )(meta, hs, w_gate, w_up, w_down)


def _final_kernel(lposp_ref, runsp_ref, xm_ref, yp_ref, gc_ref, g_ref, yp_out, ys_out, gbuf, y0buf, y1buf, gsem):
    i = pl.program_id(0)

    @pl.when(i == 0)
    def _():
        _start_run_gather(runsp_ref, 0, yp_ref, gbuf, gsem)

    _unsort_results(lposp_ref, i, gbuf, y0buf, y1buf, gsem)

    @pl.when(i + 1 < N_TILES)
    def _():
        _start_run_gather(runsp_ref, i + 1, yp_ref, gbuf, gsem)

    g = gc_ref[...]
    y0 = y0buf[...].reshape(TL, D_MODEL)
    y1 = y1buf[...].reshape(TL, D_MODEL)
    x = xm_ref[...] + g[:, 0:1] * y0 + g[:, 1:2] * y1
    out = _rms(x, g_ref[...])

    @pl.when(i < N_PROMPT_TILES)
    def _():
        yp_out[...] = out

    @pl.when(i == N_PROMPT_TILES)
    def _():
        ys_out[...] = out


def _final_call(lpos, runs, x_mid, y_pages, gcol, g):
    tile = lambda i, *_: (i, 0)
    grid_spec = pltpu.PrefetchScalarGridSpec(
        num_scalar_prefetch=2,
        grid=(N_TILES,),
        in_specs=[pl.BlockSpec((TL, D_MODEL), tile), pl.BlockSpec(memory_space=pl.ANY),
                  pl.BlockSpec((TL, 128), tile), pl.BlockSpec((1, D_MODEL), lambda i, *_: (0, 0))],
        out_specs=[pl.BlockSpec((TL, D_MODEL), lambda i, *_: (jnp.minimum(i, N_PROMPT_TILES - 1), 0)),
                   pl.BlockSpec((TL, D_MODEL), lambda i, *_: (0, 0))],
        scratch_shapes=[pltpu.VMEM((2 * TL, 1, D_MODEL), jnp.float32), pltpu.VMEM((TL, 1, D_MODEL), jnp.float32),
                        pltpu.VMEM((TL, 1, D_MODEL), jnp.float32), pltpu.SemaphoreType.DMA(())],
    )
    return pl.pallas_call(
        _final_kernel,
        grid_spec=grid_spec,
        out_shape=[jax.ShapeDtypeStruct((T_PROMPT, D_MODEL), jnp.float32),
                   jax.ShapeDtypeStruct((T_SAMPLE, D_MODEL), jnp.float32)],
        compiler_params=pltpu.CompilerParams(dimension_semantics=("arbitrary",),
                                             vmem_limit_bytes=VMEM_LIMIT),
        name="final_norm",
    )(lpos, runs, x_mid, y_pages, gcol, g)


def kernel(x_prompt, x_sample, state_conv, state_pool, norm1_g, w_in, conv_w, pool_w, pool_scale, sgu_w, sgu_b, w_out, norm2_g, router_coarse_w, router_coarse_b, router_fine_w, router_fine_b, moe_w_gate, moe_w_up, moe_w_down, final_norm_g):
    bf16 = jnp.bfloat16
    xs = (x_prompt.reshape(T_PROMPT, D_MODEL), x_sample.reshape(T_SAMPLE, D_MODEL))
    sconv_pad = jnp.pad(state_conv, ((0, 0), (0, 0), (HIST_ROWS - (CONV_WIDTH - 1), 0), (0, 0)))
    spool_pad = jnp.pad(state_pool, ((0, 0), (0, 0), (HIST_ROWS - POOL_HIST, 0), (0, 0)))
    idx = jnp.arange(TL, dtype=jnp.int32)
    triu = (idx[:, None] < idx[None, :]).astype(bf16)
    ide = jnp.arange(N_EXPERTS, dtype=jnp.int32)
    tril_e = (ide[None, :] < ide[:, None]).astype(bf16)
    conv_pr, pool_pr, conv_sm, pool_sm, sgu_v = [], [], [], [], []
    x_mid = y_pages = gcol = lpos = runs = None
    for l in range(DEPTH):
        pool_bd = jax.scipy.linalg.block_diag(*[pool_w[l, g] for g in range(4)]).astype(bf16)
        wm_all = sgu_w[l].reshape(SGU_HEADS * SGU_LEN, SGU_LEN).astype(bf16)
        bias_full = jnp.repeat(sgu_b[l].T, SGU_HEAD_DIM, axis=1)
        wr_t = jnp.zeros((ROUTER_ROWS, D_MODEL), jnp.float32)
        wr_t = wr_t.at[0:N_GROUPS].set(router_coarse_w[l].T).at[8:].set(router_fine_w[l].T).astype(bf16)
        br_col = jnp.zeros((ROUTER_ROWS, 1), jnp.float32)
        br_col = br_col.at[0:N_GROUPS, 0].set(router_coarse_b[l]).at[8:, 0].set(router_fine_b[l])
        outs = _mixer_call(l == 0, xs, sconv_pad[l], spool_pad[l], norm1_g[l].reshape(1, D_MODEL),
                           w_in[l].astype(bf16), conv_w[l], pool_bd, pool_scale[l].reshape(1, D_POOL),
                           wm_all, bias_full, w_out[l].astype(bf16), norm2_g[l].reshape(1, D_MODEL),
                           wr_t, br_col, triu, tril_e)
        x_mid, hs, lpos, runs, meta, gcol, cpr, ppr, csm, psm, sv = outs
        lpos = lpos.reshape(N_TILES * 2 * TL)
        conv_pr.append(cpr[:, HIST_ROWS - (CONV_WIDTH - 1):, :])
        pool_pr.append(ppr[:, HIST_ROWS - POOL_HIST:, :])
        conv_sm.append(csm[:, HIST_ROWS - (CONV_WIDTH - 1):, :])
        pool_sm.append(psm[:, HIST_ROWS - POOL_HIST:, :])
        sgu_v.append(sv.reshape(DEC_BATCH, DEC_SEQ, D_SGU))
        y_pages = _expert_call(l, meta, hs, moe_w_gate, moe_w_up, moe_w_down)
        xs = (lpos, runs, x_mid, y_pages, gcol)
    y_prompt, y_sample = _final_call(lpos, runs, x_mid, y_pages, gcol, final_norm_g.reshape(1, D_MODEL))
    return (y_prompt.reshape(BATCH, SEQ, D_MODEL), y_sample.reshape(DEC_BATCH, DEC_SEQ, D_MODEL),
            jnp.stack(conv_pr), jnp.stack(pool_pr), jnp.stack(conv_sm), jnp.stack(pool_sm),
            jnp.stack(sgu_v))
```

```python
import functools

import jax
import jax.numpy as jnp
from jax import lax
from jax.experimental import pallas as pl
from jax.experimental.pallas import tpu as pltpu

D_MODEL = 1024
BATCH = 8
SEQ = 2048
DEPTH = 2
DEC_BATCH = 8
DEC_SEQ = 64
PAST_LEN = 1024
D_CONV = 384
CONV_WIDTH = 3
D_POOL = 256
POOL_HIST = 15
D_SGU = 384
SGU_HEADS = 4
SGU_HEAD_DIM = 96
SGU_LEN = 128
D_PROJ = 2176
N_GROUPS = 4
EXPERTS_PER_GROUP = 8
N_EXPERTS = 32
D_EXPERT = 512
EPS = 1e-6

T_PROMPT = BATCH * SEQ
T_SAMPLE = DEC_BATCH * DEC_SEQ
T_ALL = T_PROMPT + T_SAMPLE
TL = 512
TILES_PER_SEQ = SEQ // TL
N_PROMPT_TILES = T_PROMPT // TL
N_TILES = N_PROMPT_TILES + T_SAMPLE // TL
HIST_ROWS = 16
ROUTER_ROWS = 8 + N_EXPERTS
TM = 256
N_ASSIGN = 2 * T_ALL
N_PAGES = N_ASSIGN // TM + N_EXPERTS
PAGE_LANES = 256
PAGE_ROWS = N_PAGES * TM
RUN_FIELDS = 9
RUNS_PER_TILE = N_EXPERTS * RUN_FIELDS
D_PACK = D_MODEL // 2
VMEM_LIMIT = 56 * 1024 * 1024

assert TL == 2 * TM and N_PAGES <= PAGE_LANES and T_ALL == N_TILES * TL


def _pack_rows(x):
    words = pltpu.pack_elementwise([x[:, :D_PACK], x[:, D_PACK:]], packed_dtype=jnp.bfloat16)
    return words.reshape(x.shape[0], 1, D_PACK)


def _unpack_rows(words):
    w = words.reshape(words.shape[0], D_PACK)
    halves = [pltpu.unpack_elementwise(w, index=k, packed_dtype=jnp.bfloat16, unpacked_dtype=jnp.float32)
              for k in range(2)]
    return jnp.concatenate(halves, axis=1)


def _rms(x, g):
    return x * lax.rsqrt(jnp.mean(x * x, axis=-1, keepdims=True) + EPS) * g


def _mix_rows(proj, zhist, phist, pos0, n, conv_w, pool_bd, pool_scale, wm_all, bias_full):
    a_b = proj[:, 0:384]
    a_c = proj[:, 384:768]
    a_h = proj[:, 768:1152]
    p_in = proj[:, 1152:1408]
    s_u = proj[:, 1408:1792]
    s_v = proj[:, 1792:2176]

    z = a_c * a_h
    zext = jnp.concatenate([zhist, z], axis=0)
    conv_y = (conv_w[0:1, :] * pltpu.roll(zext, 2, 0)[HIST_ROWS:, :]
              + conv_w[1:2, :] * pltpu.roll(zext, 1, 0)[HIST_ROWS:, :]
              + conv_w[2:3, :] * z)
    a_out = a_b * conv_y

    pext = jnp.concatenate([phist, p_in], axis=0)
    s2 = pext + pltpu.roll(pext, 1, 0)
    s4 = s2 + pltpu.roll(s2, 2, 0)
    s8 = s4 + pltpu.roll(s4, 4, 0)
    s16 = s8 + pltpu.roll(s8, 8, 0)
    lane = lax.broadcasted_iota(jnp.int32, (1, D_POOL), 1)
    wsum = jnp.where(lane < 64, s2, jnp.where(lane < 128, s4, jnp.where(lane < 192, s8, s16)))
    wsum = wsum[HIST_ROWS:, :]
    win = jnp.where(lane < 64, 2.0, jnp.where(lane < 128, 4.0, jnp.where(lane < 192, 8.0, 16.0)))
    pos = (pos0 + lax.broadcasted_iota(jnp.int32, (n, 1), 0) + 1).astype(jnp.float32)
    cnt = jnp.minimum(win, pos)
    pooled = wsum / cnt - p_in
    p_out = jnp.dot(pooled.astype(jnp.bfloat16), pool_bd,
                    preferred_element_type=jnp.float32) * pool_scale

    lane_s = lax.broadcasted_iota(jnp.int32, (1, D_SGU), 1)
    chunk = min(n, SGU_LEN)
    if chunk == SGU_LEN:
        wm = wm_all
    else:
        wm = jnp.concatenate([wm_all[h * SGU_LEN:h * SGU_LEN + chunk, 0:chunk]
                              for h in range(SGU_HEADS)], axis=0)
    s_rows = []
    for c in range(n // chunk):
        v_c = s_v[c * chunk:(c + 1) * chunk, :].astype(jnp.bfloat16)
        r = jnp.dot(wm, v_c, preferred_element_type=jnp.float32)
        s_c = jnp.where(lane_s < 96, r[0:chunk],
                        jnp.where(lane_s < 192, r[chunk:2 * chunk],
                                  jnp.where(lane_s < 288, r[2 * chunk:3 * chunk], r[3 * chunk:4 * chunk])))
        s_rows.append(s_c + bias_full[0:chunk, :])
    s_gate = s_rows[0] if len(s_rows) == 1 else jnp.concatenate(s_rows, axis=0)
    s_out = s_u * s_gate

    mix = jnp.concatenate([a_out, p_out, s_out], axis=-1)
    return mix, zext[n:n + HIST_ROWS, :], pext[n:n + HIST_ROWS, :]


def _route(h2_bf16, wr_t, br_col, n):
    logits = lax.dot_general(wr_t, h2_bf16, (((1,), (1,)), ((), ())),
                             preferred_element_type=jnp.float32) + br_col
    row8 = lax.broadcasted_iota(jnp.int32, (8, n), 0)
    lc = jnp.where(row8 < N_GROUPS, logits[0:8, :], -jnp.inf)
    mc = jnp.max(lc, axis=0, keepdims=True)
    g_sel = jnp.min(jnp.where(lc == mc, row8, 8), axis=0, keepdims=True)
    p_sel = 1.0 / jnp.sum(jnp.exp(lc - mc), axis=0, keepdims=True)
    lf = logits[8 + 3 * EXPERTS_PER_GROUP:8 + 4 * EXPERTS_PER_GROUP, :]
    for g in (2, 1, 0):
        lf = jnp.where(g_sel == g, logits[8 + g * EXPERTS_PER_GROUP:8 + (g + 1) * EXPERTS_PER_GROUP, :], lf)
    m1 = jnp.max(lf, axis=0, keepdims=True)
    i1 = jnp.min(jnp.where(lf == m1, row8, 8), axis=0, keepdims=True)
    lf2 = jnp.where(row8 == i1, -jnp.inf, lf)
    m2 = jnp.max(lf2, axis=0, keepdims=True)
    i2 = jnp.min(jnp.where(lf2 == m2, row8, 8), axis=0, keepdims=True)
    t = jnp.exp(m2 - m1)
    wa = 1.0 / (1.0 + t)
    wb = t / (1.0 + t)
    e0 = g_sel * EXPERTS_PER_GROUP + i1
    e1 = g_sel * EXPERTS_PER_GROUP + i2
    return e0, e1, p_sel * wa, p_sel * wb


def _place_rows(e0, e1, triu, tril_e, fill_ref, base_ref, np_ref, pexp_ref):
    row_e = lax.broadcasted_iota(jnp.int32, (N_EXPERTS, TL), 0)
    oh0 = row_e == e0
    oh1 = row_e == e1
    oh = jnp.where(oh0 | oh1, 1.0, 0.0)
    rank = jnp.dot(oh.astype(jnp.bfloat16), triu, preferred_element_type=jnp.float32).astype(jnp.int32)
    cnt = jnp.sum(oh, axis=1, keepdims=True).astype(jnp.int32)
    lower = jnp.where(e0 < row_e, 1.0, 0.0) + jnp.where(e1 < row_e, 1.0, 0.0)
    first = jnp.sum(lower, axis=1, keepdims=True).astype(jnp.int32)
    sorted_all = first + rank
    lpos0 = jnp.sum(jnp.where(oh0, sorted_all, 0), axis=0, keepdims=True)
    lpos1 = jnp.sum(jnp.where(oh1, sorted_all, 0), axis=0, keepdims=True)

    fill = fill_ref[:, 0:1]
    base = base_ref[:, 0:1]
    npages = np_ref[0:1, 0:1]
    total = fill + cnt
    need = (total > TM).astype(jnp.int32) + (total > 2 * TM).astype(jnp.int32)
    need_b = jnp.broadcast_to(need.astype(jnp.float32), (N_EXPERTS, 128)).astype(jnp.bfloat16)
    before = jnp.dot(tril_e, need_b, preferred_element_type=jnp.float32)[:, 0:1].astype(jnp.int32)
    new_id = npages + before
    new_base = new_id * TM
    fill_ref[...] = jnp.broadcast_to(total - need * TM, (N_EXPERTS, 128))
    base_ref[...] = jnp.broadcast_to(jnp.where(need > 0, new_base + (need - 1) * TM, base), (N_EXPERTS, 128))
    np_ref[...] = jnp.broadcast_to(npages + jnp.sum(need, axis=0, keepdims=True), (8, 128))
    page_lane = lax.broadcasted_iota(jnp.int32, (N_EXPERTS, PAGE_LANES), 1)
    expert_col = lax.broadcasted_iota(jnp.int32, (N_EXPERTS, 1), 0)
    owns = ((page_lane == new_id) & (need >= 1)) | ((page_lane == new_id + 1) & (need == 2))
    pexp_ref[...] = pexp_ref[...] + jnp.sum(jnp.where(owns, expert_col, 0), axis=0, keepdims=True)

    lane = lax.broadcasted_iota(jnp.int32, (N_EXPERTS, 128), 1)
    cols = jnp.where(lane == 0, cnt, jnp.where(lane == 1, first, jnp.where(lane == 2, fill,
                     jnp.where(lane == 3, base, jnp.where(lane == 4, new_base, 0)))))
    square = jnp.concatenate([cols, jnp.zeros((128 - N_EXPERTS, 128), jnp.int32)], axis=0)
    per_expert = square.astype(jnp.float32).T.astype(jnp.int32)[0:8, :]
    return lpos0, lpos1, per_expert


def _run_pieces(cnt, first, fill, base, new_base):
    n0 = jnp.minimum(cnt, TM - fill)
    n1 = jnp.minimum(cnt - n0, TM)
    n2 = cnt - n0 - n1
    return ((first, base + fill, n0), (first + n0, new_base, n1), (first + n0 + n1, new_base + TM, n2))


def _start_run_gather(runs_ref, tile, pages_ref, gbuf, gsem):
    def per_expert(e, c):
        k0 = (tile * N_EXPERTS + e) * RUN_FIELDS
        for j in range(3):
            src = runs_ref[k0 + 3 * j]
            dst = runs_ref[k0 + 3 * j + 1]
            n = runs_ref[k0 + 3 * j + 2]

            @pl.when(n > 0)
            def _():
                pltpu.make_async_copy(pages_ref.at[pl.ds(dst, n)], gbuf.at[pl.ds(src, n)], gsem).start()
        return c
    lax.fori_loop(0, N_EXPERTS, per_expert, 0)


def _unsort_results(lpos_ref, tile, gbuf, y0buf, y1buf, gsem):
    pltpu.make_async_copy(gbuf, gbuf, gsem).wait()
    k0 = tile * 2 * TL
    for r in range(TL):
        y0buf[r] = gbuf[lpos_ref[k0 + r]]
        y1buf[r] = gbuf[lpos_ref[k0 + TL + r]]


def _mixer_kernel(first_layer, *refs):
    if first_layer:
        xp_ref, xs_ref = refs[0:2]
        rest = refs[2:]
    else:
        lposp_ref, runsp_ref, xm_ref, yp_ref, gc_ref = refs[0:5]
        rest = refs[5:]
    (sconv_ref, spool_ref, g1_ref, win_ref, convw_ref, poolbd_ref, pscale_ref, wm_ref, bias_ref,
     wout_ref, g2_ref, wr_ref, br_ref, triu_ref, trile_ref,
     xmid_ref, hs_ref, lpos_ref, runs_ref, meta_ref, gcol_ref, cpr_ref, ppr_ref, csm_ref, psm_ref, sv_ref,
     zh_ref, ph_ref, h2buf, sbuf, posv, pexv, poss, pexs, fill_ref, base_ref, np_ref,
     pexp_ref, fill_s, base_s, np_s, pexp_s, cnt_s, rsem, msem) = rest[:45]
    if not first_layer:
        gbuf, y0buf, y1buf, gsem = rest[45:]

    i = pl.program_id(0)

    def load_x(prompt):
        if first_layer:
            return xp_ref[...] if prompt else xs_ref[...]
        g = gc_ref[...]
        return xm_ref[...] + g[:, 0:1] * _unpack_rows(y0buf[...]) + g[:, 1:2] * _unpack_rows(y1buf[...])

    row_m = lax.broadcasted_iota(jnp.int32, (SGU_HEADS * SGU_LEN, SGU_LEN), 0) % SGU_LEN
    col_m = lax.broadcasted_iota(jnp.int32, (SGU_HEADS * SGU_LEN, SGU_LEN), 1)

    def wait_rows(slot):
        pltpu.make_async_copy(sbuf.at[0], sbuf.at[0], rsem.at[slot]).wait()

    def sort_rows(slot):
        pltpu.make_async_copy(posv, poss.at[slot], msem).wait()
        pltpu.make_async_copy(pexv, pexs.at[slot], msem).wait()
        for r in range(TL):
            row = h2buf[r]
            sbuf[slot, poss[slot, 0, r]] = row
            sbuf[slot, poss[slot, 1, r]] = row

    def start_runs(tile, slot):
        def per_expert(e, c):
            pieces = _run_pieces(pexs[slot, 0, e], pexs[slot, 1, e], pexs[slot, 2, e], pexs[slot, 3, e],
                                 pexs[slot, 4, e])
            k0 = (tile * N_EXPERTS + e) * RUN_FIELDS
            for j, (src, dst, n) in enumerate(pieces):
                runs_ref[k0 + 3 * j] = src
                runs_ref[k0 + 3 * j + 1] = dst
                runs_ref[k0 + 3 * j + 2] = n

                @pl.when(n > 0)
                def _():
                    pltpu.make_async_copy(sbuf.at[slot, pl.ds(src, n)], hs_ref.at[pl.ds(dst, n)],
                                          rsem.at[slot]).start()
            return c
        lax.fori_loop(0, N_EXPERTS, per_expert, 0)

        @pl.when(tile >= 1)
        def _():
            wait_rows(1 - slot)

    def finish(x, mix):
        x_mid = x + jnp.dot(mix.astype(jnp.bfloat16), wout_ref[...], preferred_element_type=jnp.float32)
        xmid_ref[...] = x_mid
        h2 = _rms(x_mid, g2_ref[...])
        h2buf[...] = _pack_rows(h2)
        e0, e1, g0, g1 = _route(h2.astype(jnp.bfloat16), wr_ref[...], br_ref[...], TL)
        row128 = lax.broadcasted_iota(jnp.int32, (128, TL), 0)
        gcol_ref[...] = jnp.where(row128 == 0, g0, jnp.where(row128 == 1, g1, 0.0)).T
        lpos0, lpos1, per_expert = _place_rows(e0, e1, triu_ref[...], trile_ref[...], fill_ref, base_ref,
                                               np_ref, pexp_ref)
        row8 = lax.broadcasted_iota(jnp.int32, (8, TL), 0)
        posv[...] = jnp.where(row8 == 0, lpos0, jnp.where(row8 == 1, lpos1, 0))
        pexv[...] = per_expert
        row2 = lax.broadcasted_iota(jnp.int32, (2, TL), 0)
        lpos_ref[0] = jnp.where(row2 == 0, lpos0, lpos1)
        pltpu.make_async_copy(posv, poss.at[i % 2], msem).start()
        pltpu.make_async_copy(pexv, pexs.at[i % 2], msem).start()

    def project(x):
        h = _rms(x, g1_ref[...]).astype(jnp.bfloat16)
        return jnp.dot(h, win_ref[...], preferred_element_type=jnp.float32)

    def masked_wm():
        return jnp.where(col_m <= row_m, wm_ref[...], jnp.zeros_like(wm_ref[...]))

    @pl.when(i == 0)
    def _init():
        zh_ref[...] = jnp.zeros_like(zh_ref)
        ph_ref[...] = jnp.zeros_like(ph_ref)
        fill_ref[...] = jnp.full_like(fill_ref, TM)
        base_ref[...] = jnp.zeros_like(base_ref)
        np_ref[...] = jnp.zeros_like(np_ref)
        pexp_ref[...] = jnp.zeros_like(pexp_ref)
        h2buf[...] = jnp.zeros((TL, 1, D_PACK), jnp.uint32)
        row8 = lax.broadcasted_iota(jnp.int32, (8, TL), 0)
        posv[...] = jnp.minimum(row8, 1) * TL + lax.broadcasted_iota(jnp.int32, (8, TL), 1)
        pexv[...] = jnp.zeros((8, 128), jnp.int32)
        pltpu.make_async_copy(posv, poss.at[1], msem).start()
        pltpu.make_async_copy(pexv, pexs.at[1], msem).start()
        if not first_layer:
            _start_run_gather(runsp_ref, 0, yp_ref, gbuf, gsem)

    def _close_pages():
        copies = [pltpu.make_async_copy(fill_ref, fill_s, msem), pltpu.make_async_copy(base_ref, base_s, msem),
                  pltpu.make_async_copy(np_ref, np_s, msem), pltpu.make_async_copy(pexp_ref, pexp_s, msem)]
        for c in copies:
            c.start()
        for c in copies:
            c.wait()
        npages = np_s[0, 0]

        def zero_cnt(e, c):
            cnt_s[e] = 0
            return c
        lax.fori_loop(0, N_EXPERTS, zero_cnt, 0)

        def count(p, c):
            e = pexp_s[0, p]
            cnt_s[e] = cnt_s[e] + 1
            return c
        lax.fori_loop(0, npages, count, 0)

        def prefix(e, run):
            n = cnt_s[e]
            cnt_s[e] = run
            return run + n
        lax.fori_loop(0, N_EXPERTS, prefix, 0)

        def emit(p, c):
            e = pexp_s[0, p]
            q = cnt_s[e]
            cnt_s[e] = q + 1
            meta_ref[0, q] = p
            meta_ref[1, q] = e
            return c
        lax.fori_loop(0, npages, emit, 0)

        def pad_meta(q, c):
            meta_ref[0, q] = jnp.minimum(q, N_PAGES - 1)
            meta_ref[1, q] = meta_ref[1, npages - 1]
            return c
        lax.fori_loop(npages, PAGE_LANES, pad_meta, 0)

        def fill_row2(q, c):
            meta_ref[2, q] = npages
            return c
        lax.fori_loop(0, PAGE_LANES, fill_row2, 0)

        def mark_first(q, seen):
            is_first = (q < npages) & ((q == 0) | (meta_ref[1, q] != meta_ref[1, jnp.maximum(q - 1, 0)]))
            seen = seen + jnp.where(is_first, 1, 0)
            meta_ref[3, q] = jnp.where(is_first, 1, 0)
            meta_ref[4, q] = jnp.maximum(seen - 1, 0)
            meta_ref[5, q] = -1
            return seen
        lax.fori_loop(0, PAGE_LANES, mark_first, 0)

        def mark_next(t, carry):
            cur, nxt = carry
            q = npages - 1 - t
            e = meta_ref[1, q]
            nxt = jnp.where(e != cur, cur, nxt)
            meta_ref[5, q] = nxt
            return e, nxt
        lax.fori_loop(0, npages, mark_next, (jnp.int32(-1), jnp.int32(-1)))

        h2buf[...] = jnp.zeros((TL, 1, D_PACK), jnp.uint32)

        def tails(start):
            def tail(e, c):
                f = fill_s[e, 0]
                b = base_s[e, 0]

                @pl.when(f < TM)
                def _zero_tail():
                    cp = pltpu.make_async_copy(h2buf.at[pl.ds(0, TM - f)], hs_ref.at[pl.ds(b + f, TM - f)],
                                               rsem.at[0])
                    cp.start() if start else cp.wait()
                return c
            lax.fori_loop(0, N_EXPERTS, tail, 0)

            def unused(p, c):
                cp = pltpu.make_async_copy(h2buf.at[pl.ds(0, TM)], hs_ref.at[pl.ds(p * TM, TM)], rsem.at[0])
                cp.start() if start else cp.wait()
                return c
            lax.fori_loop(npages, N_PAGES, unused, 0)

        tails(True)
        tails(False)

    prev_slot = (i + 1) % 2

    if not first_layer:
        @pl.when(i >= 0)
        def _results():
            _unsort_results(lposp_ref, i, gbuf, y0buf, y1buf, gsem)

            @pl.when(i + 1 < N_TILES)
            def _():
                _start_run_gather(runsp_ref, i + 1, yp_ref, gbuf, gsem)

    @pl.when(i < N_PROMPT_TILES)
    def _prompt():
        s = i % TILES_PER_SEQ
        x = load_x(True)
        proj = project(x)
        zhist = jnp.where(s == 0, 0.0, zh_ref[...])
        phist = jnp.where(s == 0, 0.0, ph_ref[...])
        mix, znew, pnew = _mix_rows(proj, zhist, phist, s * TL, TL, convw_ref[...], poolbd_ref[...],
                                    pscale_ref[...], masked_wm(), bias_ref[...])
        zh_ref[...] = znew
        ph_ref[...] = pnew
        cpr_ref[0] = znew
        ppr_ref[0] = pnew
        sort_rows(prev_slot)
        finish(x, mix)

    @pl.when(i <= N_PROMPT_TILES - 1)
    def _prompt_runs():
        start_runs(jnp.maximum(i - 1, 0), prev_slot)

    @pl.when(i == N_PROMPT_TILES)
    def _sample():
        x = load_x(False)
        proj = project(x)
        wm = masked_wm()
        mixes = []
        for b in range(DEC_BATCH):
            rows = slice(b * DEC_SEQ, (b + 1) * DEC_SEQ)
            mix, znew, pnew = _mix_rows(proj[rows, :], sconv_ref[b], spool_ref[b], PAST_LEN, DEC_SEQ,
                                        convw_ref[...], poolbd_ref[...], pscale_ref[...], wm, bias_ref[...])
            csm_ref[b] = znew
            psm_ref[b] = pnew
            mixes.append(mix)
        sv_ref[...] = proj[:, 1792:2176]
        sort_rows(prev_slot)
        finish(x, jnp.concatenate(mixes, axis=0))
        start_runs(i - 1, prev_slot)
        sort_rows(i % 2)
        start_runs(i, i % 2)
        wait_rows(i % 2)
        _close_pages()


def _mixer_call(first_layer, xs, sconv_pad, spool_pad, g1, w_in, conv_w, pool_bd, pool_scale, wm_all,
                bias_full, w_out, g2, wr_t, br_col, triu, tril_e):
    tile = lambda i, *_: (i, 0)
    prompt_tile = lambda i, *_: (jnp.minimum(i, N_PROMPT_TILES - 1), 0)
    const2 = lambda i, *_: (0, 0)
    const3 = lambda i, *_: (0, 0, 0)
    if first_layer:
        prefetch = ()
        x_specs = [pl.BlockSpec((TL, D_MODEL), prompt_tile), pl.BlockSpec((TL, D_MODEL), const2)]
    else:
        prefetch = xs[0:2]
        xs = xs[2:]
        x_specs = [pl.BlockSpec((TL, D_MODEL), tile), pl.BlockSpec(memory_space=pl.ANY),
                   pl.BlockSpec((TL, 128), tile)]
    full = lambda a: pl.BlockSpec(a.shape, const2 if a.ndim == 2 else const3, pipeline_mode=pl.Buffered(1))
    weights = [sconv_pad, spool_pad, g1, w_in, conv_w, pool_bd, pool_scale, wm_all, bias_full, w_out, g2,
               wr_t, br_col, triu, tril_e]
    in_specs = x_specs + [full(a) for a in weights]
    seq_of = lambda i, *_: (jnp.minimum(i // TILES_PER_SEQ, BATCH - 1), 0, 0)
    out_shape = [
        jax.ShapeDtypeStruct((T_ALL, D_MODEL), jnp.float32),
        jax.ShapeDtypeStruct((PAGE_ROWS, 1, D_PACK), jnp.uint32),
        jax.ShapeDtypeStruct((N_TILES, 2, TL), jnp.int32),
        jax.ShapeDtypeStruct((N_TILES * RUNS_PER_TILE,), jnp.int32),
        jax.ShapeDtypeStruct((6, PAGE_LANES), jnp.int32),
        jax.ShapeDtypeStruct((T_ALL, 128), jnp.float32),
        jax.ShapeDtypeStruct((BATCH, HIST_ROWS, D_CONV), jnp.float32),
        jax.ShapeDtypeStruct((BATCH, HIST_ROWS, D_POOL), jnp.float32),
        jax.ShapeDtypeStruct((DEC_BATCH, HIST_ROWS, D_CONV), jnp.float32),
        jax.ShapeDtypeStruct((DEC_BATCH, HIST_ROWS, D_POOL), jnp.float32),
        jax.ShapeDtypeStruct((T_SAMPLE, D_SGU), jnp.float32),
    ]
    out_specs = [
        pl.BlockSpec((TL, D_MODEL), tile),
        pl.BlockSpec(memory_space=pl.ANY),
        pl.BlockSpec((1, 2, TL), lambda i, *_: (i, 0, 0)),
        pl.BlockSpec(memory_space=pltpu.SMEM),
        pl.BlockSpec(memory_space=pltpu.SMEM),
        pl.BlockSpec((TL, 128), tile),
        pl.BlockSpec((1, HIST_ROWS, D_CONV), seq_of),
        pl.BlockSpec((1, HIST_ROWS, D_POOL), seq_of),
        pl.BlockSpec((DEC_BATCH, HIST_ROWS, D_CONV), const3),
        pl.BlockSpec((DEC_BATCH, HIST_ROWS, D_POOL), const3),
        pl.BlockSpec((T_SAMPLE, D_SGU), const2),
    ]
    scratch = [
        pltpu.VMEM((HIST_ROWS, D_CONV), jnp.float32),
        pltpu.VMEM((HIST_ROWS, D_POOL), jnp.float32),
        pltpu.VMEM((TL, 1, D_PACK), jnp.uint32),
        pltpu.VMEM((2, 2 * TL, 1, D_PACK), jnp.uint32),
        pltpu.VMEM((8, TL), jnp.int32),
        pltpu.VMEM((8, 128), jnp.int32),
        pltpu.SMEM((2, 8, TL), jnp.int32),
        pltpu.SMEM((2, 8, 128), jnp.int32),
        pltpu.VMEM((N_EXPERTS, 128), jnp.int32),
        pltpu.VMEM((N_EXPERTS, 128), jnp.int32),
        pltpu.VMEM((8, 128), jnp.int32),
        pltpu.VMEM((8, PAGE_LANES), jnp.int32),
        pltpu.SMEM((N_EXPERTS, 128), jnp.int32),
        pltpu.SMEM((N_EXPERTS, 128), jnp.int32),
        pltpu.SMEM((8, 128), jnp.int32),
        pltpu.SMEM((8, PAGE_LANES), jnp.int32),
        pltpu.SMEM((N_EXPERTS,), jnp.int32),
        pltpu.SemaphoreType.DMA((2,)),
        pltpu.SemaphoreType.DMA(()),
    ]
    if not first_layer:
        scratch += [
            pltpu.VMEM((2 * TL, 1, D_PACK), jnp.uint32),
            pltpu.VMEM((TL, 1, D_PACK), jnp.uint32),
            pltpu.VMEM((TL, 1, D_PACK), jnp.uint32),
            pltpu.SemaphoreType.DMA(()),
        ]
    grid_spec = pltpu.PrefetchScalarGridSpec(num_scalar_prefetch=len(prefetch), grid=(N_TILES,),
                                             in_specs=in_specs, out_specs=out_specs, scratch_shapes=scratch)
    return pl.pallas_call(
        functools.partial(_mixer_kernel, first_layer),
        grid_spec=grid_spec,
        out_shape=out_shape,
        compiler_params=pltpu.CompilerParams(dimension_semantics=("arbitrary",),
                                             vmem_limit_bytes=VMEM_LIMIT),
        name="mixer_first" if first_layer else "mixer_next",
    )(*prefetch, *xs, *weights)


def _expert_kernel(layer, meta_ref, hs_ref, wg_hbm, wu_hbm, wd_hbm, y_ref, xbuf, wg32, wu32, wd32, wg16, wu16,
                   wd16, wsem):
    s = pl.program_id(0)
    npages = meta_ref[2, 0]

    def weight_copies(expert, slot):
        return [pltpu.make_async_copy(wg_hbm.at[layer, expert], wg32.at[slot], wsem.at[slot]),
                pltpu.make_async_copy(wu_hbm.at[layer, expert], wu32.at[slot], wsem.at[slot]),
                pltpu.make_async_copy(wd_hbm.at[layer, expert], wd32.at[slot], wsem.at[slot])]

    @pl.when(s == 0)
    def _first_weights():
        for c in weight_copies(meta_ref[1, 0], 0):
            c.start()

    @pl.when((s < npages) & (meta_ref[3, s] == 1))
    def _switch_expert():
        slot = meta_ref[4, s] % 2
        for c in weight_copies(meta_ref[1, s], slot):
            c.wait()
        nxt = meta_ref[5, s]

        @pl.when(nxt >= 0)
        def _():
            for c in weight_copies(nxt, 1 - slot):
                c.start()
        wg16[...] = wg32[slot].astype(jnp.bfloat16)
        wu16[...] = wu32[slot].astype(jnp.bfloat16)
        wd16[...] = wd32[slot].astype(jnp.bfloat16)

    @pl.when(s < npages)
    def _page():
        xbuf[...] = _unpack_rows(hs_ref[...])
        x = xbuf[...].astype(jnp.bfloat16)
        hg = jnp.dot(x, wg16[...], preferred_element_type=jnp.float32)
        hu = jnp.dot(x, wu16[...], preferred_element_type=jnp.float32)
        h = (hg * jax.nn.sigmoid(hg) * hu).astype(jnp.bfloat16)
        out = jnp.dot(h, wd16[...], preferred_element_type=jnp.float32)
        y_ref[...] = _pack_rows(out)

    @pl.when(s >= npages)
    def _unused_page():
        y_ref[...] = jnp.zeros((TM, 1, D_PACK), jnp.uint32)


def _expert_call(layer, meta, hs, w_gate, w_up, w_down):
    page = lambda s, meta: (meta[0, s], 0, 0)
    grid_spec = pltpu.PrefetchScalarGridSpec(
        num_scalar_prefetch=1,
        grid=(N_PAGES,),
        in_specs=[
            pl.BlockSpec((TM, 1, D_PACK), page),
            pl.BlockSpec(memory_space=pl.ANY),
            pl.BlockSpec(memory_space=pl.ANY),
            pl.BlockSpec(memory_space=pl.ANY),
        ],
        out_specs=pl.BlockSpec((TM, 1, D_PACK), page),
        scratch_shapes=[
            pltpu.VMEM((TM, D_MODEL), jnp.float32),
            pltpu.VMEM((2, D_MODEL, D_EXPERT), jnp.float32),
            pltpu.VMEM((2, D_MODEL, D_EXPERT), jnp.float32),
            pltpu.VMEM((2, D_EXPERT, D_MODEL), jnp.float32),
            pltpu.VMEM((D_MODEL, D_EXPERT), jnp.bfloat16),
            pltpu.VMEM((D_MODEL, D_EXPERT), jnp.bfloat16),
            pltpu.VMEM((D_EXPERT, D_MODEL), jnp.bfloat16),
            pltpu.SemaphoreType.DMA((2,)),
        ],
    )
    return pl.pallas_call(
        functools.partial(_expert_kernel, layer),
        grid_spec=grid_spec,
        out_shape=jax.ShapeDtypeStruct((PAGE_ROWS, 1, D_PACK), jnp.uint32),
        compiler_params=pltpu.CompilerParams(dimension_semantics=("arbitrary",),
                                             vmem_limit_bytes=VMEM_LIMIT),
        name="experts",
    )(meta, hs, w_gate, w_up, w_down)


def _final_kernel(lposp_ref, runsp_ref, xm_ref, yp_ref, gc_ref, g_ref, yp_out, ys_out, gbuf, y0buf, y1buf, gsem):
    i = pl.program_id(0)

    @pl.when(i == 0)
    def _():
        _start_run_gather(runsp_ref, 0, yp_ref, gbuf, gsem)

    _unsort_results(lposp_ref, i, gbuf, y0buf, y1buf, gsem)

    @pl.when(i + 1 < N_TILES)
    def _():
        _start_run_gather(runsp_ref, i + 1, yp_ref, gbuf, gsem)

    g = gc_ref[...]
    x = xm_ref[...] + g[:, 0:1] * _unpack_rows(y0buf[...]) + g[:, 1:2] * _unpack_rows(y1buf[...])
    out = _rms(x, g_ref[...])

    @pl.when(i < N_PROMPT_TILES)
    def _():
        yp_out[...] = out

    @pl.when(i == N_PROMPT_TILES)
    def _():
        ys_out[...] = out


def _final_call(lpos, runs, x_mid, y_pages, gcol, g):
    tile = lambda i, *_: (i, 0)
    grid_spec = pltpu.PrefetchScalarGridSpec(
        num_scalar_prefetch=2,
        grid=(N_TILES,),
        in_specs=[pl.BlockSpec((TL, D_MODEL), tile), pl.BlockSpec(memory_space=pl.ANY),
                  pl.BlockSpec((TL, 128), tile), pl.BlockSpec((1, D_MODEL), lambda i, *_: (0, 0))],
        out_specs=[pl.BlockSpec((TL, D_MODEL), lambda i, *_: (jnp.minimum(i, N_PROMPT_TILES - 1), 0)),
                   pl.BlockSpec((TL, D_MODEL), lambda i, *_: (0, 0))],
        scratch_shapes=[pltpu.VMEM((2 * TL, 1, D_PACK), jnp.uint32), pltpu.VMEM((TL, 1, D_PACK), jnp.uint32),
                        pltpu.VMEM((TL, 1, D_PACK), jnp.uint32), pltpu.SemaphoreType.DMA(())],
    )
    return pl.pallas_call(
        _final_kernel,
        grid_spec=grid_spec,
        out_shape=[jax.ShapeDtypeStruct((T_PROMPT, D_MODEL), jnp.float32),
                   jax.ShapeDtypeStruct((T_SAMPLE, D_MODEL), jnp.float32)],
        compiler_params=pltpu.CompilerParams(dimension_semantics=("arbitrary",),
                                             vmem_limit_bytes=VMEM_LIMIT),
        name="final_norm",
    )(lpos, runs, x_mid, y_pages, gcol, g)


def kernel(x_prompt, x_sample, state_conv, state_pool, norm1_g, w_in, conv_w, pool_w, pool_scale, sgu_w, sgu_b, w_out, norm2_g, router_coarse_w, router_coarse_b, router_fine_w, router_fine_b, moe_w_gate, moe_w_up, moe_w_down, final_norm_g):
    bf16 = jnp.bfloat16
    xs = (x_prompt.reshape(T_PROMPT, D_MODEL), x_sample.reshape(T_SAMPLE, D_MODEL))
    sconv_pad = jnp.pad(state_conv, ((0, 0), (0, 0), (HIST_ROWS - (CONV_WIDTH - 1), 0), (0, 0)))
    spool_pad = jnp.pad(state_pool, ((0, 0), (0, 0), (HIST_ROWS - POOL_HIST, 0), (0, 0)))
    idx = jnp.arange(TL, dtype=jnp.int32)
    triu = (idx[:, None] < idx[None, :]).astype(bf16)
    ide = jnp.arange(N_EXPERTS, dtype=jnp.int32)
    tril_e = (ide[None, :] < ide[:, None]).astype(bf16)
    conv_pr, pool_pr, conv_sm, pool_sm, sgu_v = [], [], [], [], []
    x_mid = y_pages = gcol = lpos = runs = None
    for l in range(DEPTH):
        pool_bd = jax.scipy.linalg.block_diag(*[pool_w[l, g] for g in range(4)]).astype(bf16)
        wm_all = sgu_w[l].reshape(SGU_HEADS * SGU_LEN, SGU_LEN).astype(bf16)
        bias_full = jnp.repeat(sgu_b[l].T, SGU_HEAD_DIM, axis=1)
        wr_t = jnp.zeros((ROUTER_ROWS, D_MODEL), jnp.float32)
        wr_t = wr_t.at[0:N_GROUPS].set(router_coarse_w[l].T).at[8:].set(router_fine_w[l].T).astype(bf16)
        br_col = jnp.zeros((ROUTER_ROWS, 1), jnp.float32)
        br_col = br_col.at[0:N_GROUPS, 0].set(router_coarse_b[l]).at[8:, 0].set(router_fine_b[l])
        outs = _mixer_call(l == 0, xs, sconv_pad[l], spool_pad[l], norm1_g[l].reshape(1, D_MODEL),
                           w_in[l].astype(bf16), conv_w[l], pool_bd, pool_scale[l].reshape(1, D_POOL),
                           wm_all, bias_full, w_out[l].astype(bf16), norm2_g[l].reshape(1, D_MODEL),
                           wr_t, br_col, triu, tril_e)
        x_mid, hs, lpos, runs, meta, gcol, cpr, ppr, csm, psm, sv = outs
        lpos = lpos.reshape(N_TILES * 2 * TL)
        conv_pr.append(cpr[:, HIST_ROWS - (CONV_WIDTH - 1):, :])
        pool_pr.append(ppr[:, HIST_ROWS - POOL_HIST:, :])
        conv_sm.append(csm[:, HIST_ROWS - (CONV_WIDTH - 1):, :])
        pool_sm.append(psm[:, HIST_ROWS - POOL_HIST:, :])
        sgu_v.append(sv.reshape(DEC_BATCH, DEC_SEQ, D_SGU))
        y_pages = _expert_call(l, meta, hs, moe_w_gate, moe_w_up, moe_w_down)
        xs = (lpos, runs, x_mid, y_pages, gcol)
    y_prompt, y_sample = _final_call(lpos, runs, x_mid, y_pages, gcol, final_norm_g.reshape(1, D_MODEL))
    return (y_prompt.reshape(BATCH, SEQ, D_MODEL), y_sample.reshape(DEC_BATCH, DEC_SEQ, D_MODEL),
            jnp.stack(conv_pr), jnp.stack(pool_pr), jnp.stack(conv_sm), jnp.stack(pool_sm),
            jnp.stack(sgu_v))
```

```python
import functools

import jax
import jax.numpy as jnp
from jax import lax
from jax.experimental import pallas as pl
from jax.experimental.pallas import tpu as pltpu

D_MODEL = 1024
BATCH = 8
SEQ = 2048
DEPTH = 2
DEC_BATCH = 8
DEC_SEQ = 64
PAST_LEN = 1024
D_CONV = 384
CONV_WIDTH = 3
D_POOL = 256
POOL_HIST = 15
D_SGU = 384
SGU_HEADS = 4
SGU_HEAD_DIM = 96
SGU_LEN = 128
D_PROJ = 2176
N_GROUPS = 4
EXPERTS_PER_GROUP = 8
N_EXPERTS = 32
D_EXPERT = 512
EPS = 1e-6

T_PROMPT = BATCH * SEQ
T_SAMPLE = DEC_BATCH * DEC_SEQ
T_ALL = T_PROMPT + T_SAMPLE
TL = 512
TILES_PER_SEQ = SEQ // TL
N_PROMPT_TILES = T_PROMPT // TL
N_TILES = N_PROMPT_TILES + T_SAMPLE // TL
HIST_ROWS = 16
ROUTER_ROWS = 8 + N_EXPERTS
TM = 256
N_ASSIGN = 2 * T_ALL
N_PAGES = N_ASSIGN // TM + N_EXPERTS
PAGE_LANES = 256
PAGE_ROWS = N_PAGES * TM
RUN_FIELDS = 9
RUNS_PER_TILE = N_EXPERTS * RUN_FIELDS
D_PACK = D_MODEL // 2
VMEM_LIMIT = 56 * 1024 * 1024

assert TL == 2 * TM and N_PAGES <= PAGE_LANES and T_ALL == N_TILES * TL


def _pack_rows(x):
    words = pltpu.pack_elementwise([x[:, :D_PACK], x[:, D_PACK:]], packed_dtype=jnp.bfloat16)
    return words.reshape(x.shape[0], 1, D_PACK)


def _unpack_rows(words):
    w = words.reshape(words.shape[0], D_PACK)
    halves = [pltpu.unpack_elementwise(w, index=k, packed_dtype=jnp.bfloat16, unpacked_dtype=jnp.float32)
              for k in range(2)]
    return jnp.concatenate(halves, axis=1)


def _split(a):
    hi = a.astype(jnp.bfloat16)
    return hi, (a - hi.astype(jnp.float32)).astype(jnp.bfloat16)


def _dot_split(a, b_hi, b_lo):
    a_hi, a_lo = _split(a)
    dot = functools.partial(jnp.dot, preferred_element_type=jnp.float32)
    return dot(a_hi, b_hi) + dot(a_lo, b_hi) + dot(a_hi, b_lo)


def _rms(x, g):
    return x * lax.rsqrt(jnp.mean(x * x, axis=-1, keepdims=True) + EPS) * g


def _mix_rows(proj, zhist, phist, pos0, n, conv_w, pool_bd, pool_scale, wm_all, bias_full, pool_lo=None,
              wm_lo=None):
    a_b = proj[:, 0:384]
    a_c = proj[:, 384:768]
    a_h = proj[:, 768:1152]
    p_in = proj[:, 1152:1408]
    s_u = proj[:, 1408:1792]
    s_v = proj[:, 1792:2176]

    z = a_c * a_h
    zext = jnp.concatenate([zhist, z], axis=0)
    conv_y = (conv_w[0:1, :] * pltpu.roll(zext, 2, 0)[HIST_ROWS:, :]
              + conv_w[1:2, :] * pltpu.roll(zext, 1, 0)[HIST_ROWS:, :]
              + conv_w[2:3, :] * z)
    a_out = a_b * conv_y

    pext = jnp.concatenate([phist, p_in], axis=0)
    s2 = pext + pltpu.roll(pext, 1, 0)
    s4 = s2 + pltpu.roll(s2, 2, 0)
    s8 = s4 + pltpu.roll(s4, 4, 0)
    s16 = s8 + pltpu.roll(s8, 8, 0)
    lane = lax.broadcasted_iota(jnp.int32, (1, D_POOL), 1)
    wsum = jnp.where(lane < 64, s2, jnp.where(lane < 128, s4, jnp.where(lane < 192, s8, s16)))
    wsum = wsum[HIST_ROWS:, :]
    win = jnp.where(lane < 64, 2.0, jnp.where(lane < 128, 4.0, jnp.where(lane < 192, 8.0, 16.0)))
    pos = (pos0 + lax.broadcasted_iota(jnp.int32, (n, 1), 0) + 1).astype(jnp.float32)
    cnt = jnp.minimum(win, pos)
    pooled = wsum / cnt - p_in
    if pool_lo is None:
        p_out = jnp.dot(pooled.astype(jnp.bfloat16), pool_bd, preferred_element_type=jnp.float32)
    else:
        p_out = _dot_split(pooled, pool_bd, pool_lo)
    p_out = p_out * pool_scale

    lane_s = lax.broadcasted_iota(jnp.int32, (1, D_SGU), 1)
    chunk = min(n, SGU_LEN)
    def head_rows(w):
        if chunk == SGU_LEN:
            return w
        return jnp.concatenate([w[h * SGU_LEN:h * SGU_LEN + chunk, 0:chunk] for h in range(SGU_HEADS)], axis=0)
    wm = head_rows(wm_all)
    s_rows = []
    for c in range(n // chunk):
        v_c = s_v[c * chunk:(c + 1) * chunk, :]
        if wm_lo is None:
            r = jnp.dot(wm, v_c.astype(jnp.bfloat16), preferred_element_type=jnp.float32)
        else:
            v_hi, v_lo = _split(v_c)
            r = (jnp.dot(wm, v_hi, preferred_element_type=jnp.float32)
                 + jnp.dot(head_rows(wm_lo), v_hi, preferred_element_type=jnp.float32)
                 + jnp.dot(wm, v_lo, preferred_element_type=jnp.float32))
        s_c = jnp.where(lane_s < 96, r[0:chunk],
                        jnp.where(lane_s < 192, r[chunk:2 * chunk],
                                  jnp.where(lane_s < 288, r[2 * chunk:3 * chunk], r[3 * chunk:4 * chunk])))
        s_rows.append(s_c + bias_full[0:chunk, :])
    s_gate = s_rows[0] if len(s_rows) == 1 else jnp.concatenate(s_rows, axis=0)
    s_out = s_u * s_gate

    mix = jnp.concatenate([a_out, p_out, s_out], axis=-1)
    return mix, zext[n:n + HIST_ROWS, :], pext[n:n + HIST_ROWS, :]


def _route(h2, wr_t, br_col, n):
    h_hi = h2.astype(jnp.bfloat16)
    h_lo = (h2 - h_hi.astype(jnp.float32)).astype(jnp.bfloat16)
    nt = (((1,), (1,)), ((), ()))
    by_hi = lax.dot_general(wr_t, h_hi, nt, preferred_element_type=jnp.float32)
    by_lo = lax.dot_general(wr_t[0:ROUTER_ROWS, :], h_lo, nt, preferred_element_type=jnp.float32)
    logits = by_hi[0:ROUTER_ROWS, :] + by_hi[ROUTER_ROWS:, :] + by_lo + br_col
    row8 = lax.broadcasted_iota(jnp.int32, (8, n), 0)
    lc = jnp.where(row8 < N_GROUPS, logits[0:8, :], -jnp.inf)
    mc = jnp.max(lc, axis=0, keepdims=True)
    g_sel = jnp.min(jnp.where(lc == mc, row8, 8), axis=0, keepdims=True)
    p_sel = 1.0 / jnp.sum(jnp.exp(lc - mc), axis=0, keepdims=True)
    lf = logits[8 + 3 * EXPERTS_PER_GROUP:8 + 4 * EXPERTS_PER_GROUP, :]
    for g in (2, 1, 0):
        lf = jnp.where(g_sel == g, logits[8 + g * EXPERTS_PER_GROUP:8 + (g + 1) * EXPERTS_PER_GROUP, :], lf)
    m1 = jnp.max(lf, axis=0, keepdims=True)
    i1 = jnp.min(jnp.where(lf == m1, row8, 8), axis=0, keepdims=True)
    lf2 = jnp.where(row8 == i1, -jnp.inf, lf)
    m2 = jnp.max(lf2, axis=0, keepdims=True)
    i2 = jnp.min(jnp.where(lf2 == m2, row8, 8), axis=0, keepdims=True)
    t = jnp.exp(m2 - m1)
    wa = 1.0 / (1.0 + t)
    wb = t / (1.0 + t)
    e0 = g_sel * EXPERTS_PER_GROUP + i1
    e1 = g_sel * EXPERTS_PER_GROUP + i2
    return e0, e1, p_sel * wa, p_sel * wb


def _place_rows(e0, e1, triu, tril_e, fill_ref, base_ref, np_ref, pexp_ref):
    row_e = lax.broadcasted_iota(jnp.int32, (N_EXPERTS, TL), 0)
    oh0 = row_e == e0
    oh1 = row_e == e1
    oh = jnp.where(oh0 | oh1, 1.0, 0.0)
    rank = jnp.dot(oh.astype(jnp.bfloat16), triu, preferred_element_type=jnp.float32).astype(jnp.int32)
    cnt = jnp.sum(oh, axis=1, keepdims=True).astype(jnp.int32)
    lower = jnp.where(e0 < row_e, 1.0, 0.0) + jnp.where(e1 < row_e, 1.0, 0.0)
    first = jnp.sum(lower, axis=1, keepdims=True).astype(jnp.int32)
    sorted_all = first + rank
    lpos0 = jnp.sum(jnp.where(oh0, sorted_all, 0), axis=0, keepdims=True)
    lpos1 = jnp.sum(jnp.where(oh1, sorted_all, 0), axis=0, keepdims=True)

    fill = fill_ref[:, 0:1]
    base = base_ref[:, 0:1]
    npages = np_ref[0:1, 0:1]
    total = fill + cnt
    need = (total > TM).astype(jnp.int32) + (total > 2 * TM).astype(jnp.int32)
    need_b = jnp.broadcast_to(need.astype(jnp.float32), (N_EXPERTS, 128)).astype(jnp.bfloat16)
    before = jnp.dot(tril_e, need_b, preferred_element_type=jnp.float32)[:, 0:1].astype(jnp.int32)
    new_id = npages + before
    new_base = new_id * TM
    fill_ref[...] = jnp.broadcast_to(total - need * TM, (N_EXPERTS, 128))
    base_ref[...] = jnp.broadcast_to(jnp.where(need > 0, new_base + (need - 1) * TM, base), (N_EXPERTS, 128))
    np_ref[...] = jnp.broadcast_to(npages + jnp.sum(need, axis=0, keepdims=True), (8, 128))
    page_lane = lax.broadcasted_iota(jnp.int32, (N_EXPERTS, PAGE_LANES), 1)
    expert_col = lax.broadcasted_iota(jnp.int32, (N_EXPERTS, 1), 0)
    owns = ((page_lane == new_id) & (need >= 1)) | ((page_lane == new_id + 1) & (need == 2))
    pexp_ref[...] = pexp_ref[...] + jnp.sum(jnp.where(owns, expert_col, 0), axis=0, keepdims=True)

    lane = lax.broadcasted_iota(jnp.int32, (N_EXPERTS, 128), 1)
    cols = jnp.where(lane == 0, cnt, jnp.where(lane == 1, first, jnp.where(lane == 2, fill,
                     jnp.where(lane == 3, base, jnp.where(lane == 4, new_base, 0)))))
    square = jnp.concatenate([cols, jnp.zeros((128 - N_EXPERTS, 128), jnp.int32)], axis=0)
    per_expert = square.astype(jnp.float32).T.astype(jnp.int32)[0:8, :]
    return lpos0, lpos1, per_expert


def _run_pieces(cnt, first, fill, base, new_base):
    n0 = jnp.minimum(cnt, TM - fill)
    n1 = jnp.minimum(cnt - n0, TM)
    n2 = cnt - n0 - n1
    return ((first, base + fill, n0), (first + n0, new_base, n1), (first + n0 + n1, new_base + TM, n2))


def _start_run_gather(runs_ref, tile, pages_ref, gbuf, gsem):
    def per_expert(e, c):
        k0 = (tile * N_EXPERTS + e) * RUN_FIELDS
        for j in range(3):
            src = runs_ref[k0 + 3 * j]
            dst = runs_ref[k0 + 3 * j + 1]
            n = runs_ref[k0 + 3 * j + 2]

            @pl.when(n > 0)
            def _():
                pltpu.make_async_copy(pages_ref.at[pl.ds(dst, n)], gbuf.at[pl.ds(src, n)], gsem).start()
        return c
    lax.fori_loop(0, N_EXPERTS, per_expert, 0)


def _unsort_results(lpos_ref, tile, gbuf, y0buf, y1buf, slot):
    k0 = tile * 2 * TL
    for r in range(TL):
        y0buf[slot, r] = gbuf[lpos_ref[k0 + r]]
        y1buf[slot, r] = gbuf[lpos_ref[k0 + TL + r]]


def _first_results(lpos_ref, runs_ref, pages_ref, gbuf, y0buf, y1buf, gsem):
    _start_run_gather(runs_ref, 0, pages_ref, gbuf, gsem)
    pltpu.make_async_copy(gbuf, gbuf, gsem).wait()
    _unsort_results(lpos_ref, 0, gbuf, y0buf, y1buf, 0)
    _start_run_gather(runs_ref, 1, pages_ref, gbuf, gsem)


def _mixer_kernel(first_layer, *refs):
    if first_layer:
        xp_ref, xs_ref = refs[0:2]
        rest = refs[2:]
    else:
        lposp_ref, runsp_ref, xm_ref, yp_ref, gc_ref = refs[0:5]
        rest = refs[5:]
    (sconv_ref, spool_ref, g1_ref, win_ref, convw_ref, poolbd_ref, pscale_ref, wm_ref, bias_ref,
     wout_ref, g2_ref, wr_ref, br_ref, triu_ref, trile_ref, winlo_ref, woutlo_ref, poollo_ref, wmlo_ref,
     xmid_ref, hs_ref, lpos_ref, runs_ref, meta_ref, gcol_ref, cpr_ref, ppr_ref, csm_ref, psm_ref, sv_ref,
     zh_ref, ph_ref, h2buf, sbuf, posv, pexv, poss, pexs, fill_ref, base_ref, np_ref,
     pexp_ref, fill_s, base_s, np_s, pexp_s, cnt_s, rsem, msem) = rest[:49]
    if not first_layer:
        gbuf, y0buf, y1buf, gsem = rest[49:]

    i = pl.program_id(0)

    def load_x(prompt):
        if first_layer:
            return xp_ref[...] if prompt else xs_ref[...]
        g = gc_ref[...]
        return (xm_ref[...] + g[:, 0:1] * _unpack_rows(y0buf[i % 2])
                + g[:, 1:2] * _unpack_rows(y1buf[i % 2]))

    row_m = lax.broadcasted_iota(jnp.int32, (SGU_HEADS * SGU_LEN, SGU_LEN), 0) % SGU_LEN
    col_m = lax.broadcasted_iota(jnp.int32, (SGU_HEADS * SGU_LEN, SGU_LEN), 1)

    def wait_rows(slot):
        pltpu.make_async_copy(sbuf.at[0], sbuf.at[0], rsem.at[slot]).wait()

    def wait_positions(slot):
        pltpu.make_async_copy(posv, poss.at[slot], msem).wait()
        pltpu.make_async_copy(pexv, pexs.at[slot], msem).wait()

    def sort_rows(slot):
        for r in range(TL):
            row = h2buf[r]
            sbuf[slot, poss[slot, 0, r]] = row
            sbuf[slot, poss[slot, 1, r]] = row

    def start_runs(tile, slot):
        def per_expert(e, c):
            pieces = _run_pieces(pexs[slot, 0, e], pexs[slot, 1, e], pexs[slot, 2, e], pexs[slot, 3, e],
                                 pexs[slot, 4, e])
            k0 = (tile * N_EXPERTS + e) * RUN_FIELDS
            for j, (src, dst, n) in enumerate(pieces):
                runs_ref[k0 + 3 * j] = src
                runs_ref[k0 + 3 * j + 1] = dst
                runs_ref[k0 + 3 * j + 2] = n

                @pl.when(n > 0)
                def _():
                    pltpu.make_async_copy(sbuf.at[slot, pl.ds(src, n)], hs_ref.at[pl.ds(dst, n)],
                                          rsem.at[slot]).start()
            return c
        lax.fori_loop(0, N_EXPERTS, per_expert, 0)

        @pl.when(tile >= 1)
        def _():
            wait_rows(1 - slot)

    def finish(x, mix, precise=False):
        if precise:
            x_mid = x + _dot_split(mix, wout_ref[...], woutlo_ref[...])
        else:
            x_mid = x + jnp.dot(mix.astype(jnp.bfloat16), wout_ref[...], preferred_element_type=jnp.float32)
        xmid_ref[...] = x_mid
        h2 = _rms(x_mid, g2_ref[...])
        h2buf[...] = _pack_rows(h2)
        e0, e1, g0, g1 = _route(h2, wr_ref[...], br_ref[...], TL)
        row128 = lax.broadcasted_iota(jnp.int32, (128, TL), 0)
        gcol_ref[...] = jnp.where(row128 == 0, g0, jnp.where(row128 == 1, g1, 0.0)).T
        lpos0, lpos1, per_expert = _place_rows(e0, e1, triu_ref[...], trile_ref[...], fill_ref, base_ref,
                                               np_ref, pexp_ref)
        row8 = lax.broadcasted_iota(jnp.int32, (8, TL), 0)
        posv[...] = jnp.where(row8 == 0, lpos0, jnp.where(row8 == 1, lpos1, 0))
        pexv[...] = per_expert
        row2 = lax.broadcasted_iota(jnp.int32, (2, TL), 0)
        lpos_ref[0] = jnp.where(row2 == 0, lpos0, lpos1)
        pltpu.make_async_copy(posv, poss.at[i % 2], msem).start()
        pltpu.make_async_copy(pexv, pexs.at[i % 2], msem).start()

    def project(x, precise=False):
        h = _rms(x, g1_ref[...])
        if precise:
            return _dot_split(h, win_ref[...], winlo_ref[...])
        return jnp.dot(h.astype(jnp.bfloat16), win_ref[...], preferred_element_type=jnp.float32)

    def masked_wm(ref=wm_ref):
        return jnp.where(col_m <= row_m, ref[...], jnp.zeros_like(ref[...]))

    @pl.when(i == 0)
    def _init():
        zh_ref[...] = jnp.zeros_like(zh_ref)
        ph_ref[...] = jnp.zeros_like(ph_ref)
        fill_ref[...] = jnp.full_like(fill_ref, TM)
        base_ref[...] = jnp.zeros_like(base_ref)
        np_ref[...] = jnp.zeros_like(np_ref)
        pexp_ref[...] = jnp.zeros_like(pexp_ref)
        h2buf[...] = jnp.zeros((TL, 1, D_PACK), jnp.uint32)
        row8 = lax.broadcasted_iota(jnp.int32, (8, TL), 0)
        posv[...] = jnp.minimum(row8, 1) * TL + lax.broadcasted_iota(jnp.int32, (8, TL), 1)
        pexv[...] = jnp.zeros((8, 128), jnp.int32)
        pltpu.make_async_copy(posv, poss.at[1], msem).start()
        pltpu.make_async_copy(pexv, pexs.at[1], msem).start()
        if not first_layer:
            _first_results(lposp_ref, runsp_ref, yp_ref, gbuf, y0buf, y1buf, gsem)

    def _close_pages():
        copies = [pltpu.make_async_copy(fill_ref, fill_s, msem), pltpu.make_async_copy(base_ref, base_s, msem),
                  pltpu.make_async_copy(np_ref, np_s, msem), pltpu.make_async_copy(pexp_ref, pexp_s, msem)]
        for c in copies:
            c.start()
        for c in copies:
            c.wait()
        npages = np_s[0, 0]

        def zero_cnt(e, c):
            cnt_s[e] = 0
            return c
        lax.fori_loop(0, N_EXPERTS, zero_cnt, 0)

        def count(p, c):
            e = pexp_s[0, p]
            cnt_s[e] = cnt_s[e] + 1
            return c
        lax.fori_loop(0, npages, count, 0)

        def prefix(e, run):
            n = cnt_s[e]
            cnt_s[e] = run
            return run + n
        lax.fori_loop(0, N_EXPERTS, prefix, 0)

        def emit(p, c):
            e = pexp_s[0, p]
            q = cnt_s[e]
            cnt_s[e] = q + 1
            meta_ref[0, q] = p
            meta_ref[1, q] = e
            return c
        lax.fori_loop(0, npages, emit, 0)

        def pad_meta(q, c):
            meta_ref[0, q] = jnp.minimum(q, N_PAGES - 1)
            meta_ref[1, q] = meta_ref[1, npages - 1]
            return c
        lax.fori_loop(npages, PAGE_LANES, pad_meta, 0)

        def fill_row2(q, c):
            meta_ref[2, q] = npages
            return c
        lax.fori_loop(0, PAGE_LANES, fill_row2, 0)

        def mark_first(q, seen):
            is_first = (q < npages) & ((q == 0) | (meta_ref[1, q] != meta_ref[1, jnp.maximum(q - 1, 0)]))
            seen = seen + jnp.where(is_first, 1, 0)
            meta_ref[3, q] = jnp.where(is_first, 1, 0)
            meta_ref[4, q] = jnp.maximum(seen - 1, 0)
            meta_ref[5, q] = -1
            return seen
        lax.fori_loop(0, PAGE_LANES, mark_first, 0)

        def mark_next(t, carry):
            cur, nxt = carry
            q = npages - 1 - t
            e = meta_ref[1, q]
            nxt = jnp.where(e != cur, cur, nxt)
            meta_ref[5, q] = nxt
            return e, nxt
        lax.fori_loop(0, npages, mark_next, (jnp.int32(-1), jnp.int32(-1)))

        h2buf[...] = jnp.zeros((TL, 1, D_PACK), jnp.uint32)

        def tails(start):
            def tail(e, c):
                f = fill_s[e, 0]
                b = base_s[e, 0]

                @pl.when(f < TM)
                def _zero_tail():
                    cp = pltpu.make_async_copy(h2buf.at[pl.ds(0, TM - f)], hs_ref.at[pl.ds(b + f, TM - f)],
                                               rsem.at[0])
                    cp.start() if start else cp.wait()
                return c
            lax.fori_loop(0, N_EXPERTS, tail, 0)

            def unused(p, c):
                cp = pltpu.make_async_copy(h2buf.at[pl.ds(0, TM)], hs_ref.at[pl.ds(p * TM, TM)], rsem.at[0])
                cp.start() if start else cp.wait()
                return c
            lax.fori_loop(npages, N_PAGES, unused, 0)

        tails(True)
        tails(False)

    prev_slot = (i + 1) % 2

    @pl.when(i < N_PROMPT_TILES)
    def _prompt():
        wait_positions(prev_slot)
        if not first_layer:
            pltpu.make_async_copy(gbuf, gbuf, gsem).wait()
        s = i % TILES_PER_SEQ
        x = load_x(True)
        proj = project(x)
        zhist = jnp.where(s == 0, 0.0, zh_ref[...])
        phist = jnp.where(s == 0, 0.0, ph_ref[...])
        mix, znew, pnew = _mix_rows(proj, zhist, phist, s * TL, TL, convw_ref[...], poolbd_ref[...],
                                    pscale_ref[...], masked_wm(), bias_ref[...])
        zh_ref[...] = znew
        ph_ref[...] = pnew
        cpr_ref[0] = znew
        ppr_ref[0] = pnew
        if not first_layer:
            _unsort_results(lposp_ref, i + 1, gbuf, y0buf, y1buf, prev_slot)
        sort_rows(prev_slot)
        finish(x, mix)

    @pl.when(i <= N_PROMPT_TILES - 1)
    def _prompt_runs():
        start_runs(jnp.maximum(i - 1, 0), prev_slot)
        if not first_layer:
            @pl.when(i + 2 < N_TILES)
            def _():
                _start_run_gather(runsp_ref, i + 2, yp_ref, gbuf, gsem)

    @pl.when(i == N_PROMPT_TILES)
    def _sample():
        x = load_x(False)
        proj = project(x, precise=True)
        wm = masked_wm()
        wm_lo = masked_wm(wmlo_ref)
        mixes = []
        for b in range(DEC_BATCH):
            rows = slice(b * DEC_SEQ, (b + 1) * DEC_SEQ)
            mix, znew, pnew = _mix_rows(proj[rows, :], sconv_ref[b], spool_ref[b], PAST_LEN, DEC_SEQ,
                                        convw_ref[...], poolbd_ref[...], pscale_ref[...], wm, bias_ref[...],
                                        pool_lo=poollo_ref[...], wm_lo=wm_lo)
            csm_ref[b] = znew
            psm_ref[b] = pnew
            mixes.append(mix)
        sv_ref[...] = proj[:, 1792:2176]
        wait_positions(prev_slot)
        sort_rows(prev_slot)
        finish(x, jnp.concatenate(mixes, axis=0), precise=True)
        start_runs(i - 1, prev_slot)
        wait_positions(i % 2)
        sort_rows(i % 2)
        start_runs(i, i % 2)
        wait_rows(i % 2)
        _close_pages()


def _mixer_call(first_layer, xs, sconv_pad, spool_pad, g1, w_in, conv_w, pool_bd, pool_scale, wm_all,
                bias_full, w_out, g2, wr_t, br_col, triu, tril_e, lows):
    tile = lambda i, *_: (i, 0)
    prompt_tile = lambda i, *_: (jnp.minimum(i, N_PROMPT_TILES - 1), 0)
    const2 = lambda i, *_: (0, 0)
    const3 = lambda i, *_: (0, 0, 0)
    if first_layer:
        prefetch = ()
        x_specs = [pl.BlockSpec((TL, D_MODEL), prompt_tile), pl.BlockSpec((TL, D_MODEL), const2)]
    else:
        prefetch = xs[0:2]
        xs = xs[2:]
        x_specs = [pl.BlockSpec((TL, D_MODEL), tile), pl.BlockSpec(memory_space=pl.ANY),
                   pl.BlockSpec((TL, 128), tile)]
    full = lambda a: pl.BlockSpec(a.shape, const2 if a.ndim == 2 else const3, pipeline_mode=pl.Buffered(1))
    weights = [sconv_pad, spool_pad, g1, w_in, conv_w, pool_bd, pool_scale, wm_all, bias_full, w_out, g2,
               wr_t, br_col, triu, tril_e, *lows]
    in_specs = x_specs + [full(a) for a in weights]
    seq_of = lambda i, *_: (jnp.minimum(i // TILES_PER_SEQ, BATCH - 1), 0, 0)
    out_shape = [
        jax.ShapeDtypeStruct((T_ALL, D_MODEL), jnp.float32),
        jax.ShapeDtypeStruct((PAGE_ROWS, 1, D_PACK), jnp.uint32),
        jax.ShapeDtypeStruct((N_TILES, 2, TL), jnp.int32),
        jax.ShapeDtypeStruct((N_TILES * RUNS_PER_TILE,), jnp.int32),
        jax.ShapeDtypeStruct((6, PAGE_LANES), jnp.int32),
        jax.ShapeDtypeStruct((T_ALL, 128), jnp.float32),
        jax.ShapeDtypeStruct((BATCH, HIST_ROWS, D_CONV), jnp.float32),
        jax.ShapeDtypeStruct((BATCH, HIST_ROWS, D_POOL), jnp.float32),
        jax.ShapeDtypeStruct((DEC_BATCH, HIST_ROWS, D_CONV), jnp.float32),
        jax.ShapeDtypeStruct((DEC_BATCH, HIST_ROWS, D_POOL), jnp.float32),
        jax.ShapeDtypeStruct((T_SAMPLE, D_SGU), jnp.float32),
    ]
    out_specs = [
        pl.BlockSpec((TL, D_MODEL), tile),
        pl.BlockSpec(memory_space=pl.ANY),
        pl.BlockSpec((1, 2, TL), lambda i, *_: (i, 0, 0)),
        pl.BlockSpec(memory_space=pltpu.SMEM),
        pl.BlockSpec(memory_space=pltpu.SMEM),
        pl.BlockSpec((TL, 128), tile),
        pl.BlockSpec((1, HIST_ROWS, D_CONV), seq_of),
        pl.BlockSpec((1, HIST_ROWS, D_POOL), seq_of),
        pl.BlockSpec((DEC_BATCH, HIST_ROWS, D_CONV), const3),
        pl.BlockSpec((DEC_BATCH, HIST_ROWS, D_POOL), const3),
        pl.BlockSpec((T_SAMPLE, D_SGU), const2),
    ]
    scratch = [
        pltpu.VMEM((HIST_ROWS, D_CONV), jnp.float32),
        pltpu.VMEM((HIST_ROWS, D_POOL), jnp.float32),
        pltpu.VMEM((TL, 1, D_PACK), jnp.uint32),
        pltpu.VMEM((2, 2 * TL, 1, D_PACK), jnp.uint32),
        pltpu.VMEM((8, TL), jnp.int32),
        pltpu.VMEM((8, 128), jnp.int32),
        pltpu.SMEM((2, 8, TL), jnp.int32),
        pltpu.SMEM((2, 8, 128), jnp.int32),
        pltpu.VMEM((N_EXPERTS, 128), jnp.int32),
        pltpu.VMEM((N_EXPERTS, 128), jnp.int32),
        pltpu.VMEM((8, 128), jnp.int32),
        pltpu.VMEM((8, PAGE_LANES), jnp.int32),
        pltpu.SMEM((N_EXPERTS, 128), jnp.int32),
        pltpu.SMEM((N_EXPERTS, 128), jnp.int32),
        pltpu.SMEM((8, 128), jnp.int32),
        pltpu.SMEM((8, PAGE_LANES), jnp.int32),
        pltpu.SMEM((N_EXPERTS,), jnp.int32),
        pltpu.SemaphoreType.DMA((2,)),
        pltpu.SemaphoreType.DMA(()),
    ]
    if not first_layer:
        scratch += [
            pltpu.VMEM((2 * TL, 1, D_PACK), jnp.uint32),
            pltpu.VMEM((2, TL, 1, D_PACK), jnp.uint32),
            pltpu.VMEM((2, TL, 1, D_PACK), jnp.uint32),
            pltpu.SemaphoreType.DMA(()),
        ]
    grid_spec = pltpu.PrefetchScalarGridSpec(num_scalar_prefetch=len(prefetch), grid=(N_TILES,),
                                             in_specs=in_specs, out_specs=out_specs, scratch_shapes=scratch)
    return pl.pallas_call(
        functools.partial(_mixer_kernel, first_layer),
        grid_spec=grid_spec,
        out_shape=out_shape,
        compiler_params=pltpu.CompilerParams(dimension_semantics=("arbitrary",),
                                             vmem_limit_bytes=VMEM_LIMIT),
        name="mixer_first" if first_layer else "mixer_next",
    )(*prefetch, *xs, *weights)


def _expert_kernel(layer, meta_ref, hs_ref, wg_hbm, wu_hbm, wd_hbm, y_ref, xbuf, wg32, wu32, wd32, wg16, wu16,
                   wd16, wsem):
    s = pl.program_id(0)
    npages = meta_ref[2, 0]

    def weight_copies(expert, slot):
        return [pltpu.make_async_copy(wg_hbm.at[layer, expert], wg32.at[slot], wsem.at[slot]),
                pltpu.make_async_copy(wu_hbm.at[layer, expert], wu32.at[slot], wsem.at[slot]),
                pltpu.make_async_copy(wd_hbm.at[layer, expert], wd32.at[slot], wsem.at[slot])]

    @pl.when(s == 0)
    def _first_weights():
        for c in weight_copies(meta_ref[1, 0], 0):
            c.start()

    @pl.when((s < npages) & (meta_ref[3, s] == 1))
    def _switch_expert():
        slot = meta_ref[4, s] % 2
        for c in weight_copies(meta_ref[1, s], slot):
            c.wait()
        nxt = meta_ref[5, s]

        @pl.when(nxt >= 0)
        def _():
            for c in weight_copies(nxt, 1 - slot):
                c.start()
        wg16[...] = wg32[slot].astype(jnp.bfloat16)
        wu16[...] = wu32[slot].astype(jnp.bfloat16)
        wd16[...] = wd32[slot].astype(jnp.bfloat16)

    @pl.when(s < npages)
    def _page():
        xbuf[...] = _unpack_rows(hs_ref[...])
        x = xbuf[...].astype(jnp.bfloat16)
        hg = jnp.dot(x, wg16[...], preferred_element_type=jnp.float32)
        hu = jnp.dot(x, wu16[...], preferred_element_type=jnp.float32)
        h = (hg * jax.nn.sigmoid(hg) * hu).astype(jnp.bfloat16)
        out = jnp.dot(h, wd16[...], preferred_element_type=jnp.float32)
        y_ref[...] = _pack_rows(out)

    @pl.when(s >= npages)
    def _unused_page():
        y_ref[...] = jnp.zeros((TM, 1, D_PACK), jnp.uint32)


def _expert_call(layer, meta, hs, w_gate, w_up, w_down):
    page = lambda s, meta: (meta[0, s], 0, 0)
    grid_spec = pltpu.PrefetchScalarGridSpec(
        num_scalar_prefetch=1,
        grid=(N_PAGES,),
        in_specs=[
            pl.BlockSpec((TM, 1, D_PACK), page),
            pl.BlockSpec(memory_space=pl.ANY),
            pl.BlockSpec(memory_space=pl.ANY),
            pl.BlockSpec(memory_space=pl.ANY),
        ],
        out_specs=pl.BlockSpec((TM, 1, D_PACK), page),
        scratch_shapes=[
            pltpu.VMEM((TM, D_MODEL), jnp.float32),
            pltpu.VMEM((2, D_MODEL, D_EXPERT), jnp.float32),
            pltpu.VMEM((2, D_MODEL, D_EXPERT), jnp.float32),
            pltpu.VMEM((2, D_EXPERT, D_MODEL), jnp.float32),
            pltpu.VMEM((D_MODEL, D_EXPERT), jnp.bfloat16),
            pltpu.VMEM((D_MODEL, D_EXPERT), jnp.bfloat16),
            pltpu.VMEM((D_EXPERT, D_MODEL), jnp.bfloat16),
            pltpu.SemaphoreType.DMA((2,)),
        ],
    )
    return pl.pallas_call(
        functools.partial(_expert_kernel, layer),
        grid_spec=grid_spec,
        out_shape=jax.ShapeDtypeStruct((PAGE_ROWS, 1, D_PACK), jnp.uint32),
        compiler_params=pltpu.CompilerParams(dimension_semantics=("arbitrary",),
                                             vmem_limit_bytes=VMEM_LIMIT),
        name="experts",
    )(meta, hs, w_gate, w_up, w_down)


def _final_kernel(lposp_ref, runsp_ref, xm_ref, yp_ref, gc_ref, g_ref, yp_out, ys_out, gbuf, y0buf, y1buf, gsem):
    i = pl.program_id(0)

    @pl.when(i == 0)
    def _():
        _first_results(lposp_ref, runsp_ref, yp_ref, gbuf, y0buf, y1buf, gsem)

    def normed():
        g = gc_ref[...]
        x = xm_ref[...] + g[:, 0:1] * _unpack_rows(y0buf[i % 2]) + g[:, 1:2] * _unpack_rows(y1buf[i % 2])
        return _rms(x, g_ref[...])

    @pl.when(i < N_PROMPT_TILES)
    def _():
        pltpu.make_async_copy(gbuf, gbuf, gsem).wait()
        yp_out[...] = normed()
        _unsort_results(lposp_ref, i + 1, gbuf, y0buf, y1buf, (i + 1) % 2)

    @pl.when(i + 2 < N_TILES)
    def _():
        _start_run_gather(runsp_ref, i + 2, yp_ref, gbuf, gsem)

    @pl.when(i == N_PROMPT_TILES)
    def _():
        ys_out[...] = normed()


def _final_call(lpos, runs, x_mid, y_pages, gcol, g):
    tile = lambda i, *_: (i, 0)
    grid_spec = pltpu.PrefetchScalarGridSpec(
        num_scalar_prefetch=2,
        grid=(N_TILES,),
        in_specs=[pl.BlockSpec((TL, D_MODEL), tile), pl.BlockSpec(memory_space=pl.ANY),
                  pl.BlockSpec((TL, 128), tile), pl.BlockSpec((1, D_MODEL), lambda i, *_: (0, 0))],
        out_specs=[pl.BlockSpec((TL, D_MODEL), lambda i, *_: (jnp.minimum(i, N_PROMPT_TILES - 1), 0)),
                   pl.BlockSpec((TL, D_MODEL), lambda i, *_: (0, 0))],
        scratch_shapes=[pltpu.VMEM((2 * TL, 1, D_PACK), jnp.uint32), pltpu.VMEM((2, TL, 1, D_PACK), jnp.uint32),
                        pltpu.VMEM((2, TL, 1, D_PACK), jnp.uint32), pltpu.SemaphoreType.DMA(())],
    )
    return pl.pallas_call(
        _final_kernel,
        grid_spec=grid_spec,
        out_shape=[jax.ShapeDtypeStruct((T_PROMPT, D_MODEL), jnp.float32),
                   jax.ShapeDtypeStruct((T_SAMPLE, D_MODEL), jnp.float32)],
        compiler_params=pltpu.CompilerParams(dimension_semantics=("arbitrary",),
                                             vmem_limit_bytes=VMEM_LIMIT),
        name="final_norm",
    )(lpos, runs, x_mid, y_pages, gcol, g)


def kernel(x_prompt, x_sample, state_conv, state_pool, norm1_g, w_in, conv_w, pool_w, pool_scale, sgu_w, sgu_b, w_out, norm2_g, router_coarse_w, router_coarse_b, router_fine_w, router_fine_b, moe_w_gate, moe_w_up, moe_w_down, final_norm_g):
    bf16 = jnp.bfloat16
    xs = (x_prompt.reshape(T_PROMPT, D_MODEL), x_sample.reshape(T_SAMPLE, D_MODEL))
    sconv_pad = jnp.pad(state_conv, ((0, 0), (0, 0), (HIST_ROWS - (CONV_WIDTH - 1), 0), (0, 0)))
    spool_pad = jnp.pad(state_pool, ((0, 0), (0, 0), (HIST_ROWS - POOL_HIST, 0), (0, 0)))
    idx = jnp.arange(TL, dtype=jnp.int32)
    triu = (idx[:, None] < idx[None, :]).astype(bf16)
    ide = jnp.arange(N_EXPERTS, dtype=jnp.int32)
    tril_e = (ide[None, :] < ide[:, None]).astype(bf16)
    conv_pr, pool_pr, conv_sm, pool_sm, sgu_v = [], [], [], [], []
    x_mid = y_pages = gcol = lpos = runs = None
    for l in range(DEPTH):
        pool_bd32 = jax.scipy.linalg.block_diag(*[pool_w[l, g] for g in range(4)])
        pool_bd = pool_bd32.astype(bf16)
        wm32 = sgu_w[l].reshape(SGU_HEADS * SGU_LEN, SGU_LEN)
        wm_all = wm32.astype(bf16)
        bias_full = jnp.repeat(sgu_b[l].T, SGU_HEAD_DIM, axis=1)
        wr = jnp.zeros((ROUTER_ROWS, D_MODEL), jnp.float32)
        wr = wr.at[0:N_GROUPS].set(router_coarse_w[l].T).at[8:].set(router_fine_w[l].T)
        wr_hi = wr.astype(bf16)
        wr_lo = (wr - wr_hi.astype(jnp.float32)).astype(bf16)
        wr_t = jnp.concatenate([wr_hi, wr_lo], axis=0)
        br_col = jnp.zeros((ROUTER_ROWS, 1), jnp.float32)
        br_col = br_col.at[0:N_GROUPS, 0].set(router_coarse_b[l]).at[8:, 0].set(router_fine_b[l])
        low = lambda w: (w - w.astype(bf16).astype(jnp.float32)).astype(bf16)
        lows = (low(w_in[l]), low(w_out[l]), low(pool_bd32), low(wm32))
        outs = _mixer_call(l == 0, xs, sconv_pad[l], spool_pad[l], norm1_g[l].reshape(1, D_MODEL),
                           w_in[l].astype(bf16), conv_w[l], pool_bd, pool_scale[l].reshape(1, D_POOL),
                           wm_all, bias_full, w_out[l].astype(bf16), norm2_g[l].reshape(1, D_MODEL),
                           wr_t, br_col, triu, tril_e, lows)
        x_mid, hs, lpos, runs, meta, gcol, cpr, ppr, csm, psm, sv = outs
        lpos = lpos.reshape(N_TILES * 2 * TL)
        conv_pr.append(cpr[:, HIST_ROWS - (CONV_WIDTH - 1):, :])
        pool_pr.append(ppr[:, HIST_ROWS - POOL_HIST:, :])
        conv_sm.append(csm[:, HIST_ROWS - (CONV_WIDTH - 1):, :])
        pool_sm.append(psm[:, HIST_ROWS - POOL_HIST:, :])
        sgu_v.append(sv.reshape(DEC_BATCH, DEC_SEQ, D_SGU))
        y_pages = _expert_call(l, meta, hs, moe_w_gate, moe_w_up, moe_w_down)
        xs = (lpos, runs, x_mid, y_pages, gcol)
    y_prompt, y_sample = _final_call(lpos, runs, x_mid, y_pages, gcol, final_norm_g.reshape(1, D_MODEL))
    return (y_prompt.reshape(BATCH, SEQ, D_MODEL), y_sample.reshape(DEC_BATCH, DEC_SEQ, D_MODEL),
            jnp.stack(conv_pr), jnp.stack(pool_pr), jnp.stack(conv_sm), jnp.stack(pool_sm),
            jnp.stack(sgu_v))
```

```python
import functools

import jax
import jax.numpy as jnp
from jax import lax
from jax.experimental import pallas as pl
from jax.experimental.pallas import tpu as pltpu

D_MODEL = 1024
BATCH = 8
SEQ = 2048
DEPTH = 2
DEC_BATCH = 8
DEC_SEQ = 64
PAST_LEN = 1024
D_CONV = 384
CONV_WIDTH = 3
D_POOL = 256
POOL_HIST = 15
D_SGU = 384
SGU_HEADS = 4
SGU_HEAD_DIM = 96
SGU_LEN = 128
D_PROJ = 2176
N_GROUPS = 4
EXPERTS_PER_GROUP = 8
N_EXPERTS = 32
D_EXPERT = 512
EPS = 1e-6

T_PROMPT = BATCH * SEQ
T_SAMPLE = DEC_BATCH * DEC_SEQ
T_ALL = T_PROMPT + T_SAMPLE
TL = 512
TILES_PER_SEQ = SEQ // TL
N_PROMPT_TILES = T_PROMPT // TL
N_TILES = N_PROMPT_TILES + T_SAMPLE // TL
HIST_ROWS = 16
ROUTER_ROWS = 8 + N_EXPERTS
TM = 256
N_ASSIGN = 2 * T_ALL
N_PAGES = N_ASSIGN // TM + N_EXPERTS
PAGE_LANES = 256
PAGE_ROWS = N_PAGES * TM
RUN_FIELDS = 9
RUNS_PER_TILE = N_EXPERTS * RUN_FIELDS
D_PACK = D_MODEL // 2
VMEM_LIMIT = 56 * 1024 * 1024

assert TL == 2 * TM and N_PAGES <= PAGE_LANES and T_ALL == N_TILES * TL


def _pack_rows(x):
    words = pltpu.pack_elementwise([x[:, :D_PACK], x[:, D_PACK:]], packed_dtype=jnp.bfloat16)
    return words.reshape(x.shape[0], 1, D_PACK)


def _unpack_rows(words):
    w = words.reshape(words.shape[0], D_PACK)
    halves = [pltpu.unpack_elementwise(w, index=k, packed_dtype=jnp.bfloat16, unpacked_dtype=jnp.float32)
              for k in range(2)]
    return jnp.concatenate(halves, axis=1)


def _split(a):
    hi = a.astype(jnp.bfloat16)
    return hi, (a - hi.astype(jnp.float32)).astype(jnp.bfloat16)


def _dot_split(a, b_hi, b_lo):
    a_hi, a_lo = _split(a)
    dot = functools.partial(jnp.dot, preferred_element_type=jnp.float32)
    return dot(a_hi, b_hi) + dot(a_lo, b_hi) + dot(a_hi, b_lo)


def _rms(x, g):
    return x * lax.rsqrt(jnp.mean(x * x, axis=-1, keepdims=True) + EPS) * g


def _mix_rows(proj, zhist, phist, pos0, n, conv_w, pool_bd, pool_scale, wm_all, bias_full, pool_lo=None,
              wm_lo=None):
    a_b = proj[:, 0:384]
    a_c = proj[:, 384:768]
    a_h = proj[:, 768:1152]
    p_in = proj[:, 1152:1408]
    s_u = proj[:, 1408:1792]
    s_v = proj[:, 1792:2176]

    z = a_c * a_h
    zext = jnp.concatenate([zhist, z], axis=0)
    conv_y = (conv_w[0:1, :] * pltpu.roll(zext, 2, 0)[HIST_ROWS:, :]
              + conv_w[1:2, :] * pltpu.roll(zext, 1, 0)[HIST_ROWS:, :]
              + conv_w[2:3, :] * z)
    a_out = a_b * conv_y

    pext = jnp.concatenate([phist, p_in], axis=0)
    s2 = pext + pltpu.roll(pext, 1, 0)
    s4 = s2 + pltpu.roll(s2, 2, 0)
    s8 = s4 + pltpu.roll(s4, 4, 0)
    s16 = s8 + pltpu.roll(s8, 8, 0)
    lane = lax.broadcasted_iota(jnp.int32, (1, D_POOL), 1)
    wsum = jnp.where(lane < 64, s2, jnp.where(lane < 128, s4, jnp.where(lane < 192, s8, s16)))
    wsum = wsum[HIST_ROWS:, :]
    win = jnp.where(lane < 64, 2.0, jnp.where(lane < 128, 4.0, jnp.where(lane < 192, 8.0, 16.0)))
    pos = (pos0 + lax.broadcasted_iota(jnp.int32, (n, 1), 0) + 1).astype(jnp.float32)
    cnt = jnp.minimum(win, pos)
    pooled = wsum / cnt - p_in
    if pool_lo is None:
        p_out = jnp.dot(pooled.astype(jnp.bfloat16), pool_bd, preferred_element_type=jnp.float32)
    else:
        p_out = _dot_split(pooled, pool_bd, pool_lo)
    p_out = p_out * pool_scale

    lane_s = lax.broadcasted_iota(jnp.int32, (1, D_SGU), 1)
    chunk = min(n, SGU_LEN)
    def head_rows(w):
        if chunk == SGU_LEN:
            return w
        return jnp.concatenate([w[h * SGU_LEN:h * SGU_LEN + chunk, 0:chunk] for h in range(SGU_HEADS)], axis=0)
    wm = head_rows(wm_all)
    s_rows = []
    for c in range(n // chunk):
        v_c = s_v[c * chunk:(c + 1) * chunk, :]
        if wm_lo is None:
            r = jnp.dot(wm, v_c.astype(jnp.bfloat16), preferred_element_type=jnp.float32)
        else:
            v_hi, v_lo = _split(v_c)
            r = (jnp.dot(wm, v_hi, preferred_element_type=jnp.float32)
                 + jnp.dot(head_rows(wm_lo), v_hi, preferred_element_type=jnp.float32)
                 + jnp.dot(wm, v_lo, preferred_element_type=jnp.float32))
        s_c = jnp.where(lane_s < 96, r[0:chunk],
                        jnp.where(lane_s < 192, r[chunk:2 * chunk],
                                  jnp.where(lane_s < 288, r[2 * chunk:3 * chunk], r[3 * chunk:4 * chunk])))
        s_rows.append(s_c + bias_full[0:chunk, :])
    s_gate = s_rows[0] if len(s_rows) == 1 else jnp.concatenate(s_rows, axis=0)
    s_out = s_u * s_gate

    mix = jnp.concatenate([a_out, p_out, s_out], axis=-1)
    return mix, zext[n:n + HIST_ROWS, :], pext[n:n + HIST_ROWS, :]


def _route(h2, wr_t, br_col, n):
    h_hi = h2.astype(jnp.bfloat16)
    h_lo = (h2 - h_hi.astype(jnp.float32)).astype(jnp.bfloat16)
    nt = (((1,), (1,)), ((), ()))
    by_hi = lax.dot_general(wr_t, h_hi, nt, preferred_element_type=jnp.float32)
    by_lo = lax.dot_general(wr_t[0:ROUTER_ROWS, :], h_lo, nt, preferred_element_type=jnp.float32)
    logits = by_hi[0:ROUTER_ROWS, :] + by_hi[ROUTER_ROWS:, :] + by_lo + br_col
    row8 = lax.broadcasted_iota(jnp.int32, (8, n), 0)
    lc = jnp.where(row8 < N_GROUPS, logits[0:8, :], -jnp.inf)
    mc = jnp.max(lc, axis=0, keepdims=True)
    g_sel = jnp.min(jnp.where(lc == mc, row8, 8), axis=0, keepdims=True)
    p_sel = 1.0 / jnp.sum(jnp.exp(lc - mc), axis=0, keepdims=True)
    lf = logits[8 + 3 * EXPERTS_PER_GROUP:8 + 4 * EXPERTS_PER_GROUP, :]
    for g in (2, 1, 0):
        lf = jnp.where(g_sel == g, logits[8 + g * EXPERTS_PER_GROUP:8 + (g + 1) * EXPERTS_PER_GROUP, :], lf)
    m1 = jnp.max(lf, axis=0, keepdims=True)
    i1 = jnp.min(jnp.where(lf == m1, row8, 8), axis=0, keepdims=True)
    lf2 = jnp.where(row8 == i1, -jnp.inf, lf)
    m2 = jnp.max(lf2, axis=0, keepdims=True)
    i2 = jnp.min(jnp.where(lf2 == m2, row8, 8), axis=0, keepdims=True)
    t = jnp.exp(m2 - m1)
    wa = 1.0 / (1.0 + t)
    wb = t / (1.0 + t)
    e0 = g_sel * EXPERTS_PER_GROUP + i1
    e1 = g_sel * EXPERTS_PER_GROUP + i2
    return e0, e1, p_sel * wa, p_sel * wb


def _place_rows(e0, e1, triu, tril_e, fill_ref, base_ref, np_ref, pexp_ref):
    row_e = lax.broadcasted_iota(jnp.int32, (N_EXPERTS, TL), 0)
    oh0 = row_e == e0
    oh1 = row_e == e1
    oh = jnp.where(oh0 | oh1, 1.0, 0.0)
    rank = jnp.dot(oh.astype(jnp.bfloat16), triu, preferred_element_type=jnp.float32).astype(jnp.int32)
    cnt = jnp.sum(oh, axis=1, keepdims=True).astype(jnp.int32)
    lower = jnp.where(e0 < row_e, 1.0, 0.0) + jnp.where(e1 < row_e, 1.0, 0.0)
    first = jnp.sum(lower, axis=1, keepdims=True).astype(jnp.int32)
    sorted_all = first + rank
    lpos0 = jnp.sum(jnp.where(oh0, sorted_all, 0), axis=0, keepdims=True)
    lpos1 = jnp.sum(jnp.where(oh1, sorted_all, 0), axis=0, keepdims=True)

    fill = fill_ref[:, 0:1]
    base = base_ref[:, 0:1]
    npages = np_ref[0:1, 0:1]
    total = fill + cnt
    need = (total > TM).astype(jnp.int32) + (total > 2 * TM).astype(jnp.int32)
    need_b = jnp.broadcast_to(need.astype(jnp.float32), (N_EXPERTS, 128)).astype(jnp.bfloat16)
    before = jnp.dot(tril_e, need_b, preferred_element_type=jnp.float32)[:, 0:1].astype(jnp.int32)
    new_id = npages + before
    new_base = new_id * TM
    fill_ref[...] = jnp.broadcast_to(total - need * TM, (N_EXPERTS, 128))
    base_ref[...] = jnp.broadcast_to(jnp.where(need > 0, new_base + (need - 1) * TM, base), (N_EXPERTS, 128))
    np_ref[...] = jnp.broadcast_to(npages + jnp.sum(need, axis=0, keepdims=True), (8, 128))
    page_lane = lax.broadcasted_iota(jnp.int32, (N_EXPERTS, PAGE_LANES), 1)
    expert_col = lax.broadcasted_iota(jnp.int32, (N_EXPERTS, 1), 0)
    owns = ((page_lane == new_id) & (need >= 1)) | ((page_lane == new_id + 1) & (need == 2))
    pexp_ref[...] = pexp_ref[...] + jnp.sum(jnp.where(owns, expert_col, 0), axis=0, keepdims=True)

    lane = lax.broadcasted_iota(jnp.int32, (N_EXPERTS, 128), 1)
    cols = jnp.where(lane == 0, cnt, jnp.where(lane == 1, first, jnp.where(lane == 2, fill,
                     jnp.where(lane == 3, base, jnp.where(lane == 4, new_base, 0)))))
    square = jnp.concatenate([cols, jnp.zeros((128 - N_EXPERTS, 128), jnp.int32)], axis=0)
    per_expert = square.astype(jnp.float32).T.astype(jnp.int32)[0:8, :]
    return lpos0, lpos1, per_expert


def _run_pieces(cnt, first, fill, base, new_base):
    n0 = jnp.minimum(cnt, TM - fill)
    n1 = jnp.minimum(cnt - n0, TM)
    n2 = cnt - n0 - n1
    return ((first, base + fill, n0), (first + n0, new_base, n1), (first + n0 + n1, new_base + TM, n2))


def _start_run_gather(runs_ref, tile, pages_ref, gbuf, gsem):
    slot = tile % 2

    def per_expert(e, c):
        k0 = (tile * N_EXPERTS + e) * RUN_FIELDS
        for j in range(3):
            src = runs_ref[k0 + 3 * j]
            dst = runs_ref[k0 + 3 * j + 1]
            n = runs_ref[k0 + 3 * j + 2]

            @pl.when(n > 0)
            def _():
                pltpu.make_async_copy(pages_ref.at[pl.ds(dst, n)], gbuf.at[slot, pl.ds(src, n)],
                                      gsem.at[slot]).start()
        return c
    lax.fori_loop(0, N_EXPERTS, per_expert, 0)


def _wait_run_gather(tile, gbuf, gsem):
    pltpu.make_async_copy(gbuf.at[0], gbuf.at[0], gsem.at[tile % 2]).wait()


def _unsort_results(lpos_ref, tile, gbuf, y0buf, y1buf):
    k0 = tile * 2 * TL
    slot = tile % 2
    for r in range(TL):
        y0buf[slot, r] = gbuf[slot, lpos_ref[k0 + r]]
        y1buf[slot, r] = gbuf[slot, lpos_ref[k0 + TL + r]]


def _first_results(lpos_ref, runs_ref, pages_ref, gbuf, y0buf, y1buf, gsem):
    _start_run_gather(runs_ref, 0, pages_ref, gbuf, gsem)
    _wait_run_gather(0, gbuf, gsem)
    _unsort_results(lpos_ref, 0, gbuf, y0buf, y1buf)
    _start_run_gather(runs_ref, 1, pages_ref, gbuf, gsem)
    _start_run_gather(runs_ref, 2, pages_ref, gbuf, gsem)


def _mixer_kernel(first_layer, *refs):
    if first_layer:
        xp_ref, xs_ref = refs[0:2]
        rest = refs[2:]
    else:
        lposp_ref, runsp_ref, xm_ref, yp_ref, gc_ref = refs[0:5]
        rest = refs[5:]
    (sconv_ref, spool_ref, g1_ref, win_ref, convw_ref, poolbd_ref, pscale_ref, wm_ref, bias_ref,
     wout_ref, g2_ref, wr_ref, br_ref, triu_ref, trile_ref, winlo_ref, woutlo_ref, poollo_ref, wmlo_ref,
     xmid_ref, hs_ref, lpos_ref, runs_ref, meta_ref, gcol_ref, cpr_ref, ppr_ref, csm_ref, psm_ref, sv_ref,
     zh_ref, ph_ref, h2buf, sbuf, posv, pexv, poss, pexs, fill_ref, base_ref, np_ref,
     pexp_ref, fill_s, base_s, np_s, pexp_s, cnt_s, rsem, msem) = rest[:49]
    if not first_layer:
        gbuf, y0buf, y1buf, gsem = rest[49:]

    i = pl.program_id(0)

    def load_x(prompt):
        if first_layer:
            return xp_ref[...] if prompt else xs_ref[...]
        g = gc_ref[...]
        return (xm_ref[...] + g[:, 0:1] * _unpack_rows(y0buf[i % 2])
                + g[:, 1:2] * _unpack_rows(y1buf[i % 2]))

    row_m = lax.broadcasted_iota(jnp.int32, (SGU_HEADS * SGU_LEN, SGU_LEN), 0) % SGU_LEN
    col_m = lax.broadcasted_iota(jnp.int32, (SGU_HEADS * SGU_LEN, SGU_LEN), 1)

    def wait_rows(slot):
        pltpu.make_async_copy(sbuf.at[0], sbuf.at[0], rsem.at[slot]).wait()

    def wait_positions(slot):
        pltpu.make_async_copy(posv, poss.at[slot], msem).wait()
        pltpu.make_async_copy(pexv, pexs.at[slot], msem).wait()

    def sort_rows(slot):
        for r in range(TL):
            row = h2buf[r]
            sbuf[slot, poss[slot, 0, r]] = row
            sbuf[slot, poss[slot, 1, r]] = row

    def start_runs(tile, slot):
        def per_expert(e, c):
            pieces = _run_pieces(pexs[slot, 0, e], pexs[slot, 1, e], pexs[slot, 2, e], pexs[slot, 3, e],
                                 pexs[slot, 4, e])
            k0 = (tile * N_EXPERTS + e) * RUN_FIELDS
            for j, (src, dst, n) in enumerate(pieces):
                runs_ref[k0 + 3 * j] = src
                runs_ref[k0 + 3 * j + 1] = dst
                runs_ref[k0 + 3 * j + 2] = n

                @pl.when(n > 0)
                def _():
                    pltpu.make_async_copy(sbuf.at[slot, pl.ds(src, n)], hs_ref.at[pl.ds(dst, n)],
                                          rsem.at[slot]).start()
            return c
        lax.fori_loop(0, N_EXPERTS, per_expert, 0)

        @pl.when(tile >= 1)
        def _():
            wait_rows(1 - slot)

    def finish(x, mix, precise=False):
        if precise:
            x_mid = x + _dot_split(mix, wout_ref[...], woutlo_ref[...])
        else:
            x_mid = x + jnp.dot(mix.astype(jnp.bfloat16), wout_ref[...], preferred_element_type=jnp.float32)
        xmid_ref[...] = x_mid
        h2 = _rms(x_mid, g2_ref[...])
        h2buf[...] = _pack_rows(h2)
        e0, e1, g0, g1 = _route(h2, wr_ref[...], br_ref[...], TL)
        row128 = lax.broadcasted_iota(jnp.int32, (128, TL), 0)
        gcol_ref[...] = jnp.where(row128 == 0, g0, jnp.where(row128 == 1, g1, 0.0)).T
        lpos0, lpos1, per_expert = _place_rows(e0, e1, triu_ref[...], trile_ref[...], fill_ref, base_ref,
                                               np_ref, pexp_ref)
        row8 = lax.broadcasted_iota(jnp.int32, (8, TL), 0)
        posv[...] = jnp.where(row8 == 0, lpos0, jnp.where(row8 == 1, lpos1, 0))
        pexv[...] = per_expert
        row2 = lax.broadcasted_iota(jnp.int32, (2, TL), 0)
        lpos_ref[0] = jnp.where(row2 == 0, lpos0, lpos1)
        pltpu.make_async_copy(posv, poss.at[i % 2], msem).start()
        pltpu.make_async_copy(pexv, pexs.at[i % 2], msem).start()

    def project(x, precise=False):
        h = _rms(x, g1_ref[...])
        if precise:
            return _dot_split(h, win_ref[...], winlo_ref[...])
        return jnp.dot(h.astype(jnp.bfloat16), win_ref[...], preferred_element_type=jnp.float32)

    def masked_wm(ref=wm_ref):
        return jnp.where(col_m <= row_m, ref[...], jnp.zeros_like(ref[...]))

    @pl.when(i == 0)
    def _init():
        zh_ref[...] = jnp.zeros_like(zh_ref)
        ph_ref[...] = jnp.zeros_like(ph_ref)
        fill_ref[...] = jnp.full_like(fill_ref, TM)
        base_ref[...] = jnp.zeros_like(base_ref)
        np_ref[...] = jnp.zeros_like(np_ref)
        pexp_ref[...] = jnp.zeros_like(pexp_ref)
        h2buf[...] = jnp.zeros((TL, 1, D_PACK), jnp.uint32)
        row8 = lax.broadcasted_iota(jnp.int32, (8, TL), 0)
        posv[...] = jnp.minimum(row8, 1) * TL + lax.broadcasted_iota(jnp.int32, (8, TL), 1)
        pexv[...] = jnp.zeros((8, 128), jnp.int32)
        pltpu.make_async_copy(posv, poss.at[1], msem).start()
        pltpu.make_async_copy(pexv, pexs.at[1], msem).start()
        if not first_layer:
            _first_results(lposp_ref, runsp_ref, yp_ref, gbuf, y0buf, y1buf, gsem)

    def _close_pages():
        copies = [pltpu.make_async_copy(fill_ref, fill_s, msem), pltpu.make_async_copy(base_ref, base_s, msem),
                  pltpu.make_async_copy(np_ref, np_s, msem), pltpu.make_async_copy(pexp_ref, pexp_s, msem)]
        for c in copies:
            c.start()
        for c in copies:
            c.wait()
        npages = np_s[0, 0]

        def zero_cnt(e, c):
            cnt_s[e] = 0
            return c
        lax.fori_loop(0, N_EXPERTS, zero_cnt, 0)

        def count(p, c):
            e = pexp_s[0, p]
            cnt_s[e] = cnt_s[e] + 1
            return c
        lax.fori_loop(0, npages, count, 0)

        def prefix(e, run):
            n = cnt_s[e]
            cnt_s[e] = run
            return run + n
        lax.fori_loop(0, N_EXPERTS, prefix, 0)

        def emit(p, c):
            e = pexp_s[0, p]
            q = cnt_s[e]
            cnt_s[e] = q + 1
            meta_ref[0, q] = p
            meta_ref[1, q] = e
            return c
        lax.fori_loop(0, npages, emit, 0)

        def pad_meta(q, c):
            meta_ref[0, q] = jnp.minimum(q, N_PAGES - 1)
            meta_ref[1, q] = meta_ref[1, npages - 1]
            return c
        lax.fori_loop(npages, PAGE_LANES, pad_meta, 0)

        def fill_row2(q, c):
            meta_ref[2, q] = npages
            return c
        lax.fori_loop(0, PAGE_LANES, fill_row2, 0)

        def mark_first(q, seen):
            is_first = (q < npages) & ((q == 0) | (meta_ref[1, q] != meta_ref[1, jnp.maximum(q - 1, 0)]))
            seen = seen + jnp.where(is_first, 1, 0)
            meta_ref[3, q] = jnp.where(is_first, 1, 0)
            meta_ref[4, q] = jnp.maximum(seen - 1, 0)
            meta_ref[5, q] = -1
            return seen
        lax.fori_loop(0, PAGE_LANES, mark_first, 0)

        def mark_next(t, carry):
            cur, nxt = carry
            q = npages - 1 - t
            e = meta_ref[1, q]
            nxt = jnp.where(e != cur, cur, nxt)
            meta_ref[5, q] = nxt
            return e, nxt
        lax.fori_loop(0, npages, mark_next, (jnp.int32(-1), jnp.int32(-1)))

        h2buf[...] = jnp.zeros((TL, 1, D_PACK), jnp.uint32)

        def tails(start):
            def tail(e, c):
                f = fill_s[e, 0]
                b = base_s[e, 0]

                @pl.when(f < TM)
                def _zero_tail():
                    cp = pltpu.make_async_copy(h2buf.at[pl.ds(0, TM - f)], hs_ref.at[pl.ds(b + f, TM - f)],
                                               rsem.at[0])
                    cp.start() if start else cp.wait()
                return c
            lax.fori_loop(0, N_EXPERTS, tail, 0)

            def unused(p, c):
                cp = pltpu.make_async_copy(h2buf.at[pl.ds(0, TM)], hs_ref.at[pl.ds(p * TM, TM)], rsem.at[0])
                cp.start() if start else cp.wait()
                return c
            lax.fori_loop(npages, N_PAGES, unused, 0)

        tails(True)
        tails(False)

    prev_slot = (i + 1) % 2

    @pl.when(i < N_PROMPT_TILES)
    def _prompt():
        if not first_layer:
            _wait_run_gather(i + 1, gbuf, gsem)
        s = i % TILES_PER_SEQ
        x = load_x(True)
        proj = project(x)
        zhist = jnp.where(s == 0, 0.0, zh_ref[...])
        phist = jnp.where(s == 0, 0.0, ph_ref[...])
        mix, znew, pnew = _mix_rows(proj, zhist, phist, s * TL, TL, convw_ref[...], poolbd_ref[...],
                                    pscale_ref[...], masked_wm(), bias_ref[...])
        zh_ref[...] = znew
        ph_ref[...] = pnew
        cpr_ref[0] = znew
        ppr_ref[0] = pnew
        if not first_layer:
            _unsort_results(lposp_ref, i + 1, gbuf, y0buf, y1buf)
        wait_positions(prev_slot)
        sort_rows(prev_slot)
        finish(x, mix)

    @pl.when(i <= N_PROMPT_TILES - 1)
    def _prompt_runs():
        start_runs(jnp.maximum(i - 1, 0), prev_slot)
        if not first_layer:
            @pl.when(i + 3 < N_TILES)
            def _():
                _start_run_gather(runsp_ref, i + 3, yp_ref, gbuf, gsem)

    @pl.when(i == N_PROMPT_TILES)
    def _sample():
        x = load_x(False)
        proj = project(x, precise=True)
        wm = masked_wm()
        wm_lo = masked_wm(wmlo_ref)
        mixes = []
        for b in range(DEC_BATCH):
            rows = slice(b * DEC_SEQ, (b + 1) * DEC_SEQ)
            mix, znew, pnew = _mix_rows(proj[rows, :], sconv_ref[b], spool_ref[b], PAST_LEN, DEC_SEQ,
                                        convw_ref[...], poolbd_ref[...], pscale_ref[...], wm, bias_ref[...],
                                        pool_lo=poollo_ref[...], wm_lo=wm_lo)
            csm_ref[b] = znew
            psm_ref[b] = pnew
            mixes.append(mix)
        sv_ref[...] = proj[:, 1792:2176]
        wait_positions(prev_slot)
        sort_rows(prev_slot)
        finish(x, jnp.concatenate(mixes, axis=0), precise=True)
        start_runs(i - 1, prev_slot)
        wait_positions(i % 2)
        sort_rows(i % 2)
        start_runs(i, i % 2)
        wait_rows(i % 2)
        _close_pages()


def _mixer_call(first_layer, xs, sconv_pad, spool_pad, g1, w_in, conv_w, pool_bd, pool_scale, wm_all,
                bias_full, w_out, g2, wr_t, br_col, triu, tril_e, lows):
    tile = lambda i, *_: (i, 0)
    prompt_tile = lambda i, *_: (jnp.minimum(i, N_PROMPT_TILES - 1), 0)
    const2 = lambda i, *_: (0, 0)
    const3 = lambda i, *_: (0, 0, 0)
    if first_layer:
        prefetch = ()
        x_specs = [pl.BlockSpec((TL, D_MODEL), prompt_tile), pl.BlockSpec((TL, D_MODEL), const2)]
    else:
        prefetch = xs[0:2]
        xs = xs[2:]
        x_specs = [pl.BlockSpec((TL, D_MODEL), tile), pl.BlockSpec(memory_space=pl.ANY),
                   pl.BlockSpec((TL, 128), tile)]
    full = lambda a: pl.BlockSpec(a.shape, const2 if a.ndim == 2 else const3, pipeline_mode=pl.Buffered(1))
    weights = [sconv_pad, spool_pad, g1, w_in, conv_w, pool_bd, pool_scale, wm_all, bias_full, w_out, g2,
               wr_t, br_col, triu, tril_e, *lows]
    in_specs = x_specs + [full(a) for a in weights]
    seq_of = lambda i, *_: (jnp.minimum(i // TILES_PER_SEQ, BATCH - 1), 0, 0)
    out_shape = [
        jax.ShapeDtypeStruct((T_ALL, D_MODEL), jnp.float32),
        jax.ShapeDtypeStruct((PAGE_ROWS, 1, D_PACK), jnp.uint32),
        jax.ShapeDtypeStruct((N_TILES, 2, TL), jnp.int32),
        jax.ShapeDtypeStruct((N_TILES * RUNS_PER_TILE,), jnp.int32),
        jax.ShapeDtypeStruct((6, PAGE_LANES), jnp.int32),
        jax.ShapeDtypeStruct((T_ALL, 128), jnp.float32),
        jax.ShapeDtypeStruct((BATCH, HIST_ROWS, D_CONV), jnp.float32),
        jax.ShapeDtypeStruct((BATCH, HIST_ROWS, D_POOL), jnp.float32),
        jax.ShapeDtypeStruct((DEC_BATCH, HIST_ROWS, D_CONV), jnp.float32),
        jax.ShapeDtypeStruct((DEC_BATCH, HIST_ROWS, D_POOL), jnp.float32),
        jax.ShapeDtypeStruct((T_SAMPLE, D_SGU), jnp.float32),
    ]
    out_specs = [
        pl.BlockSpec((TL, D_MODEL), tile),
        pl.BlockSpec(memory_space=pl.ANY),
        pl.BlockSpec((1, 2, TL), lambda i, *_: (i, 0, 0)),
        pl.BlockSpec(memory_space=pltpu.SMEM),
        pl.BlockSpec(memory_space=pltpu.SMEM),
        pl.BlockSpec((TL, 128), tile),
        pl.BlockSpec((1, HIST_ROWS, D_CONV), seq_of),
        pl.BlockSpec((1, HIST_ROWS, D_POOL), seq_of),
        pl.BlockSpec((DEC_BATCH, HIST_ROWS, D_CONV), const3),
        pl.BlockSpec((DEC_BATCH, HIST_ROWS, D_POOL), const3),
        pl.BlockSpec((T_SAMPLE, D_SGU), const2),
    ]
    scratch = [
        pltpu.VMEM((HIST_ROWS, D_CONV), jnp.float32),
        pltpu.VMEM((HIST_ROWS, D_POOL), jnp.float32),
        pltpu.VMEM((TL, 1, D_PACK), jnp.uint32),
        pltpu.VMEM((2, 2 * TL, 1, D_PACK), jnp.uint32),
        pltpu.VMEM((8, TL), jnp.int32),
        pltpu.VMEM((8, 128), jnp.int32),
        pltpu.SMEM((2, 8, TL), jnp.int32),
        pltpu.SMEM((2, 8, 128), jnp.int32),
        pltpu.VMEM((N_EXPERTS, 128), jnp.int32),
        pltpu.VMEM((N_EXPERTS, 128), jnp.int32),
        pltpu.VMEM((8, 128), jnp.int32),
        pltpu.VMEM((8, PAGE_LANES), jnp.int32),
        pltpu.SMEM((N_EXPERTS, 128), jnp.int32),
        pltpu.SMEM((N_EXPERTS, 128), jnp.int32),
        pltpu.SMEM((8, 128), jnp.int32),
        pltpu.SMEM((8, PAGE_LANES), jnp.int32),
        pltpu.SMEM((N_EXPERTS,), jnp.int32),
        pltpu.SemaphoreType.DMA((2,)),
        pltpu.SemaphoreType.DMA(()),
    ]
    if not first_layer:
        scratch += [
            pltpu.VMEM((2, 2 * TL, 1, D_PACK), jnp.uint32),
            pltpu.VMEM((2, TL, 1, D_PACK), jnp.uint32),
            pltpu.VMEM((2, TL, 1, D_PACK), jnp.uint32),
            pltpu.SemaphoreType.DMA((2,)),
        ]
    grid_spec = pltpu.PrefetchScalarGridSpec(num_scalar_prefetch=len(prefetch), grid=(N_TILES,),
                                             in_specs=in_specs, out_specs=out_specs, scratch_shapes=scratch)
    return pl.pallas_call(
        functools.partial(_mixer_kernel, first_layer),
        grid_spec=grid_spec,
        out_shape=out_shape,
        compiler_params=pltpu.CompilerParams(dimension_semantics=("arbitrary",),
                                             vmem_limit_bytes=VMEM_LIMIT),
        name="mixer_first" if first_layer else "mixer_next",
    )(*prefetch, *xs, *weights)


def _expert_kernel(layer, meta_ref, hs_ref, wg_hbm, wu_hbm, wd_hbm, y_ref, xbuf, wg32, wu32, wd32, wg16, wu16,
                   wd16, wsem):
    s = pl.program_id(0)
    npages = meta_ref[2, 0]

    def weight_copies(expert, slot):
        return [pltpu.make_async_copy(wg_hbm.at[layer, expert], wg32.at[slot], wsem.at[slot]),
                pltpu.make_async_copy(wu_hbm.at[layer, expert], wu32.at[slot], wsem.at[slot]),
                pltpu.make_async_copy(wd_hbm.at[layer, expert], wd32.at[slot], wsem.at[slot])]

    @pl.when(s == 0)
    def _first_weights():
        for c in weight_copies(meta_ref[1, 0], 0):
            c.start()

    @pl.when((s < npages) & (meta_ref[3, s] == 1))
    def _switch_expert():
        slot = meta_ref[4, s] % 2
        for c in weight_copies(meta_ref[1, s], slot):
            c.wait()
        nxt = meta_ref[5, s]

        @pl.when(nxt >= 0)
        def _():
            for c in weight_copies(nxt, 1 - slot):
                c.start()
        wg16[...] = wg32[slot].astype(jnp.bfloat16)
        wu16[...] = wu32[slot].astype(jnp.bfloat16)
        wd16[...] = wd32[slot].astype(jnp.bfloat16)

    @pl.when(s < npages)
    def _page():
        xbuf[...] = _unpack_rows(hs_ref[...])
        x = xbuf[...].astype(jnp.bfloat16)
        hg = jnp.dot(x, wg16[...], preferred_element_type=jnp.float32)
        hu = jnp.dot(x, wu16[...], preferred_element_type=jnp.float32)
        h = (hg * jax.nn.sigmoid(hg) * hu).astype(jnp.bfloat16)
        out = jnp.dot(h, wd16[...], preferred_element_type=jnp.float32)
        y_ref[...] = _pack_rows(out)

    @pl.when(s >= npages)
    def _unused_page():
        y_ref[...] = jnp.zeros((TM, 1, D_PACK), jnp.uint32)


def _expert_call(layer, meta, hs, w_gate, w_up, w_down):
    page = lambda s, meta: (meta[0, s], 0, 0)
    grid_spec = pltpu.PrefetchScalarGridSpec(
        num_scalar_prefetch=1,
        grid=(N_PAGES,),
        in_specs=[
            pl.BlockSpec((TM, 1, D_PACK), page),
            pl.BlockSpec(memory_space=pl.ANY),
            pl.BlockSpec(memory_space=pl.ANY),
            pl.BlockSpec(memory_space=pl.ANY),
        ],
        out_specs=pl.BlockSpec((TM, 1, D_PACK), page),
        scratch_shapes=[
            pltpu.VMEM((TM, D_MODEL), jnp.float32),
            pltpu.VMEM((2, D_MODEL, D_EXPERT), jnp.float32),
            pltpu.VMEM((2, D_MODEL, D_EXPERT), jnp.float32),
            pltpu.VMEM((2, D_EXPERT, D_MODEL), jnp.float32),
            pltpu.VMEM((D_MODEL, D_EXPERT), jnp.bfloat16),
            pltpu.VMEM((D_MODEL, D_EXPERT), jnp.bfloat16),
            pltpu.VMEM((D_EXPERT, D_MODEL), jnp.bfloat16),
            pltpu.SemaphoreType.DMA((2,)),
        ],
    )
    return pl.pallas_call(
        functools.partial(_expert_kernel, layer),
        grid_spec=grid_spec,
        out_shape=jax.ShapeDtypeStruct((PAGE_ROWS, 1, D_PACK), jnp.uint32),
        compiler_params=pltpu.CompilerParams(dimension_semantics=("arbitrary",),
                                             vmem_limit_bytes=VMEM_LIMIT),
        name="experts",
    )(meta, hs, w_gate, w_up, w_down)


def _final_kernel(lposp_ref, runsp_ref, xm_ref, yp_ref, gc_ref, g_ref, yp_out, ys_out, gbuf, y0buf, y1buf, gsem):
    i = pl.program_id(0)

    @pl.when(i == 0)
    def _():
        _first_results(lposp_ref, runsp_ref, yp_ref, gbuf, y0buf, y1buf, gsem)

    def normed():
        g = gc_ref[...]
        x = xm_ref[...] + g[:, 0:1] * _unpack_rows(y0buf[i % 2]) + g[:, 1:2] * _unpack_rows(y1buf[i % 2])
        return _rms(x, g_ref[...])

    @pl.when(i < N_PROMPT_TILES)
    def _():
        _wait_run_gather(i + 1, gbuf, gsem)
        yp_out[...] = normed()
        _unsort_results(lposp_ref, i + 1, gbuf, y0buf, y1buf)

    @pl.when(i + 3 < N_TILES)
    def _():
        _start_run_gather(runsp_ref, i + 3, yp_ref, gbuf, gsem)

    @pl.when(i == N_PROMPT_TILES)
    def _():
        ys_out[...] = normed()


def _final_call(lpos, runs, x_mid, y_pages, gcol, g):
    tile = lambda i, *_: (i, 0)
    grid_spec = pltpu.PrefetchScalarGridSpec(
        num_scalar_prefetch=2,
        grid=(N_TILES,),
        in_specs=[pl.BlockSpec((TL, D_MODEL), tile), pl.BlockSpec(memory_space=pl.ANY),
                  pl.BlockSpec((TL, 128), tile), pl.BlockSpec((1, D_MODEL), lambda i, *_: (0, 0))],
        out_specs=[pl.BlockSpec((TL, D_MODEL), lambda i, *_: (jnp.minimum(i, N_PROMPT_TILES - 1), 0)),
                   pl.BlockSpec((TL, D_MODEL), lambda i, *_: (0, 0))],
        scratch_shapes=[pltpu.VMEM((2, 2 * TL, 1, D_PACK), jnp.uint32), pltpu.VMEM((2, TL, 1, D_PACK), jnp.uint32),
                        pltpu.VMEM((2, TL, 1, D_PACK), jnp.uint32), pltpu.SemaphoreType.DMA((2,))],
    )
    return pl.pallas_call(
        _final_kernel,
        grid_spec=grid_spec,
        out_shape=[jax.ShapeDtypeStruct((T_PROMPT, D_MODEL), jnp.float32),
                   jax.ShapeDtypeStruct((T_SAMPLE, D_MODEL), jnp.float32)],
        compiler_params=pltpu.CompilerParams(dimension_semantics=("arbitrary",),
                                             vmem_limit_bytes=VMEM_LIMIT),
        name="final_norm",
    )(lpos, runs, x_mid, y_pages, gcol, g)


def kernel(x_prompt, x_sample, state_conv, state_pool, norm1_g, w_in, conv_w, pool_w, pool_scale, sgu_w, sgu_b, w_out, norm2_g, router_coarse_w, router_coarse_b, router_fine_w, router_fine_b, moe_w_gate, moe_w_up, moe_w_down, final_norm_g):
    bf16 = jnp.bfloat16
    xs = (x_prompt.reshape(T_PROMPT, D_MODEL), x_sample.reshape(T_SAMPLE, D_MODEL))
    sconv_pad = jnp.pad(state_conv, ((0, 0), (0, 0), (HIST_ROWS - (CONV_WIDTH - 1), 0), (0, 0)))
    spool_pad = jnp.pad(state_pool, ((0, 0), (0, 0), (HIST_ROWS - POOL_HIST, 0), (0, 0)))
    idx = jnp.arange(TL, dtype=jnp.int32)
    triu = (idx[:, None] < idx[None, :]).astype(bf16)
    ide = jnp.arange(N_EXPERTS, dtype=jnp.int32)
    tril_e = (ide[None, :] < ide[:, None]).astype(bf16)
    conv_pr, pool_pr, conv_sm, pool_sm, sgu_v = [], [], [], [], []
    x_mid = y_pages = gcol = lpos = runs = None
    for l in range(DEPTH):
        pool_bd32 = jax.scipy.linalg.block_diag(*[pool_w[l, g] for g in range(4)])
        pool_bd = pool_bd32.astype(bf16)
        wm32 = sgu_w[l].reshape(SGU_HEADS * SGU_LEN, SGU_LEN)
        wm_all = wm32.astype(bf16)
        bias_full = jnp.repeat(sgu_b[l].T, SGU_HEAD_DIM, axis=1)
        wr = jnp.zeros((ROUTER_ROWS, D_MODEL), jnp.float32)
        wr = wr.at[0:N_GROUPS].set(router_coarse_w[l].T).at[8:].set(router_fine_w[l].T)
        wr_hi = wr.astype(bf16)
        wr_lo = (wr - wr_hi.astype(jnp.float32)).astype(bf16)
        wr_t = jnp.concatenate([wr_hi, wr_lo], axis=0)
        br_col = jnp.zeros((ROUTER_ROWS, 1), jnp.float32)
        br_col = br_col.at[0:N_GROUPS, 0].set(router_coarse_b[l]).at[8:, 0].set(router_fine_b[l])
        low = lambda w: (w - w.astype(bf16).astype(jnp.float32)).astype(bf16)
        lows = (low(w_in[l]), low(w_out[l]), low(pool_bd32), low(wm32))
        outs = _mixer_call(l == 0, xs, sconv_pad[l], spool_pad[l], norm1_g[l].reshape(1, D_MODEL),
                           w_in[l].astype(bf16), conv_w[l], pool_bd, pool_scale[l].reshape(1, D_POOL),
                           wm_all, bias_full, w_out[l].astype(bf16), norm2_g[l].reshape(1, D_MODEL),
                           wr_t, br_col, triu, tril_e, lows)
        x_mid, hs, lpos, runs, meta, gcol, cpr, ppr, csm, psm, sv = outs
        lpos = lpos.reshape(N_TILES * 2 * TL)
        conv_pr.append(cpr[:, HIST_ROWS - (CONV_WIDTH - 1):, :])
        pool_pr.append(ppr[:, HIST_ROWS - POOL_HIST:, :])
        conv_sm.append(csm[:, HIST_ROWS - (CONV_WIDTH - 1):, :])
        pool_sm.append(psm[:, HIST_ROWS - POOL_HIST:, :])
        sgu_v.append(sv.reshape(DEC_BATCH, DEC_SEQ, D_SGU))
        y_pages = _expert_call(l, meta, hs, moe_w_gate, moe_w_up, moe_w_down)
        xs = (lpos, runs, x_mid, y_pages, gcol)
    y_prompt, y_sample = _final_call(lpos, runs, x_mid, y_pages, gcol, final_norm_g.reshape(1, D_MODEL))
    return (y_prompt.reshape(BATCH, SEQ, D_MODEL), y_sample.reshape(DEC_BATCH, DEC_SEQ, D_MODEL),
            jnp.stack(conv_pr), jnp.stack(pool_pr), jnp.stack(conv_sm), jnp.stack(pool_sm),
            jnp.stack(sgu_v))
```

```python
import functools

import jax
import jax.numpy as jnp
from jax import lax
from jax.experimental import pallas as pl
from jax.experimental.pallas import tpu as pltpu

D_MODEL = 1024
BATCH = 8
SEQ = 2048
DEPTH = 2
DEC_BATCH = 8
DEC_SEQ = 64
PAST_LEN = 1024
D_CONV = 384
CONV_WIDTH = 3
D_POOL = 256
POOL_HIST = 15
D_SGU = 384
SGU_HEADS = 4
SGU_HEAD_DIM = 96
SGU_LEN = 128
D_PROJ = 2176
N_GROUPS = 4
EXPERTS_PER_GROUP = 8
N_EXPERTS = 32
D_EXPERT = 512
EPS = 1e-6

T_PROMPT = BATCH * SEQ
T_SAMPLE = DEC_BATCH * DEC_SEQ
T_ALL = T_PROMPT + T_SAMPLE
TL = 512
TILES_PER_SEQ = SEQ // TL
N_PROMPT_TILES = T_PROMPT // TL
N_TILES = N_PROMPT_TILES + T_SAMPLE // TL
HIST_ROWS = 16
ROUTER_ROWS = 8 + N_EXPERTS
TM = 512
N_ASSIGN = 2 * T_ALL
N_PAGES = N_ASSIGN // TM + N_EXPERTS
PAGE_LANES = 256
PAGE_ROWS = N_PAGES * TM
RUN_FIELDS = 9
RUNS_PER_TILE = N_EXPERTS * RUN_FIELDS
D_PACK = D_MODEL // 2
VMEM_LIMIT = 56 * 1024 * 1024

assert TM <= TL <= 2 * TM and N_PAGES <= PAGE_LANES and T_ALL == N_TILES * TL


def _pack_rows(x):
    words = pltpu.pack_elementwise([x[:, :D_PACK], x[:, D_PACK:]], packed_dtype=jnp.bfloat16)
    return words.reshape(x.shape[0], 1, D_PACK)


def _unpack_rows(words):
    w = words.reshape(words.shape[0], D_PACK)
    halves = [pltpu.unpack_elementwise(w, index=k, packed_dtype=jnp.bfloat16, unpacked_dtype=jnp.float32)
              for k in range(2)]
    return jnp.concatenate(halves, axis=1)


def _split(a):
    hi = a.astype(jnp.bfloat16)
    return hi, (a - hi.astype(jnp.float32)).astype(jnp.bfloat16)


def _dot_split(a, b_hi, b_lo):
    a_hi, a_lo = _split(a)
    dot = functools.partial(jnp.dot, preferred_element_type=jnp.float32)
    return dot(a_hi, b_hi) + dot(a_lo, b_hi) + dot(a_hi, b_lo)


def _rms(x, g):
    return x * lax.rsqrt(jnp.mean(x * x, axis=-1, keepdims=True) + EPS) * g


def _mix_rows(proj, zhist, phist, pos0, n, conv_w, pool_bd, pool_scale, wm_all, bias_full, pool_lo=None,
              wm_lo=None):
    a_b = proj[:, 0:384]
    a_c = proj[:, 384:768]
    a_h = proj[:, 768:1152]
    p_in = proj[:, 1152:1408]
    s_u = proj[:, 1408:1792]
    s_v = proj[:, 1792:2176]

    z = a_c * a_h
    zext = jnp.concatenate([zhist, z], axis=0)
    conv_y = (conv_w[0:1, :] * pltpu.roll(zext, 2, 0)[HIST_ROWS:, :]
              + conv_w[1:2, :] * pltpu.roll(zext, 1, 0)[HIST_ROWS:, :]
              + conv_w[2:3, :] * z)
    a_out = a_b * conv_y

    pext = jnp.concatenate([phist, p_in], axis=0)
    s2 = pext + pltpu.roll(pext, 1, 0)
    s4 = s2 + pltpu.roll(s2, 2, 0)
    s8 = s4 + pltpu.roll(s4, 4, 0)
    s16 = s8 + pltpu.roll(s8, 8, 0)
    lane = lax.broadcasted_iota(jnp.int32, (1, D_POOL), 1)
    wsum = jnp.where(lane < 64, s2, jnp.where(lane < 128, s4, jnp.where(lane < 192, s8, s16)))
    wsum = wsum[HIST_ROWS:, :]
    win = jnp.where(lane < 64, 2.0, jnp.where(lane < 128, 4.0, jnp.where(lane < 192, 8.0, 16.0)))
    pos = (pos0 + lax.broadcasted_iota(jnp.int32, (n, 1), 0) + 1).astype(jnp.float32)
    cnt = jnp.minimum(win, pos)
    pooled = wsum / cnt - p_in
    if pool_lo is None:
        p_out = jnp.dot(pooled.astype(jnp.bfloat16), pool_bd, preferred_element_type=jnp.float32)
    else:
        p_out = _dot_split(pooled, pool_bd, pool_lo)
    p_out = p_out * pool_scale

    lane_s = lax.broadcasted_iota(jnp.int32, (1, D_SGU), 1)
    chunk = min(n, SGU_LEN)
    def head_rows(w):
        if chunk == SGU_LEN:
            return w
        return jnp.concatenate([w[h * SGU_LEN:h * SGU_LEN + chunk, 0:chunk] for h in range(SGU_HEADS)], axis=0)
    wm = head_rows(wm_all)
    s_rows = []
    for c in range(n // chunk):
        v_c = s_v[c * chunk:(c + 1) * chunk, :]
        if wm_lo is None:
            r = jnp.dot(wm, v_c.astype(jnp.bfloat16), preferred_element_type=jnp.float32)
        else:
            v_hi, v_lo = _split(v_c)
            r = (jnp.dot(wm, v_hi, preferred_element_type=jnp.float32)
                 + jnp.dot(head_rows(wm_lo), v_hi, preferred_element_type=jnp.float32)
                 + jnp.dot(wm, v_lo, preferred_element_type=jnp.float32))
        s_c = jnp.where(lane_s < 96, r[0:chunk],
                        jnp.where(lane_s < 192, r[chunk:2 * chunk],
                                  jnp.where(lane_s < 288, r[2 * chunk:3 * chunk], r[3 * chunk:4 * chunk])))
        s_rows.append(s_c + bias_full[0:chunk, :])
    s_gate = s_rows[0] if len(s_rows) == 1 else jnp.concatenate(s_rows, axis=0)
    s_out = s_u * s_gate

    mix = jnp.concatenate([a_out, p_out, s_out], axis=-1)
    return mix, zext[n:n + HIST_ROWS, :], pext[n:n + HIST_ROWS, :]


def _route(h2, wr_t, br_col, n):
    h_hi = h2.astype(jnp.bfloat16)
    h_lo = (h2 - h_hi.astype(jnp.float32)).astype(jnp.bfloat16)
    nt = (((1,), (1,)), ((), ()))
    by_hi = lax.dot_general(wr_t, h_hi, nt, preferred_element_type=jnp.float32)
    by_lo = lax.dot_general(wr_t[0:ROUTER_ROWS, :], h_lo, nt, preferred_element_type=jnp.float32)
    logits = by_hi[0:ROUTER_ROWS, :] + by_hi[ROUTER_ROWS:, :] + by_lo + br_col
    row8 = lax.broadcasted_iota(jnp.int32, (8, n), 0)
    lc = jnp.where(row8 < N_GROUPS, logits[0:8, :], -jnp.inf)
    mc = jnp.max(lc, axis=0, keepdims=True)
    g_sel = jnp.min(jnp.where(lc == mc, row8, 8), axis=0, keepdims=True)
    p_sel = 1.0 / jnp.sum(jnp.exp(lc - mc), axis=0, keepdims=True)
    lf = logits[8 + 3 * EXPERTS_PER_GROUP:8 + 4 * EXPERTS_PER_GROUP, :]
    for g in (2, 1, 0):
        lf = jnp.where(g_sel == g, logits[8 + g * EXPERTS_PER_GROUP:8 + (g + 1) * EXPERTS_PER_GROUP, :], lf)
    m1 = jnp.max(lf, axis=0, keepdims=True)
    i1 = jnp.min(jnp.where(lf == m1, row8, 8), axis=0, keepdims=True)
    lf2 = jnp.where(row8 == i1, -jnp.inf, lf)
    m2 = jnp.max(lf2, axis=0, keepdims=True)
    i2 = jnp.min(jnp.where(lf2 == m2, row8, 8), axis=0, keepdims=True)
    t = jnp.exp(m2 - m1)
    wa = 1.0 / (1.0 + t)
    wb = t / (1.0 + t)
    e0 = g_sel * EXPERTS_PER_GROUP + i1
    e1 = g_sel * EXPERTS_PER_GROUP + i2
    return e0, e1, p_sel * wa, p_sel * wb


def _place_rows(e0, e1, triu, tril_e, fill_ref, base_ref, np_ref, pexp_ref):
    row_e = lax.broadcasted_iota(jnp.int32, (N_EXPERTS, TL), 0)
    oh0 = row_e == e0
    oh1 = row_e == e1
    oh = jnp.where(oh0 | oh1, 1.0, 0.0)
    rank = jnp.dot(oh.astype(jnp.bfloat16), triu, preferred_element_type=jnp.float32).astype(jnp.int32)
    cnt = jnp.sum(oh, axis=1, keepdims=True).astype(jnp.int32)
    lower = jnp.where(e0 < row_e, 1.0, 0.0) + jnp.where(e1 < row_e, 1.0, 0.0)
    first = jnp.sum(lower, axis=1, keepdims=True).astype(jnp.int32)
    sorted_all = first + rank
    lpos0 = jnp.sum(jnp.where(oh0, sorted_all, 0), axis=0, keepdims=True)
    lpos1 = jnp.sum(jnp.where(oh1, sorted_all, 0), axis=0, keepdims=True)

    fill = fill_ref[:, 0:1]
    base = base_ref[:, 0:1]
    npages = np_ref[0:1, 0:1]
    total = fill + cnt
    need = (total > TM).astype(jnp.int32) + (total > 2 * TM).astype(jnp.int32)
    need_b = jnp.broadcast_to(need.astype(jnp.float32), (N_EXPERTS, 128)).astype(jnp.bfloat16)
    before = jnp.dot(tril_e, need_b, preferred_element_type=jnp.float32)[:, 0:1].astype(jnp.int32)
    new_id = npages + before
    new_base = new_id * TM
    fill_ref[...] = jnp.broadcast_to(total - need * TM, (N_EXPERTS, 128))
    base_ref[...] = jnp.broadcast_to(jnp.where(need > 0, new_base + (need - 1) * TM, base), (N_EXPERTS, 128))
    np_ref[...] = jnp.broadcast_to(npages + jnp.sum(need, axis=0, keepdims=True), (8, 128))
    page_lane = lax.broadcasted_iota(jnp.int32, (N_EXPERTS, PAGE_LANES), 1)
    expert_col = lax.broadcasted_iota(jnp.int32, (N_EXPERTS, 1), 0)
    owns = ((page_lane == new_id) & (need >= 1)) | ((page_lane == new_id + 1) & (need == 2))
    pexp_ref[...] = pexp_ref[...] + jnp.sum(jnp.where(owns, expert_col, 0), axis=0, keepdims=True)

    lane = lax.broadcasted_iota(jnp.int32, (N_EXPERTS, 128), 1)
    cols = jnp.where(lane == 0, cnt, jnp.where(lane == 1, first, jnp.where(lane == 2, fill,
                     jnp.where(lane == 3, base, jnp.where(lane == 4, new_base, 0)))))
    square = jnp.concatenate([cols, jnp.zeros((128 - N_EXPERTS, 128), jnp.int32)], axis=0)
    per_expert = square.astype(jnp.float32).T.astype(jnp.int32)[0:8, :]
    return lpos0, lpos1, per_expert


def _run_pieces(cnt, first, fill, base, new_base):
    n0 = jnp.minimum(cnt, TM - fill)
    n1 = jnp.minimum(cnt - n0, TM)
    n2 = cnt - n0 - n1
    return ((first, base + fill, n0), (first + n0, new_base, n1), (first + n0 + n1, new_base + TM, n2))


def _start_run_gather(runs_ref, tile, pages_ref, gbuf, gsem):
    slot = tile % 2

    def per_expert(e, c):
        k0 = (tile * N_EXPERTS + e) * RUN_FIELDS
        for j in range(3):
            src = runs_ref[k0 + 3 * j]
            dst = runs_ref[k0 + 3 * j + 1]
            n = runs_ref[k0 + 3 * j + 2]

            @pl.when(n > 0)
            def _():
                pltpu.make_async_copy(pages_ref.at[pl.ds(dst, n)], gbuf.at[slot, pl.ds(src, n)],
                                      gsem.at[slot]).start()
        return c
    lax.fori_loop(0, N_EXPERTS, per_expert, 0)


def _wait_run_gather(tile, gbuf, gsem):
    pltpu.make_async_copy(gbuf.at[0], gbuf.at[0], gsem.at[tile % 2]).wait()


def _unsort_results(lpos_ref, tile, gbuf, y0buf, y1buf):
    k0 = tile * 2 * TL
    slot = tile % 2
    for r in range(TL):
        y0buf[slot, r] = gbuf[slot, lpos_ref[k0 + r]]
        y1buf[slot, r] = gbuf[slot, lpos_ref[k0 + TL + r]]


def _first_results(lpos_ref, runs_ref, pages_ref, gbuf, y0buf, y1buf, gsem):
    _start_run_gather(runs_ref, 0, pages_ref, gbuf, gsem)
    _wait_run_gather(0, gbuf, gsem)
    _unsort_results(lpos_ref, 0, gbuf, y0buf, y1buf)
    _start_run_gather(runs_ref, 1, pages_ref, gbuf, gsem)
    _start_run_gather(runs_ref, 2, pages_ref, gbuf, gsem)


def _mixer_kernel(first_layer, *refs):
    if first_layer:
        xp_ref, xs_ref = refs[0:2]
        rest = refs[2:]
    else:
        lposp_ref, runsp_ref, xm_ref, yp_ref, gc_ref = refs[0:5]
        rest = refs[5:]
    (sconv_ref, spool_ref, g1_ref, win_ref, convw_ref, poolbd_ref, pscale_ref, wm_ref, bias_ref,
     wout_ref, g2_ref, wr_ref, br_ref, triu_ref, trile_ref, winlo_ref, woutlo_ref, poollo_ref, wmlo_ref,
     xmid_ref, hs_ref, lpos_ref, runs_ref, meta_ref, gcol_ref, cpr_ref, ppr_ref, csm_ref, psm_ref, sv_ref,
     zh_ref, ph_ref, h2buf, sbuf, posv, pexv, poss, pexs, fill_ref, base_ref, np_ref,
     pexp_ref, fill_s, base_s, np_s, pexp_s, cnt_s, rsem, msem) = rest[:49]
    if not first_layer:
        gbuf, y0buf, y1buf, gsem = rest[49:]

    i = pl.program_id(0)

    def load_x(prompt):
        if first_layer:
            return xp_ref[...] if prompt else xs_ref[...]
        g = gc_ref[...]
        return (xm_ref[...] + g[:, 0:1] * _unpack_rows(y0buf[i % 2])
                + g[:, 1:2] * _unpack_rows(y1buf[i % 2]))

    row_m = lax.broadcasted_iota(jnp.int32, (SGU_HEADS * SGU_LEN, SGU_LEN), 0) % SGU_LEN
    col_m = lax.broadcasted_iota(jnp.int32, (SGU_HEADS * SGU_LEN, SGU_LEN), 1)

    def wait_rows(slot):
        pltpu.make_async_copy(sbuf.at[0], sbuf.at[0], rsem.at[slot]).wait()

    def wait_positions(slot):
        pltpu.make_async_copy(posv, poss.at[slot], msem).wait()
        pltpu.make_async_copy(pexv, pexs.at[slot], msem).wait()

    def sort_rows(slot):
        for r in range(TL):
            row = h2buf[r]
            sbuf[slot, poss[slot, 0, r]] = row
            sbuf[slot, poss[slot, 1, r]] = row

    def start_runs(tile, slot):
        def per_expert(e, c):
            pieces = _run_pieces(pexs[slot, 0, e], pexs[slot, 1, e], pexs[slot, 2, e], pexs[slot, 3, e],
                                 pexs[slot, 4, e])
            k0 = (tile * N_EXPERTS + e) * RUN_FIELDS
            for j, (src, dst, n) in enumerate(pieces):
                runs_ref[k0 + 3 * j] = src
                runs_ref[k0 + 3 * j + 1] = dst
                runs_ref[k0 + 3 * j + 2] = n

                @pl.when(n > 0)
                def _():
                    pltpu.make_async_copy(sbuf.at[slot, pl.ds(src, n)], hs_ref.at[pl.ds(dst, n)],
                                          rsem.at[slot]).start()
            return c
        lax.fori_loop(0, N_EXPERTS, per_expert, 0)

        @pl.when(tile >= 1)
        def _():
            wait_rows(1 - slot)

    def finish(x, mix, precise=False):
        if precise:
            x_mid = x + _dot_split(mix, wout_ref[...], woutlo_ref[...])
        else:
            x_mid = x + jnp.dot(mix.astype(jnp.bfloat16), wout_ref[...], preferred_element_type=jnp.float32)
        xmid_ref[...] = x_mid
        h2 = _rms(x_mid, g2_ref[...])
        h2buf[...] = _pack_rows(h2)
        e0, e1, g0, g1 = _route(h2, wr_ref[...], br_ref[...], TL)
        row128 = lax.broadcasted_iota(jnp.int32, (128, TL), 0)
        gcol_ref[...] = jnp.where(row128 == 0, g0, jnp.where(row128 == 1, g1, 0.0)).T
        lpos0, lpos1, per_expert = _place_rows(e0, e1, triu_ref[...], trile_ref[...], fill_ref, base_ref,
                                               np_ref, pexp_ref)
        row8 = lax.broadcasted_iota(jnp.int32, (8, TL), 0)
        posv[...] = jnp.where(row8 == 0, lpos0, jnp.where(row8 == 1, lpos1, 0))
        pexv[...] = per_expert
        row2 = lax.broadcasted_iota(jnp.int32, (2, TL), 0)
        lpos_ref[0] = jnp.where(row2 == 0, lpos0, lpos1)
        pltpu.make_async_copy(posv, poss.at[i % 2], msem).start()
        pltpu.make_async_copy(pexv, pexs.at[i % 2], msem).start()

    def project(x, precise=False):
        h = _rms(x, g1_ref[...])
        if precise:
            return _dot_split(h, win_ref[...], winlo_ref[...])
        return jnp.dot(h.astype(jnp.bfloat16), win_ref[...], preferred_element_type=jnp.float32)

    def masked_wm(ref=wm_ref):
        return jnp.where(col_m <= row_m, ref[...], jnp.zeros_like(ref[...]))

    @pl.when(i == 0)
    def _init():
        zh_ref[...] = jnp.zeros_like(zh_ref)
        ph_ref[...] = jnp.zeros_like(ph_ref)
        fill_ref[...] = jnp.full_like(fill_ref, TM)
        base_ref[...] = jnp.zeros_like(base_ref)
        np_ref[...] = jnp.zeros_like(np_ref)
        pexp_ref[...] = jnp.zeros_like(pexp_ref)
        h2buf[...] = jnp.zeros((TL, 1, D_PACK), jnp.uint32)
        row8 = lax.broadcasted_iota(jnp.int32, (8, TL), 0)
        posv[...] = jnp.minimum(row8, 1) * TL + lax.broadcasted_iota(jnp.int32, (8, TL), 1)
        pexv[...] = jnp.zeros((8, 128), jnp.int32)
        pltpu.make_async_copy(posv, poss.at[1], msem).start()
        pltpu.make_async_copy(pexv, pexs.at[1], msem).start()
        if not first_layer:
            _first_results(lposp_ref, runsp_ref, yp_ref, gbuf, y0buf, y1buf, gsem)

    def _close_pages():
        copies = [pltpu.make_async_copy(fill_ref, fill_s, msem), pltpu.make_async_copy(base_ref, base_s, msem),
                  pltpu.make_async_copy(np_ref, np_s, msem), pltpu.make_async_copy(pexp_ref, pexp_s, msem)]
        for c in copies:
            c.start()
        for c in copies:
            c.wait()
        npages = np_s[0, 0]

        def zero_cnt(e, c):
            cnt_s[e] = 0
            return c
        lax.fori_loop(0, N_EXPERTS, zero_cnt, 0)

        def count(p, c):
            e = pexp_s[0, p]
            cnt_s[e] = cnt_s[e] + 1
            return c
        lax.fori_loop(0, npages, count, 0)

        def prefix(e, run):
            n = cnt_s[e]
            cnt_s[e] = run
            return run + n
        lax.fori_loop(0, N_EXPERTS, prefix, 0)

        def emit(p, c):
            e = pexp_s[0, p]
            q = cnt_s[e]
            cnt_s[e] = q + 1
            meta_ref[0, q] = p
            meta_ref[1, q] = e
            return c
        lax.fori_loop(0, npages, emit, 0)

        def pad_meta(q, c):
            meta_ref[0, q] = jnp.minimum(q, N_PAGES - 1)
            meta_ref[1, q] = meta_ref[1, npages - 1]
            return c
        lax.fori_loop(npages, PAGE_LANES, pad_meta, 0)

        def fill_row2(q, c):
            meta_ref[2, q] = npages
            return c
        lax.fori_loop(0, PAGE_LANES, fill_row2, 0)

        def mark_first(q, seen):
            is_first = (q < npages) & ((q == 0) | (meta_ref[1, q] != meta_ref[1, jnp.maximum(q - 1, 0)]))
            seen = seen + jnp.where(is_first, 1, 0)
            meta_ref[3, q] = jnp.where(is_first, 1, 0)
            meta_ref[4, q] = jnp.maximum(seen - 1, 0)
            meta_ref[5, q] = -1
            return seen
        lax.fori_loop(0, PAGE_LANES, mark_first, 0)

        def mark_next(t, carry):
            cur, nxt = carry
            q = npages - 1 - t
            e = meta_ref[1, q]
            nxt = jnp.where(e != cur, cur, nxt)
            meta_ref[5, q] = nxt
            return e, nxt
        lax.fori_loop(0, npages, mark_next, (jnp.int32(-1), jnp.int32(-1)))

        h2buf[...] = jnp.zeros((TL, 1, D_PACK), jnp.uint32)

        def tails(start):
            def tail(e, c):
                f = fill_s[e, 0]
                b = base_s[e, 0]

                @pl.when(f < TM)
                def _zero_tail():
                    cp = pltpu.make_async_copy(h2buf.at[pl.ds(0, TM - f)], hs_ref.at[pl.ds(b + f, TM - f)],
                                               rsem.at[0])
                    cp.start() if start else cp.wait()
                return c
            lax.fori_loop(0, N_EXPERTS, tail, 0)

            def unused(p, c):
                cp = pltpu.make_async_copy(h2buf.at[pl.ds(0, TM)], hs_ref.at[pl.ds(p * TM, TM)], rsem.at[0])
                cp.start() if start else cp.wait()
                return c
            lax.fori_loop(npages, N_PAGES, unused, 0)

        tails(True)
        tails(False)

    prev_slot = (i + 1) % 2

    @pl.when(i < N_PROMPT_TILES)
    def _prompt():
        if not first_layer:
            _wait_run_gather(i + 1, gbuf, gsem)
        s = i % TILES_PER_SEQ
        x = load_x(True)
        proj = project(x)
        zhist = jnp.where(s == 0, 0.0, zh_ref[...])
        phist = jnp.where(s == 0, 0.0, ph_ref[...])
        mix, znew, pnew = _mix_rows(proj, zhist, phist, s * TL, TL, convw_ref[...], poolbd_ref[...],
                                    pscale_ref[...], masked_wm(), bias_ref[...])
        zh_ref[...] = znew
        ph_ref[...] = pnew
        cpr_ref[0] = znew
        ppr_ref[0] = pnew
        if not first_layer:
            _unsort_results(lposp_ref, i + 1, gbuf, y0buf, y1buf)
        wait_positions(prev_slot)
        sort_rows(prev_slot)
        finish(x, mix)

    @pl.when(i <= N_PROMPT_TILES - 1)
    def _prompt_runs():
        start_runs(jnp.maximum(i - 1, 0), prev_slot)
        if not first_layer:
            @pl.when(i + 3 < N_TILES)
            def _():
                _start_run_gather(runsp_ref, i + 3, yp_ref, gbuf, gsem)

    @pl.when(i == N_PROMPT_TILES)
    def _sample():
        x = load_x(False)
        proj = project(x, precise=True)
        wm = masked_wm()
        wm_lo = masked_wm(wmlo_ref)
        mixes = []
        for b in range(DEC_BATCH):
            rows = slice(b * DEC_SEQ, (b + 1) * DEC_SEQ)
            mix, znew, pnew = _mix_rows(proj[rows, :], sconv_ref[b], spool_ref[b], PAST_LEN, DEC_SEQ,
                                        convw_ref[...], poolbd_ref[...], pscale_ref[...], wm, bias_ref[...],
                                        pool_lo=poollo_ref[...], wm_lo=wm_lo)
            csm_ref[b] = znew
            psm_ref[b] = pnew
            mixes.append(mix)
        sv_ref[...] = proj[:, 1792:2176]
        wait_positions(prev_slot)
        sort_rows(prev_slot)
        finish(x, jnp.concatenate(mixes, axis=0), precise=True)
        start_runs(i - 1, prev_slot)
        wait_positions(i % 2)
        sort_rows(i % 2)
        start_runs(i, i % 2)
        wait_rows(i % 2)
        _close_pages()


def _mixer_call(first_layer, xs, sconv_pad, spool_pad, g1, w_in, conv_w, pool_bd, pool_scale, wm_all,
                bias_full, w_out, g2, wr_t, br_col, triu, tril_e, lows):
    tile = lambda i, *_: (i, 0)
    prompt_tile = lambda i, *_: (jnp.minimum(i, N_PROMPT_TILES - 1), 0)
    const2 = lambda i, *_: (0, 0)
    const3 = lambda i, *_: (0, 0, 0)
    if first_layer:
        prefetch = ()
        x_specs = [pl.BlockSpec((TL, D_MODEL), prompt_tile), pl.BlockSpec((TL, D_MODEL), const2)]
    else:
        prefetch = xs[0:2]
        xs = xs[2:]
        x_specs = [pl.BlockSpec((TL, D_MODEL), tile), pl.BlockSpec(memory_space=pl.ANY),
                   pl.BlockSpec((TL, 128), tile)]
    full = lambda a: pl.BlockSpec(a.shape, const2 if a.ndim == 2 else const3, pipeline_mode=pl.Buffered(1))
    weights = [sconv_pad, spool_pad, g1, w_in, conv_w, pool_bd, pool_scale, wm_all, bias_full, w_out, g2,
               wr_t, br_col, triu, tril_e, *lows]
    in_specs = x_specs + [full(a) for a in weights]
    seq_of = lambda i, *_: (jnp.minimum(i // TILES_PER_SEQ, BATCH - 1), 0, 0)
    out_shape = [
        jax.ShapeDtypeStruct((T_ALL, D_MODEL), jnp.float32),
        jax.ShapeDtypeStruct((PAGE_ROWS, 1, D_PACK), jnp.uint32),
        jax.ShapeDtypeStruct((N_TILES, 2, TL), jnp.int32),
        jax.ShapeDtypeStruct((N_TILES * RUNS_PER_TILE,), jnp.int32),
        jax.ShapeDtypeStruct((6, PAGE_LANES), jnp.int32),
        jax.ShapeDtypeStruct((T_ALL, 128), jnp.float32),
        jax.ShapeDtypeStruct((BATCH, HIST_ROWS, D_CONV), jnp.float32),
        jax.ShapeDtypeStruct((BATCH, HIST_ROWS, D_POOL), jnp.float32),
        jax.ShapeDtypeStruct((DEC_BATCH, HIST_ROWS, D_CONV), jnp.float32),
        jax.ShapeDtypeStruct((DEC_BATCH, HIST_ROWS, D_POOL), jnp.float32),
        jax.ShapeDtypeStruct((T_SAMPLE, D_SGU), jnp.float32),
    ]
    out_specs = [
        pl.BlockSpec((TL, D_MODEL), tile),
        pl.BlockSpec(memory_space=pl.ANY),
        pl.BlockSpec((1, 2, TL), lambda i, *_: (i, 0, 0)),
        pl.BlockSpec(memory_space=pltpu.SMEM),
        pl.BlockSpec(memory_space=pltpu.SMEM),
        pl.BlockSpec((TL, 128), tile),
        pl.BlockSpec((1, HIST_ROWS, D_CONV), seq_of),
        pl.BlockSpec((1, HIST_ROWS, D_POOL), seq_of),
        pl.BlockSpec((DEC_BATCH, HIST_ROWS, D_CONV), const3),
        pl.BlockSpec((DEC_BATCH, HIST_ROWS, D_POOL), const3),
        pl.BlockSpec((T_SAMPLE, D_SGU), const2),
    ]
    scratch = [
        pltpu.VMEM((HIST_ROWS, D_CONV), jnp.float32),
        pltpu.VMEM((HIST_ROWS, D_POOL), jnp.float32),
        pltpu.VMEM((TL, 1, D_PACK), jnp.uint32),
        pltpu.VMEM((2, 2 * TL, 1, D_PACK), jnp.uint32),
        pltpu.VMEM((8, TL), jnp.int32),
        pltpu.VMEM((8, 128), jnp.int32),
        pltpu.SMEM((2, 8, TL), jnp.int32),
        pltpu.SMEM((2, 8, 128), jnp.int32),
        pltpu.VMEM((N_EXPERTS, 128), jnp.int32),
        pltpu.VMEM((N_EXPERTS, 128), jnp.int32),
        pltpu.VMEM((8, 128), jnp.int32),
        pltpu.VMEM((8, PAGE_LANES), jnp.int32),
        pltpu.SMEM((N_EXPERTS, 128), jnp.int32),
        pltpu.SMEM((N_EXPERTS, 128), jnp.int32),
        pltpu.SMEM((8, 128), jnp.int32),
        pltpu.SMEM((8, PAGE_LANES), jnp.int32),
        pltpu.SMEM((N_EXPERTS,), jnp.int32),
        pltpu.SemaphoreType.DMA((2,)),
        pltpu.SemaphoreType.DMA(()),
    ]
    if not first_layer:
        scratch += [
            pltpu.VMEM((2, 2 * TL, 1, D_PACK), jnp.uint32),
            pltpu.VMEM((2, TL, 1, D_PACK), jnp.uint32),
            pltpu.VMEM((2, TL, 1, D_PACK), jnp.uint32),
            pltpu.SemaphoreType.DMA((2,)),
        ]
    grid_spec = pltpu.PrefetchScalarGridSpec(num_scalar_prefetch=len(prefetch), grid=(N_TILES,),
                                             in_specs=in_specs, out_specs=out_specs, scratch_shapes=scratch)
    return pl.pallas_call(
        functools.partial(_mixer_kernel, first_layer),
        grid_spec=grid_spec,
        out_shape=out_shape,
        compiler_params=pltpu.CompilerParams(dimension_semantics=("arbitrary",),
                                             vmem_limit_bytes=VMEM_LIMIT),
        name="mixer_first" if first_layer else "mixer_next",
    )(*prefetch, *xs, *weights)


def _expert_kernel(layer, meta_ref, hs_ref, wg_hbm, wu_hbm, wd_hbm, y_ref, xbuf, wg32, wu32, wd32, wg16, wu16,
                   wd16, wsem):
    s = pl.program_id(0)
    npages = meta_ref[2, 0]

    def weight_copies(expert, slot):
        return [pltpu.make_async_copy(wg_hbm.at[layer, expert], wg32.at[slot], wsem.at[slot]),
                pltpu.make_async_copy(wu_hbm.at[layer, expert], wu32.at[slot], wsem.at[slot]),
                pltpu.make_async_copy(wd_hbm.at[layer, expert], wd32.at[slot], wsem.at[slot])]

    @pl.when(s == 0)
    def _first_weights():
        for c in weight_copies(meta_ref[1, 0], 0):
            c.start()

    @pl.when((s < npages) & (meta_ref[3, s] == 1))
    def _switch_expert():
        slot = meta_ref[4, s] % 2
        for c in weight_copies(meta_ref[1, s], slot):
            c.wait()
        nxt = meta_ref[5, s]

        @pl.when(nxt >= 0)
        def _():
            for c in weight_copies(nxt, 1 - slot):
                c.start()
        wg16[...] = wg32[slot].astype(jnp.bfloat16)
        wu16[...] = wu32[slot].astype(jnp.bfloat16)
        wd16[...] = wd32[slot].astype(jnp.bfloat16)

    @pl.when(s < npages)
    def _page():
        xbuf[...] = _unpack_rows(hs_ref[...])
        x = xbuf[...].astype(jnp.bfloat16)
        hg = jnp.dot(x, wg16[...], preferred_element_type=jnp.float32)
        hu = jnp.dot(x, wu16[...], preferred_element_type=jnp.float32)
        h = (hg * jax.nn.sigmoid(hg) * hu).astype(jnp.bfloat16)
        out = jnp.dot(h, wd16[...], preferred_element_type=jnp.float32)
        y_ref[...] = _pack_rows(out)

    @pl.when(s >= npages)
    def _unused_page():
        y_ref[...] = jnp.zeros((TM, 1, D_PACK), jnp.uint32)


def _expert_call(layer, meta, hs, w_gate, w_up, w_down):
    page = lambda s, meta: (meta[0, s], 0, 0)
    grid_spec = pltpu.PrefetchScalarGridSpec(
        num_scalar_prefetch=1,
        grid=(N_PAGES,),
        in_specs=[
            pl.BlockSpec((TM, 1, D_PACK), page),
            pl.BlockSpec(memory_space=pl.ANY),
            pl.BlockSpec(memory_space=pl.ANY),
            pl.BlockSpec(memory_space=pl.ANY),
        ],
        out_specs=pl.BlockSpec((TM, 1, D_PACK), page),
        scratch_shapes=[
            pltpu.VMEM((TM, D_MODEL), jnp.float32),
            pltpu.VMEM((2, D_MODEL, D_EXPERT), jnp.float32),
            pltpu.VMEM((2, D_MODEL, D_EXPERT), jnp.float32),
            pltpu.VMEM((2, D_EXPERT, D_MODEL), jnp.float32),
            pltpu.VMEM((D_MODEL, D_EXPERT), jnp.bfloat16),
            pltpu.VMEM((D_MODEL, D_EXPERT), jnp.bfloat16),
            pltpu.VMEM((D_EXPERT, D_MODEL), jnp.bfloat16),
            pltpu.SemaphoreType.DMA((2,)),
        ],
    )
    return pl.pallas_call(
        functools.partial(_expert_kernel, layer),
        grid_spec=grid_spec,
        out_shape=jax.ShapeDtypeStruct((PAGE_ROWS, 1, D_PACK), jnp.uint32),
        compiler_params=pltpu.CompilerParams(dimension_semantics=("arbitrary",),
                                             vmem_limit_bytes=VMEM_LIMIT),
        name="experts",
    )(meta, hs, w_gate, w_up, w_down)


def _final_kernel(lposp_ref, runsp_ref, xm_ref, yp_ref, gc_ref, g_ref, yp_out, ys_out, gbuf, y0buf, y1buf, gsem):
    i = pl.program_id(0)

    @pl.when(i == 0)
    def _():
        _first_results(lposp_ref, runsp_ref, yp_ref, gbuf, y0buf, y1buf, gsem)

    def normed():
        g = gc_ref[...]
        x = xm_ref[...] + g[:, 0:1] * _unpack_rows(y0buf[i % 2]) + g[:, 1:2] * _unpack_rows(y1buf[i % 2])
        return _rms(x, g_ref[...])

    @pl.when(i < N_PROMPT_TILES)
    def _():
        _wait_run_gather(i + 1, gbuf, gsem)
        yp_out[...] = normed()
        _unsort_results(lposp_ref, i + 1, gbuf, y0buf, y1buf)

    @pl.when(i + 3 < N_TILES)
    def _():
        _start_run_gather(runsp_ref, i + 3, yp_ref, gbuf, gsem)

    @pl.when(i == N_PROMPT_TILES)
    def _():
        ys_out[...] = normed()


def _final_call(lpos, runs, x_mid, y_pages, gcol, g):
    tile = lambda i, *_: (i, 0)
    grid_spec = pltpu.PrefetchScalarGridSpec(
        num_scalar_prefetch=2,
        grid=(N_TILES,),
        in_specs=[pl.BlockSpec((TL, D_MODEL), tile), pl.BlockSpec(memory_space=pl.ANY),
                  pl.BlockSpec((TL, 128), tile), pl.BlockSpec((1, D_MODEL), lambda i, *_: (0, 0))],
        out_specs=[pl.BlockSpec((TL, D_MODEL), lambda i, *_: (jnp.minimum(i, N_PROMPT_TILES - 1), 0)),
                   pl.BlockSpec((TL, D_MODEL), lambda i, *_: (0, 0))],
        scratch_shapes=[pltpu.VMEM((2, 2 * TL, 1, D_PACK), jnp.uint32), pltpu.VMEM((2, TL, 1, D_PACK), jnp.uint32),
                        pltpu.VMEM((2, TL, 1, D_PACK), jnp.uint32), pltpu.SemaphoreType.DMA((2,))],
    )
    return pl.pallas_call(
        _final_kernel,
        grid_spec=grid_spec,
        out_shape=[jax.ShapeDtypeStruct((T_PROMPT, D_MODEL), jnp.float32),
                   jax.ShapeDtypeStruct((T_SAMPLE, D_MODEL), jnp.float32)],
        compiler_params=pltpu.CompilerParams(dimension_semantics=("arbitrary",),
                                             vmem_limit_bytes=VMEM_LIMIT),
        name="final_norm",
    )(lpos, runs, x_mid, y_pages, gcol, g)


def kernel(x_prompt, x_sample, state_conv, state_pool, norm1_g, w_in, conv_w, pool_w, pool_scale, sgu_w, sgu_b, w_out, norm2_g, router_coarse_w, router_coarse_b, router_fine_w, router_fine_b, moe_w_gate, moe_w_up, moe_w_down, final_norm_g):
    bf16 = jnp.bfloat16
    xs = (x_prompt.reshape(T_PROMPT, D_MODEL), x_sample.reshape(T_SAMPLE, D_MODEL))
    sconv_pad = jnp.pad(state_conv, ((0, 0), (0, 0), (HIST_ROWS - (CONV_WIDTH - 1), 0), (0, 0)))
    spool_pad = jnp.pad(state_pool, ((0, 0), (0, 0), (HIST_ROWS - POOL_HIST, 0), (0, 0)))
    idx = jnp.arange(TL, dtype=jnp.int32)
    triu = (idx[:, None] < idx[None, :]).astype(bf16)
    ide = jnp.arange(N_EXPERTS, dtype=jnp.int32)
    tril_e = (ide[None, :] < ide[:, None]).astype(bf16)
    conv_pr, pool_pr, conv_sm, pool_sm, sgu_v = [], [], [], [], []
    x_mid = y_pages = gcol = lpos = runs = None
    for l in range(DEPTH):
        pool_bd32 = jax.scipy.linalg.block_diag(*[pool_w[l, g] for g in range(4)])
        pool_bd = pool_bd32.astype(bf16)
        wm32 = sgu_w[l].reshape(SGU_HEADS * SGU_LEN, SGU_LEN)
        wm_all = wm32.astype(bf16)
        bias_full = jnp.repeat(sgu_b[l].T, SGU_HEAD_DIM, axis=1)
        wr = jnp.zeros((ROUTER_ROWS, D_MODEL), jnp.float32)
        wr = wr.at[0:N_GROUPS].set(router_coarse_w[l].T).at[8:].set(router_fine_w[l].T)
        wr_hi = wr.astype(bf16)
        wr_lo = (wr - wr_hi.astype(jnp.float32)).astype(bf16)
        wr_t = jnp.concatenate([wr_hi, wr_lo], axis=0)
        br_col = jnp.zeros((ROUTER_ROWS, 1), jnp.float32)
        br_col = br_col.at[0:N_GROUPS, 0].set(router_coarse_b[l]).at[8:, 0].set(router_fine_b[l])
        low = lambda w: (w - w.astype(bf16).astype(jnp.float32)).astype(bf16)
        lows = (low(w_in[l]), low(w_out[l]), low(pool_bd32), low(wm32))
        outs = _mixer_call(l == 0, xs, sconv_pad[l], spool_pad[l], norm1_g[l].reshape(1, D_MODEL),
                           w_in[l].astype(bf16), conv_w[l], pool_bd, pool_scale[l].reshape(1, D_POOL),
                           wm_all, bias_full, w_out[l].astype(bf16), norm2_g[l].reshape(1, D_MODEL),
                           wr_t, br_col, triu, tril_e, lows)
        x_mid, hs, lpos, runs, meta, gcol, cpr, ppr, csm, psm, sv = outs
        lpos = lpos.reshape(N_TILES * 2 * TL)
        conv_pr.append(cpr[:, HIST_ROWS - (CONV_WIDTH - 1):, :])
        pool_pr.append(ppr[:, HIST_ROWS - POOL_HIST:, :])
        conv_sm.append(csm[:, HIST_ROWS - (CONV_WIDTH - 1):, :])
        pool_sm.append(psm[:, HIST_ROWS - POOL_HIST:, :])
        sgu_v.append(sv.reshape(DEC_BATCH, DEC_SEQ, D_SGU))
        y_pages = _expert_call(l, meta, hs, moe_w_gate, moe_w_up, moe_w_down)
        xs = (lpos, runs, x_mid, y_pages, gcol)
    y_prompt, y_sample = _final_call(lpos, runs, x_mid, y_pages, gcol, final_norm_g.reshape(1, D_MODEL))
    return (y_prompt.reshape(BATCH, SEQ, D_MODEL), y_sample.reshape(DEC_BATCH, DEC_SEQ, D_MODEL),
            jnp.stack(conv_pr), jnp.stack(pool_pr), jnp.stack(conv_sm), jnp.stack(pool_sm),
            jnp.stack(sgu_v))
```

```python
import functools

import jax
import jax.numpy as jnp
from jax import lax
from jax.experimental import pallas as pl
from jax.experimental.pallas import tpu as pltpu

D_MODEL = 1024
BATCH = 8
SEQ = 2048
DEPTH = 2
DEC_BATCH = 8
DEC_SEQ = 64
PAST_LEN = 1024
D_CONV = 384
CONV_WIDTH = 3
D_POOL = 256
POOL_HIST = 15
D_SGU = 384
SGU_HEADS = 4
SGU_HEAD_DIM = 96
SGU_LEN = 128
D_PROJ = 2176
N_GROUPS = 4
EXPERTS_PER_GROUP = 8
N_EXPERTS = 32
D_EXPERT = 512
EPS = 1e-6

T_PROMPT = BATCH * SEQ
T_SAMPLE = DEC_BATCH * DEC_SEQ
T_ALL = T_PROMPT + T_SAMPLE
TL = 512
TILES_PER_SEQ = SEQ // TL
N_PROMPT_TILES = T_PROMPT // TL
N_TILES = N_PROMPT_TILES + T_SAMPLE // TL
HIST_ROWS = 16
ROUTER_ROWS = 8 + N_EXPERTS
TM = 512
N_ASSIGN = 2 * T_ALL
N_PAGES = N_ASSIGN // TM + N_EXPERTS
PAGE_LANES = 256
PAGE_ROWS = N_PAGES * TM
RUN_FIELDS = 9
RUNS_PER_TILE = N_EXPERTS * RUN_FIELDS
ROW_DTYPE = jnp.bfloat16
VMEM_LIMIT = 56 * 1024 * 1024

assert TM <= TL <= 2 * TM and N_PAGES <= PAGE_LANES and T_ALL == N_TILES * TL


def _pack_rows(x):
    return x.astype(ROW_DTYPE).reshape(x.shape[0], 2, D_MODEL // 2)


def _unpack_rows(rows):
    return rows.reshape(rows.shape[0], D_MODEL).astype(jnp.float32)


def _split(a):
    hi = a.astype(jnp.bfloat16)
    return hi, (a - hi.astype(jnp.float32)).astype(jnp.bfloat16)


def _dot_split(a, b_hi, b_lo):
    a_hi, a_lo = _split(a)
    dot = functools.partial(jnp.dot, preferred_element_type=jnp.float32)
    return dot(a_hi, b_hi) + dot(a_lo, b_hi) + dot(a_hi, b_lo)


def _rms(x, g):
    return x * lax.rsqrt(jnp.mean(x * x, axis=-1, keepdims=True) + EPS) * g


def _mix_rows(proj, zhist, phist, pos0, n, conv_w, pool_bd, pool_scale, wm_all, bias_full, pool_lo=None,
              wm_lo=None):
    a_b = proj[:, 0:384]
    a_c = proj[:, 384:768]
    a_h = proj[:, 768:1152]
    p_in = proj[:, 1152:1408]
    s_u = proj[:, 1408:1792]
    s_v = proj[:, 1792:2176]

    z = a_c * a_h
    zext = jnp.concatenate([zhist, z], axis=0)
    conv_y = (conv_w[0:1, :] * pltpu.roll(zext, 2, 0)[HIST_ROWS:, :]
              + conv_w[1:2, :] * pltpu.roll(zext, 1, 0)[HIST_ROWS:, :]
              + conv_w[2:3, :] * z)
    a_out = a_b * conv_y

    pext = jnp.concatenate([phist, p_in], axis=0)
    s2 = pext + pltpu.roll(pext, 1, 0)
    s4 = s2 + pltpu.roll(s2, 2, 0)
    s8 = s4 + pltpu.roll(s4, 4, 0)
    s16 = s8 + pltpu.roll(s8, 8, 0)
    lane = lax.broadcasted_iota(jnp.int32, (1, D_POOL), 1)
    wsum = jnp.where(lane < 64, s2, jnp.where(lane < 128, s4, jnp.where(lane < 192, s8, s16)))
    wsum = wsum[HIST_ROWS:, :]
    win = jnp.where(lane < 64, 2.0, jnp.where(lane < 128, 4.0, jnp.where(lane < 192, 8.0, 16.0)))
    pos = (pos0 + lax.broadcasted_iota(jnp.int32, (n, 1), 0) + 1).astype(jnp.float32)
    cnt = jnp.minimum(win, pos)
    pooled = wsum / cnt - p_in
    if pool_lo is None:
        p_out = jnp.dot(pooled.astype(jnp.bfloat16), pool_bd, preferred_element_type=jnp.float32)
    else:
        p_out = _dot_split(pooled, pool_bd, pool_lo)
    p_out = p_out * pool_scale

    lane_s = lax.broadcasted_iota(jnp.int32, (1, D_SGU), 1)
    chunk = min(n, SGU_LEN)
    def head_rows(w):
        if chunk == SGU_LEN:
            return w
        return jnp.concatenate([w[h * SGU_LEN:h * SGU_LEN + chunk, 0:chunk] for h in range(SGU_HEADS)], axis=0)
    wm = head_rows(wm_all)
    s_rows = []
    for c in range(n // chunk):
        v_c = s_v[c * chunk:(c + 1) * chunk, :]
        if wm_lo is None:
            r = jnp.dot(wm, v_c.astype(jnp.bfloat16), preferred_element_type=jnp.float32)
        else:
            v_hi, v_lo = _split(v_c)
            r = (jnp.dot(wm, v_hi, preferred_element_type=jnp.float32)
                 + jnp.dot(head_rows(wm_lo), v_hi, preferred_element_type=jnp.float32)
                 + jnp.dot(wm, v_lo, preferred_element_type=jnp.float32))
        s_c = jnp.where(lane_s < 96, r[0:chunk],
                        jnp.where(lane_s < 192, r[chunk:2 * chunk],
                                  jnp.where(lane_s < 288, r[2 * chunk:3 * chunk], r[3 * chunk:4 * chunk])))
        s_rows.append(s_c + bias_full[0:chunk, :])
    s_gate = s_rows[0] if len(s_rows) == 1 else jnp.concatenate(s_rows, axis=0)
    s_out = s_u * s_gate

    mix = jnp.concatenate([a_out, p_out, s_out], axis=-1)
    return mix, zext[n:n + HIST_ROWS, :], pext[n:n + HIST_ROWS, :]


def _route(h2, wr_t, br_col, n):
    h_hi = h2.astype(jnp.bfloat16)
    h_lo = (h2 - h_hi.astype(jnp.float32)).astype(jnp.bfloat16)
    nt = (((1,), (1,)), ((), ()))
    by_hi = lax.dot_general(wr_t, h_hi, nt, preferred_element_type=jnp.float32)
    by_lo = lax.dot_general(wr_t[0:ROUTER_ROWS, :], h_lo, nt, preferred_element_type=jnp.float32)
    logits = by_hi[0:ROUTER_ROWS, :] + by_hi[ROUTER_ROWS:, :] + by_lo + br_col
    row8 = lax.broadcasted_iota(jnp.int32, (8, n), 0)
    lc = jnp.where(row8 < N_GROUPS, logits[0:8, :], -jnp.inf)
    mc = jnp.max(lc, axis=0, keepdims=True)
    g_sel = jnp.min(jnp.where(lc == mc, row8, 8), axis=0, keepdims=True)
    p_sel = 1.0 / jnp.sum(jnp.exp(lc - mc), axis=0, keepdims=True)
    lf = logits[8 + 3 * EXPERTS_PER_GROUP:8 + 4 * EXPERTS_PER_GROUP, :]
    for g in (2, 1, 0):
        lf = jnp.where(g_sel == g, logits[8 + g * EXPERTS_PER_GROUP:8 + (g + 1) * EXPERTS_PER_GROUP, :], lf)
    m1 = jnp.max(lf, axis=0, keepdims=True)
    i1 = jnp.min(jnp.where(lf == m1, row8, 8), axis=0, keepdims=True)
    lf2 = jnp.where(row8 == i1, -jnp.inf, lf)
    m2 = jnp.max(lf2, axis=0, keepdims=True)
    i2 = jnp.min(jnp.where(lf2 == m2, row8, 8), axis=0, keepdims=True)
    t = jnp.exp(m2 - m1)
    wa = 1.0 / (1.0 + t)
    wb = t / (1.0 + t)
    e0 = g_sel * EXPERTS_PER_GROUP + i1
    e1 = g_sel * EXPERTS_PER_GROUP + i2
    return e0, e1, p_sel * wa, p_sel * wb


def _place_rows(e0, e1, triu, tril_e, fill_ref, base_ref, np_ref, pexp_ref):
    row_e = lax.broadcasted_iota(jnp.int32, (N_EXPERTS, TL), 0)
    oh0 = row_e == e0
    oh1 = row_e == e1
    oh = jnp.where(oh0 | oh1, 1.0, 0.0)
    rank = jnp.dot(oh.astype(jnp.bfloat16), triu, preferred_element_type=jnp.float32).astype(jnp.int32)
    cnt = jnp.sum(oh, axis=1, keepdims=True).astype(jnp.int32)
    lower = jnp.where(e0 < row_e, 1.0, 0.0) + jnp.where(e1 < row_e, 1.0, 0.0)
    first = jnp.sum(lower, axis=1, keepdims=True).astype(jnp.int32)
    sorted_all = first + rank
    lpos0 = jnp.sum(jnp.where(oh0, sorted_all, 0), axis=0, keepdims=True)
    lpos1 = jnp.sum(jnp.where(oh1, sorted_all, 0), axis=0, keepdims=True)

    fill = fill_ref[:, 0:1]
    base = base_ref[:, 0:1]
    npages = np_ref[0:1, 0:1]
    total = fill + cnt
    need = (total > TM).astype(jnp.int32) + (total > 2 * TM).astype(jnp.int32)
    need_b = jnp.broadcast_to(need.astype(jnp.float32), (N_EXPERTS, 128)).astype(jnp.bfloat16)
    before = jnp.dot(tril_e, need_b, preferred_element_type=jnp.float32)[:, 0:1].astype(jnp.int32)
    new_id = npages + before
    new_base = new_id * TM
    fill_ref[...] = jnp.broadcast_to(total - need * TM, (N_EXPERTS, 128))
    base_ref[...] = jnp.broadcast_to(jnp.where(need > 0, new_base + (need - 1) * TM, base), (N_EXPERTS, 128))
    np_ref[...] = jnp.broadcast_to(npages + jnp.sum(need, axis=0, keepdims=True), (8, 128))
    page_lane = lax.broadcasted_iota(jnp.int32, (N_EXPERTS, PAGE_LANES), 1)
    expert_col = lax.broadcasted_iota(jnp.int32, (N_EXPERTS, 1), 0)
    owns = ((page_lane == new_id) & (need >= 1)) | ((page_lane == new_id + 1) & (need == 2))
    pexp_ref[...] = pexp_ref[...] + jnp.sum(jnp.where(owns, expert_col, 0), axis=0, keepdims=True)

    lane = lax.broadcasted_iota(jnp.int32, (N_EXPERTS, 128), 1)
    cols = jnp.where(lane == 0, cnt, jnp.where(lane == 1, first, jnp.where(lane == 2, fill,
                     jnp.where(lane == 3, base, jnp.where(lane == 4, new_base, 0)))))
    square = jnp.concatenate([cols, jnp.zeros((128 - N_EXPERTS, 128), jnp.int32)], axis=0)
    per_expert = square.astype(jnp.float32).T.astype(jnp.int32)[0:8, :]
    return lpos0, lpos1, per_expert


def _run_pieces(cnt, first, fill, base, new_base):
    n0 = jnp.minimum(cnt, TM - fill)
    n1 = jnp.minimum(cnt - n0, TM)
    n2 = cnt - n0 - n1
    return ((first, base + fill, n0), (first + n0, new_base, n1), (first + n0 + n1, new_base + TM, n2))


def _start_run_gather(runs_ref, tile, pages_ref, gbuf, gsem):
    slot = tile % 2

    def per_expert(e, c):
        k0 = (tile * N_EXPERTS + e) * RUN_FIELDS
        for j in range(3):
            src = runs_ref[k0 + 3 * j]
            dst = runs_ref[k0 + 3 * j + 1]
            n = runs_ref[k0 + 3 * j + 2]

            @pl.when(n > 0)
            def _():
                pltpu.make_async_copy(pages_ref.at[pl.ds(dst, n)], gbuf.at[slot, pl.ds(src, n)],
                                      gsem.at[slot]).start()
        return c
    lax.fori_loop(0, N_EXPERTS, per_expert, 0)


def _wait_run_gather(tile, gbuf, gsem):
    pltpu.make_async_copy(gbuf.at[0], gbuf.at[0], gsem.at[tile % 2]).wait()


def _unsort_results(lpos_ref, tile, gbuf, y0buf, y1buf):
    k0 = tile * 2 * TL
    slot = tile % 2
    for r in range(TL):
        y0buf[slot, r] = gbuf[slot, lpos_ref[k0 + r]]
        y1buf[slot, r] = gbuf[slot, lpos_ref[k0 + TL + r]]


def _first_results(lpos_ref, runs_ref, pages_ref, gbuf, y0buf, y1buf, gsem):
    _start_run_gather(runs_ref, 0, pages_ref, gbuf, gsem)
    _wait_run_gather(0, gbuf, gsem)
    _unsort_results(lpos_ref, 0, gbuf, y0buf, y1buf)
    _start_run_gather(runs_ref, 1, pages_ref, gbuf, gsem)
    _start_run_gather(runs_ref, 2, pages_ref, gbuf, gsem)


def _mixer_kernel(first_layer, *refs):
    if first_layer:
        xp_ref, xs_ref = refs[0:2]
        rest = refs[2:]
    else:
        lposp_ref, runsp_ref, xm_ref, yp_ref, gc_ref = refs[0:5]
        rest = refs[5:]
    (sconv_ref, spool_ref, g1_ref, win_ref, convw_ref, poolbd_ref, pscale_ref, wm_ref, bias_ref,
     wout_ref, g2_ref, wr_ref, br_ref, triu_ref, trile_ref, winlo_ref, woutlo_ref, poollo_ref, wmlo_ref,
     xmid_ref, hs_ref, lpos_ref, runs_ref, meta_ref, gcol_ref, cpr_ref, ppr_ref, csm_ref, psm_ref, sv_ref,
     zh_ref, ph_ref, h2buf, sbuf, posv, pexv, poss, pexs, fill_ref, base_ref, np_ref,
     pexp_ref, fill_s, base_s, np_s, pexp_s, cnt_s, rsem, msem) = rest[:49]
    if not first_layer:
        gbuf, y0buf, y1buf, gsem = rest[49:]

    i = pl.program_id(0)

    def load_x(prompt):
        if first_layer:
            return xp_ref[...] if prompt else xs_ref[...]
        g = gc_ref[...]
        return (xm_ref[...] + g[:, 0:1] * _unpack_rows(y0buf[i % 2])
                + g[:, 1:2] * _unpack_rows(y1buf[i % 2]))

    row_m = lax.broadcasted_iota(jnp.int32, (SGU_HEADS * SGU_LEN, SGU_LEN), 0) % SGU_LEN
    col_m = lax.broadcasted_iota(jnp.int32, (SGU_HEADS * SGU_LEN, SGU_LEN), 1)

    def wait_rows(slot):
        pltpu.make_async_copy(sbuf.at[0], sbuf.at[0], rsem.at[slot]).wait()

    def wait_positions(slot):
        pltpu.make_async_copy(posv, poss.at[slot], msem).wait()
        pltpu.make_async_copy(pexv, pexs.at[slot], msem).wait()

    def sort_rows(slot):
        for r in range(TL):
            row = h2buf[r]
            sbuf[slot, poss[slot, 0, r]] = row
            sbuf[slot, poss[slot, 1, r]] = row

    def start_runs(tile, slot):
        def per_expert(e, c):
            pieces = _run_pieces(pexs[slot, 0, e], pexs[slot, 1, e], pexs[slot, 2, e], pexs[slot, 3, e],
                                 pexs[slot, 4, e])
            k0 = (tile * N_EXPERTS + e) * RUN_FIELDS
            for j, (src, dst, n) in enumerate(pieces):
                runs_ref[k0 + 3 * j] = src
                runs_ref[k0 + 3 * j + 1] = dst
                runs_ref[k0 + 3 * j + 2] = n

                @pl.when(n > 0)
                def _():
                    pltpu.make_async_copy(sbuf.at[slot, pl.ds(src, n)], hs_ref.at[pl.ds(dst, n)],
                                          rsem.at[slot]).start()
            return c
        lax.fori_loop(0, N_EXPERTS, per_expert, 0)

        @pl.when(tile >= 1)
        def _():
            wait_rows(1 - slot)

    def finish(x, mix, precise=False):
        if precise:
            x_mid = x + _dot_split(mix, wout_ref[...], woutlo_ref[...])
        else:
            x_mid = x + jnp.dot(mix.astype(jnp.bfloat16), wout_ref[...], preferred_element_type=jnp.float32)
        xmid_ref[...] = x_mid
        h2 = _rms(x_mid, g2_ref[...])
        h2buf[...] = _pack_rows(h2)
        e0, e1, g0, g1 = _route(h2, wr_ref[...], br_ref[...], TL)
        row128 = lax.broadcasted_iota(jnp.int32, (128, TL), 0)
        gcol_ref[...] = jnp.where(row128 == 0, g0, jnp.where(row128 == 1, g1, 0.0)).T
        lpos0, lpos1, per_expert = _place_rows(e0, e1, triu_ref[...], trile_ref[...], fill_ref, base_ref,
                                               np_ref, pexp_ref)
        row8 = lax.broadcasted_iota(jnp.int32, (8, TL), 0)
        posv[...] = jnp.where(row8 == 0, lpos0, jnp.where(row8 == 1, lpos1, 0))
        pexv[...] = per_expert
        row2 = lax.broadcasted_iota(jnp.int32, (2, TL), 0)
        lpos_ref[0] = jnp.where(row2 == 0, lpos0, lpos1)
        pltpu.make_async_copy(posv, poss.at[i % 2], msem).start()
        pltpu.make_async_copy(pexv, pexs.at[i % 2], msem).start()

    def project(x, precise=False):
        h = _rms(x, g1_ref[...])
        if precise:
            return _dot_split(h, win_ref[...], winlo_ref[...])
        return jnp.dot(h.astype(jnp.bfloat16), win_ref[...], preferred_element_type=jnp.float32)

    def masked_wm(ref=wm_ref):
        return jnp.where(col_m <= row_m, ref[...], jnp.zeros_like(ref[...]))

    @pl.when(i == 0)
    def _init():
        zh_ref[...] = jnp.zeros_like(zh_ref)
        ph_ref[...] = jnp.zeros_like(ph_ref)
        fill_ref[...] = jnp.full_like(fill_ref, TM)
        base_ref[...] = jnp.zeros_like(base_ref)
        np_ref[...] = jnp.zeros_like(np_ref)
        pexp_ref[...] = jnp.zeros_like(pexp_ref)
        h2buf[...] = jnp.zeros((TL, 2, D_MODEL // 2), ROW_DTYPE)
        row8 = lax.broadcasted_iota(jnp.int32, (8, TL), 0)
        posv[...] = jnp.minimum(row8, 1) * TL + lax.broadcasted_iota(jnp.int32, (8, TL), 1)
        pexv[...] = jnp.zeros((8, 128), jnp.int32)
        pltpu.make_async_copy(posv, poss.at[1], msem).start()
        pltpu.make_async_copy(pexv, pexs.at[1], msem).start()
        if not first_layer:
            _first_results(lposp_ref, runsp_ref, yp_ref, gbuf, y0buf, y1buf, gsem)

    def _close_pages():
        copies = [pltpu.make_async_copy(fill_ref, fill_s, msem), pltpu.make_async_copy(base_ref, base_s, msem),
                  pltpu.make_async_copy(np_ref, np_s, msem), pltpu.make_async_copy(pexp_ref, pexp_s, msem)]
        for c in copies:
            c.start()
        for c in copies:
            c.wait()
        npages = np_s[0, 0]

        def zero_cnt(e, c):
            cnt_s[e] = 0
            return c
        lax.fori_loop(0, N_EXPERTS, zero_cnt, 0)

        def count(p, c):
            e = pexp_s[0, p]
            cnt_s[e] = cnt_s[e] + 1
            return c
        lax.fori_loop(0, npages, count, 0)

        def prefix(e, run):
            n = cnt_s[e]
            cnt_s[e] = run
            return run + n
        lax.fori_loop(0, N_EXPERTS, prefix, 0)

        def emit(p, c):
            e = pexp_s[0, p]
            q = cnt_s[e]
            cnt_s[e] = q + 1
            meta_ref[0, q] = p
            meta_ref[1, q] = e
            return c
        lax.fori_loop(0, npages, emit, 0)

        def pad_meta(q, c):
            meta_ref[0, q] = jnp.minimum(q, N_PAGES - 1)
            meta_ref[1, q] = meta_ref[1, npages - 1]
            return c
        lax.fori_loop(npages, PAGE_LANES, pad_meta, 0)

        def fill_row2(q, c):
            meta_ref[2, q] = npages
            return c
        lax.fori_loop(0, PAGE_LANES, fill_row2, 0)

        def mark_first(q, seen):
            is_first = (q < npages) & ((q == 0) | (meta_ref[1, q] != meta_ref[1, jnp.maximum(q - 1, 0)]))
            seen = seen + jnp.where(is_first, 1, 0)
            meta_ref[3, q] = jnp.where(is_first, 1, 0)
            meta_ref[4, q] = jnp.maximum(seen - 1, 0)
            meta_ref[5, q] = -1
            return seen
        lax.fori_loop(0, PAGE_LANES, mark_first, 0)

        def mark_next(t, carry):
            cur, nxt = carry
            q = npages - 1 - t
            e = meta_ref[1, q]
            nxt = jnp.where(e != cur, cur, nxt)
            meta_ref[5, q] = nxt
            return e, nxt
        lax.fori_loop(0, npages, mark_next, (jnp.int32(-1), jnp.int32(-1)))

        h2buf[...] = jnp.zeros((TL, 2, D_MODEL // 2), ROW_DTYPE)

        def tails(start):
            def tail(e, c):
                f = fill_s[e, 0]
                b = base_s[e, 0]

                @pl.when(f < TM)
                def _zero_tail():
                    cp = pltpu.make_async_copy(h2buf.at[pl.ds(0, TM - f)], hs_ref.at[pl.ds(b + f, TM - f)],
                                               rsem.at[0])
                    cp.start() if start else cp.wait()
                return c
            lax.fori_loop(0, N_EXPERTS, tail, 0)

            def unused(p, c):
                cp = pltpu.make_async_copy(h2buf.at[pl.ds(0, TM)], hs_ref.at[pl.ds(p * TM, TM)], rsem.at[0])
                cp.start() if start else cp.wait()
                return c
            lax.fori_loop(npages, N_PAGES, unused, 0)

        tails(True)
        tails(False)

    prev_slot = (i + 1) % 2

    @pl.when(i < N_PROMPT_TILES)
    def _prompt():
        if not first_layer:
            _wait_run_gather(i + 1, gbuf, gsem)
        s = i % TILES_PER_SEQ
        x = load_x(True)
        proj = project(x)
        zhist = jnp.where(s == 0, 0.0, zh_ref[...])
        phist = jnp.where(s == 0, 0.0, ph_ref[...])
        mix, znew, pnew = _mix_rows(proj, zhist, phist, s * TL, TL, convw_ref[...], poolbd_ref[...],
                                    pscale_ref[...], masked_wm(), bias_ref[...])
        zh_ref[...] = znew
        ph_ref[...] = pnew
        cpr_ref[0] = znew
        ppr_ref[0] = pnew
        if not first_layer:
            _unsort_results(lposp_ref, i + 1, gbuf, y0buf, y1buf)
        wait_positions(prev_slot)
        sort_rows(prev_slot)
        finish(x, mix)

    @pl.when(i <= N_PROMPT_TILES - 1)
    def _prompt_runs():
        start_runs(jnp.maximum(i - 1, 0), prev_slot)
        if not first_layer:
            @pl.when(i + 3 < N_TILES)
            def _():
                _start_run_gather(runsp_ref, i + 3, yp_ref, gbuf, gsem)

    @pl.when(i == N_PROMPT_TILES)
    def _sample():
        x = load_x(False)
        proj = project(x, precise=True)
        wm = masked_wm()
        wm_lo = masked_wm(wmlo_ref)
        mixes = []
        for b in range(DEC_BATCH):
            rows = slice(b * DEC_SEQ, (b + 1) * DEC_SEQ)
            mix, znew, pnew = _mix_rows(proj[rows, :], sconv_ref[b], spool_ref[b], PAST_LEN, DEC_SEQ,
                                        convw_ref[...], poolbd_ref[...], pscale_ref[...], wm, bias_ref[...],
                                        pool_lo=poollo_ref[...], wm_lo=wm_lo)
            csm_ref[b] = znew
            psm_ref[b] = pnew
            mixes.append(mix)
        sv_ref[...] = proj[:, 1792:2176]
        wait_positions(prev_slot)
        sort_rows(prev_slot)
        finish(x, jnp.concatenate(mixes, axis=0), precise=True)
        start_runs(i - 1, prev_slot)
        wait_positions(i % 2)
        sort_rows(i % 2)
        start_runs(i, i % 2)
        wait_rows(i % 2)
        _close_pages()


def _mixer_call(first_layer, xs, sconv_pad, spool_pad, g1, w_in, conv_w, pool_bd, pool_scale, wm_all,
                bias_full, w_out, g2, wr_t, br_col, triu, tril_e, lows):
    tile = lambda i, *_: (i, 0)
    prompt_tile = lambda i, *_: (jnp.minimum(i, N_PROMPT_TILES - 1), 0)
    const2 = lambda i, *_: (0, 0)
    const3 = lambda i, *_: (0, 0, 0)
    if first_layer:
        prefetch = ()
        x_specs = [pl.BlockSpec((TL, D_MODEL), prompt_tile), pl.BlockSpec((TL, D_MODEL), const2)]
    else:
        prefetch = xs[0:2]
        xs = xs[2:]
        x_specs = [pl.BlockSpec((TL, D_MODEL), tile), pl.BlockSpec(memory_space=pl.ANY),
                   pl.BlockSpec((TL, 128), tile)]
    full = lambda a: pl.BlockSpec(a.shape, const2 if a.ndim == 2 else const3, pipeline_mode=pl.Buffered(1))
    weights = [sconv_pad, spool_pad, g1, w_in, conv_w, pool_bd, pool_scale, wm_all, bias_full, w_out, g2,
               wr_t, br_col, triu, tril_e, *lows]
    in_specs = x_specs + [full(a) for a in weights]
    seq_of = lambda i, *_: (jnp.minimum(i // TILES_PER_SEQ, BATCH - 1), 0, 0)
    out_shape = [
        jax.ShapeDtypeStruct((T_ALL, D_MODEL), jnp.float32),
        jax.ShapeDtypeStruct((PAGE_ROWS, 2, D_MODEL // 2), ROW_DTYPE),
        jax.ShapeDtypeStruct((N_TILES, 2, TL), jnp.int32),
        jax.ShapeDtypeStruct((N_TILES * RUNS_PER_TILE,), jnp.int32),
        jax.ShapeDtypeStruct((6, PAGE_LANES), jnp.int32),
        jax.ShapeDtypeStruct((T_ALL, 128), jnp.float32),
        jax.ShapeDtypeStruct((BATCH, HIST_ROWS, D_CONV), jnp.float32),
        jax.ShapeDtypeStruct((BATCH, HIST_ROWS, D_POOL), jnp.float32),
        jax.ShapeDtypeStruct((DEC_BATCH, HIST_ROWS, D_CONV), jnp.float32),
        jax.ShapeDtypeStruct((DEC_BATCH, HIST_ROWS, D_POOL), jnp.float32),
        jax.ShapeDtypeStruct((T_SAMPLE, D_SGU), jnp.float32),
    ]
    out_specs = [
        pl.BlockSpec((TL, D_MODEL), tile),
        pl.BlockSpec(memory_space=pl.ANY),
        pl.BlockSpec((1, 2, TL), lambda i, *_: (i, 0, 0)),
        pl.BlockSpec(memory_space=pltpu.SMEM),
        pl.BlockSpec(memory_space=pltpu.SMEM),
        pl.BlockSpec((TL, 128), tile),
        pl.BlockSpec((1, HIST_ROWS, D_CONV), seq_of),
        pl.BlockSpec((1, HIST_ROWS, D_POOL), seq_of),
        pl.BlockSpec((DEC_BATCH, HIST_ROWS, D_CONV), const3),
        pl.BlockSpec((DEC_BATCH, HIST_ROWS, D_POOL), const3),
        pl.BlockSpec((T_SAMPLE, D_SGU), const2),
    ]
    scratch = [
        pltpu.VMEM((HIST_ROWS, D_CONV), jnp.float32),
        pltpu.VMEM((HIST_ROWS, D_POOL), jnp.float32),
        pltpu.VMEM((TL, 2, D_MODEL // 2), ROW_DTYPE),
        pltpu.VMEM((2, 2 * TL, 2, D_MODEL // 2), ROW_DTYPE),
        pltpu.VMEM((8, TL), jnp.int32),
        pltpu.VMEM((8, 128), jnp.int32),
        pltpu.SMEM((2, 8, TL), jnp.int32),
        pltpu.SMEM((2, 8, 128), jnp.int32),
        pltpu.VMEM((N_EXPERTS, 128), jnp.int32),
        pltpu.VMEM((N_EXPERTS, 128), jnp.int32),
        pltpu.VMEM((8, 128), jnp.int32),
        pltpu.VMEM((8, PAGE_LANES), jnp.int32),
        pltpu.SMEM((N_EXPERTS, 128), jnp.int32),
        pltpu.SMEM((N_EXPERTS, 128), jnp.int32),
        pltpu.SMEM((8, 128), jnp.int32),
        pltpu.SMEM((8, PAGE_LANES), jnp.int32),
        pltpu.SMEM((N_EXPERTS,), jnp.int32),
        pltpu.SemaphoreType.DMA((2,)),
        pltpu.SemaphoreType.DMA(()),
    ]
    if not first_layer:
        scratch += [
            pltpu.VMEM((2, 2 * TL, 2, D_MODEL // 2), ROW_DTYPE),
            pltpu.VMEM((2, TL, 2, D_MODEL // 2), ROW_DTYPE),
            pltpu.VMEM((2, TL, 2, D_MODEL // 2), ROW_DTYPE),
            pltpu.SemaphoreType.DMA((2,)),
        ]
    grid_spec = pltpu.PrefetchScalarGridSpec(num_scalar_prefetch=len(prefetch), grid=(N_TILES,),
                                             in_specs=in_specs, out_specs=out_specs, scratch_shapes=scratch)
    return pl.pallas_call(
        functools.partial(_mixer_kernel, first_layer),
        grid_spec=grid_spec,
        out_shape=out_shape,
        compiler_params=pltpu.CompilerParams(dimension_semantics=("arbitrary",),
                                             vmem_limit_bytes=VMEM_LIMIT),
        name="mixer_first" if first_layer else "mixer_next",
    )(*prefetch, *xs, *weights)


def _expert_kernel(layer, meta_ref, hs_ref, wg_hbm, wu_hbm, wd_hbm, y_ref, xbuf, wg32, wu32, wd32, wg16, wu16,
                   wd16, wsem):
    s = pl.program_id(0)
    npages = meta_ref[2, 0]

    def weight_copies(expert, slot):
        return [pltpu.make_async_copy(wg_hbm.at[layer, expert], wg32.at[slot], wsem.at[slot]),
                pltpu.make_async_copy(wu_hbm.at[layer, expert], wu32.at[slot], wsem.at[slot]),
                pltpu.make_async_copy(wd_hbm.at[layer, expert], wd32.at[slot], wsem.at[slot])]

    @pl.when(s == 0)
    def _first_weights():
        for c in weight_copies(meta_ref[1, 0], 0):
            c.start()

    @pl.when((s < npages) & (meta_ref[3, s] == 1))
    def _switch_expert():
        slot = meta_ref[4, s] % 2
        for c in weight_copies(meta_ref[1, s], slot):
            c.wait()
        nxt = meta_ref[5, s]

        @pl.when(nxt >= 0)
        def _():
            for c in weight_copies(nxt, 1 - slot):
                c.start()
        wg16[...] = wg32[slot].astype(jnp.bfloat16)
        wu16[...] = wu32[slot].astype(jnp.bfloat16)
        wd16[...] = wd32[slot].astype(jnp.bfloat16)

    @pl.when(s < npages)
    def _page():
        xbuf[...] = hs_ref[...].reshape(TM, D_MODEL)
        x = xbuf[...]
        hg = jnp.dot(x, wg16[...], preferred_element_type=jnp.float32)
        hu = jnp.dot(x, wu16[...], preferred_element_type=jnp.float32)
        h = (hg * jax.nn.sigmoid(hg) * hu).astype(jnp.bfloat16)
        out = jnp.dot(h, wd16[...], preferred_element_type=jnp.float32)
        y_ref[...] = _pack_rows(out)

    @pl.when(s >= npages)
    def _unused_page():
        y_ref[...] = jnp.zeros((TM, 2, D_MODEL // 2), ROW_DTYPE)


def _expert_call(layer, meta, hs, w_gate, w_up, w_down):
    page = lambda s, meta: (meta[0, s], 0, 0)
    grid_spec = pltpu.PrefetchScalarGridSpec(
        num_scalar_prefetch=1,
        grid=(N_PAGES,),
        in_specs=[
            pl.BlockSpec((TM, 2, D_MODEL // 2), page),
            pl.BlockSpec(memory_space=pl.ANY),
            pl.BlockSpec(memory_space=pl.ANY),
            pl.BlockSpec(memory_space=pl.ANY),
        ],
        out_specs=pl.BlockSpec((TM, 2, D_MODEL // 2), page),
        scratch_shapes=[
            pltpu.VMEM((TM, D_MODEL), ROW_DTYPE),
            pltpu.VMEM((2, D_MODEL, D_EXPERT), jnp.float32),
            pltpu.VMEM((2, D_MODEL, D_EXPERT), jnp.float32),
            pltpu.VMEM((2, D_EXPERT, D_MODEL), jnp.float32),
            pltpu.VMEM((D_MODEL, D_EXPERT), jnp.bfloat16),
            pltpu.VMEM((D_MODEL, D_EXPERT), jnp.bfloat16),
            pltpu.VMEM((D_EXPERT, D_MODEL), jnp.bfloat16),
            pltpu.SemaphoreType.DMA((2,)),
        ],
    )
    return pl.pallas_call(
        functools.partial(_expert_kernel, layer),
        grid_spec=grid_spec,
        out_shape=jax.ShapeDtypeStruct((PAGE_ROWS, 2, D_MODEL // 2), ROW_DTYPE),
        compiler_params=pltpu.CompilerParams(dimension_semantics=("arbitrary",),
                                             vmem_limit_bytes=VMEM_LIMIT),
        name="experts",
    )(meta, hs, w_gate, w_up, w_down)


def _final_kernel(lposp_ref, runsp_ref, xm_ref, yp_ref, gc_ref, g_ref, yp_out, ys_out, gbuf, y0buf, y1buf, gsem):
    i = pl.program_id(0)

    @pl.when(i == 0)
    def _():
        _first_results(lposp_ref, runsp_ref, yp_ref, gbuf, y0buf, y1buf, gsem)

    def normed():
        g = gc_ref[...]
        x = xm_ref[...] + g[:, 0:1] * _unpack_rows(y0buf[i % 2]) + g[:, 1:2] * _unpack_rows(y1buf[i % 2])
        return _rms(x, g_ref[...])

    @pl.when(i < N_PROMPT_TILES)
    def _():
        _wait_run_gather(i + 1, gbuf, gsem)
        yp_out[...] = normed()
        _unsort_results(lposp_ref, i + 1, gbuf, y0buf, y1buf)

    @pl.when(i + 3 < N_TILES)
    def _():
        _start_run_gather(runsp_ref, i + 3, yp_ref, gbuf, gsem)

    @pl.when(i == N_PROMPT_TILES)
    def _():
        ys_out[...] = normed()


def _final_call(lpos, runs, x_mid, y_pages, gcol, g):
    tile = lambda i, *_: (i, 0)
    grid_spec = pltpu.PrefetchScalarGridSpec(
        num_scalar_prefetch=2,
        grid=(N_TILES,),
        in_specs=[pl.BlockSpec((TL, D_MODEL), tile), pl.BlockSpec(memory_space=pl.ANY),
                  pl.BlockSpec((TL, 128), tile), pl.BlockSpec((1, D_MODEL), lambda i, *_: (0, 0))],
        out_specs=[pl.BlockSpec((TL, D_MODEL), lambda i, *_: (jnp.minimum(i, N_PROMPT_TILES - 1), 0)),
                   pl.BlockSpec((TL, D_MODEL), lambda i, *_: (0, 0))],
        scratch_shapes=[pltpu.VMEM((2, 2 * TL, 2, D_MODEL // 2), ROW_DTYPE), pltpu.VMEM((2, TL, 2, D_MODEL // 2), ROW_DTYPE),
                        pltpu.VMEM((2, TL, 2, D_MODEL // 2), ROW_DTYPE), pltpu.SemaphoreType.DMA((2,))],
    )
    return pl.pallas_call(
        _final_kernel,
        grid_spec=grid_spec,
        out_shape=[jax.ShapeDtypeStruct((T_PROMPT, D_MODEL), jnp.float32),
                   jax.ShapeDtypeStruct((T_SAMPLE, D_MODEL), jnp.float32)],
        compiler_params=pltpu.CompilerParams(dimension_semantics=("arbitrary",),
                                             vmem_limit_bytes=VMEM_LIMIT),
        name="final_norm",
    )(lpos, runs, x_mid, y_pages, gcol, g)


def kernel(x_prompt, x_sample, state_conv, state_pool, norm1_g, w_in, conv_w, pool_w, pool_scale, sgu_w, sgu_b, w_out, norm2_g, router_coarse_w, router_coarse_b, router_fine_w, router_fine_b, moe_w_gate, moe_w_up, moe_w_down, final_norm_g):
    bf16 = jnp.bfloat16
    xs = (x_prompt.reshape(T_PROMPT, D_MODEL), x_sample.reshape(T_SAMPLE, D_MODEL))
    sconv_pad = jnp.pad(state_conv, ((0, 0), (0, 0), (HIST_ROWS - (CONV_WIDTH - 1), 0), (0, 0)))
    spool_pad = jnp.pad(state_pool, ((0, 0), (0, 0), (HIST_ROWS - POOL_HIST, 0), (0, 0)))
    idx = jnp.arange(TL, dtype=jnp.int32)
    triu = (idx[:, None] < idx[None, :]).astype(bf16)
    ide = jnp.arange(N_EXPERTS, dtype=jnp.int32)
    tril_e = (ide[None, :] < ide[:, None]).astype(bf16)
    conv_pr, pool_pr, conv_sm, pool_sm, sgu_v = [], [], [], [], []
    x_mid = y_pages = gcol = lpos = runs = None
    for l in range(DEPTH):
        pool_bd32 = jax.scipy.linalg.block_diag(*[pool_w[l, g] for g in range(4)])
        pool_bd = pool_bd32.astype(bf16)
        wm32 = sgu_w[l].reshape(SGU_HEADS * SGU_LEN, SGU_LEN)
        wm_all = wm32.astype(bf16)
        bias_full = jnp.repeat(sgu_b[l].T, SGU_HEAD_DIM, axis=1)
        wr = jnp.zeros((ROUTER_ROWS, D_MODEL), jnp.float32)
        wr = wr.at[0:N_GROUPS].set(router_coarse_w[l].T).at[8:].set(router_fine_w[l].T)
        wr_hi = wr.astype(bf16)
        wr_lo = (wr - wr_hi.astype(jnp.float32)).astype(bf16)
        wr_t = jnp.concatenate([wr_hi, wr_lo], axis=0)
        br_col = jnp.zeros((ROUTER_ROWS, 1), jnp.float32)
        br_col = br_col.at[0:N_GROUPS, 0].set(router_coarse_b[l]).at[8:, 0].set(router_fine_b[l])
        low = lambda w: (w - w.astype(bf16).astype(jnp.float32)).astype(bf16)
        lows = (low(w_in[l]), low(w_out[l]), low(pool_bd32), low(wm32))
        outs = _mixer_call(l == 0, xs, sconv_pad[l], spool_pad[l], norm1_g[l].reshape(1, D_MODEL),
                           w_in[l].astype(bf16), conv_w[l], pool_bd, pool_scale[l].reshape(1, D_POOL),
                           wm_all, bias_full, w_out[l].astype(bf16), norm2_g[l].reshape(1, D_MODEL),
                           wr_t, br_col, triu, tril_e, lows)
        x_mid, hs, lpos, runs, meta, gcol, cpr, ppr, csm, psm, sv = outs
        lpos = lpos.reshape(N_TILES * 2 * TL)
        conv_pr.append(cpr[:, HIST_ROWS - (CONV_WIDTH - 1):, :])
        pool_pr.append(ppr[:, HIST_ROWS - POOL_HIST:, :])
        conv_sm.append(csm[:, HIST_ROWS - (CONV_WIDTH - 1):, :])
        pool_sm.append(psm[:, HIST_ROWS - POOL_HIST:, :])
        sgu_v.append(sv.reshape(DEC_BATCH, DEC_SEQ, D_SGU))
        y_pages = _expert_call(l, meta, hs, moe_w_gate, moe_w_up, moe_w_down)
        xs = (lpos, runs, x_mid, y_pages, gcol)
    y_prompt, y_sample = _final_call(lpos, runs, x_mid, y_pages, gcol, final_norm_g.reshape(1, D_MODEL))
    return (y_prompt.reshape(BATCH, SEQ, D_MODEL), y_sample.reshape(DEC_BATCH, DEC_SEQ, D_MODEL),
            jnp.stack(conv_pr), jnp.stack(pool_pr), jnp.stack(conv_sm), jnp.stack(pool_sm),
            jnp.stack(sgu_v))
```

```python
import functools

import jax
import jax.numpy as jnp
from jax import lax
from jax.experimental import pallas as pl
from jax.experimental.pallas import tpu as pltpu

D_MODEL = 1024
BATCH = 8
SEQ = 2048
DEPTH = 2
DEC_BATCH = 8
DEC_SEQ = 64
PAST_LEN = 1024
D_CONV = 384
CONV_WIDTH = 3
D_POOL = 256
POOL_HIST = 15
D_SGU = 384
SGU_HEADS = 4
SGU_HEAD_DIM = 96
SGU_LEN = 128
D_PROJ = 2176
N_GROUPS = 4
EXPERTS_PER_GROUP = 8
N_EXPERTS = 32
D_EXPERT = 512
EPS = 1e-6

T_PROMPT = BATCH * SEQ
T_SAMPLE = DEC_BATCH * DEC_SEQ
T_ALL = T_PROMPT + T_SAMPLE
TL = 512
TILES_PER_SEQ = SEQ // TL
N_PROMPT_TILES = T_PROMPT // TL
N_TILES = N_PROMPT_TILES + T_SAMPLE // TL
HIST_ROWS = 16
ROUTER_ROWS = 8 + N_EXPERTS
TM = 512
N_ASSIGN = 2 * T_ALL
N_PAGES = N_ASSIGN // TM + N_EXPERTS
PAGE_LANES = 256
PAGE_ROWS = N_PAGES * TM
RUN_FIELDS = 9
RUNS_PER_TILE = N_EXPERTS * RUN_FIELDS
ROW_DTYPE = jnp.bfloat16
VMEM_LIMIT = 56 * 1024 * 1024

assert TM <= TL <= 2 * TM and N_PAGES <= PAGE_LANES and T_ALL == N_TILES * TL


def _pack_rows(x):
    return x.astype(ROW_DTYPE).reshape(x.shape[0], 2, D_MODEL // 2)


def _unpack_rows(rows):
    return rows.reshape(rows.shape[0], D_MODEL).astype(jnp.float32)


def _split(a):
    hi = a.astype(jnp.bfloat16)
    return hi, (a - hi.astype(jnp.float32)).astype(jnp.bfloat16)


def _dot_split(a, b_hi, b_lo):
    a_hi, a_lo = _split(a)
    dot = functools.partial(jnp.dot, preferred_element_type=jnp.float32)
    return dot(a_hi, b_hi) + dot(a_lo, b_hi) + dot(a_hi, b_lo)


def _rms(x, g):
    return x * lax.rsqrt(jnp.mean(x * x, axis=-1, keepdims=True) + EPS) * g


def _mix_rows(proj, zhist, phist, pos0, n, conv_w, pool_bd, pool_scale, wm_all, bias_full, pool_lo=None,
              wm_lo=None):
    a_b = proj[:, 0:384]
    a_c = proj[:, 384:768]
    a_h = proj[:, 768:1152]
    p_in = proj[:, 1152:1408]
    s_u = proj[:, 1408:1792]
    s_v = proj[:, 1792:2176]

    z = a_c * a_h
    zext = jnp.concatenate([zhist, z], axis=0)
    conv_y = (conv_w[0:1, :] * pltpu.roll(zext, 2, 0)[HIST_ROWS:, :]
              + conv_w[1:2, :] * pltpu.roll(zext, 1, 0)[HIST_ROWS:, :]
              + conv_w[2:3, :] * z)
    a_out = a_b * conv_y

    pext = jnp.concatenate([phist, p_in], axis=0)
    s2 = pext + pltpu.roll(pext, 1, 0)
    s4 = s2 + pltpu.roll(s2, 2, 0)
    s8 = s4 + pltpu.roll(s4, 4, 0)
    s16 = s8 + pltpu.roll(s8, 8, 0)
    lane = lax.broadcasted_iota(jnp.int32, (1, D_POOL), 1)
    wsum = jnp.where(lane < 64, s2, jnp.where(lane < 128, s4, jnp.where(lane < 192, s8, s16)))
    wsum = wsum[HIST_ROWS:, :]
    win = jnp.where(lane < 64, 2.0, jnp.where(lane < 128, 4.0, jnp.where(lane < 192, 8.0, 16.0)))
    pos = (pos0 + lax.broadcasted_iota(jnp.int32, (n, 1), 0) + 1).astype(jnp.float32)
    cnt = jnp.minimum(win, pos)
    pooled = wsum / cnt - p_in
    if pool_lo is None:
        p_out = jnp.dot(pooled.astype(jnp.bfloat16), pool_bd, preferred_element_type=jnp.float32)
    else:
        p_out = _dot_split(pooled, pool_bd, pool_lo)
    p_out = p_out * pool_scale

    lane_s = lax.broadcasted_iota(jnp.int32, (1, D_SGU), 1)
    chunk = min(n, SGU_LEN)
    def head_rows(w):
        if chunk == SGU_LEN:
            return w
        return jnp.concatenate([w[h * SGU_LEN:h * SGU_LEN + chunk, 0:chunk] for h in range(SGU_HEADS)], axis=0)
    wm = head_rows(wm_all)
    s_rows = []
    for c in range(n // chunk):
        v_c = s_v[c * chunk:(c + 1) * chunk, :]
        if wm_lo is None:
            r = jnp.dot(wm, v_c.astype(jnp.bfloat16), preferred_element_type=jnp.float32)
        else:
            v_hi, v_lo = _split(v_c)
            r = (jnp.dot(wm, v_hi, preferred_element_type=jnp.float32)
                 + jnp.dot(head_rows(wm_lo), v_hi, preferred_element_type=jnp.float32)
                 + jnp.dot(wm, v_lo, preferred_element_type=jnp.float32))
        s_c = jnp.where(lane_s < 96, r[0:chunk],
                        jnp.where(lane_s < 192, r[chunk:2 * chunk],
                                  jnp.where(lane_s < 288, r[2 * chunk:3 * chunk], r[3 * chunk:4 * chunk])))
        s_rows.append(s_c + bias_full[0:chunk, :])
    s_gate = s_rows[0] if len(s_rows) == 1 else jnp.concatenate(s_rows, axis=0)
    s_out = s_u * s_gate

    mix = jnp.concatenate([a_out, p_out, s_out], axis=-1)
    return mix, zext[n:n + HIST_ROWS, :], pext[n:n + HIST_ROWS, :]


def _route(h2, wr_t, br_col, n):
    h_hi = h2.astype(jnp.bfloat16)
    h_lo = (h2 - h_hi.astype(jnp.float32)).astype(jnp.bfloat16)
    nt = (((1,), (1,)), ((), ()))
    by_hi = lax.dot_general(wr_t, h_hi, nt, preferred_element_type=jnp.float32)
    by_lo = lax.dot_general(wr_t[0:ROUTER_ROWS, :], h_lo, nt, preferred_element_type=jnp.float32)
    logits = by_hi[0:ROUTER_ROWS, :] + by_hi[ROUTER_ROWS:, :] + by_lo + br_col
    row8 = lax.broadcasted_iota(jnp.int32, (8, n), 0)
    lc = jnp.where(row8 < N_GROUPS, logits[0:8, :], -jnp.inf)
    mc = jnp.max(lc, axis=0, keepdims=True)
    g_sel = jnp.min(jnp.where(lc == mc, row8, 8), axis=0, keepdims=True)
    p_sel = 1.0 / jnp.sum(jnp.exp(lc - mc), axis=0, keepdims=True)
    lf = logits[8 + 3 * EXPERTS_PER_GROUP:8 + 4 * EXPERTS_PER_GROUP, :]
    for g in (2, 1, 0):
        lf = jnp.where(g_sel == g, logits[8 + g * EXPERTS_PER_GROUP:8 + (g + 1) * EXPERTS_PER_GROUP, :], lf)
    m1 = jnp.max(lf, axis=0, keepdims=True)
    i1 = jnp.min(jnp.where(lf == m1, row8, 8), axis=0, keepdims=True)
    lf2 = jnp.where(row8 == i1, -jnp.inf, lf)
    m2 = jnp.max(lf2, axis=0, keepdims=True)
    i2 = jnp.min(jnp.where(lf2 == m2, row8, 8), axis=0, keepdims=True)
    t = jnp.exp(m2 - m1)
    wa = 1.0 / (1.0 + t)
    wb = t / (1.0 + t)
    e0 = g_sel * EXPERTS_PER_GROUP + i1
    e1 = g_sel * EXPERTS_PER_GROUP + i2
    return e0, e1, p_sel * wa, p_sel * wb


def _place_rows(e0, e1, triu, tril_e, fill_ref, base_ref, np_ref, pexp_ref):
    row_e = lax.broadcasted_iota(jnp.int32, (N_EXPERTS, TL), 0)
    oh0 = row_e == e0
    oh1 = row_e == e1
    oh = jnp.where(oh0 | oh1, 1.0, 0.0)
    rank = jnp.dot(oh.astype(jnp.bfloat16), triu, preferred_element_type=jnp.float32).astype(jnp.int32)
    cnt = jnp.sum(oh, axis=1, keepdims=True).astype(jnp.int32)
    lower = jnp.where(e0 < row_e, 1.0, 0.0) + jnp.where(e1 < row_e, 1.0, 0.0)
    first = jnp.sum(lower, axis=1, keepdims=True).astype(jnp.int32)
    sorted_all = first + rank
    lpos0 = jnp.sum(jnp.where(oh0, sorted_all, 0), axis=0, keepdims=True)
    lpos1 = jnp.sum(jnp.where(oh1, sorted_all, 0), axis=0, keepdims=True)

    fill = fill_ref[:, 0:1]
    base = base_ref[:, 0:1]
    npages = np_ref[0:1, 0:1]
    total = fill + cnt
    need = (total > TM).astype(jnp.int32) + (total > 2 * TM).astype(jnp.int32)
    need_b = jnp.broadcast_to(need.astype(jnp.float32), (N_EXPERTS, 128)).astype(jnp.bfloat16)
    before = jnp.dot(tril_e, need_b, preferred_element_type=jnp.float32)[:, 0:1].astype(jnp.int32)
    new_id = npages + before
    new_base = new_id * TM
    fill_ref[...] = jnp.broadcast_to(total - need * TM, (N_EXPERTS, 128))
    base_ref[...] = jnp.broadcast_to(jnp.where(need > 0, new_base + (need - 1) * TM, base), (N_EXPERTS, 128))
    np_ref[...] = jnp.broadcast_to(npages + jnp.sum(need, axis=0, keepdims=True), (8, 128))
    page_lane = lax.broadcasted_iota(jnp.int32, (N_EXPERTS, PAGE_LANES), 1)
    expert_col = lax.broadcasted_iota(jnp.int32, (N_EXPERTS, 1), 0)
    owns = ((page_lane == new_id) & (need >= 1)) | ((page_lane == new_id + 1) & (need == 2))
    pexp_ref[...] = pexp_ref[...] + jnp.sum(jnp.where(owns, expert_col, 0), axis=0, keepdims=True)

    lane = lax.broadcasted_iota(jnp.int32, (N_EXPERTS, 128), 1)
    cols = jnp.where(lane == 0, cnt, jnp.where(lane == 1, first, jnp.where(lane == 2, fill,
                     jnp.where(lane == 3, base, jnp.where(lane == 4, new_base, 0)))))
    square = jnp.concatenate([cols, jnp.zeros((128 - N_EXPERTS, 128), jnp.int32)], axis=0)
    per_expert = square.astype(jnp.float32).T.astype(jnp.int32)[0:8, :]
    return lpos0, lpos1, per_expert


def _run_pieces(cnt, first, fill, base, new_base):
    n0 = jnp.minimum(cnt, TM - fill)
    n1 = jnp.minimum(cnt - n0, TM)
    n2 = cnt - n0 - n1
    return ((first, base + fill, n0), (first + n0, new_base, n1), (first + n0 + n1, new_base + TM, n2))


def _start_run_gather(runs_ref, tile, pages_ref, gbuf, gsem):
    slot = tile % 2

    def per_expert(e, c):
        k0 = (tile * N_EXPERTS + e) * RUN_FIELDS
        for j in range(3):
            src = runs_ref[k0 + 3 * j]
            dst = runs_ref[k0 + 3 * j + 1]
            n = runs_ref[k0 + 3 * j + 2]

            @pl.when(n > 0)
            def _():
                pltpu.make_async_copy(pages_ref.at[pl.ds(dst, n)], gbuf.at[pl.ds(slot * 2 * TL + src, n)],
                                      gsem.at[slot]).start()
        return c
    lax.fori_loop(0, N_EXPERTS, per_expert, 0)


def _wait_run_gather(tile, gbuf, gsem):
    half = gbuf.at[pl.ds(0, 2 * TL)]
    pltpu.make_async_copy(half, half, gsem.at[tile % 2]).wait()


def _unsort_results(lpos_ref, tile, gbuf, y0buf, y1buf):
    k0 = tile * 2 * TL
    slot = tile % 2
    for r in range(TL):
        y0buf[slot, r] = gbuf[lpos_ref[k0 + r]]
        y1buf[slot, r] = gbuf[lpos_ref[k0 + TL + r]]


def _first_results(lpos_ref, runs_ref, pages_ref, gbuf, y0buf, y1buf, gsem):
    _start_run_gather(runs_ref, 0, pages_ref, gbuf, gsem)
    _wait_run_gather(0, gbuf, gsem)
    _unsort_results(lpos_ref, 0, gbuf, y0buf, y1buf)
    _start_run_gather(runs_ref, 1, pages_ref, gbuf, gsem)
    _start_run_gather(runs_ref, 2, pages_ref, gbuf, gsem)


def _mixer_kernel(first_layer, *refs):
    if first_layer:
        xp_ref, xs_ref = refs[0:2]
        rest = refs[2:]
    else:
        lposp_ref, runsp_ref, xm_ref, yp_ref, gc_ref = refs[0:5]
        rest = refs[5:]
    (sconv_ref, spool_ref, g1_ref, win_ref, convw_ref, poolbd_ref, pscale_ref, wm_ref, bias_ref,
     wout_ref, g2_ref, wr_ref, br_ref, triu_ref, trile_ref, winlo_ref, woutlo_ref, poollo_ref, wmlo_ref,
     xmid_ref, hs_ref, lpos_ref, runs_ref, meta_ref, gcol_ref, cpr_ref, ppr_ref, csm_ref, psm_ref, sv_ref,
     zh_ref, ph_ref, h2buf, sbuf, posv, pexv, poss, pexs, fill_ref, base_ref, np_ref,
     pexp_ref, fill_s, base_s, np_s, pexp_s, cnt_s, rsem, msem) = rest[:49]
    if not first_layer:
        gbuf, y0buf, y1buf, gsem = rest[49:]

    i = pl.program_id(0)

    def load_x(prompt):
        if first_layer:
            return xp_ref[...] if prompt else xs_ref[...]
        g = gc_ref[...]
        return (xm_ref[...] + g[:, 0:1] * _unpack_rows(y0buf[i % 2])
                + g[:, 1:2] * _unpack_rows(y1buf[i % 2]))

    row_m = lax.broadcasted_iota(jnp.int32, (SGU_HEADS * SGU_LEN, SGU_LEN), 0) % SGU_LEN
    col_m = lax.broadcasted_iota(jnp.int32, (SGU_HEADS * SGU_LEN, SGU_LEN), 1)

    def wait_rows(slot):
        half = sbuf.at[pl.ds(0, 2 * TL)]
        pltpu.make_async_copy(half, half, rsem.at[slot]).wait()

    def wait_positions(slot):
        pltpu.make_async_copy(posv, poss, msem).wait()
        pltpu.make_async_copy(pexv, pexs.at[slot], msem).wait()

    def sort_rows(slot):
        for r in range(TL):
            row = h2buf[r]
            sbuf[poss[0, r]] = row
            sbuf[poss[1, r]] = row

    def start_runs(tile, slot):
        def per_expert(e, c):
            pieces = _run_pieces(pexs[slot, 0, e], pexs[slot, 1, e], pexs[slot, 2, e], pexs[slot, 3, e],
                                 pexs[slot, 4, e])
            k0 = (tile * N_EXPERTS + e) * RUN_FIELDS
            for j, (src, dst, n) in enumerate(pieces):
                runs_ref[k0 + 3 * j] = src
                runs_ref[k0 + 3 * j + 1] = dst
                runs_ref[k0 + 3 * j + 2] = n

                @pl.when(n > 0)
                def _():
                    pltpu.make_async_copy(sbuf.at[pl.ds(slot * 2 * TL + src, n)], hs_ref.at[pl.ds(dst, n)],
                                          rsem.at[slot]).start()
            return c
        lax.fori_loop(0, N_EXPERTS, per_expert, 0)

        @pl.when(tile >= 1)
        def _():
            wait_rows(1 - slot)

    def finish(x, mix, precise=False):
        if precise:
            x_mid = x + _dot_split(mix, wout_ref[...], woutlo_ref[...])
        else:
            x_mid = x + jnp.dot(mix.astype(jnp.bfloat16), wout_ref[...], preferred_element_type=jnp.float32)
        xmid_ref[...] = x_mid
        h2 = _rms(x_mid, g2_ref[...])
        h2buf[...] = _pack_rows(h2)
        e0, e1, g0, g1 = _route(h2, wr_ref[...], br_ref[...], TL)
        row128 = lax.broadcasted_iota(jnp.int32, (128, TL), 0)
        gcol_ref[...] = jnp.where(row128 == 0, g0, jnp.where(row128 == 1, g1, 0.0)).T
        lpos0, lpos1, per_expert = _place_rows(e0, e1, triu_ref[...], trile_ref[...], fill_ref, base_ref,
                                               np_ref, pexp_ref)
        row8 = lax.broadcasted_iota(jnp.int32, (8, TL), 0)
        half = (i % 2) * (2 * TL)
        posv[...] = jnp.where(row8 == 0, lpos0, jnp.where(row8 == 1, lpos1, 0)) + half
        pexv[...] = per_expert
        row2 = lax.broadcasted_iota(jnp.int32, (2, TL), 0)
        lpos_ref[0] = jnp.where(row2 == 0, lpos0, lpos1) + half
        pltpu.make_async_copy(posv, poss, msem).start()
        pltpu.make_async_copy(pexv, pexs.at[i % 2], msem).start()

    def project(x, precise=False):
        h = _rms(x, g1_ref[...])
        if precise:
            return _dot_split(h, win_ref[...], winlo_ref[...])
        return jnp.dot(h.astype(jnp.bfloat16), win_ref[...], preferred_element_type=jnp.float32)

    def masked_wm(ref=wm_ref):
        return jnp.where(col_m <= row_m, ref[...], jnp.zeros_like(ref[...]))

    @pl.when(i == 0)
    def _init():
        zh_ref[...] = jnp.zeros_like(zh_ref)
        ph_ref[...] = jnp.zeros_like(ph_ref)
        fill_ref[...] = jnp.full_like(fill_ref, TM)
        base_ref[...] = jnp.zeros_like(base_ref)
        np_ref[...] = jnp.zeros_like(np_ref)
        pexp_ref[...] = jnp.zeros_like(pexp_ref)
        h2buf[...] = jnp.zeros((TL, 2, D_MODEL // 2), ROW_DTYPE)
        row8 = lax.broadcasted_iota(jnp.int32, (8, TL), 0)
        posv[...] = 2 * TL + jnp.minimum(row8, 1) * TL + lax.broadcasted_iota(jnp.int32, (8, TL), 1)
        pexv[...] = jnp.zeros((8, 128), jnp.int32)
        pltpu.make_async_copy(posv, poss, msem).start()
        pltpu.make_async_copy(pexv, pexs.at[1], msem).start()
        if not first_layer:
            _first_results(lposp_ref, runsp_ref, yp_ref, gbuf, y0buf, y1buf, gsem)

    def _close_pages():
        copies = [pltpu.make_async_copy(fill_ref, fill_s, msem), pltpu.make_async_copy(base_ref, base_s, msem),
                  pltpu.make_async_copy(np_ref, np_s, msem), pltpu.make_async_copy(pexp_ref, pexp_s, msem)]
        for c in copies:
            c.start()
        for c in copies:
            c.wait()
        npages = np_s[0, 0]

        def zero_cnt(e, c):
            cnt_s[e] = 0
            return c
        lax.fori_loop(0, N_EXPERTS, zero_cnt, 0)

        def count(p, c):
            e = pexp_s[0, p]
            cnt_s[e] = cnt_s[e] + 1
            return c
        lax.fori_loop(0, npages, count, 0)

        def prefix(e, run):
            n = cnt_s[e]
            cnt_s[e] = run
            return run + n
        lax.fori_loop(0, N_EXPERTS, prefix, 0)

        def emit(p, c):
            e = pexp_s[0, p]
            q = cnt_s[e]
            cnt_s[e] = q + 1
            meta_ref[0, q] = p
            meta_ref[1, q] = e
            return c
        lax.fori_loop(0, npages, emit, 0)

        def pad_meta(q, c):
            meta_ref[0, q] = jnp.minimum(q, N_PAGES - 1)
            meta_ref[1, q] = meta_ref[1, npages - 1]
            return c
        lax.fori_loop(npages, PAGE_LANES, pad_meta, 0)

        def fill_row2(q, c):
            meta_ref[2, q] = npages
            return c
        lax.fori_loop(0, PAGE_LANES, fill_row2, 0)

        def mark_first(q, seen):
            is_first = (q < npages) & ((q == 0) | (meta_ref[1, q] != meta_ref[1, jnp.maximum(q - 1, 0)]))
            seen = seen + jnp.where(is_first, 1, 0)
            meta_ref[3, q] = jnp.where(is_first, 1, 0)
            meta_ref[4, q] = jnp.maximum(seen - 1, 0)
            meta_ref[5, q] = -1
            return seen
        lax.fori_loop(0, PAGE_LANES, mark_first, 0)

        def mark_next(t, carry):
            cur, nxt = carry
            q = npages - 1 - t
            e = meta_ref[1, q]
            nxt = jnp.where(e != cur, cur, nxt)
            meta_ref[5, q] = nxt
            return e, nxt
        lax.fori_loop(0, npages, mark_next, (jnp.int32(-1), jnp.int32(-1)))

        h2buf[...] = jnp.zeros((TL, 2, D_MODEL // 2), ROW_DTYPE)

        def tails(start):
            def tail(e, c):
                f = fill_s[e, 0]
                b = base_s[e, 0]

                @pl.when(f < TM)
                def _zero_tail():
                    cp = pltpu.make_async_copy(h2buf.at[pl.ds(0, TM - f)], hs_ref.at[pl.ds(b + f, TM - f)],
                                               rsem.at[0])
                    cp.start() if start else cp.wait()
                return c
            lax.fori_loop(0, N_EXPERTS, tail, 0)

            def unused(p, c):
                cp = pltpu.make_async_copy(h2buf.at[pl.ds(0, TM)], hs_ref.at[pl.ds(p * TM, TM)], rsem.at[0])
                cp.start() if start else cp.wait()
                return c
            lax.fori_loop(npages, N_PAGES, unused, 0)

        tails(True)
        tails(False)

    prev_slot = (i + 1) % 2

    @pl.when(i < N_PROMPT_TILES)
    def _prompt():
        if not first_layer:
            _wait_run_gather(i + 1, gbuf, gsem)
        s = i % TILES_PER_SEQ
        x = load_x(True)
        proj = project(x)
        zhist = jnp.where(s == 0, 0.0, zh_ref[...])
        phist = jnp.where(s == 0, 0.0, ph_ref[...])
        mix, znew, pnew = _mix_rows(proj, zhist, phist, s * TL, TL, convw_ref[...], poolbd_ref[...],
                                    pscale_ref[...], masked_wm(), bias_ref[...])
        zh_ref[...] = znew
        ph_ref[...] = pnew
        cpr_ref[0] = znew
        ppr_ref[0] = pnew
        if not first_layer:
            _unsort_results(lposp_ref, i + 1, gbuf, y0buf, y1buf)
        wait_positions(prev_slot)
        sort_rows(prev_slot)
        finish(x, mix)

    @pl.when(i <= N_PROMPT_TILES - 1)
    def _prompt_runs():
        start_runs(jnp.maximum(i - 1, 0), prev_slot)
        if not first_layer:
            @pl.when(i + 3 < N_TILES)
            def _():
                _start_run_gather(runsp_ref, i + 3, yp_ref, gbuf, gsem)

    @pl.when(i == N_PROMPT_TILES)
    def _sample():
        x = load_x(False)
        proj = project(x, precise=True)
        wm = masked_wm()
        wm_lo = masked_wm(wmlo_ref)
        mixes = []
        for b in range(DEC_BATCH):
            rows = slice(b * DEC_SEQ, (b + 1) * DEC_SEQ)
            mix, znew, pnew = _mix_rows(proj[rows, :], sconv_ref[b], spool_ref[b], PAST_LEN, DEC_SEQ,
                                        convw_ref[...], poolbd_ref[...], pscale_ref[...], wm, bias_ref[...],
                                        pool_lo=poollo_ref[...], wm_lo=wm_lo)
            csm_ref[b] = znew
            psm_ref[b] = pnew
            mixes.append(mix)
        sv_ref[...] = proj[:, 1792:2176]
        wait_positions(prev_slot)
        sort_rows(prev_slot)
        finish(x, jnp.concatenate(mixes, axis=0), precise=True)
        start_runs(i - 1, prev_slot)
        wait_positions(i % 2)
        sort_rows(i % 2)
        start_runs(i, i % 2)
        wait_rows(i % 2)
        _close_pages()


def _mixer_call(first_layer, xs, sconv_pad, spool_pad, g1, w_in, conv_w, pool_bd, pool_scale, wm_all,
                bias_full, w_out, g2, wr_t, br_col, triu, tril_e, lows):
    tile = lambda i, *_: (i, 0)
    prompt_tile = lambda i, *_: (jnp.minimum(i, N_PROMPT_TILES - 1), 0)
    const2 = lambda i, *_: (0, 0)
    const3 = lambda i, *_: (0, 0, 0)
    if first_layer:
        prefetch = ()
        x_specs = [pl.BlockSpec((TL, D_MODEL), prompt_tile), pl.BlockSpec((TL, D_MODEL), const2)]
    else:
        prefetch = xs[0:2]
        xs = xs[2:]
        x_specs = [pl.BlockSpec((TL, D_MODEL), tile), pl.BlockSpec(memory_space=pl.ANY),
                   pl.BlockSpec((TL, 128), tile)]
    full = lambda a: pl.BlockSpec(a.shape, const2 if a.ndim == 2 else const3, pipeline_mode=pl.Buffered(1))
    weights = [sconv_pad, spool_pad, g1, w_in, conv_w, pool_bd, pool_scale, wm_all, bias_full, w_out, g2,
               wr_t, br_col, triu, tril_e, *lows]
    in_specs = x_specs + [full(a) for a in weights]
    seq_of = lambda i, *_: (jnp.minimum(i // TILES_PER_SEQ, BATCH - 1), 0, 0)
    out_shape = [
        jax.ShapeDtypeStruct((T_ALL, D_MODEL), jnp.float32),
        jax.ShapeDtypeStruct((PAGE_ROWS, 2, D_MODEL // 2), ROW_DTYPE),
        jax.ShapeDtypeStruct((N_TILES, 2, TL), jnp.int32),
        jax.ShapeDtypeStruct((N_TILES * RUNS_PER_TILE,), jnp.int32),
        jax.ShapeDtypeStruct((6, PAGE_LANES), jnp.int32),
        jax.ShapeDtypeStruct((T_ALL, 128), jnp.float32),
        jax.ShapeDtypeStruct((BATCH, HIST_ROWS, D_CONV), jnp.float32),
        jax.ShapeDtypeStruct((BATCH, HIST_ROWS, D_POOL), jnp.float32),
        jax.ShapeDtypeStruct((DEC_BATCH, HIST_ROWS, D_CONV), jnp.float32),
        jax.ShapeDtypeStruct((DEC_BATCH, HIST_ROWS, D_POOL), jnp.float32),
        jax.ShapeDtypeStruct((T_SAMPLE, D_SGU), jnp.float32),
    ]
    out_specs = [
        pl.BlockSpec((TL, D_MODEL), tile),
        pl.BlockSpec(memory_space=pl.ANY),
        pl.BlockSpec((1, 2, TL), lambda i, *_: (i, 0, 0)),
        pl.BlockSpec(memory_space=pltpu.SMEM),
        pl.BlockSpec(memory_space=pltpu.SMEM),
        pl.BlockSpec((TL, 128), tile),
        pl.BlockSpec((1, HIST_ROWS, D_CONV), seq_of),
        pl.BlockSpec((1, HIST_ROWS, D_POOL), seq_of),
        pl.BlockSpec((DEC_BATCH, HIST_ROWS, D_CONV), const3),
        pl.BlockSpec((DEC_BATCH, HIST_ROWS, D_POOL), const3),
        pl.BlockSpec((T_SAMPLE, D_SGU), const2),
    ]
    scratch = [
        pltpu.VMEM((HIST_ROWS, D_CONV), jnp.float32),
        pltpu.VMEM((HIST_ROWS, D_POOL), jnp.float32),
        pltpu.VMEM((TL, 2, D_MODEL // 2), ROW_DTYPE),
        pltpu.VMEM((4 * TL, 2, D_MODEL // 2), ROW_DTYPE),
        pltpu.VMEM((8, TL), jnp.int32),
        pltpu.VMEM((8, 128), jnp.int32),
        pltpu.SMEM((8, TL), jnp.int32),
        pltpu.SMEM((2, 8, 128), jnp.int32),
        pltpu.VMEM((N_EXPERTS, 128), jnp.int32),
        pltpu.VMEM((N_EXPERTS, 128), jnp.int32),
        pltpu.VMEM((8, 128), jnp.int32),
        pltpu.VMEM((8, PAGE_LANES), jnp.int32),
        pltpu.SMEM((N_EXPERTS, 128), jnp.int32),
        pltpu.SMEM((N_EXPERTS, 128), jnp.int32),
        pltpu.SMEM((8, 128), jnp.int32),
        pltpu.SMEM((8, PAGE_LANES), jnp.int32),
        pltpu.SMEM((N_EXPERTS,), jnp.int32),
        pltpu.SemaphoreType.DMA((2,)),
        pltpu.SemaphoreType.DMA(()),
    ]
    if not first_layer:
        scratch += [
            pltpu.VMEM((4 * TL, 2, D_MODEL // 2), ROW_DTYPE),
            pltpu.VMEM((2, TL, 2, D_MODEL // 2), ROW_DTYPE),
            pltpu.VMEM((2, TL, 2, D_MODEL // 2), ROW_DTYPE),
            pltpu.SemaphoreType.DMA((2,)),
        ]
    grid_spec = pltpu.PrefetchScalarGridSpec(num_scalar_prefetch=len(prefetch), grid=(N_TILES,),
                                             in_specs=in_specs, out_specs=out_specs, scratch_shapes=scratch)
    return pl.pallas_call(
        functools.partial(_mixer_kernel, first_layer),
        grid_spec=grid_spec,
        out_shape=out_shape,
        compiler_params=pltpu.CompilerParams(dimension_semantics=("arbitrary",),
                                             vmem_limit_bytes=VMEM_LIMIT),
        name="mixer_first" if first_layer else "mixer_next",
    )(*prefetch, *xs, *weights)


def _expert_kernel(layer, meta_ref, hs_ref, wg_hbm, wu_hbm, wd_hbm, y_ref, xbuf, wg32, wu32, wd32, wg16, wu16,
                   wd16, wsem):
    s = pl.program_id(0)
    npages = meta_ref[2, 0]

    def weight_copies(expert, slot):
        return [pltpu.make_async_copy(wg_hbm.at[layer, expert], wg32.at[slot], wsem.at[slot]),
                pltpu.make_async_copy(wu_hbm.at[layer, expert], wu32.at[slot], wsem.at[slot]),
                pltpu.make_async_copy(wd_hbm.at[layer, expert], wd32.at[slot], wsem.at[slot])]

    @pl.when(s == 0)
    def _first_weights():
        for c in weight_copies(meta_ref[1, 0], 0):
            c.start()

    @pl.when((s < npages) & (meta_ref[3, s] == 1))
    def _switch_expert():
        slot = meta_ref[4, s] % 2
        for c in weight_copies(meta_ref[1, s], slot):
            c.wait()
        nxt = meta_ref[5, s]

        @pl.when(nxt >= 0)
        def _():
            for c in weight_copies(nxt, 1 - slot):
                c.start()
        wg16[...] = wg32[slot].astype(jnp.bfloat16)
        wu16[...] = wu32[slot].astype(jnp.bfloat16)
        wd16[...] = wd32[slot].astype(jnp.bfloat16)

    @pl.when(s < npages)
    def _page():
        xbuf[...] = hs_ref[...].reshape(TM, D_MODEL)
        x = xbuf[...]
        hg = jnp.dot(x, wg16[...], preferred_element_type=jnp.float32)
        hu = jnp.dot(x, wu16[...], preferred_element_type=jnp.float32)
        h = (hg * jax.nn.sigmoid(hg) * hu).astype(jnp.bfloat16)
        out = jnp.dot(h, wd16[...], preferred_element_type=jnp.float32)
        y_ref[...] = _pack_rows(out)

    @pl.when(s >= npages)
    def _unused_page():
        y_ref[...] = jnp.zeros((TM, 2, D_MODEL // 2), ROW_DTYPE)


def _expert_call(layer, meta, hs, w_gate, w_up, w_down):
    page = lambda s, meta: (meta[0, s], 0, 0)
    grid_spec = pltpu.PrefetchScalarGridSpec(
        num_scalar_prefetch=1,
        grid=(N_PAGES,),
        in_specs=[
            pl.BlockSpec((TM, 2, D_MODEL // 2), page),
            pl.BlockSpec(memory_space=pl.ANY),
            pl.BlockSpec(memory_space=pl.ANY),
            pl.BlockSpec(memory_space=pl.ANY),
        ],
        out_specs=pl.BlockSpec((TM, 2, D_MODEL // 2), page),
        scratch_shapes=[
            pltpu.VMEM((TM, D_MODEL), ROW_DTYPE),
            pltpu.VMEM((2, D_MODEL, D_EXPERT), jnp.float32),
            pltpu.VMEM((2, D_MODEL, D_EXPERT), jnp.float32),
            pltpu.VMEM((2, D_EXPERT, D_MODEL), jnp.float32),
            pltpu.VMEM((D_MODEL, D_EXPERT), jnp.bfloat16),
            pltpu.VMEM((D_MODEL, D_EXPERT), jnp.bfloat16),
            pltpu.VMEM((D_EXPERT, D_MODEL), jnp.bfloat16),
            pltpu.SemaphoreType.DMA((2,)),
        ],
    )
    return pl.pallas_call(
        functools.partial(_expert_kernel, layer),
        grid_spec=grid_spec,
        out_shape=jax.ShapeDtypeStruct((PAGE_ROWS, 2, D_MODEL // 2), ROW_DTYPE),
        compiler_params=pltpu.CompilerParams(dimension_semantics=("arbitrary",),
                                             vmem_limit_bytes=VMEM_LIMIT),
        name="experts",
    )(meta, hs, w_gate, w_up, w_down)


def _final_kernel(lposp_ref, runsp_ref, xm_ref, yp_ref, gc_ref, g_ref, yp_out, ys_out, gbuf, y0buf, y1buf, gsem):
    i = pl.program_id(0)

    @pl.when(i == 0)
    def _():
        _first_results(lposp_ref, runsp_ref, yp_ref, gbuf, y0buf, y1buf, gsem)

    def normed():
        g = gc_ref[...]
        x = xm_ref[...] + g[:, 0:1] * _unpack_rows(y0buf[i % 2]) + g[:, 1:2] * _unpack_rows(y1buf[i % 2])
        return _rms(x, g_ref[...])

    @pl.when(i < N_PROMPT_TILES)
    def _():
        _wait_run_gather(i + 1, gbuf, gsem)
        yp_out[...] = normed()
        _unsort_results(lposp_ref, i + 1, gbuf, y0buf, y1buf)

    @pl.when(i + 3 < N_TILES)
    def _():
        _start_run_gather(runsp_ref, i + 3, yp_ref, gbuf, gsem)

    @pl.when(i == N_PROMPT_TILES)
    def _():
        ys_out[...] = normed()


def _final_call(lpos, runs, x_mid, y_pages, gcol, g):
    tile = lambda i, *_: (i, 0)
    grid_spec = pltpu.PrefetchScalarGridSpec(
        num_scalar_prefetch=2,
        grid=(N_TILES,),
        in_specs=[pl.BlockSpec((TL, D_MODEL), tile), pl.BlockSpec(memory_space=pl.ANY),
                  pl.BlockSpec((TL, 128), tile), pl.BlockSpec((1, D_MODEL), lambda i, *_: (0, 0))],
        out_specs=[pl.BlockSpec((TL, D_MODEL), lambda i, *_: (jnp.minimum(i, N_PROMPT_TILES - 1), 0)),
                   pl.BlockSpec((TL, D_MODEL), lambda i, *_: (0, 0))],
        scratch_shapes=[pltpu.VMEM((4 * TL, 2, D_MODEL // 2), ROW_DTYPE), pltpu.VMEM((2, TL, 2, D_MODEL // 2), ROW_DTYPE),
                        pltpu.VMEM((2, TL, 2, D_MODEL // 2), ROW_DTYPE), pltpu.SemaphoreType.DMA((2,))],
    )
    return pl.pallas_call(
        _final_kernel,
        grid_spec=grid_spec,
        out_shape=[jax.ShapeDtypeStruct((T_PROMPT, D_MODEL), jnp.float32),
                   jax.ShapeDtypeStruct((T_SAMPLE, D_MODEL), jnp.float32)],
        compiler_params=pltpu.CompilerParams(dimension_semantics=("arbitrary",),
                                             vmem_limit_bytes=VMEM_LIMIT),
        name="final_norm",
    )(lpos, runs, x_mid, y_pages, gcol, g)


def kernel(x_prompt, x_sample, state_conv, state_pool, norm1_g, w_in, conv_w, pool_w, pool_scale, sgu_w, sgu_b, w_out, norm2_g, router_coarse_w, router_coarse_b, router_fine_w, router_fine_b, moe_w_gate, moe_w_up, moe_w_down, final_norm_g):
    bf16 = jnp.bfloat16
    xs = (x_prompt.reshape(T_PROMPT, D_MODEL), x_sample.reshape(T_SAMPLE, D_MODEL))
    sconv_pad = jnp.pad(state_conv, ((0, 0), (0, 0), (HIST_ROWS - (CONV_WIDTH - 1), 0), (0, 0)))
    spool_pad = jnp.pad(state_pool, ((0, 0), (0, 0), (HIST_ROWS - POOL_HIST, 0), (0, 0)))
    idx = jnp.arange(TL, dtype=jnp.int32)
    triu = (idx[:, None] < idx[None, :]).astype(bf16)
    ide = jnp.arange(N_EXPERTS, dtype=jnp.int32)
    tril_e = (ide[None, :] < ide[:, None]).astype(bf16)
    conv_pr, pool_pr, conv_sm, pool_sm, sgu_v = [], [], [], [], []
    x_mid = y_pages = gcol = lpos = runs = None
    for l in range(DEPTH):
        pool_bd32 = jax.scipy.linalg.block_diag(*[pool_w[l, g] for g in range(4)])
        pool_bd = pool_bd32.astype(bf16)
        wm32 = sgu_w[l].reshape(SGU_HEADS * SGU_LEN, SGU_LEN)
        wm_all = wm32.astype(bf16)
        bias_full = jnp.repeat(sgu_b[l].T, SGU_HEAD_DIM, axis=1)
        wr = jnp.zeros((ROUTER_ROWS, D_MODEL), jnp.float32)
        wr = wr.at[0:N_GROUPS].set(router_coarse_w[l].T).at[8:].set(router_fine_w[l].T)
        wr_hi = wr.astype(bf16)
        wr_lo = (wr - wr_hi.astype(jnp.float32)).astype(bf16)
        wr_t = jnp.concatenate([wr_hi, wr_lo], axis=0)
        br_col = jnp.zeros((ROUTER_ROWS, 1), jnp.float32)
        br_col = br_col.at[0:N_GROUPS, 0].set(router_coarse_b[l]).at[8:, 0].set(router_fine_b[l])
        low = lambda w: (w - w.astype(bf16).astype(jnp.float32)).astype(bf16)
        lows = (low(w_in[l]), low(w_out[l]), low(pool_bd32), low(wm32))
        outs = _mixer_call(l == 0, xs, sconv_pad[l], spool_pad[l], norm1_g[l].reshape(1, D_MODEL),
                           w_in[l].astype(bf16), conv_w[l], pool_bd, pool_scale[l].reshape(1, D_POOL),
                           wm_all, bias_full, w_out[l].astype(bf16), norm2_g[l].reshape(1, D_MODEL),
                           wr_t, br_col, triu, tril_e, lows)
        x_mid, hs, lpos, runs, meta, gcol, cpr, ppr, csm, psm, sv = outs
        lpos = lpos.reshape(N_TILES * 2 * TL)
        conv_pr.append(cpr[:, HIST_ROWS - (CONV_WIDTH - 1):, :])
        pool_pr.append(ppr[:, HIST_ROWS - POOL_HIST:, :])
        conv_sm.append(csm[:, HIST_ROWS - (CONV_WIDTH - 1):, :])
        pool_sm.append(psm[:, HIST_ROWS - POOL_HIST:, :])
        sgu_v.append(sv.reshape(DEC_BATCH, DEC_SEQ, D_SGU))
        y_pages = _expert_call(l, meta, hs, moe_w_gate, moe_w_up, moe_w_down)
        xs = (lpos, runs, x_mid, y_pages, gcol)
    y_prompt, y_sample = _final_call(lpos, runs, x_mid, y_pages, gcol, final_norm_g.reshape(1, D_MODEL))
    return (y_prompt.reshape(BATCH, SEQ, D_MODEL), y_sample.reshape(DEC_BATCH, DEC_SEQ, D_MODEL),
            jnp.stack(conv_pr), jnp.stack(pool_pr), jnp.stack(conv_sm), jnp.stack(pool_sm),
            jnp.stack(sgu_v))
```

```python
import functools

import jax
import jax.numpy as jnp
from jax import lax
from jax.experimental import pallas as pl
from jax.experimental.pallas import tpu as pltpu

D_MODEL = 1024
BATCH = 8
SEQ = 2048
DEPTH = 2
DEC_BATCH = 8
DEC_SEQ = 64
PAST_LEN = 1024
D_CONV = 384
CONV_WIDTH = 3
D_POOL = 256
POOL_HIST = 15
D_SGU = 384
SGU_HEADS = 4
SGU_HEAD_DIM = 96
SGU_LEN = 128
D_PROJ = 2176
N_GROUPS = 4
EXPERTS_PER_GROUP = 8
N_EXPERTS = 32
D_EXPERT = 512
EPS = 1e-6

T_PROMPT = BATCH * SEQ
T_SAMPLE = DEC_BATCH * DEC_SEQ
T_ALL = T_PROMPT + T_SAMPLE
TL = 512
TILES_PER_SEQ = SEQ // TL
N_PROMPT_TILES = T_PROMPT // TL
N_TILES = N_PROMPT_TILES + T_SAMPLE // TL
HIST_ROWS = 16
ROUTER_ROWS = 8 + N_EXPERTS
TM = 512
N_ASSIGN = 2 * T_ALL
N_PAGES = N_ASSIGN // TM + N_EXPERTS
PAGE_LANES = 256
PAGE_ROWS = N_PAGES * TM
RUN_FIELDS = 9
RUNS_PER_TILE = N_EXPERTS * RUN_FIELDS
ROW_DTYPE = jnp.bfloat16
VMEM_LIMIT = 56 * 1024 * 1024

assert TM <= TL <= 2 * TM and N_PAGES <= PAGE_LANES and T_ALL == N_TILES * TL


def _pack_rows(x):
    return x.astype(ROW_DTYPE).reshape(x.shape[0], 2, D_MODEL // 2)


def _unpack_rows(rows):
    return rows.reshape(rows.shape[0], D_MODEL).astype(jnp.float32)


def _split(a):
    hi = a.astype(jnp.bfloat16)
    return hi, (a - hi.astype(jnp.float32)).astype(jnp.bfloat16)


def _dot_split(a, b_hi, b_lo):
    a_hi, a_lo = _split(a)
    dot = functools.partial(jnp.dot, preferred_element_type=jnp.float32)
    return dot(a_hi, b_hi) + dot(a_lo, b_hi) + dot(a_hi, b_lo)


def _rms(x, g):
    return x * lax.rsqrt(jnp.mean(x * x, axis=-1, keepdims=True) + EPS) * g


def _mix_rows(proj, zhist, phist, pos0, n, conv_w, pool_bd, pool_scale, wm_all, bias_full, pool_lo=None,
              wm_lo=None):
    a_b = proj[:, 0:384]
    a_c = proj[:, 384:768]
    a_h = proj[:, 768:1152]
    p_in = proj[:, 1152:1408]
    s_u = proj[:, 1408:1792]
    s_v = proj[:, 1792:2176]

    z = a_c * a_h
    zext = jnp.concatenate([zhist, z], axis=0)
    conv_y = (conv_w[0:1, :] * pltpu.roll(zext, 2, 0)[HIST_ROWS:, :]
              + conv_w[1:2, :] * pltpu.roll(zext, 1, 0)[HIST_ROWS:, :]
              + conv_w[2:3, :] * z)
    a_out = a_b * conv_y

    pext = jnp.concatenate([phist, p_in], axis=0)
    s2 = pext + pltpu.roll(pext, 1, 0)
    s4 = s2 + pltpu.roll(s2, 2, 0)
    s8 = s4 + pltpu.roll(s4, 4, 0)
    s16 = s8 + pltpu.roll(s8, 8, 0)
    lane = lax.broadcasted_iota(jnp.int32, (1, D_POOL), 1)
    wsum = jnp.where(lane < 64, s2, jnp.where(lane < 128, s4, jnp.where(lane < 192, s8, s16)))
    wsum = wsum[HIST_ROWS:, :]
    win = jnp.where(lane < 64, 2.0, jnp.where(lane < 128, 4.0, jnp.where(lane < 192, 8.0, 16.0)))
    pos = (pos0 + lax.broadcasted_iota(jnp.int32, (n, 1), 0) + 1).astype(jnp.float32)
    cnt = jnp.minimum(win, pos)
    pooled = wsum / cnt - p_in
    if pool_lo is None:
        p_out = jnp.dot(pooled.astype(jnp.bfloat16), pool_bd, preferred_element_type=jnp.float32)
    else:
        p_out = _dot_split(pooled, pool_bd, pool_lo)
    p_out = p_out * pool_scale

    lane_s = lax.broadcasted_iota(jnp.int32, (1, D_SGU), 1)
    chunk = min(n, SGU_LEN)
    def head_rows(w):
        if chunk == SGU_LEN:
            return w
        return jnp.concatenate([w[h * SGU_LEN:h * SGU_LEN + chunk, 0:chunk] for h in range(SGU_HEADS)], axis=0)
    wm = head_rows(wm_all)
    s_rows = []
    for c in range(n // chunk):
        v_c = s_v[c * chunk:(c + 1) * chunk, :]
        if wm_lo is None:
            r = jnp.dot(wm, v_c.astype(jnp.bfloat16), preferred_element_type=jnp.float32)
        else:
            v_hi, v_lo = _split(v_c)
            r = (jnp.dot(wm, v_hi, preferred_element_type=jnp.float32)
                 + jnp.dot(head_rows(wm_lo), v_hi, preferred_element_type=jnp.float32)
                 + jnp.dot(wm, v_lo, preferred_element_type=jnp.float32))
        s_c = jnp.where(lane_s < 96, r[0:chunk],
                        jnp.where(lane_s < 192, r[chunk:2 * chunk],
                                  jnp.where(lane_s < 288, r[2 * chunk:3 * chunk], r[3 * chunk:4 * chunk])))
        s_rows.append(s_c + bias_full[0:chunk, :])
    s_gate = s_rows[0] if len(s_rows) == 1 else jnp.concatenate(s_rows, axis=0)
    s_out = s_u * s_gate

    mix = jnp.concatenate([a_out, p_out, s_out], axis=-1)
    return mix, zext[n:n + HIST_ROWS, :], pext[n:n + HIST_ROWS, :]


def _route(h2, wr_t, br_col, n):
    h_hi = h2.astype(jnp.bfloat16)
    h_lo = (h2 - h_hi.astype(jnp.float32)).astype(jnp.bfloat16)
    nt = (((1,), (1,)), ((), ()))
    by_hi = lax.dot_general(wr_t, h_hi, nt, preferred_element_type=jnp.float32)
    by_lo = lax.dot_general(wr_t[0:ROUTER_ROWS, :], h_lo, nt, preferred_element_type=jnp.float32)
    logits = by_hi[0:ROUTER_ROWS, :] + by_hi[ROUTER_ROWS:, :] + by_lo + br_col
    row8 = lax.broadcasted_iota(jnp.int32, (8, n), 0)
    lc = jnp.where(row8 < N_GROUPS, logits[0:8, :], -jnp.inf)
    mc = jnp.max(lc, axis=0, keepdims=True)
    g_sel = jnp.min(jnp.where(lc == mc, row8, 8), axis=0, keepdims=True)
    p_sel = 1.0 / jnp.sum(jnp.exp(lc - mc), axis=0, keepdims=True)
    lf = logits[8 + 3 * EXPERTS_PER_GROUP:8 + 4 * EXPERTS_PER_GROUP, :]
    for g in (2, 1, 0):
        lf = jnp.where(g_sel == g, logits[8 + g * EXPERTS_PER_GROUP:8 + (g + 1) * EXPERTS_PER_GROUP, :], lf)
    m1 = jnp.max(lf, axis=0, keepdims=True)
    i1 = jnp.min(jnp.where(lf == m1, row8, 8), axis=0, keepdims=True)
    lf2 = jnp.where(row8 == i1, -jnp.inf, lf)
    m2 = jnp.max(lf2, axis=0, keepdims=True)
    i2 = jnp.min(jnp.where(lf2 == m2, row8, 8), axis=0, keepdims=True)
    t = jnp.exp(m2 - m1)
    wa = 1.0 / (1.0 + t)
    wb = t / (1.0 + t)
    e0 = g_sel * EXPERTS_PER_GROUP + i1
    e1 = g_sel * EXPERTS_PER_GROUP + i2
    return e0, e1, p_sel * wa, p_sel * wb


def _place_rows(e0, e1, triu, tril_e, fill_ref, base_ref, np_ref, pexp_ref):
    row_e = lax.broadcasted_iota(jnp.int32, (N_EXPERTS, TL), 0)
    oh0 = row_e == e0
    oh1 = row_e == e1
    oh = jnp.where(oh0 | oh1, 1.0, 0.0)
    rank = jnp.dot(oh.astype(jnp.bfloat16), triu, preferred_element_type=jnp.float32).astype(jnp.int32)
    cnt = jnp.sum(oh, axis=1, keepdims=True).astype(jnp.int32)
    lower = jnp.where(e0 < row_e, 1.0, 0.0) + jnp.where(e1 < row_e, 1.0, 0.0)
    first = jnp.sum(lower, axis=1, keepdims=True).astype(jnp.int32)
    sorted_all = first + rank
    lpos0 = jnp.sum(jnp.where(oh0, sorted_all, 0), axis=0, keepdims=True)
    lpos1 = jnp.sum(jnp.where(oh1, sorted_all, 0), axis=0, keepdims=True)

    fill = fill_ref[:, 0:1]
    base = base_ref[:, 0:1]
    npages = np_ref[0:1, 0:1]
    total = fill + cnt
    need = (total > TM).astype(jnp.int32) + (total > 2 * TM).astype(jnp.int32)
    need_b = jnp.broadcast_to(need.astype(jnp.float32), (N_EXPERTS, 128)).astype(jnp.bfloat16)
    before = jnp.dot(tril_e, need_b, preferred_element_type=jnp.float32)[:, 0:1].astype(jnp.int32)
    new_id = npages + before
    new_base = new_id * TM
    fill_ref[...] = jnp.broadcast_to(total - need * TM, (N_EXPERTS, 128))
    base_ref[...] = jnp.broadcast_to(jnp.where(need > 0, new_base + (need - 1) * TM, base), (N_EXPERTS, 128))
    np_ref[...] = jnp.broadcast_to(npages + jnp.sum(need, axis=0, keepdims=True), (8, 128))
    page_lane = lax.broadcasted_iota(jnp.int32, (N_EXPERTS, PAGE_LANES), 1)
    expert_col = lax.broadcasted_iota(jnp.int32, (N_EXPERTS, 1), 0)
    owns = ((page_lane == new_id) & (need >= 1)) | ((page_lane == new_id + 1) & (need == 2))
    pexp_ref[...] = pexp_ref[...] + jnp.sum(jnp.where(owns, expert_col, 0), axis=0, keepdims=True)

    lane = lax.broadcasted_iota(jnp.int32, (N_EXPERTS, 128), 1)
    cols = jnp.where(lane == 0, cnt, jnp.where(lane == 1, first, jnp.where(lane == 2, fill,
                     jnp.where(lane == 3, base, jnp.where(lane == 4, new_base, 0)))))
    square = jnp.concatenate([cols, jnp.zeros((128 - N_EXPERTS, 128), jnp.int32)], axis=0)
    per_expert = square.astype(jnp.float32).T.astype(jnp.int32)[0:8, :]
    return lpos0, lpos1, per_expert


def _run_pieces(cnt, first, fill, base, new_base):
    n0 = jnp.minimum(cnt, TM - fill)
    n1 = jnp.minimum(cnt - n0, TM)
    n2 = cnt - n0 - n1
    return ((first, base + fill, n0), (first + n0, new_base, n1), (first + n0 + n1, new_base + TM, n2))


def _start_run_gather(runs_ref, tile, pages_ref, gbuf, gsem):
    slot = tile % 2

    tile0 = tile * RUNS_PER_TILE
    for e in range(N_EXPERTS):
        for j in range(3):
            k = tile0 + e * RUN_FIELDS + 3 * j
            src = runs_ref[k]
            dst = runs_ref[k + 1]
            n = runs_ref[k + 2]

            @pl.when(n > 0)
            def _():
                pltpu.make_async_copy(pages_ref.at[pl.ds(dst, n)], gbuf.at[pl.ds(slot * 2 * TL + src, n)],
                                      gsem.at[slot]).start()


def _wait_run_gather(tile, gbuf, gsem):
    half = gbuf.at[pl.ds(0, 2 * TL)]
    pltpu.make_async_copy(half, half, gsem.at[tile % 2]).wait()


def _unsort_results(lpos_ref, tile, gbuf, y0buf, y1buf):
    k0 = tile * 2 * TL
    slot = tile % 2
    for r in range(TL):
        y0buf[slot, r] = gbuf[lpos_ref[k0 + r]]
        y1buf[slot, r] = gbuf[lpos_ref[k0 + TL + r]]


def _first_results(lpos_ref, runs_ref, pages_ref, gbuf, y0buf, y1buf, gsem):
    _start_run_gather(runs_ref, 0, pages_ref, gbuf, gsem)
    _wait_run_gather(0, gbuf, gsem)
    _unsort_results(lpos_ref, 0, gbuf, y0buf, y1buf)
    _start_run_gather(runs_ref, 1, pages_ref, gbuf, gsem)
    _start_run_gather(runs_ref, 2, pages_ref, gbuf, gsem)


def _mixer_kernel(first_layer, *refs):
    if first_layer:
        xp_ref, xs_ref = refs[0:2]
        rest = refs[2:]
    else:
        lposp_ref, runsp_ref, xm_ref, yp_ref, gc_ref = refs[0:5]
        rest = refs[5:]
    (sconv_ref, spool_ref, g1_ref, win_ref, convw_ref, poolbd_ref, pscale_ref, wm_ref, bias_ref,
     wout_ref, g2_ref, wr_ref, br_ref, triu_ref, trile_ref, winlo_ref, woutlo_ref, poollo_ref, wmlo_ref,
     xmid_ref, hs_ref, lpos_ref, runs_ref, meta_ref, gcol_ref, cpr_ref, ppr_ref, csm_ref, psm_ref, sv_ref,
     zh_ref, ph_ref, h2buf, sbuf, posv, pexv, poss, pexs, fill_ref, base_ref, np_ref,
     pexp_ref, fill_s, base_s, np_s, pexp_s, cnt_s, rsem, msem) = rest[:49]
    if not first_layer:
        gbuf, y0buf, y1buf, gsem = rest[49:]

    i = pl.program_id(0)

    def load_x(prompt):
        if first_layer:
            return xp_ref[...] if prompt else xs_ref[...]
        g = gc_ref[...]
        return (xm_ref[...] + g[:, 0:1] * _unpack_rows(y0buf[i % 2])
                + g[:, 1:2] * _unpack_rows(y1buf[i % 2]))

    row_m = lax.broadcasted_iota(jnp.int32, (SGU_HEADS * SGU_LEN, SGU_LEN), 0) % SGU_LEN
    col_m = lax.broadcasted_iota(jnp.int32, (SGU_HEADS * SGU_LEN, SGU_LEN), 1)

    def wait_rows(slot):
        half = sbuf.at[pl.ds(0, 2 * TL)]
        pltpu.make_async_copy(half, half, rsem.at[slot]).wait()

    def wait_positions(slot):
        pltpu.make_async_copy(posv, poss, msem).wait()
        pltpu.make_async_copy(pexv, pexs.at[slot], msem).wait()

    def sort_rows(slot):
        for r in range(TL):
            row = h2buf[r]
            sbuf[poss[0, r]] = row
            sbuf[poss[1, r]] = row

    def start_runs(tile, slot):
        tile0 = tile * RUNS_PER_TILE
        for e in range(N_EXPERTS):
            pieces = _run_pieces(pexs[slot, 0, e], pexs[slot, 1, e], pexs[slot, 2, e], pexs[slot, 3, e],
                                 pexs[slot, 4, e])
            for j, (src, dst, n) in enumerate(pieces):
                k = tile0 + e * RUN_FIELDS + 3 * j
                runs_ref[k] = src
                runs_ref[k + 1] = dst
                runs_ref[k + 2] = n

                @pl.when(n > 0)
                def _():
                    pltpu.make_async_copy(sbuf.at[pl.ds(slot * 2 * TL + src, n)], hs_ref.at[pl.ds(dst, n)],
                                          rsem.at[slot]).start()

        @pl.when(tile >= 1)
        def _():
            wait_rows(1 - slot)

    def finish(x, mix, precise=False):
        if precise:
            x_mid = x + _dot_split(mix, wout_ref[...], woutlo_ref[...])
        else:
            x_mid = x + jnp.dot(mix.astype(jnp.bfloat16), wout_ref[...], preferred_element_type=jnp.float32)
        xmid_ref[...] = x_mid
        h2 = _rms(x_mid, g2_ref[...])
        h2buf[...] = _pack_rows(h2)
        e0, e1, g0, g1 = _route(h2, wr_ref[...], br_ref[...], TL)
        row128 = lax.broadcasted_iota(jnp.int32, (128, TL), 0)
        gcol_ref[...] = jnp.where(row128 == 0, g0, jnp.where(row128 == 1, g1, 0.0)).T
        lpos0, lpos1, per_expert = _place_rows(e0, e1, triu_ref[...], trile_ref[...], fill_ref, base_ref,
                                               np_ref, pexp_ref)
        row8 = lax.broadcasted_iota(jnp.int32, (8, TL), 0)
        half = (i % 2) * (2 * TL)
        posv[...] = jnp.where(row8 == 0, lpos0, jnp.where(row8 == 1, lpos1, 0)) + half
        pexv[...] = per_expert
        row2 = lax.broadcasted_iota(jnp.int32, (2, TL), 0)
        lpos_ref[0] = jnp.where(row2 == 0, lpos0, lpos1) + half
        pltpu.make_async_copy(posv, poss, msem).start()
        pltpu.make_async_copy(pexv, pexs.at[i % 2], msem).start()

    def project(x, precise=False):
        h = _rms(x, g1_ref[...])
        if precise:
            return _dot_split(h, win_ref[...], winlo_ref[...])
        return jnp.dot(h.astype(jnp.bfloat16), win_ref[...], preferred_element_type=jnp.float32)

    def masked_wm(ref=wm_ref):
        return jnp.where(col_m <= row_m, ref[...], jnp.zeros_like(ref[...]))

    @pl.when(i == 0)
    def _init():
        zh_ref[...] = jnp.zeros_like(zh_ref)
        ph_ref[...] = jnp.zeros_like(ph_ref)
        fill_ref[...] = jnp.full_like(fill_ref, TM)
        base_ref[...] = jnp.zeros_like(base_ref)
        np_ref[...] = jnp.zeros_like(np_ref)
        pexp_ref[...] = jnp.zeros_like(pexp_ref)
        h2buf[...] = jnp.zeros((TL, 2, D_MODEL // 2), ROW_DTYPE)
        row8 = lax.broadcasted_iota(jnp.int32, (8, TL), 0)
        posv[...] = 2 * TL + jnp.minimum(row8, 1) * TL + lax.broadcasted_iota(jnp.int32, (8, TL), 1)
        pexv[...] = jnp.zeros((8, 128), jnp.int32)
        pltpu.make_async_copy(posv, poss, msem).start()
        pltpu.make_async_copy(pexv, pexs.at[1], msem).start()
        if not first_layer:
            _first_results(lposp_ref, runsp_ref, yp_ref, gbuf, y0buf, y1buf, gsem)

    def _close_pages():
        copies = [pltpu.make_async_copy(fill_ref, fill_s, msem), pltpu.make_async_copy(base_ref, base_s, msem),
                  pltpu.make_async_copy(np_ref, np_s, msem), pltpu.make_async_copy(pexp_ref, pexp_s, msem)]
        for c in copies:
            c.start()
        for c in copies:
            c.wait()
        npages = np_s[0, 0]

        def zero_cnt(e, c):
            cnt_s[e] = 0
            return c
        lax.fori_loop(0, N_EXPERTS, zero_cnt, 0)

        def count(p, c):
            e = pexp_s[0, p]
            cnt_s[e] = cnt_s[e] + 1
            return c
        lax.fori_loop(0, npages, count, 0)

        def prefix(e, run):
            n = cnt_s[e]
            cnt_s[e] = run
            return run + n
        lax.fori_loop(0, N_EXPERTS, prefix, 0)

        def emit(p, c):
            e = pexp_s[0, p]
            q = cnt_s[e]
            cnt_s[e] = q + 1
            meta_ref[0, q] = p
            meta_ref[1, q] = e
            return c
        lax.fori_loop(0, npages, emit, 0)

        def pad_meta(q, c):
            meta_ref[0, q] = jnp.minimum(q, N_PAGES - 1)
            meta_ref[1, q] = meta_ref[1, npages - 1]
            return c
        lax.fori_loop(npages, PAGE_LANES, pad_meta, 0)

        def fill_row2(q, c):
            meta_ref[2, q] = npages
            return c
        lax.fori_loop(0, PAGE_LANES, fill_row2, 0)

        def mark_first(q, seen):
            is_first = (q < npages) & ((q == 0) | (meta_ref[1, q] != meta_ref[1, jnp.maximum(q - 1, 0)]))
            seen = seen + jnp.where(is_first, 1, 0)
            meta_ref[3, q] = jnp.where(is_first, 1, 0)
            meta_ref[4, q] = jnp.maximum(seen - 1, 0)
            meta_ref[5, q] = -1
            return seen
        lax.fori_loop(0, PAGE_LANES, mark_first, 0)

        def mark_next(t, carry):
            cur, nxt = carry
            q = npages - 1 - t
            e = meta_ref[1, q]
            nxt = jnp.where(e != cur, cur, nxt)
            meta_ref[5, q] = nxt
            return e, nxt
        lax.fori_loop(0, npages, mark_next, (jnp.int32(-1), jnp.int32(-1)))

        h2buf[...] = jnp.zeros((TL, 2, D_MODEL // 2), ROW_DTYPE)

        def tails(start):
            def tail(e, c):
                f = fill_s[e, 0]
                b = base_s[e, 0]

                @pl.when(f < TM)
                def _zero_tail():
                    cp = pltpu.make_async_copy(h2buf.at[pl.ds(0, TM - f)], hs_ref.at[pl.ds(b + f, TM - f)],
                                               rsem.at[0])
                    cp.start() if start else cp.wait()
                return c
            lax.fori_loop(0, N_EXPERTS, tail, 0)

            def unused(p, c):
                cp = pltpu.make_async_copy(h2buf.at[pl.ds(0, TM)], hs_ref.at[pl.ds(p * TM, TM)], rsem.at[0])
                cp.start() if start else cp.wait()
                return c
            lax.fori_loop(npages, N_PAGES, unused, 0)

        tails(True)
        tails(False)

    prev_slot = (i + 1) % 2

    @pl.when(i < N_PROMPT_TILES)
    def _prompt():
        if not first_layer:
            _wait_run_gather(i + 1, gbuf, gsem)
        s = i % TILES_PER_SEQ
        x = load_x(True)
        proj = project(x)
        zhist = jnp.where(s == 0, 0.0, zh_ref[...])
        phist = jnp.where(s == 0, 0.0, ph_ref[...])
        mix, znew, pnew = _mix_rows(proj, zhist, phist, s * TL, TL, convw_ref[...], poolbd_ref[...],
                                    pscale_ref[...], masked_wm(), bias_ref[...])
        zh_ref[...] = znew
        ph_ref[...] = pnew
        cpr_ref[0] = znew
        ppr_ref[0] = pnew
        if not first_layer:
            _unsort_results(lposp_ref, i + 1, gbuf, y0buf, y1buf)
        wait_positions(prev_slot)
        sort_rows(prev_slot)
        finish(x, mix)

    @pl.when(i <= N_PROMPT_TILES - 1)
    def _prompt_runs():
        start_runs(jnp.maximum(i - 1, 0), prev_slot)
        if not first_layer:
            @pl.when(i + 3 < N_TILES)
            def _():
                _start_run_gather(runsp_ref, i + 3, yp_ref, gbuf, gsem)

    @pl.when(i == N_PROMPT_TILES)
    def _sample():
        x = load_x(False)
        proj = project(x, precise=True)
        wm = masked_wm()
        wm_lo = masked_wm(wmlo_ref)
        mixes = []
        for b in range(DEC_BATCH):
            rows = slice(b * DEC_SEQ, (b + 1) * DEC_SEQ)
            mix, znew, pnew = _mix_rows(proj[rows, :], sconv_ref[b], spool_ref[b], PAST_LEN, DEC_SEQ,
                                        convw_ref[...], poolbd_ref[...], pscale_ref[...], wm, bias_ref[...],
                                        pool_lo=poollo_ref[...], wm_lo=wm_lo)
            csm_ref[b] = znew
            psm_ref[b] = pnew
            mixes.append(mix)
        sv_ref[...] = proj[:, 1792:2176]
        wait_positions(prev_slot)
        sort_rows(prev_slot)
        finish(x, jnp.concatenate(mixes, axis=0), precise=True)
        start_runs(i - 1, prev_slot)
        wait_positions(i % 2)
        sort_rows(i % 2)
        start_runs(i, i % 2)
        wait_rows(i % 2)
        _close_pages()


def _mixer_call(first_layer, xs, sconv_pad, spool_pad, g1, w_in, conv_w, pool_bd, pool_scale, wm_all,
                bias_full, w_out, g2, wr_t, br_col, triu, tril_e, lows):
    tile = lambda i, *_: (i, 0)
    prompt_tile = lambda i, *_: (jnp.minimum(i, N_PROMPT_TILES - 1), 0)
    const2 = lambda i, *_: (0, 0)
    const3 = lambda i, *_: (0, 0, 0)
    if first_layer:
        prefetch = ()
        x_specs = [pl.BlockSpec((TL, D_MODEL), prompt_tile), pl.BlockSpec((TL, D_MODEL), const2)]
    else:
        prefetch = xs[0:2]
        xs = xs[2:]
        x_specs = [pl.BlockSpec((TL, D_MODEL), tile), pl.BlockSpec(memory_space=pl.ANY),
                   pl.BlockSpec((TL, 128), tile)]
    full = lambda a: pl.BlockSpec(a.shape, const2 if a.ndim == 2 else const3, pipeline_mode=pl.Buffered(1))
    weights = [sconv_pad, spool_pad, g1, w_in, conv_w, pool_bd, pool_scale, wm_all, bias_full, w_out, g2,
               wr_t, br_col, triu, tril_e, *lows]
    in_specs = x_specs + [full(a) for a in weights]
    seq_of = lambda i, *_: (jnp.minimum(i // TILES_PER_SEQ, BATCH - 1), 0, 0)
    out_shape = [
        jax.ShapeDtypeStruct((T_ALL, D_MODEL), jnp.float32),
        jax.ShapeDtypeStruct((PAGE_ROWS, 2, D_MODEL // 2), ROW_DTYPE),
        jax.ShapeDtypeStruct((N_TILES, 2, TL), jnp.int32),
        jax.ShapeDtypeStruct((N_TILES * RUNS_PER_TILE,), jnp.int32),
        jax.ShapeDtypeStruct((6, PAGE_LANES), jnp.int32),
        jax.ShapeDtypeStruct((T_ALL, 128), jnp.float32),
        jax.ShapeDtypeStruct((BATCH, HIST_ROWS, D_CONV), jnp.float32),
        jax.ShapeDtypeStruct((BATCH, HIST_ROWS, D_POOL), jnp.float32),
        jax.ShapeDtypeStruct((DEC_BATCH, HIST_ROWS, D_CONV), jnp.float32),
        jax.ShapeDtypeStruct((DEC_BATCH, HIST_ROWS, D_POOL), jnp.float32),
        jax.ShapeDtypeStruct((T_SAMPLE, D_SGU), jnp.float32),
    ]
    out_specs = [
        pl.BlockSpec((TL, D_MODEL), tile),
        pl.BlockSpec(memory_space=pl.ANY),
        pl.BlockSpec((1, 2, TL), lambda i, *_: (i, 0, 0)),
        pl.BlockSpec(memory_space=pltpu.SMEM),
        pl.BlockSpec(memory_space=pltpu.SMEM),
        pl.BlockSpec((TL, 128), tile),
        pl.BlockSpec((1, HIST_ROWS, D_CONV), seq_of),
        pl.BlockSpec((1, HIST_ROWS, D_POOL), seq_of),
        pl.BlockSpec((DEC_BATCH, HIST_ROWS, D_CONV), const3),
        pl.BlockSpec((DEC_BATCH, HIST_ROWS, D_POOL), const3),
        pl.BlockSpec((T_SAMPLE, D_SGU), const2),
    ]
    scratch = [
        pltpu.VMEM((HIST_ROWS, D_CONV), jnp.float32),
        pltpu.VMEM((HIST_ROWS, D_POOL), jnp.float32),
        pltpu.VMEM((TL, 2, D_MODEL // 2), ROW_DTYPE),
        pltpu.VMEM((4 * TL, 2, D_MODEL // 2), ROW_DTYPE),
        pltpu.VMEM((8, TL), jnp.int32),
        pltpu.VMEM((8, 128), jnp.int32),
        pltpu.SMEM((8, TL), jnp.int32),
        pltpu.SMEM((2, 8, 128), jnp.int32),
        pltpu.VMEM((N_EXPERTS, 128), jnp.int32),
        pltpu.VMEM((N_EXPERTS, 128), jnp.int32),
        pltpu.VMEM((8, 128), jnp.int32),
        pltpu.VMEM((8, PAGE_LANES), jnp.int32),
        pltpu.SMEM((N_EXPERTS, 128), jnp.int32),
        pltpu.SMEM((N_EXPERTS, 128), jnp.int32),
        pltpu.SMEM((8, 128), jnp.int32),
        pltpu.SMEM((8, PAGE_LANES), jnp.int32),
        pltpu.SMEM((N_EXPERTS,), jnp.int32),
        pltpu.SemaphoreType.DMA((2,)),
        pltpu.SemaphoreType.DMA(()),
    ]
    if not first_layer:
        scratch += [
            pltpu.VMEM((4 * TL, 2, D_MODEL // 2), ROW_DTYPE),
            pltpu.VMEM((2, TL, 2, D_MODEL // 2), ROW_DTYPE),
            pltpu.VMEM((2, TL, 2, D_MODEL // 2), ROW_DTYPE),
            pltpu.SemaphoreType.DMA((2,)),
        ]
    grid_spec = pltpu.PrefetchScalarGridSpec(num_scalar_prefetch=len(prefetch), grid=(N_TILES,),
                                             in_specs=in_specs, out_specs=out_specs, scratch_shapes=scratch)
    return pl.pallas_call(
        functools.partial(_mixer_kernel, first_layer),
        grid_spec=grid_spec,
        out_shape=out_shape,
        compiler_params=pltpu.CompilerParams(dimension_semantics=("arbitrary",),
                                             vmem_limit_bytes=VMEM_LIMIT),
        name="mixer_first" if first_layer else "mixer_next",
    )(*prefetch, *xs, *weights)


def _expert_kernel(layer, meta_ref, hs_ref, wg_hbm, wu_hbm, wd_hbm, y_ref, xbuf, wg32, wu32, wd32, wg16, wu16,
                   wd16, wsem):
    s = pl.program_id(0)
    npages = meta_ref[2, 0]

    def weight_copies(expert, slot):
        return [pltpu.make_async_copy(wg_hbm.at[layer, expert], wg32.at[slot], wsem.at[slot]),
                pltpu.make_async_copy(wu_hbm.at[layer, expert], wu32.at[slot], wsem.at[slot]),
                pltpu.make_async_copy(wd_hbm.at[layer, expert], wd32.at[slot], wsem.at[slot])]

    @pl.when(s == 0)
    def _first_weights():
        for c in weight_copies(meta_ref[1, 0], 0):
            c.start()

    @pl.when((s < npages) & (meta_ref[3, s] == 1))
    def _switch_expert():
        slot = meta_ref[4, s] % 2
        for c in weight_copies(meta_ref[1, s], slot):
            c.wait()
        nxt = meta_ref[5, s]

        @pl.when(nxt >= 0)
        def _():
            for c in weight_copies(nxt, 1 - slot):
                c.start()
        wg16[...] = wg32[slot].astype(jnp.bfloat16)
        wu16[...] = wu32[slot].astype(jnp.bfloat16)
        wd16[...] = wd32[slot].astype(jnp.bfloat16)

    @pl.when(s < npages)
    def _page():
        xbuf[...] = hs_ref[...].reshape(TM, D_MODEL)
        x = xbuf[...]
        hg = jnp.dot(x, wg16[...], preferred_element_type=jnp.float32)
        hu = jnp.dot(x, wu16[...], preferred_element_type=jnp.float32)
        h = (hg * jax.nn.sigmoid(hg) * hu).astype(jnp.bfloat16)
        out = jnp.dot(h, wd16[...], preferred_element_type=jnp.float32)
        y_ref[...] = _pack_rows(out)

    @pl.when(s >= npages)
    def _unused_page():
        y_ref[...] = jnp.zeros((TM, 2, D_MODEL // 2), ROW_DTYPE)


def _expert_call(layer, meta, hs, w_gate, w_up, w_down):
    page = lambda s, meta: (meta[0, s], 0, 0)
    grid_spec = pltpu.PrefetchScalarGridSpec(
        num_scalar_prefetch=1,
        grid=(N_PAGES,),
        in_specs=[
            pl.BlockSpec((TM, 2, D_MODEL // 2), page),
            pl.BlockSpec(memory_space=pl.ANY),
            pl.BlockSpec(memory_space=pl.ANY),
            pl.BlockSpec(memory_space=pl.ANY),
        ],
        out_specs=pl.BlockSpec((TM, 2, D_MODEL // 2), page),
        scratch_shapes=[
            pltpu.VMEM((TM, D_MODEL), ROW_DTYPE),
            pltpu.VMEM((2, D_MODEL, D_EXPERT), jnp.float32),
            pltpu.VMEM((2, D_MODEL, D_EXPERT), jnp.float32),
            pltpu.VMEM((2, D_EXPERT, D_MODEL), jnp.float32),
            pltpu.VMEM((D_MODEL, D_EXPERT), jnp.bfloat16),
            pltpu.VMEM((D_MODEL, D_EXPERT), jnp.bfloat16),
            pltpu.VMEM((D_EXPERT, D_MODEL), jnp.bfloat16),
            pltpu.SemaphoreType.DMA((2,)),
        ],
    )
    return pl.pallas_call(
        functools.partial(_expert_kernel, layer),
        grid_spec=grid_spec,
        out_shape=jax.ShapeDtypeStruct((PAGE_ROWS, 2, D_MODEL // 2), ROW_DTYPE),
        compiler_params=pltpu.CompilerParams(dimension_semantics=("arbitrary",),
                                             vmem_limit_bytes=VMEM_LIMIT),
        name="experts",
    )(meta, hs, w_gate, w_up, w_down)


def _final_kernel(lposp_ref, runsp_ref, xm_ref, yp_ref, gc_ref, g_ref, yp_out, ys_out, gbuf, y0buf, y1buf, gsem):
    i = pl.program_id(0)

    @pl.when(i == 0)
    def _():
        _first_results(lposp_ref, runsp_ref, yp_ref, gbuf, y0buf, y1buf, gsem)

    def normed():
        g = gc_ref[...]
        x = xm_ref[...] + g[:, 0:1] * _unpack_rows(y0buf[i % 2]) + g[:, 1:2] * _unpack_rows(y1buf[i % 2])
        return _rms(x, g_ref[...])

    @pl.when(i < N_PROMPT_TILES)
    def _():
        _wait_run_gather(i + 1, gbuf, gsem)
        yp_out[...] = normed()
        _unsort_results(lposp_ref, i + 1, gbuf, y0buf, y1buf)

    @pl.when(i + 3 < N_TILES)
    def _():
        _start_run_gather(runsp_ref, i + 3, yp_ref, gbuf, gsem)

    @pl.when(i == N_PROMPT_TILES)
    def _():
        ys_out[...] = normed()


def _final_call(lpos, runs, x_mid, y_pages, gcol, g):
    tile = lambda i, *_: (i, 0)
    grid_spec = pltpu.PrefetchScalarGridSpec(
        num_scalar_prefetch=2,
        grid=(N_TILES,),
        in_specs=[pl.BlockSpec((TL, D_MODEL), tile), pl.BlockSpec(memory_space=pl.ANY),
                  pl.BlockSpec((TL, 128), tile), pl.BlockSpec((1, D_MODEL), lambda i, *_: (0, 0))],
        out_specs=[pl.BlockSpec((TL, D_MODEL), lambda i, *_: (jnp.minimum(i, N_PROMPT_TILES - 1), 0)),
                   pl.BlockSpec((TL, D_MODEL), lambda i, *_: (0, 0))],
        scratch_shapes=[pltpu.VMEM((4 * TL, 2, D_MODEL // 2), ROW_DTYPE), pltpu.VMEM((2, TL, 2, D_MODEL // 2), ROW_DTYPE),
                        pltpu.VMEM((2, TL, 2, D_MODEL // 2), ROW_DTYPE), pltpu.SemaphoreType.DMA((2,))],
    )
    return pl.pallas_call(
        _final_kernel,
        grid_spec=grid_spec,
        out_shape=[jax.ShapeDtypeStruct((T_PROMPT, D_MODEL), jnp.float32),
                   jax.ShapeDtypeStruct((T_SAMPLE, D_MODEL), jnp.float32)],
        compiler_params=pltpu.CompilerParams(dimension_semantics=("arbitrary",),
                                             vmem_limit_bytes=VMEM_LIMIT),
        name="final_norm",
    )(lpos, runs, x_mid, y_pages, gcol, g)


def kernel(x_prompt, x_sample, state_conv, state_pool, norm1_g, w_in, conv_w, pool_w, pool_scale, sgu_w, sgu_b, w_out, norm2_g, router_coarse_w, router_coarse_b, router_fine_w, router_fine_b, moe_w_gate, moe_w_up, moe_w_down, final_norm_g):
    bf16 = jnp.bfloat16
    xs = (x_prompt.reshape(T_PROMPT, D_MODEL), x_sample.reshape(T_SAMPLE, D_MODEL))
    sconv_pad = jnp.pad(state_conv, ((0, 0), (0, 0), (HIST_ROWS - (CONV_WIDTH - 1), 0), (0, 0)))
    spool_pad = jnp.pad(state_pool, ((0, 0), (0, 0), (HIST_ROWS - POOL_HIST, 0), (0, 0)))
    idx = jnp.arange(TL, dtype=jnp.int32)
    triu = (idx[:, None] < idx[None, :]).astype(bf16)
    ide = jnp.arange(N_EXPERTS, dtype=jnp.int32)
    tril_e = (ide[None, :] < ide[:, None]).astype(bf16)
    conv_pr, pool_pr, conv_sm, pool_sm, sgu_v = [], [], [], [], []
    x_mid = y_pages = gcol = lpos = runs = None
    for l in range(DEPTH):
        pool_bd32 = jax.scipy.linalg.block_diag(*[pool_w[l, g] for g in range(4)])
        pool_bd = pool_bd32.astype(bf16)
        wm32 = sgu_w[l].reshape(SGU_HEADS * SGU_LEN, SGU_LEN)
        wm_all = wm32.astype(bf16)
        bias_full = jnp.repeat(sgu_b[l].T, SGU_HEAD_DIM, axis=1)
        wr = jnp.zeros((ROUTER_ROWS, D_MODEL), jnp.float32)
        wr = wr.at[0:N_GROUPS].set(router_coarse_w[l].T).at[8:].set(router_fine_w[l].T)
        wr_hi = wr.astype(bf16)
        wr_lo = (wr - wr_hi.astype(jnp.float32)).astype(bf16)
        wr_t = jnp.concatenate([wr_hi, wr_lo], axis=0)
        br_col = jnp.zeros((ROUTER_ROWS, 1), jnp.float32)
        br_col = br_col.at[0:N_GROUPS, 0].set(router_coarse_b[l]).at[8:, 0].set(router_fine_b[l])
        low = lambda w: (w - w.astype(bf16).astype(jnp.float32)).astype(bf16)
        lows = (low(w_in[l]), low(w_out[l]), low(pool_bd32), low(wm32))
        outs = _mixer_call(l == 0, xs, sconv_pad[l], spool_pad[l], norm1_g[l].reshape(1, D_MODEL),
                           w_in[l].astype(bf16), conv_w[l], pool_bd, pool_scale[l].reshape(1, D_POOL),
                           wm_all, bias_full, w_out[l].astype(bf16), norm2_g[l].reshape(1, D_MODEL),
                           wr_t, br_col, triu, tril_e, lows)
        x_mid, hs, lpos, runs, meta, gcol, cpr, ppr, csm, psm, sv = outs
        lpos = lpos.reshape(N_TILES * 2 * TL)
        conv_pr.append(cpr[:, HIST_ROWS - (CONV_WIDTH - 1):, :])
        pool_pr.append(ppr[:, HIST_ROWS - POOL_HIST:, :])
        conv_sm.append(csm[:, HIST_ROWS - (CONV_WIDTH - 1):, :])
        pool_sm.append(psm[:, HIST_ROWS - POOL_HIST:, :])
        sgu_v.append(sv.reshape(DEC_BATCH, DEC_SEQ, D_SGU))
        y_pages = _expert_call(l, meta, hs, moe_w_gate, moe_w_up, moe_w_down)
        xs = (lpos, runs, x_mid, y_pages, gcol)
    y_prompt, y_sample = _final_call(lpos, runs, x_mid, y_pages, gcol, final_norm_g.reshape(1, D_MODEL))
    return (y_prompt.reshape(BATCH, SEQ, D_MODEL), y_sample.reshape(DEC_BATCH, DEC_SEQ, D_MODEL),
            jnp.stack(conv_pr), jnp.stack(pool_pr), jnp.stack(conv_sm), jnp.stack(pool_sm),
            jnp.stack(sgu_v))
```

```python
import functools

import jax
import jax.numpy as jnp
from jax import lax
from jax.experimental import pallas as pl
from jax.experimental.pallas import tpu as pltpu

D_MODEL = 1024
BATCH = 8
SEQ = 2048
DEPTH = 2
DEC_BATCH = 8
DEC_SEQ = 64
PAST_LEN = 1024
D_CONV = 384
CONV_WIDTH = 3
D_POOL = 256
POOL_HIST = 15
D_SGU = 384
SGU_HEADS = 4
SGU_HEAD_DIM = 96
SGU_LEN = 128
D_PROJ = 2176
N_GROUPS = 4
EXPERTS_PER_GROUP = 8
N_EXPERTS = 32
D_EXPERT = 512
EPS = 1e-6

T_PROMPT = BATCH * SEQ
T_SAMPLE = DEC_BATCH * DEC_SEQ
T_ALL = T_PROMPT + T_SAMPLE
TL = 512
TILES_PER_SEQ = SEQ // TL
N_PROMPT_TILES = T_PROMPT // TL
N_TILES = N_PROMPT_TILES + T_SAMPLE // TL
HIST_ROWS = 16
ROUTER_ROWS = 8 + N_EXPERTS
TM = 512
N_ASSIGN = 2 * T_ALL
N_PAGES = N_ASSIGN // TM + N_EXPERTS
PAGE_LANES = 256
PAGE_ROWS = N_PAGES * TM
N_PIECES = 2 if TL <= TM else 3
RUN_FIELDS = 3 * N_PIECES
RUNS_PER_TILE = N_EXPERTS * RUN_FIELDS
ROW_DTYPE = jnp.bfloat16
VMEM_LIMIT = 56 * 1024 * 1024

assert TM <= TL <= 2 * TM and N_PAGES <= PAGE_LANES and T_ALL == N_TILES * TL


def _pack_rows(x):
    return x.astype(ROW_DTYPE).reshape(x.shape[0], 2, D_MODEL // 2)


def _unpack_rows(rows):
    return rows.reshape(rows.shape[0], D_MODEL).astype(jnp.float32)


def _split(a):
    hi = a.astype(jnp.bfloat16)
    return hi, (a - hi.astype(jnp.float32)).astype(jnp.bfloat16)


def _dot_split(a, b_hi, b_lo):
    a_hi, a_lo = _split(a)
    dot = functools.partial(jnp.dot, preferred_element_type=jnp.float32)
    return dot(a_hi, b_hi) + dot(a_lo, b_hi) + dot(a_hi, b_lo)


def _rms(x, g):
    return x * lax.rsqrt(jnp.mean(x * x, axis=-1, keepdims=True) + EPS) * g


def _mix_rows(proj, zhist, phist, pos0, n, conv_w, pool_bd, pool_scale, wm_all, bias_full, pool_lo=None,
              wm_lo=None):
    a_b = proj[:, 0:384]
    a_c = proj[:, 384:768]
    a_h = proj[:, 768:1152]
    p_in = proj[:, 1152:1408]
    s_u = proj[:, 1408:1792]
    s_v = proj[:, 1792:2176]

    z = a_c * a_h
    zext = jnp.concatenate([zhist, z], axis=0)
    conv_y = (conv_w[0:1, :] * pltpu.roll(zext, 2, 0)[HIST_ROWS:, :]
              + conv_w[1:2, :] * pltpu.roll(zext, 1, 0)[HIST_ROWS:, :]
              + conv_w[2:3, :] * z)
    a_out = a_b * conv_y

    pext = jnp.concatenate([phist, p_in], axis=0)
    s2 = pext + pltpu.roll(pext, 1, 0)
    s4 = s2 + pltpu.roll(s2, 2, 0)
    s8 = s4 + pltpu.roll(s4, 4, 0)
    s16 = s8 + pltpu.roll(s8, 8, 0)
    lane = lax.broadcasted_iota(jnp.int32, (1, D_POOL), 1)
    wsum = jnp.where(lane < 64, s2, jnp.where(lane < 128, s4, jnp.where(lane < 192, s8, s16)))
    wsum = wsum[HIST_ROWS:, :]
    win = jnp.where(lane < 64, 2.0, jnp.where(lane < 128, 4.0, jnp.where(lane < 192, 8.0, 16.0)))
    pos = (pos0 + lax.broadcasted_iota(jnp.int32, (n, 1), 0) + 1).astype(jnp.float32)
    cnt = jnp.minimum(win, pos)
    pooled = wsum / cnt - p_in
    if pool_lo is None:
        p_out = jnp.dot(pooled.astype(jnp.bfloat16), pool_bd, preferred_element_type=jnp.float32)
    else:
        p_out = _dot_split(pooled, pool_bd, pool_lo)
    p_out = p_out * pool_scale

    lane_s = lax.broadcasted_iota(jnp.int32, (1, D_SGU), 1)
    chunk = min(n, SGU_LEN)
    def head_rows(w):
        if chunk == SGU_LEN:
            return w
        return jnp.concatenate([w[h * SGU_LEN:h * SGU_LEN + chunk, 0:chunk] for h in range(SGU_HEADS)], axis=0)
    wm = head_rows(wm_all)
    s_rows = []
    for c in range(n // chunk):
        v_c = s_v[c * chunk:(c + 1) * chunk, :]
        if wm_lo is None:
            r = jnp.dot(wm, v_c.astype(jnp.bfloat16), preferred_element_type=jnp.float32)
        else:
            v_hi, v_lo = _split(v_c)
            r = (jnp.dot(wm, v_hi, preferred_element_type=jnp.float32)
                 + jnp.dot(head_rows(wm_lo), v_hi, preferred_element_type=jnp.float32)
                 + jnp.dot(wm, v_lo, preferred_element_type=jnp.float32))
        s_c = jnp.where(lane_s < 96, r[0:chunk],
                        jnp.where(lane_s < 192, r[chunk:2 * chunk],
                                  jnp.where(lane_s < 288, r[2 * chunk:3 * chunk], r[3 * chunk:4 * chunk])))
        s_rows.append(s_c + bias_full[0:chunk, :])
    s_gate = s_rows[0] if len(s_rows) == 1 else jnp.concatenate(s_rows, axis=0)
    s_out = s_u * s_gate

    mix = jnp.concatenate([a_out, p_out, s_out], axis=-1)
    return mix, zext[n:n + HIST_ROWS, :], pext[n:n + HIST_ROWS, :]


def _route(h2, wr_t, br_col, n):
    h_hi = h2.astype(jnp.bfloat16)
    h_lo = (h2 - h_hi.astype(jnp.float32)).astype(jnp.bfloat16)
    nt = (((1,), (1,)), ((), ()))
    by_hi = lax.dot_general(wr_t, h_hi, nt, preferred_element_type=jnp.float32)
    by_lo = lax.dot_general(wr_t[0:ROUTER_ROWS, :], h_lo, nt, preferred_element_type=jnp.float32)
    logits = by_hi[0:ROUTER_ROWS, :] + by_hi[ROUTER_ROWS:, :] + by_lo + br_col
    row8 = lax.broadcasted_iota(jnp.int32, (8, n), 0)
    lc = jnp.where(row8 < N_GROUPS, logits[0:8, :], -jnp.inf)
    mc = jnp.max(lc, axis=0, keepdims=True)
    g_sel = jnp.min(jnp.where(lc == mc, row8, 8), axis=0, keepdims=True)
    p_sel = 1.0 / jnp.sum(jnp.exp(lc - mc), axis=0, keepdims=True)
    lf = logits[8 + 3 * EXPERTS_PER_GROUP:8 + 4 * EXPERTS_PER_GROUP, :]
    for g in (2, 1, 0):
        lf = jnp.where(g_sel == g, logits[8 + g * EXPERTS_PER_GROUP:8 + (g + 1) * EXPERTS_PER_GROUP, :], lf)
    m1 = jnp.max(lf, axis=0, keepdims=True)
    i1 = jnp.min(jnp.where(lf == m1, row8, 8), axis=0, keepdims=True)
    lf2 = jnp.where(row8 == i1, -jnp.inf, lf)
    m2 = jnp.max(lf2, axis=0, keepdims=True)
    i2 = jnp.min(jnp.where(lf2 == m2, row8, 8), axis=0, keepdims=True)
    t = jnp.exp(m2 - m1)
    wa = 1.0 / (1.0 + t)
    wb = t / (1.0 + t)
    e0 = g_sel * EXPERTS_PER_GROUP + i1
    e1 = g_sel * EXPERTS_PER_GROUP + i2
    return e0, e1, p_sel * wa, p_sel * wb


def _place_rows(e0, e1, triu, tril_e, fill_ref, base_ref, np_ref, pexp_ref):
    row_e = lax.broadcasted_iota(jnp.int32, (N_EXPERTS, TL), 0)
    oh0 = row_e == e0
    oh1 = row_e == e1
    oh = jnp.where(oh0 | oh1, 1.0, 0.0)
    rank = jnp.dot(oh.astype(jnp.bfloat16), triu, preferred_element_type=jnp.float32).astype(jnp.int32)
    cnt = jnp.sum(oh, axis=1, keepdims=True).astype(jnp.int32)
    lower = jnp.where(e0 < row_e, 1.0, 0.0) + jnp.where(e1 < row_e, 1.0, 0.0)
    first = jnp.sum(lower, axis=1, keepdims=True).astype(jnp.int32)
    sorted_all = first + rank
    lpos0 = jnp.sum(jnp.where(oh0, sorted_all, 0), axis=0, keepdims=True)
    lpos1 = jnp.sum(jnp.where(oh1, sorted_all, 0), axis=0, keepdims=True)

    fill = fill_ref[:, 0:1]
    base = base_ref[:, 0:1]
    npages = np_ref[0:1, 0:1]
    total = fill + cnt
    need = (total > TM).astype(jnp.int32) + (total > 2 * TM).astype(jnp.int32)
    need_b = jnp.broadcast_to(need.astype(jnp.float32), (N_EXPERTS, 128)).astype(jnp.bfloat16)
    before = jnp.dot(tril_e, need_b, preferred_element_type=jnp.float32)[:, 0:1].astype(jnp.int32)
    new_id = npages + before
    new_base = new_id * TM
    fill_ref[...] = jnp.broadcast_to(total - need * TM, (N_EXPERTS, 128))
    base_ref[...] = jnp.broadcast_to(jnp.where(need > 0, new_base + (need - 1) * TM, base), (N_EXPERTS, 128))
    np_ref[...] = jnp.broadcast_to(npages + jnp.sum(need, axis=0, keepdims=True), (8, 128))
    page_lane = lax.broadcasted_iota(jnp.int32, (N_EXPERTS, PAGE_LANES), 1)
    expert_col = lax.broadcasted_iota(jnp.int32, (N_EXPERTS, 1), 0)
    owns = ((page_lane == new_id) & (need >= 1)) | ((page_lane == new_id + 1) & (need == 2))
    pexp_ref[...] = pexp_ref[...] + jnp.sum(jnp.where(owns, expert_col, 0), axis=0, keepdims=True)

    lane = lax.broadcasted_iota(jnp.int32, (N_EXPERTS, 128), 1)
    cols = jnp.where(lane == 0, cnt, jnp.where(lane == 1, first, jnp.where(lane == 2, fill,
                     jnp.where(lane == 3, base, jnp.where(lane == 4, new_base, 0)))))
    square = jnp.concatenate([cols, jnp.zeros((128 - N_EXPERTS, 128), jnp.int32)], axis=0)
    per_expert = square.astype(jnp.float32).T.astype(jnp.int32)[0:8, :]
    return lpos0, lpos1, per_expert


def _run_pieces(cnt, first, fill, base, new_base):
    n0 = jnp.minimum(cnt, TM - fill)
    n1 = jnp.minimum(cnt - n0, TM)
    n2 = cnt - n0 - n1
    pieces = ((first, base + fill, n0), (first + n0, new_base, n1), (first + n0 + n1, new_base + TM, n2))
    return pieces[:N_PIECES]


def _start_run_gather(runs_ref, tile, pages_ref, gbuf, gsem):
    slot = tile % 2

    tile0 = tile * RUNS_PER_TILE
    for e in range(N_EXPERTS):
        for j in range(N_PIECES):
            k = tile0 + e * RUN_FIELDS + 3 * j
            src = runs_ref[k]
            dst = runs_ref[k + 1]
            n = runs_ref[k + 2]

            @pl.when(n > 0)
            def _():
                pltpu.make_async_copy(pages_ref.at[pl.ds(dst, n)], gbuf.at[pl.ds(slot * 2 * TL + src, n)],
                                      gsem.at[slot]).start()


def _wait_run_gather(tile, gbuf, gsem):
    half = gbuf.at[pl.ds(0, 2 * TL)]
    pltpu.make_async_copy(half, half, gsem.at[tile % 2]).wait()


def _unsort_results(lpos_ref, tile, gbuf, y0buf, y1buf):
    k0 = tile * 2 * TL
    slot = tile % 2
    for r in range(TL):
        y0buf[slot, r] = gbuf[lpos_ref[k0 + r]]
        y1buf[slot, r] = gbuf[lpos_ref[k0 + TL + r]]


def _first_results(lpos_ref, runs_ref, pages_ref, gbuf, y0buf, y1buf, gsem):
    _start_run_gather(runs_ref, 0, pages_ref, gbuf, gsem)
    _wait_run_gather(0, gbuf, gsem)
    _unsort_results(lpos_ref, 0, gbuf, y0buf, y1buf)
    _start_run_gather(runs_ref, 1, pages_ref, gbuf, gsem)
    _start_run_gather(runs_ref, 2, pages_ref, gbuf, gsem)


def _mixer_kernel(first_layer, *refs):
    if first_layer:
        xp_ref, xs_ref = refs[0:2]
        rest = refs[2:]
    else:
        lposp_ref, runsp_ref, xm_ref, yp_ref, gc_ref = refs[0:5]
        rest = refs[5:]
    (sconv_ref, spool_ref, g1_ref, win_ref, convw_ref, poolbd_ref, pscale_ref, wm_ref, bias_ref,
     wout_ref, g2_ref, wr_ref, br_ref, triu_ref, trile_ref, winlo_ref, woutlo_ref, poollo_ref, wmlo_ref,
     xmid_ref, hs_ref, lpos_ref, runs_ref, meta_ref, gcol_ref, cpr_ref, ppr_ref, csm_ref, psm_ref, sv_ref,
     zh_ref, ph_ref, h2buf, sbuf, posv, pexv, poss, pexs, fill_ref, base_ref, np_ref,
     pexp_ref, fill_s, base_s, np_s, pexp_s, cnt_s, rsem, msem) = rest[:49]
    if not first_layer:
        gbuf, y0buf, y1buf, gsem = rest[49:]

    i = pl.program_id(0)

    def load_x(prompt):
        if first_layer:
            return xp_ref[...] if prompt else xs_ref[...]
        g = gc_ref[...]
        return (xm_ref[...] + g[:, 0:1] * _unpack_rows(y0buf[i % 2])
                + g[:, 1:2] * _unpack_rows(y1buf[i % 2]))

    row_m = lax.broadcasted_iota(jnp.int32, (SGU_HEADS * SGU_LEN, SGU_LEN), 0) % SGU_LEN
    col_m = lax.broadcasted_iota(jnp.int32, (SGU_HEADS * SGU_LEN, SGU_LEN), 1)

    def wait_rows(slot):
        half = sbuf.at[pl.ds(0, 2 * TL)]
        pltpu.make_async_copy(half, half, rsem.at[slot]).wait()

    def wait_positions(slot):
        pltpu.make_async_copy(posv, poss, msem).wait()
        pltpu.make_async_copy(pexv, pexs.at[slot], msem).wait()

    def sort_rows(slot):
        for r in range(TL):
            row = h2buf[r]
            sbuf[poss[0, r]] = row
            sbuf[poss[1, r]] = row

    def start_runs(tile, slot):
        tile0 = tile * RUNS_PER_TILE
        for e in range(N_EXPERTS):
            pieces = _run_pieces(pexs[slot, 0, e], pexs[slot, 1, e], pexs[slot, 2, e], pexs[slot, 3, e],
                                 pexs[slot, 4, e])
            for j, (src, dst, n) in enumerate(pieces):
                k = tile0 + e * RUN_FIELDS + 3 * j
                runs_ref[k] = src
                runs_ref[k + 1] = dst
                runs_ref[k + 2] = n

                @pl.when(n > 0)
                def _():
                    pltpu.make_async_copy(sbuf.at[pl.ds(slot * 2 * TL + src, n)], hs_ref.at[pl.ds(dst, n)],
                                          rsem.at[slot]).start()

        @pl.when(tile >= 1)
        def _():
            wait_rows(1 - slot)

    def finish(x, mix, precise=False):
        if precise:
            x_mid = x + _dot_split(mix, wout_ref[...], woutlo_ref[...])
        else:
            x_mid = x + jnp.dot(mix.astype(jnp.bfloat16), wout_ref[...], preferred_element_type=jnp.float32)
        xmid_ref[...] = x_mid
        h2 = _rms(x_mid, g2_ref[...])
        h2buf[...] = _pack_rows(h2)
        e0, e1, g0, g1 = _route(h2, wr_ref[...], br_ref[...], TL)
        row128 = lax.broadcasted_iota(jnp.int32, (128, TL), 0)
        gcol_ref[...] = jnp.where(row128 == 0, g0, jnp.where(row128 == 1, g1, 0.0)).T
        lpos0, lpos1, per_expert = _place_rows(e0, e1, triu_ref[...], trile_ref[...], fill_ref, base_ref,
                                               np_ref, pexp_ref)
        row8 = lax.broadcasted_iota(jnp.int32, (8, TL), 0)
        half = (i % 2) * (2 * TL)
        posv[...] = jnp.where(row8 == 0, lpos0, jnp.where(row8 == 1, lpos1, 0)) + half
        pexv[...] = per_expert
        row2 = lax.broadcasted_iota(jnp.int32, (2, TL), 0)
        lpos_ref[0] = jnp.where(row2 == 0, lpos0, lpos1) + half
        pltpu.make_async_copy(posv, poss, msem).start()
        pltpu.make_async_copy(pexv, pexs.at[i % 2], msem).start()

    def project(x, precise=False):
        h = _rms(x, g1_ref[...])
        if precise:
            return _dot_split(h, win_ref[...], winlo_ref[...])
        return jnp.dot(h.astype(jnp.bfloat16), win_ref[...], preferred_element_type=jnp.float32)

    def masked_wm(ref=wm_ref):
        return jnp.where(col_m <= row_m, ref[...], jnp.zeros_like(ref[...]))

    @pl.when(i == 0)
    def _init():
        zh_ref[...] = jnp.zeros_like(zh_ref)
        ph_ref[...] = jnp.zeros_like(ph_ref)
        fill_ref[...] = jnp.full_like(fill_ref, TM)
        base_ref[...] = jnp.zeros_like(base_ref)
        np_ref[...] = jnp.zeros_like(np_ref)
        pexp_ref[...] = jnp.zeros_like(pexp_ref)
        h2buf[...] = jnp.zeros((TL, 2, D_MODEL // 2), ROW_DTYPE)
        row8 = lax.broadcasted_iota(jnp.int32, (8, TL), 0)
        posv[...] = 2 * TL + jnp.minimum(row8, 1) * TL + lax.broadcasted_iota(jnp.int32, (8, TL), 1)
        pexv[...] = jnp.zeros((8, 128), jnp.int32)
        pltpu.make_async_copy(posv, poss, msem).start()
        pltpu.make_async_copy(pexv, pexs.at[1], msem).start()
        if not first_layer:
            _first_results(lposp_ref, runsp_ref, yp_ref, gbuf, y0buf, y1buf, gsem)

    def _close_pages():
        copies = [pltpu.make_async_copy(fill_ref, fill_s, msem), pltpu.make_async_copy(base_ref, base_s, msem),
                  pltpu.make_async_copy(np_ref, np_s, msem), pltpu.make_async_copy(pexp_ref, pexp_s, msem)]
        for c in copies:
            c.start()
        for c in copies:
            c.wait()
        npages = np_s[0, 0]

        def zero_cnt(e, c):
            cnt_s[e] = 0
            return c
        lax.fori_loop(0, N_EXPERTS, zero_cnt, 0)

        def count(p, c):
            e = pexp_s[0, p]
            cnt_s[e] = cnt_s[e] + 1
            return c
        lax.fori_loop(0, npages, count, 0)

        def prefix(e, run):
            n = cnt_s[e]
            cnt_s[e] = run
            return run + n
        lax.fori_loop(0, N_EXPERTS, prefix, 0)

        def emit(p, c):
            e = pexp_s[0, p]
            q = cnt_s[e]
            cnt_s[e] = q + 1
            meta_ref[0, q] = p
            meta_ref[1, q] = e
            return c
        lax.fori_loop(0, npages, emit, 0)

        def pad_meta(q, c):
            meta_ref[0, q] = jnp.minimum(q, N_PAGES - 1)
            meta_ref[1, q] = meta_ref[1, npages - 1]
            return c
        lax.fori_loop(npages, PAGE_LANES, pad_meta, 0)

        def fill_row2(q, c):
            meta_ref[2, q] = npages
            return c
        lax.fori_loop(0, PAGE_LANES, fill_row2, 0)

        def mark_first(q, seen):
            is_first = (q < npages) & ((q == 0) | (meta_ref[1, q] != meta_ref[1, jnp.maximum(q - 1, 0)]))
            seen = seen + jnp.where(is_first, 1, 0)
            meta_ref[3, q] = jnp.where(is_first, 1, 0)
            meta_ref[4, q] = jnp.maximum(seen - 1, 0)
            meta_ref[5, q] = -1
            return seen
        lax.fori_loop(0, PAGE_LANES, mark_first, 0)

        def mark_next(t, carry):
            cur, nxt = carry
            q = npages - 1 - t
            e = meta_ref[1, q]
            nxt = jnp.where(e != cur, cur, nxt)
            meta_ref[5, q] = nxt
            return e, nxt
        lax.fori_loop(0, npages, mark_next, (jnp.int32(-1), jnp.int32(-1)))

        h2buf[...] = jnp.zeros((TL, 2, D_MODEL // 2), ROW_DTYPE)

        def tails(start):
            def tail(e, c):
                f = fill_s[e, 0]
                b = base_s[e, 0]

                @pl.when(f < TM)
                def _zero_tail():
                    cp = pltpu.make_async_copy(h2buf.at[pl.ds(0, TM - f)], hs_ref.at[pl.ds(b + f, TM - f)],
                                               rsem.at[0])
                    cp.start() if start else cp.wait()
                return c
            lax.fori_loop(0, N_EXPERTS, tail, 0)

            def unused(p, c):
                cp = pltpu.make_async_copy(h2buf.at[pl.ds(0, TM)], hs_ref.at[pl.ds(p * TM, TM)], rsem.at[0])
                cp.start() if start else cp.wait()
                return c
            lax.fori_loop(npages, N_PAGES, unused, 0)

        tails(True)
        tails(False)

    prev_slot = (i + 1) % 2

    @pl.when(i < N_PROMPT_TILES)
    def _prompt():
        if not first_layer:
            _wait_run_gather(i + 1, gbuf, gsem)
        s = i % TILES_PER_SEQ
        x = load_x(True)
        proj = project(x)
        zhist = jnp.where(s == 0, 0.0, zh_ref[...])
        phist = jnp.where(s == 0, 0.0, ph_ref[...])
        mix, znew, pnew = _mix_rows(proj, zhist, phist, s * TL, TL, convw_ref[...], poolbd_ref[...],
                                    pscale_ref[...], masked_wm(), bias_ref[...])
        zh_ref[...] = znew
        ph_ref[...] = pnew
        cpr_ref[0] = znew
        ppr_ref[0] = pnew
        if not first_layer:
            _unsort_results(lposp_ref, i + 1, gbuf, y0buf, y1buf)
        wait_positions(prev_slot)
        sort_rows(prev_slot)
        finish(x, mix)

    @pl.when(i <= N_PROMPT_TILES - 1)
    def _prompt_runs():
        start_runs(jnp.maximum(i - 1, 0), prev_slot)
        if not first_layer:
            @pl.when(i + 3 < N_TILES)
            def _():
                _start_run_gather(runsp_ref, i + 3, yp_ref, gbuf, gsem)

    @pl.when(i == N_PROMPT_TILES)
    def _sample():
        x = load_x(False)
        proj = project(x, precise=True)
        wm = masked_wm()
        wm_lo = masked_wm(wmlo_ref)
        mixes = []
        for b in range(DEC_BATCH):
            rows = slice(b * DEC_SEQ, (b + 1) * DEC_SEQ)
            mix, znew, pnew = _mix_rows(proj[rows, :], sconv_ref[b], spool_ref[b], PAST_LEN, DEC_SEQ,
                                        convw_ref[...], poolbd_ref[...], pscale_ref[...], wm, bias_ref[...],
                                        pool_lo=poollo_ref[...], wm_lo=wm_lo)
            csm_ref[b] = znew
            psm_ref[b] = pnew
            mixes.append(mix)
        sv_ref[...] = proj[:, 1792:2176]
        wait_positions(prev_slot)
        sort_rows(prev_slot)
        finish(x, jnp.concatenate(mixes, axis=0), precise=True)
        start_runs(i - 1, prev_slot)
        wait_positions(i % 2)
        sort_rows(i % 2)
        start_runs(i, i % 2)
        wait_rows(i % 2)
        _close_pages()


def _mixer_call(first_layer, xs, sconv_pad, spool_pad, g1, w_in, conv_w, pool_bd, pool_scale, wm_all,
                bias_full, w_out, g2, wr_t, br_col, triu, tril_e, lows):
    tile = lambda i, *_: (i, 0)
    prompt_tile = lambda i, *_: (jnp.minimum(i, N_PROMPT_TILES - 1), 0)
    const2 = lambda i, *_: (0, 0)
    const3 = lambda i, *_: (0, 0, 0)
    if first_layer:
        prefetch = ()
        x_specs = [pl.BlockSpec((TL, D_MODEL), prompt_tile), pl.BlockSpec((TL, D_MODEL), const2)]
    else:
        prefetch = xs[0:2]
        xs = xs[2:]
        x_specs = [pl.BlockSpec((TL, D_MODEL), tile), pl.BlockSpec(memory_space=pl.ANY),
                   pl.BlockSpec((TL, 128), tile)]
    full = lambda a: pl.BlockSpec(a.shape, const2 if a.ndim == 2 else const3, pipeline_mode=pl.Buffered(1))
    weights = [sconv_pad, spool_pad, g1, w_in, conv_w, pool_bd, pool_scale, wm_all, bias_full, w_out, g2,
               wr_t, br_col, triu, tril_e, *lows]
    in_specs = x_specs + [full(a) for a in weights]
    seq_of = lambda i, *_: (jnp.minimum(i // TILES_PER_SEQ, BATCH - 1), 0, 0)
    out_shape = [
        jax.ShapeDtypeStruct((T_ALL, D_MODEL), jnp.float32),
        jax.ShapeDtypeStruct((PAGE_ROWS, 2, D_MODEL // 2), ROW_DTYPE),
        jax.ShapeDtypeStruct((N_TILES, 2, TL), jnp.int32),
        jax.ShapeDtypeStruct((N_TILES * RUNS_PER_TILE,), jnp.int32),
        jax.ShapeDtypeStruct((6, PAGE_LANES), jnp.int32),
        jax.ShapeDtypeStruct((T_ALL, 128), jnp.float32),
        jax.ShapeDtypeStruct((BATCH, HIST_ROWS, D_CONV), jnp.float32),
        jax.ShapeDtypeStruct((BATCH, HIST_ROWS, D_POOL), jnp.float32),
        jax.ShapeDtypeStruct((DEC_BATCH, HIST_ROWS, D_CONV), jnp.float32),
        jax.ShapeDtypeStruct((DEC_BATCH, HIST_ROWS, D_POOL), jnp.float32),
        jax.ShapeDtypeStruct((T_SAMPLE, D_SGU), jnp.float32),
    ]
    out_specs = [
        pl.BlockSpec((TL, D_MODEL), tile),
        pl.BlockSpec(memory_space=pl.ANY),
        pl.BlockSpec((1, 2, TL), lambda i, *_: (i, 0, 0)),
        pl.BlockSpec(memory_space=pltpu.SMEM),
        pl.BlockSpec(memory_space=pltpu.SMEM),
        pl.BlockSpec((TL, 128), tile),
        pl.BlockSpec((1, HIST_ROWS, D_CONV), seq_of),
        pl.BlockSpec((1, HIST_ROWS, D_POOL), seq_of),
        pl.BlockSpec((DEC_BATCH, HIST_ROWS, D_CONV), const3),
        pl.BlockSpec((DEC_BATCH, HIST_ROWS, D_POOL), const3),
        pl.BlockSpec((T_SAMPLE, D_SGU), const2),
    ]
    scratch = [
        pltpu.VMEM((HIST_ROWS, D_CONV), jnp.float32),
        pltpu.VMEM((HIST_ROWS, D_POOL), jnp.float32),
        pltpu.VMEM((TL, 2, D_MODEL // 2), ROW_DTYPE),
        pltpu.VMEM((4 * TL, 2, D_MODEL // 2), ROW_DTYPE),
        pltpu.VMEM((8, TL), jnp.int32),
        pltpu.VMEM((8, 128), jnp.int32),
        pltpu.SMEM((8, TL), jnp.int32),
        pltpu.SMEM((2, 8, 128), jnp.int32),
        pltpu.VMEM((N_EXPERTS, 128), jnp.int32),
        pltpu.VMEM((N_EXPERTS, 128), jnp.int32),
        pltpu.VMEM((8, 128), jnp.int32),
        pltpu.VMEM((8, PAGE_LANES), jnp.int32),
        pltpu.SMEM((N_EXPERTS, 128), jnp.int32),
        pltpu.SMEM((N_EXPERTS, 128), jnp.int32),
        pltpu.SMEM((8, 128), jnp.int32),
        pltpu.SMEM((8, PAGE_LANES), jnp.int32),
        pltpu.SMEM((N_EXPERTS,), jnp.int32),
        pltpu.SemaphoreType.DMA((2,)),
        pltpu.SemaphoreType.DMA(()),
    ]
    if not first_layer:
        scratch += [
            pltpu.VMEM((4 * TL, 2, D_MODEL // 2), ROW_DTYPE),
            pltpu.VMEM((2, TL, 2, D_MODEL // 2), ROW_DTYPE),
            pltpu.VMEM((2, TL, 2, D_MODEL // 2), ROW_DTYPE),
            pltpu.SemaphoreType.DMA((2,)),
        ]
    grid_spec = pltpu.PrefetchScalarGridSpec(num_scalar_prefetch=len(prefetch), grid=(N_TILES,),
                                             in_specs=in_specs, out_specs=out_specs, scratch_shapes=scratch)
    return pl.pallas_call(
        functools.partial(_mixer_kernel, first_layer),
        grid_spec=grid_spec,
        out_shape=out_shape,
        compiler_params=pltpu.CompilerParams(dimension_semantics=("arbitrary",),
                                             vmem_limit_bytes=VMEM_LIMIT),
        name="mixer_first" if first_layer else "mixer_next",
    )(*prefetch, *xs, *weights)


def _expert_kernel(layer, meta_ref, hs_ref, wg_hbm, wu_hbm, wd_hbm, y_ref, xbuf, wg32, wu32, wd32, wg16, wu16,
                   wd16, wsem):
    s = pl.program_id(0)
    npages = meta_ref[2, 0]

    def weight_copies(expert, slot):
        return [pltpu.make_async_copy(wg_hbm.at[layer, expert], wg32.at[slot], wsem.at[slot]),
                pltpu.make_async_copy(wu_hbm.at[layer, expert], wu32.at[slot], wsem.at[slot]),
                pltpu.make_async_copy(wd_hbm.at[layer, expert], wd32.at[slot], wsem.at[slot])]

    @pl.when(s == 0)
    def _first_weights():
        for c in weight_copies(meta_ref[1, 0], 0):
            c.start()

    @pl.when((s < npages) & (meta_ref[3, s] == 1))
    def _switch_expert():
        slot = meta_ref[4, s] % 2
        for c in weight_copies(meta_ref[1, s], slot):
            c.wait()
        nxt = meta_ref[5, s]

        @pl.when(nxt >= 0)
        def _():
            for c in weight_copies(nxt, 1 - slot):
                c.start()
        wg16[...] = wg32[slot].astype(jnp.bfloat16)
        wu16[...] = wu32[slot].astype(jnp.bfloat16)
        wd16[...] = wd32[slot].astype(jnp.bfloat16)

    @pl.when(s < npages)
    def _page():
        xbuf[...] = hs_ref[...].reshape(TM, D_MODEL)
        x = xbuf[...]
        hg = jnp.dot(x, wg16[...], preferred_element_type=jnp.float32)
        hu = jnp.dot(x, wu16[...], preferred_element_type=jnp.float32)
        h = (hg * jax.nn.sigmoid(hg) * hu).astype(jnp.bfloat16)
        out = jnp.dot(h, wd16[...], preferred_element_type=jnp.float32)
        y_ref[...] = _pack_rows(out)

    @pl.when(s >= npages)
    def _unused_page():
        y_ref[...] = jnp.zeros((TM, 2, D_MODEL // 2), ROW_DTYPE)


def _expert_call(layer, meta, hs, w_gate, w_up, w_down):
    page = lambda s, meta: (meta[0, s], 0, 0)
    grid_spec = pltpu.PrefetchScalarGridSpec(
        num_scalar_prefetch=1,
        grid=(N_PAGES,),
        in_specs=[
            pl.BlockSpec((TM, 2, D_MODEL // 2), page),
            pl.BlockSpec(memory_space=pl.ANY),
            pl.BlockSpec(memory_space=pl.ANY),
            pl.BlockSpec(memory_space=pl.ANY),
        ],
        out_specs=pl.BlockSpec((TM, 2, D_MODEL // 2), page),
        scratch_shapes=[
            pltpu.VMEM((TM, D_MODEL), ROW_DTYPE),
            pltpu.VMEM((2, D_MODEL, D_EXPERT), jnp.float32),
            pltpu.VMEM((2, D_MODEL, D_EXPERT), jnp.float32),
            pltpu.VMEM((2, D_EXPERT, D_MODEL), jnp.float32),
            pltpu.VMEM((D_MODEL, D_EXPERT), jnp.bfloat16),
            pltpu.VMEM((D_MODEL, D_EXPERT), jnp.bfloat16),
            pltpu.VMEM((D_EXPERT, D_MODEL), jnp.bfloat16),
            pltpu.SemaphoreType.DMA((2,)),
        ],
    )
    return pl.pallas_call(
        functools.partial(_expert_kernel, layer),
        grid_spec=grid_spec,
        out_shape=jax.ShapeDtypeStruct((PAGE_ROWS, 2, D_MODEL // 2), ROW_DTYPE),
        compiler_params=pltpu.CompilerParams(dimension_semantics=("arbitrary",),
                                             vmem_limit_bytes=VMEM_LIMIT),
        name="experts",
    )(meta, hs, w_gate, w_up, w_down)


def _final_kernel(lposp_ref, runsp_ref, xm_ref, yp_ref, gc_ref, g_ref, yp_out, ys_out, gbuf, y0buf, y1buf, gsem):
    i = pl.program_id(0)

    @pl.when(i == 0)
    def _():
        _first_results(lposp_ref, runsp_ref, yp_ref, gbuf, y0buf, y1buf, gsem)

    def normed():
        g = gc_ref[...]
        x = xm_ref[...] + g[:, 0:1] * _unpack_rows(y0buf[i % 2]) + g[:, 1:2] * _unpack_rows(y1buf[i % 2])
        return _rms(x, g_ref[...])

    @pl.when(i < N_PROMPT_TILES)
    def _():
        _wait_run_gather(i + 1, gbuf, gsem)
        yp_out[...] = normed()
        _unsort_results(lposp_ref, i + 1, gbuf, y0buf, y1buf)

    @pl.when(i + 3 < N_TILES)
    def _():
        _start_run_gather(runsp_ref, i + 3, yp_ref, gbuf, gsem)

    @pl.when(i == N_PROMPT_TILES)
    def _():
        ys_out[...] = normed()


def _final_call(lpos, runs, x_mid, y_pages, gcol, g):
    tile = lambda i, *_: (i, 0)
    grid_spec = pltpu.PrefetchScalarGridSpec(
        num_scalar_prefetch=2,
        grid=(N_TILES,),
        in_specs=[pl.BlockSpec((TL, D_MODEL), tile), pl.BlockSpec(memory_space=pl.ANY),
                  pl.BlockSpec((TL, 128), tile), pl.BlockSpec((1, D_MODEL), lambda i, *_: (0, 0))],
        out_specs=[pl.BlockSpec((TL, D_MODEL), lambda i, *_: (jnp.minimum(i, N_PROMPT_TILES - 1), 0)),
                   pl.BlockSpec((TL, D_MODEL), lambda i, *_: (0, 0))],
        scratch_shapes=[pltpu.VMEM((4 * TL, 2, D_MODEL // 2), ROW_DTYPE), pltpu.VMEM((2, TL, 2, D_MODEL // 2), ROW_DTYPE),
                        pltpu.VMEM((2, TL, 2, D_MODEL // 2), ROW_DTYPE), pltpu.SemaphoreType.DMA((2,))],
    )
    return pl.pallas_call(
        _final_kernel,
        grid_spec=grid_spec,
        out_shape=[jax.ShapeDtypeStruct((T_PROMPT, D_MODEL), jnp.float32),
                   jax.ShapeDtypeStruct((T_SAMPLE, D_MODEL), jnp.float32)],
        compiler_params=pltpu.CompilerParams(dimension_semantics=("arbitrary",),
                                             vmem_limit_bytes=VMEM_LIMIT),
        name="final_norm",
    )(lpos, runs, x_mid, y_pages, gcol, g)


def kernel(x_prompt, x_sample, state_conv, state_pool, norm1_g, w_in, conv_w, pool_w, pool_scale, sgu_w, sgu_b, w_out, norm2_g, router_coarse_w, router_coarse_b, router_fine_w, router_fine_b, moe_w_gate, moe_w_up, moe_w_down, final_norm_g):
    bf16 = jnp.bfloat16
    xs = (x_prompt.reshape(T_PROMPT, D_MODEL), x_sample.reshape(T_SAMPLE, D_MODEL))
    sconv_pad = jnp.pad(state_conv, ((0, 0), (0, 0), (HIST_ROWS - (CONV_WIDTH - 1), 0), (0, 0)))
    spool_pad = jnp.pad(state_pool, ((0, 0), (0, 0), (HIST_ROWS - POOL_HIST, 0), (0, 0)))
    idx = jnp.arange(TL, dtype=jnp.int32)
    triu = (idx[:, None] < idx[None, :]).astype(bf16)
    ide = jnp.arange(N_EXPERTS, dtype=jnp.int32)
    tril_e = (ide[None, :] < ide[:, None]).astype(bf16)
    conv_pr, pool_pr, conv_sm, pool_sm, sgu_v = [], [], [], [], []
    x_mid = y_pages = gcol = lpos = runs = None
    for l in range(DEPTH):
        pool_bd32 = jax.scipy.linalg.block_diag(*[pool_w[l, g] for g in range(4)])
        pool_bd = pool_bd32.astype(bf16)
        wm32 = sgu_w[l].reshape(SGU_HEADS * SGU_LEN, SGU_LEN)
        wm_all = wm32.astype(bf16)
        bias_full = jnp.repeat(sgu_b[l].T, SGU_HEAD_DIM, axis=1)
        wr = jnp.zeros((ROUTER_ROWS, D_MODEL), jnp.float32)
        wr = wr.at[0:N_GROUPS].set(router_coarse_w[l].T).at[8:].set(router_fine_w[l].T)
        wr_hi = wr.astype(bf16)
        wr_lo = (wr - wr_hi.astype(jnp.float32)).astype(bf16)
        wr_t = jnp.concatenate([wr_hi, wr_lo], axis=0)
        br_col = jnp.zeros((ROUTER_ROWS, 1), jnp.float32)
        br_col = br_col.at[0:N_GROUPS, 0].set(router_coarse_b[l]).at[8:, 0].set(router_fine_b[l])
        low = lambda w: (w - w.astype(bf16).astype(jnp.float32)).astype(bf16)
        lows = (low(w_in[l]), low(w_out[l]), low(pool_bd32), low(wm32))
        outs = _mixer_call(l == 0, xs, sconv_pad[l], spool_pad[l], norm1_g[l].reshape(1, D_MODEL),
                           w_in[l].astype(bf16), conv_w[l], pool_bd, pool_scale[l].reshape(1, D_POOL),
                           wm_all, bias_full, w_out[l].astype(bf16), norm2_g[l].reshape(1, D_MODEL),
                           wr_t, br_col, triu, tril_e, lows)
        x_mid, hs, lpos, runs, meta, gcol, cpr, ppr, csm, psm, sv = outs
        lpos = lpos.reshape(N_TILES * 2 * TL)
        conv_pr.append(cpr[:, HIST_ROWS - (CONV_WIDTH - 1):, :])
        pool_pr.append(ppr[:, HIST_ROWS - POOL_HIST:, :])
        conv_sm.append(csm[:, HIST_ROWS - (CONV_WIDTH - 1):, :])
        pool_sm.append(psm[:, HIST_ROWS - POOL_HIST:, :])
        sgu_v.append(sv.reshape(DEC_BATCH, DEC_SEQ, D_SGU))
        y_pages = _expert_call(l, meta, hs, moe_w_gate, moe_w_up, moe_w_down)
        xs = (lpos, runs, x_mid, y_pages, gcol)
    y_prompt, y_sample = _final_call(lpos, runs, x_mid, y_pages, gcol, final_norm_g.reshape(1, D_MODEL))
    return (y_prompt.reshape(BATCH, SEQ, D_MODEL), y_sample.reshape(DEC_BATCH, DEC_SEQ, D_MODEL),
            jnp.stack(conv_pr), jnp.stack(pool_pr), jnp.stack(conv_sm), jnp.stack(pool_sm),
            jnp.stack(sgu_v))
```

```python
import functools

import jax
import jax.numpy as jnp
from jax import lax
from jax.experimental import pallas as pl
from jax.experimental.pallas import tpu as pltpu

D_MODEL = 1024
BATCH = 8
SEQ = 2048
DEPTH = 2
DEC_BATCH = 8
DEC_SEQ = 64
PAST_LEN = 1024
D_CONV = 384
CONV_WIDTH = 3
D_POOL = 256
POOL_HIST = 15
D_SGU = 384
SGU_HEADS = 4
SGU_HEAD_DIM = 96
SGU_LEN = 128
D_PROJ = 2176
N_GROUPS = 4
EXPERTS_PER_GROUP = 8
N_EXPERTS = 32
D_EXPERT = 512
EPS = 1e-6

T_PROMPT = BATCH * SEQ
T_SAMPLE = DEC_BATCH * DEC_SEQ
T_ALL = T_PROMPT + T_SAMPLE
TL = 512
TILES_PER_SEQ = SEQ // TL
N_PROMPT_TILES = T_PROMPT // TL
N_TILES = N_PROMPT_TILES + T_SAMPLE // TL
HIST_ROWS = 16
ROUTER_ROWS = 8 + N_EXPERTS
TM = 512
N_ASSIGN = 2 * T_ALL
N_PAGES = N_ASSIGN // TM + N_EXPERTS
PAGE_LANES = 128
PAGE_ROWS = N_PAGES * TM
N_PIECES = 2 if TL <= TM else 3
RUN_FIELDS = 3 * N_PIECES
RUNS_PER_TILE = N_EXPERTS * RUN_FIELDS
ROW_DTYPE = jnp.bfloat16
VMEM_LIMIT = 56 * 1024 * 1024

assert TM <= TL <= 2 * TM and N_PAGES <= PAGE_LANES and T_ALL == N_TILES * TL


def _pack_rows(x):
    return x.astype(ROW_DTYPE).reshape(x.shape[0], 2, D_MODEL // 2)


def _unpack_rows(rows):
    return rows.reshape(rows.shape[0], D_MODEL).astype(jnp.float32)


def _split(a):
    hi = a.astype(jnp.bfloat16)
    return hi, (a - hi.astype(jnp.float32)).astype(jnp.bfloat16)


def _dot_split(a, b_hi, b_lo):
    a_hi, a_lo = _split(a)
    dot = functools.partial(jnp.dot, preferred_element_type=jnp.float32)
    return dot(a_hi, b_hi) + dot(a_lo, b_hi) + dot(a_hi, b_lo)


def _rms(x, g):
    return x * lax.rsqrt(jnp.mean(x * x, axis=-1, keepdims=True) + EPS) * g


def _mix_rows(proj, zhist, phist, pos0, n, conv_w, pool_bd, pool_scale, wm_all, bias_full, pool_lo=None,
              wm_lo=None):
    a_b = proj[:, 0:384]
    a_c = proj[:, 384:768]
    a_h = proj[:, 768:1152]
    p_in = proj[:, 1152:1408]
    s_u = proj[:, 1408:1792]
    s_v = proj[:, 1792:2176]

    z = a_c * a_h
    zext = jnp.concatenate([zhist, z], axis=0)
    conv_y = (conv_w[0:1, :] * pltpu.roll(zext, 2, 0)[HIST_ROWS:, :]
              + conv_w[1:2, :] * pltpu.roll(zext, 1, 0)[HIST_ROWS:, :]
              + conv_w[2:3, :] * z)
    a_out = a_b * conv_y

    pext = jnp.concatenate([phist, p_in], axis=0)
    s2 = pext + pltpu.roll(pext, 1, 0)
    s4 = s2 + pltpu.roll(s2, 2, 0)
    s8 = s4 + pltpu.roll(s4, 4, 0)
    s16 = s8 + pltpu.roll(s8, 8, 0)
    lane = lax.broadcasted_iota(jnp.int32, (1, D_POOL), 1)
    wsum = jnp.where(lane < 64, s2, jnp.where(lane < 128, s4, jnp.where(lane < 192, s8, s16)))
    wsum = wsum[HIST_ROWS:, :]
    win = jnp.where(lane < 64, 2.0, jnp.where(lane < 128, 4.0, jnp.where(lane < 192, 8.0, 16.0)))
    pos = (pos0 + lax.broadcasted_iota(jnp.int32, (n, 1), 0) + 1).astype(jnp.float32)
    cnt = jnp.minimum(win, pos)
    pooled = wsum / cnt - p_in
    if pool_lo is None:
        p_out = jnp.dot(pooled.astype(jnp.bfloat16), pool_bd, preferred_element_type=jnp.float32)
    else:
        p_out = _dot_split(pooled, pool_bd, pool_lo)
    p_out = p_out * pool_scale

    lane_s = lax.broadcasted_iota(jnp.int32, (1, D_SGU), 1)
    chunk = min(n, SGU_LEN)
    def head_rows(w):
        if chunk == SGU_LEN:
            return w
        return jnp.concatenate([w[h * SGU_LEN:h * SGU_LEN + chunk, 0:chunk] for h in range(SGU_HEADS)], axis=0)
    wm = head_rows(wm_all)
    s_rows = []
    for c in range(n // chunk):
        v_c = s_v[c * chunk:(c + 1) * chunk, :]
        if wm_lo is None:
            r = jnp.dot(wm, v_c.astype(jnp.bfloat16), preferred_element_type=jnp.float32)
        else:
            v_hi, v_lo = _split(v_c)
            r = (jnp.dot(wm, v_hi, preferred_element_type=jnp.float32)
                 + jnp.dot(head_rows(wm_lo), v_hi, preferred_element_type=jnp.float32)
                 + jnp.dot(wm, v_lo, preferred_element_type=jnp.float32))
        s_c = jnp.where(lane_s < 96, r[0:chunk],
                        jnp.where(lane_s < 192, r[chunk:2 * chunk],
                                  jnp.where(lane_s < 288, r[2 * chunk:3 * chunk], r[3 * chunk:4 * chunk])))
        s_rows.append(s_c + bias_full[0:chunk, :])
    s_gate = s_rows[0] if len(s_rows) == 1 else jnp.concatenate(s_rows, axis=0)
    s_out = s_u * s_gate

    mix = jnp.concatenate([a_out, p_out, s_out], axis=-1)
    return mix, zext[n:n + HIST_ROWS, :], pext[n:n + HIST_ROWS, :]


def _route(h2, wr_t, br_col, n):
    h_hi = h2.astype(jnp.bfloat16)
    h_lo = (h2 - h_hi.astype(jnp.float32)).astype(jnp.bfloat16)
    nt = (((1,), (1,)), ((), ()))
    by_hi = lax.dot_general(wr_t, h_hi, nt, preferred_element_type=jnp.float32)
    by_lo = lax.dot_general(wr_t[0:ROUTER_ROWS, :], h_lo, nt, preferred_element_type=jnp.float32)
    logits = by_hi[0:ROUTER_ROWS, :] + by_hi[ROUTER_ROWS:, :] + by_lo + br_col
    row8 = lax.broadcasted_iota(jnp.int32, (8, n), 0)
    lc = jnp.where(row8 < N_GROUPS, logits[0:8, :], -jnp.inf)
    mc = jnp.max(lc, axis=0, keepdims=True)
    g_sel = jnp.min(jnp.where(lc == mc, row8, 8), axis=0, keepdims=True)
    p_sel = 1.0 / jnp.sum(jnp.exp(lc - mc), axis=0, keepdims=True)
    lf = logits[8 + 3 * EXPERTS_PER_GROUP:8 + 4 * EXPERTS_PER_GROUP, :]
    for g in (2, 1, 0):
        lf = jnp.where(g_sel == g, logits[8 + g * EXPERTS_PER_GROUP:8 + (g + 1) * EXPERTS_PER_GROUP, :], lf)
    m1 = jnp.max(lf, axis=0, keepdims=True)
    i1 = jnp.min(jnp.where(lf == m1, row8, 8), axis=0, keepdims=True)
    lf2 = jnp.where(row8 == i1, -jnp.inf, lf)
    m2 = jnp.max(lf2, axis=0, keepdims=True)
    i2 = jnp.min(jnp.where(lf2 == m2, row8, 8), axis=0, keepdims=True)
    t = jnp.exp(m2 - m1)
    wa = 1.0 / (1.0 + t)
    wb = t / (1.0 + t)
    e0 = g_sel * EXPERTS_PER_GROUP + i1
    e1 = g_sel * EXPERTS_PER_GROUP + i2
    return e0, e1, p_sel * wa, p_sel * wb


def _place_rows(e0, e1, triu, tril_e, fill_ref, base_ref, np_ref, pexp_ref):
    row_e = lax.broadcasted_iota(jnp.int32, (N_EXPERTS, TL), 0)
    oh0 = row_e == e0
    oh1 = row_e == e1
    oh = jnp.where(oh0 | oh1, 1.0, 0.0)
    rank = jnp.dot(oh.astype(jnp.bfloat16), triu, preferred_element_type=jnp.float32).astype(jnp.int32)
    cnt = jnp.sum(oh, axis=1, keepdims=True).astype(jnp.int32)
    lower = jnp.where(e0 < row_e, 1.0, 0.0) + jnp.where(e1 < row_e, 1.0, 0.0)
    first = jnp.sum(lower, axis=1, keepdims=True).astype(jnp.int32)
    sorted_all = first + rank
    lpos0 = jnp.sum(jnp.where(oh0, sorted_all, 0), axis=0, keepdims=True)
    lpos1 = jnp.sum(jnp.where(oh1, sorted_all, 0), axis=0, keepdims=True)

    fill = fill_ref[:, 0:1]
    base = base_ref[:, 0:1]
    npages = np_ref[0:1, 0:1]
    total = fill + cnt
    need = (total > TM).astype(jnp.int32) + (total > 2 * TM).astype(jnp.int32)
    need_b = jnp.broadcast_to(need.astype(jnp.float32), (N_EXPERTS, 128)).astype(jnp.bfloat16)
    before = jnp.dot(tril_e, need_b, preferred_element_type=jnp.float32)[:, 0:1].astype(jnp.int32)
    new_id = npages + before
    new_base = new_id * TM
    fill_ref[...] = jnp.broadcast_to(total - need * TM, (N_EXPERTS, 128))
    base_ref[...] = jnp.broadcast_to(jnp.where(need > 0, new_base + (need - 1) * TM, base), (N_EXPERTS, 128))
    np_ref[...] = jnp.broadcast_to(npages + jnp.sum(need, axis=0, keepdims=True), (8, 128))
    page_lane = lax.broadcasted_iota(jnp.int32, (N_EXPERTS, PAGE_LANES), 1)
    expert_col = lax.broadcasted_iota(jnp.int32, (N_EXPERTS, 1), 0)
    owns = ((page_lane == new_id) & (need >= 1)) | ((page_lane == new_id + 1) & (need == 2))
    pexp_ref[...] = pexp_ref[...] + jnp.sum(jnp.where(owns, expert_col, 0), axis=0, keepdims=True)

    lane = lax.broadcasted_iota(jnp.int32, (N_EXPERTS, 128), 1)
    cols = jnp.where(lane == 0, cnt, jnp.where(lane == 1, first, jnp.where(lane == 2, fill,
                     jnp.where(lane == 3, base, jnp.where(lane == 4, new_base, 0)))))
    square = jnp.concatenate([cols, jnp.zeros((128 - N_EXPERTS, 128), jnp.int32)], axis=0)
    per_expert = square.astype(jnp.float32).T.astype(jnp.int32)[0:8, :]
    return lpos0, lpos1, per_expert


def _run_pieces(cnt, first, fill, base, new_base):
    n0 = jnp.minimum(cnt, TM - fill)
    n1 = jnp.minimum(cnt - n0, TM)
    n2 = cnt - n0 - n1
    pieces = ((first, base + fill, n0), (first + n0, new_base, n1), (first + n0 + n1, new_base + TM, n2))
    return pieces[:N_PIECES]


def _start_run_gather(runs_ref, tile, pages_ref, gbuf, gsem):
    slot = tile % 2

    tile0 = tile * RUNS_PER_TILE
    for e in range(N_EXPERTS):
        for j in range(N_PIECES):
            k = tile0 + e * RUN_FIELDS + 3 * j
            src = runs_ref[k]
            dst = runs_ref[k + 1]
            n = runs_ref[k + 2]

            @pl.when(n > 0)
            def _():
                pltpu.make_async_copy(pages_ref.at[pl.ds(dst, n)], gbuf.at[pl.ds(slot * 2 * TL + src, n)],
                                      gsem.at[slot]).start()


def _wait_run_gather(tile, gbuf, gsem):
    half = gbuf.at[pl.ds(0, 2 * TL)]
    pltpu.make_async_copy(half, half, gsem.at[tile % 2]).wait()


def _unsort_results(lpos_ref, tile, gbuf, y0buf, y1buf):
    k0 = tile * 2 * TL
    slot = tile % 2
    for r in range(TL):
        y0buf[slot, r] = gbuf[lpos_ref[k0 + r]]
        y1buf[slot, r] = gbuf[lpos_ref[k0 + TL + r]]


def _first_results(lpos_ref, runs_ref, pages_ref, gbuf, y0buf, y1buf, gsem):
    _start_run_gather(runs_ref, 0, pages_ref, gbuf, gsem)
    _wait_run_gather(0, gbuf, gsem)
    _unsort_results(lpos_ref, 0, gbuf, y0buf, y1buf)
    _start_run_gather(runs_ref, 1, pages_ref, gbuf, gsem)
    _start_run_gather(runs_ref, 2, pages_ref, gbuf, gsem)


def _mixer_kernel(first_layer, *refs):
    if first_layer:
        xp_ref, xs_ref = refs[0:2]
        rest = refs[2:]
    else:
        lposp_ref, runsp_ref, xm_ref, yp_ref, gc_ref = refs[0:5]
        rest = refs[5:]
    (sconv_ref, spool_ref, g1_ref, win_ref, convw_ref, poolbd_ref, pscale_ref, wm_ref, bias_ref,
     wout_ref, g2_ref, wr_ref, br_ref, triu_ref, trile_ref, winlo_ref, woutlo_ref, poollo_ref, wmlo_ref,
     xmid_ref, hs_ref, lpos_ref, runs_ref, meta_ref, gcol_ref, cpr_ref, ppr_ref, csm_ref, psm_ref, sv_ref,
     zh_ref, ph_ref, h2buf, sbuf, posv, pexv, poss, pexs, fill_ref, base_ref, np_ref,
     pexp_ref, fill_s, base_s, np_s, pexp_s, cnt_s, rsem, msem) = rest[:49]
    if not first_layer:
        gbuf, y0buf, y1buf, gsem = rest[49:]

    i = pl.program_id(0)

    def load_x(prompt):
        if first_layer:
            return xp_ref[...] if prompt else xs_ref[...]
        g = gc_ref[...]
        return (xm_ref[...] + g[:, 0:1] * _unpack_rows(y0buf[i % 2])
                + g[:, 1:2] * _unpack_rows(y1buf[i % 2]))

    row_m = lax.broadcasted_iota(jnp.int32, (SGU_HEADS * SGU_LEN, SGU_LEN), 0) % SGU_LEN
    col_m = lax.broadcasted_iota(jnp.int32, (SGU_HEADS * SGU_LEN, SGU_LEN), 1)

    def wait_rows(slot):
        half = sbuf.at[pl.ds(0, 2 * TL)]
        pltpu.make_async_copy(half, half, rsem.at[slot]).wait()

    def wait_positions(slot):
        pltpu.make_async_copy(posv, poss, msem).wait()
        pltpu.make_async_copy(pexv, pexs.at[slot], msem).wait()

    def sort_rows(slot):
        for r in range(TL):
            row = h2buf[r]
            sbuf[poss[0, r]] = row
            sbuf[poss[1, r]] = row

    def start_runs(tile, slot):
        tile0 = tile * RUNS_PER_TILE
        for e in range(N_EXPERTS):
            pieces = _run_pieces(pexs[slot, 0, e], pexs[slot, 1, e], pexs[slot, 2, e], pexs[slot, 3, e],
                                 pexs[slot, 4, e])
            for j, (src, dst, n) in enumerate(pieces):
                k = tile0 + e * RUN_FIELDS + 3 * j
                runs_ref[k] = src
                runs_ref[k + 1] = dst
                runs_ref[k + 2] = n

                @pl.when(n > 0)
                def _():
                    pltpu.make_async_copy(sbuf.at[pl.ds(slot * 2 * TL + src, n)], hs_ref.at[pl.ds(dst, n)],
                                          rsem.at[slot]).start()

        @pl.when(tile >= 1)
        def _():
            wait_rows(1 - slot)

    def finish(x, mix, precise=False):
        if precise:
            x_mid = x + _dot_split(mix, wout_ref[...], woutlo_ref[...])
        else:
            x_mid = x + jnp.dot(mix.astype(jnp.bfloat16), wout_ref[...], preferred_element_type=jnp.float32)
        xmid_ref[...] = x_mid
        h2 = _rms(x_mid, g2_ref[...])
        h2buf[...] = _pack_rows(h2)
        e0, e1, g0, g1 = _route(h2, wr_ref[...], br_ref[...], TL)
        row128 = lax.broadcasted_iota(jnp.int32, (128, TL), 0)
        gcol_ref[...] = jnp.where(row128 == 0, g0, jnp.where(row128 == 1, g1, 0.0)).T
        lpos0, lpos1, per_expert = _place_rows(e0, e1, triu_ref[...], trile_ref[...], fill_ref, base_ref,
                                               np_ref, pexp_ref)
        row8 = lax.broadcasted_iota(jnp.int32, (8, TL), 0)
        half = (i % 2) * (2 * TL)
        posv[...] = jnp.where(row8 == 0, lpos0, jnp.where(row8 == 1, lpos1, 0)) + half
        pexv[...] = per_expert
        row2 = lax.broadcasted_iota(jnp.int32, (2, TL), 0)
        lpos_ref[0] = jnp.where(row2 == 0, lpos0, lpos1) + half
        pltpu.make_async_copy(posv, poss, msem).start()
        pltpu.make_async_copy(pexv, pexs.at[i % 2], msem).start()

    def project(x, precise=False):
        h = _rms(x, g1_ref[...])
        if precise:
            return _dot_split(h, win_ref[...], winlo_ref[...])
        return jnp.dot(h.astype(jnp.bfloat16), win_ref[...], preferred_element_type=jnp.float32)

    def masked_wm(ref=wm_ref):
        return jnp.where(col_m <= row_m, ref[...], jnp.zeros_like(ref[...]))

    @pl.when(i == 0)
    def _init():
        zh_ref[...] = jnp.zeros_like(zh_ref)
        ph_ref[...] = jnp.zeros_like(ph_ref)
        fill_ref[...] = jnp.full_like(fill_ref, TM)
        base_ref[...] = jnp.zeros_like(base_ref)
        np_ref[...] = jnp.zeros_like(np_ref)
        pexp_ref[...] = jnp.zeros_like(pexp_ref)
        h2buf[...] = jnp.zeros((TL, 2, D_MODEL // 2), ROW_DTYPE)
        row8 = lax.broadcasted_iota(jnp.int32, (8, TL), 0)
        posv[...] = 2 * TL + jnp.minimum(row8, 1) * TL + lax.broadcasted_iota(jnp.int32, (8, TL), 1)
        pexv[...] = jnp.zeros((8, 128), jnp.int32)
        pltpu.make_async_copy(posv, poss, msem).start()
        pltpu.make_async_copy(pexv, pexs.at[1], msem).start()
        if not first_layer:
            _first_results(lposp_ref, runsp_ref, yp_ref, gbuf, y0buf, y1buf, gsem)

    def _close_pages():
        copies = [pltpu.make_async_copy(fill_ref, fill_s, msem), pltpu.make_async_copy(base_ref, base_s, msem),
                  pltpu.make_async_copy(np_ref, np_s, msem), pltpu.make_async_copy(pexp_ref, pexp_s, msem)]
        for c in copies:
            c.start()
        for c in copies:
            c.wait()
        npages = np_s[0, 0]

        def zero_cnt(e, c):
            cnt_s[e] = 0
            return c
        lax.fori_loop(0, N_EXPERTS, zero_cnt, 0)

        def count(p, c):
            e = pexp_s[0, p]
            cnt_s[e] = cnt_s[e] + 1
            return c
        lax.fori_loop(0, npages, count, 0)

        def prefix(e, run):
            n = cnt_s[e]
            cnt_s[e] = run
            return run + n
        lax.fori_loop(0, N_EXPERTS, prefix, 0)

        def emit(p, c):
            e = pexp_s[0, p]
            q = cnt_s[e]
            cnt_s[e] = q + 1
            meta_ref[0, q] = p
            meta_ref[1, q] = e
            return c
        lax.fori_loop(0, npages, emit, 0)

        def pad_meta(q, c):
            meta_ref[0, q] = jnp.minimum(q, N_PAGES - 1)
            meta_ref[1, q] = meta_ref[1, npages - 1]
            return c
        lax.fori_loop(npages, PAGE_LANES, pad_meta, 0)

        def fill_row2(q, c):
            meta_ref[2, q] = npages
            return c
        lax.fori_loop(0, PAGE_LANES, fill_row2, 0)

        def mark_first(q, seen):
            is_first = (q < npages) & ((q == 0) | (meta_ref[1, q] != meta_ref[1, jnp.maximum(q - 1, 0)]))
            seen = seen + jnp.where(is_first, 1, 0)
            meta_ref[3, q] = jnp.where(is_first, 1, 0)
            meta_ref[4, q] = jnp.maximum(seen - 1, 0)
            meta_ref[5, q] = -1
            return seen
        lax.fori_loop(0, PAGE_LANES, mark_first, 0)

        def mark_next(t, carry):
            cur, nxt = carry
            q = npages - 1 - t
            e = meta_ref[1, q]
            nxt = jnp.where(e != cur, cur, nxt)
            meta_ref[5, q] = nxt
            return e, nxt
        lax.fori_loop(0, npages, mark_next, (jnp.int32(-1), jnp.int32(-1)))

        h2buf[...] = jnp.zeros((TL, 2, D_MODEL // 2), ROW_DTYPE)

        def tails(start):
            def tail(e, c):
                f = fill_s[e, 0]
                b = base_s[e, 0]

                @pl.when(f < TM)
                def _zero_tail():
                    cp = pltpu.make_async_copy(h2buf.at[pl.ds(0, TM - f)], hs_ref.at[pl.ds(b + f, TM - f)],
                                               rsem.at[0])
                    cp.start() if start else cp.wait()
                return c
            lax.fori_loop(0, N_EXPERTS, tail, 0)

            def unused(p, c):
                cp = pltpu.make_async_copy(h2buf.at[pl.ds(0, TM)], hs_ref.at[pl.ds(p * TM, TM)], rsem.at[0])
                cp.start() if start else cp.wait()
                return c
            lax.fori_loop(npages, N_PAGES, unused, 0)

        tails(True)
        tails(False)

    prev_slot = (i + 1) % 2

    @pl.when(i < N_PROMPT_TILES)
    def _prompt():
        if not first_layer:
            _wait_run_gather(i + 1, gbuf, gsem)
        s = i % TILES_PER_SEQ
        x = load_x(True)
        proj = project(x)
        zhist = jnp.where(s == 0, 0.0, zh_ref[...])
        phist = jnp.where(s == 0, 0.0, ph_ref[...])
        mix, znew, pnew = _mix_rows(proj, zhist, phist, s * TL, TL, convw_ref[...], poolbd_ref[...],
                                    pscale_ref[...], masked_wm(), bias_ref[...])
        zh_ref[...] = znew
        ph_ref[...] = pnew
        cpr_ref[0] = znew
        ppr_ref[0] = pnew
        if not first_layer:
            _unsort_results(lposp_ref, i + 1, gbuf, y0buf, y1buf)
        wait_positions(prev_slot)
        sort_rows(prev_slot)
        finish(x, mix)

    @pl.when(i <= N_PROMPT_TILES - 1)
    def _prompt_runs():
        start_runs(jnp.maximum(i - 1, 0), prev_slot)
        if not first_layer:
            @pl.when(i + 3 < N_TILES)
            def _():
                _start_run_gather(runsp_ref, i + 3, yp_ref, gbuf, gsem)

    @pl.when(i == N_PROMPT_TILES)
    def _sample():
        x = load_x(False)
        proj = project(x, precise=True)
        wm = masked_wm()
        wm_lo = masked_wm(wmlo_ref)
        mixes = []
        for b in range(DEC_BATCH):
            rows = slice(b * DEC_SEQ, (b + 1) * DEC_SEQ)
            mix, znew, pnew = _mix_rows(proj[rows, :], sconv_ref[b], spool_ref[b], PAST_LEN, DEC_SEQ,
                                        convw_ref[...], poolbd_ref[...], pscale_ref[...], wm, bias_ref[...],
                                        pool_lo=poollo_ref[...], wm_lo=wm_lo)
            csm_ref[b] = znew
            psm_ref[b] = pnew
            mixes.append(mix)
        sv_ref[...] = proj[:, 1792:2176]
        wait_positions(prev_slot)
        sort_rows(prev_slot)
        finish(x, jnp.concatenate(mixes, axis=0), precise=True)
        start_runs(i - 1, prev_slot)
        wait_positions(i % 2)
        sort_rows(i % 2)
        start_runs(i, i % 2)
        wait_rows(i % 2)
        _close_pages()


def _mixer_call(first_layer, xs, sconv_pad, spool_pad, g1, w_in, conv_w, pool_bd, pool_scale, wm_all,
                bias_full, w_out, g2, wr_t, br_col, triu, tril_e, lows):
    tile = lambda i, *_: (i, 0)
    prompt_tile = lambda i, *_: (jnp.minimum(i, N_PROMPT_TILES - 1), 0)
    const2 = lambda i, *_: (0, 0)
    const3 = lambda i, *_: (0, 0, 0)
    if first_layer:
        prefetch = ()
        x_specs = [pl.BlockSpec((TL, D_MODEL), prompt_tile), pl.BlockSpec((TL, D_MODEL), const2)]
    else:
        prefetch = xs[0:2]
        xs = xs[2:]
        x_specs = [pl.BlockSpec((TL, D_MODEL), tile), pl.BlockSpec(memory_space=pl.ANY),
                   pl.BlockSpec((TL, 128), tile)]
    full = lambda a: pl.BlockSpec(a.shape, const2 if a.ndim == 2 else const3, pipeline_mode=pl.Buffered(1))
    weights = [sconv_pad, spool_pad, g1, w_in, conv_w, pool_bd, pool_scale, wm_all, bias_full, w_out, g2,
               wr_t, br_col, triu, tril_e, *lows]
    in_specs = x_specs + [full(a) for a in weights]
    seq_of = lambda i, *_: (jnp.minimum(i // TILES_PER_SEQ, BATCH - 1), 0, 0)
    out_shape = [
        jax.ShapeDtypeStruct((T_ALL, D_MODEL), jnp.float32),
        jax.ShapeDtypeStruct((PAGE_ROWS, 2, D_MODEL // 2), ROW_DTYPE),
        jax.ShapeDtypeStruct((N_TILES, 2, TL), jnp.int32),
        jax.ShapeDtypeStruct((N_TILES * RUNS_PER_TILE,), jnp.int32),
        jax.ShapeDtypeStruct((6, PAGE_LANES), jnp.int32),
        jax.ShapeDtypeStruct((T_ALL, 128), jnp.float32),
        jax.ShapeDtypeStruct((BATCH, HIST_ROWS, D_CONV), jnp.float32),
        jax.ShapeDtypeStruct((BATCH, HIST_ROWS, D_POOL), jnp.float32),
        jax.ShapeDtypeStruct((DEC_BATCH, HIST_ROWS, D_CONV), jnp.float32),
        jax.ShapeDtypeStruct((DEC_BATCH, HIST_ROWS, D_POOL), jnp.float32),
        jax.ShapeDtypeStruct((T_SAMPLE, D_SGU), jnp.float32),
    ]
    out_specs = [
        pl.BlockSpec((TL, D_MODEL), tile),
        pl.BlockSpec(memory_space=pl.ANY),
        pl.BlockSpec((1, 2, TL), lambda i, *_: (i, 0, 0)),
        pl.BlockSpec(memory_space=pltpu.SMEM),
        pl.BlockSpec(memory_space=pltpu.SMEM),
        pl.BlockSpec((TL, 128), tile),
        pl.BlockSpec((1, HIST_ROWS, D_CONV), seq_of),
        pl.BlockSpec((1, HIST_ROWS, D_POOL), seq_of),
        pl.BlockSpec((DEC_BATCH, HIST_ROWS, D_CONV), const3),
        pl.BlockSpec((DEC_BATCH, HIST_ROWS, D_POOL), const3),
        pl.BlockSpec((T_SAMPLE, D_SGU), const2),
    ]
    scratch = [
        pltpu.VMEM((HIST_ROWS, D_CONV), jnp.float32),
        pltpu.VMEM((HIST_ROWS, D_POOL), jnp.float32),
        pltpu.VMEM((TL, 2, D_MODEL // 2), ROW_DTYPE),
        pltpu.VMEM((4 * TL, 2, D_MODEL // 2), ROW_DTYPE),
        pltpu.VMEM((8, TL), jnp.int32),
        pltpu.VMEM((8, 128), jnp.int32),
        pltpu.SMEM((8, TL), jnp.int32),
        pltpu.SMEM((2, 8, 128), jnp.int32),
        pltpu.VMEM((N_EXPERTS, 128), jnp.int32),
        pltpu.VMEM((N_EXPERTS, 128), jnp.int32),
        pltpu.VMEM((8, 128), jnp.int32),
        pltpu.VMEM((8, PAGE_LANES), jnp.int32),
        pltpu.SMEM((N_EXPERTS, 128), jnp.int32),
        pltpu.SMEM((N_EXPERTS, 128), jnp.int32),
        pltpu.SMEM((8, 128), jnp.int32),
        pltpu.SMEM((8, PAGE_LANES), jnp.int32),
        pltpu.SMEM((N_EXPERTS,), jnp.int32),
        pltpu.SemaphoreType.DMA((2,)),
        pltpu.SemaphoreType.DMA(()),
    ]
    if not first_layer:
        scratch += [
            pltpu.VMEM((4 * TL, 2, D_MODEL // 2), ROW_DTYPE),
            pltpu.VMEM((2, TL, 2, D_MODEL // 2), ROW_DTYPE),
            pltpu.VMEM((2, TL, 2, D_MODEL // 2), ROW_DTYPE),
            pltpu.SemaphoreType.DMA((2,)),
        ]
    grid_spec = pltpu.PrefetchScalarGridSpec(num_scalar_prefetch=len(prefetch), grid=(N_TILES,),
                                             in_specs=in_specs, out_specs=out_specs, scratch_shapes=scratch)
    return pl.pallas_call(
        functools.partial(_mixer_kernel, first_layer),
        grid_spec=grid_spec,
        out_shape=out_shape,
        compiler_params=pltpu.CompilerParams(dimension_semantics=("arbitrary",),
                                             vmem_limit_bytes=VMEM_LIMIT),
        name="mixer_first" if first_layer else "mixer_next",
    )(*prefetch, *xs, *weights)


def _expert_kernel(layer, meta_ref, hs_ref, wg_hbm, wu_hbm, wd_hbm, y_ref, xbuf, wg32, wu32, wd32, wg16, wu16,
                   wd16, wsem):
    s = pl.program_id(0)
    npages = meta_ref[2, 0]

    def weight_copies(expert, slot):
        return [pltpu.make_async_copy(wg_hbm.at[layer, expert], wg32.at[slot], wsem.at[slot]),
                pltpu.make_async_copy(wu_hbm.at[layer, expert], wu32.at[slot], wsem.at[slot]),
                pltpu.make_async_copy(wd_hbm.at[layer, expert], wd32.at[slot], wsem.at[slot])]

    @pl.when(s == 0)
    def _first_weights():
        for c in weight_copies(meta_ref[1, 0], 0):
            c.start()

    @pl.when((s < npages) & (meta_ref[3, s] == 1))
    def _switch_expert():
        slot = meta_ref[4, s] % 2
        for c in weight_copies(meta_ref[1, s], slot):
            c.wait()
        nxt = meta_ref[5, s]

        @pl.when(nxt >= 0)
        def _():
            for c in weight_copies(nxt, 1 - slot):
                c.start()
        wg16[...] = wg32[slot].astype(jnp.bfloat16)
        wu16[...] = wu32[slot].astype(jnp.bfloat16)
        wd16[...] = wd32[slot].astype(jnp.bfloat16)

    @pl.when(s < npages)
    def _page():
        xbuf[...] = hs_ref[...].reshape(TM, D_MODEL)
        x = xbuf[...]
        hg = jnp.dot(x, wg16[...], preferred_element_type=jnp.float32)
        hu = jnp.dot(x, wu16[...], preferred_element_type=jnp.float32)
        h = (hg * jax.nn.sigmoid(hg) * hu).astype(jnp.bfloat16)
        out = jnp.dot(h, wd16[...], preferred_element_type=jnp.float32)
        y_ref[...] = _pack_rows(out)

    @pl.when(s >= npages)
    def _unused_page():
        y_ref[...] = jnp.zeros((TM, 2, D_MODEL // 2), ROW_DTYPE)


def _expert_call(layer, meta, hs, w_gate, w_up, w_down):
    page = lambda s, meta: (meta[0, s], 0, 0)
    grid_spec = pltpu.PrefetchScalarGridSpec(
        num_scalar_prefetch=1,
        grid=(N_PAGES,),
        in_specs=[
            pl.BlockSpec((TM, 2, D_MODEL // 2), page),
            pl.BlockSpec(memory_space=pl.ANY),
            pl.BlockSpec(memory_space=pl.ANY),
            pl.BlockSpec(memory_space=pl.ANY),
        ],
        out_specs=pl.BlockSpec((TM, 2, D_MODEL // 2), page),
        scratch_shapes=[
            pltpu.VMEM((TM, D_MODEL), ROW_DTYPE),
            pltpu.VMEM((2, D_MODEL, D_EXPERT), jnp.float32),
            pltpu.VMEM((2, D_MODEL, D_EXPERT), jnp.float32),
            pltpu.VMEM((2, D_EXPERT, D_MODEL), jnp.float32),
            pltpu.VMEM((D_MODEL, D_EXPERT), jnp.bfloat16),
            pltpu.VMEM((D_MODEL, D_EXPERT), jnp.bfloat16),
            pltpu.VMEM((D_EXPERT, D_MODEL), jnp.bfloat16),
            pltpu.SemaphoreType.DMA((2,)),
        ],
    )
    return pl.pallas_call(
        functools.partial(_expert_kernel, layer),
        grid_spec=grid_spec,
        out_shape=jax.ShapeDtypeStruct((PAGE_ROWS, 2, D_MODEL // 2), ROW_DTYPE),
        compiler_params=pltpu.CompilerParams(dimension_semantics=("arbitrary",),
                                             vmem_limit_bytes=VMEM_LIMIT),
        name="experts",
    )(meta, hs, w_gate, w_up, w_down)


def _final_kernel(lposp_ref, runsp_ref, xm_ref, yp_ref, gc_ref, g_ref, yp_out, ys_out, gbuf, y0buf, y1buf, gsem):
    i = pl.program_id(0)

    @pl.when(i == 0)
    def _():
        _first_results(lposp_ref, runsp_ref, yp_ref, gbuf, y0buf, y1buf, gsem)

    def normed():
        g = gc_ref[...]
        x = xm_ref[...] + g[:, 0:1] * _unpack_rows(y0buf[i % 2]) + g[:, 1:2] * _unpack_rows(y1buf[i % 2])
        return _rms(x, g_ref[...])

    @pl.when(i < N_PROMPT_TILES)
    def _():
        _wait_run_gather(i + 1, gbuf, gsem)
        yp_out[...] = normed()
        _unsort_results(lposp_ref, i + 1, gbuf, y0buf, y1buf)

    @pl.when(i + 3 < N_TILES)
    def _():
        _start_run_gather(runsp_ref, i + 3, yp_ref, gbuf, gsem)

    @pl.when(i == N_PROMPT_TILES)
    def _():
        ys_out[...] = normed()


def _final_call(lpos, runs, x_mid, y_pages, gcol, g):
    tile = lambda i, *_: (i, 0)
    grid_spec = pltpu.PrefetchScalarGridSpec(
        num_scalar_prefetch=2,
        grid=(N_TILES,),
        in_specs=[pl.BlockSpec((TL, D_MODEL), tile), pl.BlockSpec(memory_space=pl.ANY),
                  pl.BlockSpec((TL, 128), tile), pl.BlockSpec((1, D_MODEL), lambda i, *_: (0, 0))],
        out_specs=[pl.BlockSpec((TL, D_MODEL), lambda i, *_: (jnp.minimum(i, N_PROMPT_TILES - 1), 0)),
                   pl.BlockSpec((TL, D_MODEL), lambda i, *_: (0, 0))],
        scratch_shapes=[pltpu.VMEM((4 * TL, 2, D_MODEL // 2), ROW_DTYPE), pltpu.VMEM((2, TL, 2, D_MODEL // 2), ROW_DTYPE),
                        pltpu.VMEM((2, TL, 2, D_MODEL // 2), ROW_DTYPE), pltpu.SemaphoreType.DMA((2,))],
    )
    return pl.pallas_call(
        _final_kernel,
        grid_spec=grid_spec,
        out_shape=[jax.ShapeDtypeStruct((T_PROMPT, D_MODEL), jnp.float32),
                   jax.ShapeDtypeStruct((T_SAMPLE, D_MODEL), jnp.float32)],
        compiler_params=pltpu.CompilerParams(dimension_semantics=("arbitrary",),
                                             vmem_limit_bytes=VMEM_LIMIT),
        name="final_norm",
    )(lpos, runs, x_mid, y_pages, gcol, g)


def kernel(x_prompt, x_sample, state_conv, state_pool, norm1_g, w_in, conv_w, pool_w, pool_scale, sgu_w, sgu_b, w_out, norm2_g, router_coarse_w, router_coarse_b, router_fine_w, router_fine_b, moe_w_gate, moe_w_up, moe_w_down, final_norm_g):
    bf16 = jnp.bfloat16
    xs = (x_prompt.reshape(T_PROMPT, D_MODEL), x_sample.reshape(T_SAMPLE, D_MODEL))
    sconv_pad = jnp.pad(state_conv, ((0, 0), (0, 0), (HIST_ROWS - (CONV_WIDTH - 1), 0), (0, 0)))
    spool_pad = jnp.pad(state_pool, ((0, 0), (0, 0), (HIST_ROWS - POOL_HIST, 0), (0, 0)))
    idx = jnp.arange(TL, dtype=jnp.int32)
    triu = (idx[:, None] < idx[None, :]).astype(bf16)
    ide = jnp.arange(N_EXPERTS, dtype=jnp.int32)
    tril_e = (ide[None, :] < ide[:, None]).astype(bf16)
    conv_pr, pool_pr, conv_sm, pool_sm, sgu_v = [], [], [], [], []
    x_mid = y_pages = gcol = lpos = runs = None
    for l in range(DEPTH):
        pool_bd32 = jax.scipy.linalg.block_diag(*[pool_w[l, g] for g in range(4)])
        pool_bd = pool_bd32.astype(bf16)
        wm32 = sgu_w[l].reshape(SGU_HEADS * SGU_LEN, SGU_LEN)
        wm_all = wm32.astype(bf16)
        bias_full = jnp.repeat(sgu_b[l].T, SGU_HEAD_DIM, axis=1)
        wr = jnp.zeros((ROUTER_ROWS, D_MODEL), jnp.float32)
        wr = wr.at[0:N_GROUPS].set(router_coarse_w[l].T).at[8:].set(router_fine_w[l].T)
        wr_hi = wr.astype(bf16)
        wr_lo = (wr - wr_hi.astype(jnp.float32)).astype(bf16)
        wr_t = jnp.concatenate([wr_hi, wr_lo], axis=0)
        br_col = jnp.zeros((ROUTER_ROWS, 1), jnp.float32)
        br_col = br_col.at[0:N_GROUPS, 0].set(router_coarse_b[l]).at[8:, 0].set(router_fine_b[l])
        low = lambda w: (w - w.astype(bf16).astype(jnp.float32)).astype(bf16)
        lows = (low(w_in[l]), low(w_out[l]), low(pool_bd32), low(wm32))
        outs = _mixer_call(l == 0, xs, sconv_pad[l], spool_pad[l], norm1_g[l].reshape(1, D_MODEL),
                           w_in[l].astype(bf16), conv_w[l], pool_bd, pool_scale[l].reshape(1, D_POOL),
                           wm_all, bias_full, w_out[l].astype(bf16), norm2_g[l].reshape(1, D_MODEL),
                           wr_t, br_col, triu, tril_e, lows)
        x_mid, hs, lpos, runs, meta, gcol, cpr, ppr, csm, psm, sv = outs
        lpos = lpos.reshape(N_TILES * 2 * TL)
        conv_pr.append(cpr[:, HIST_ROWS - (CONV_WIDTH - 1):, :])
        pool_pr.append(ppr[:, HIST_ROWS - POOL_HIST:, :])
        conv_sm.append(csm[:, HIST_ROWS - (CONV_WIDTH - 1):, :])
        pool_sm.append(psm[:, HIST_ROWS - POOL_HIST:, :])
        sgu_v.append(sv.reshape(DEC_BATCH, DEC_SEQ, D_SGU))
        y_pages = _expert_call(l, meta, hs, moe_w_gate, moe_w_up, moe_w_down)
        xs = (lpos, runs, x_mid, y_pages, gcol)
    y_prompt, y_sample = _final_call(lpos, runs, x_mid, y_pages, gcol, final_norm_g.reshape(1, D_MODEL))
    return (y_prompt.reshape(BATCH, SEQ, D_MODEL), y_sample.reshape(DEC_BATCH, DEC_SEQ, D_MODEL),
            jnp.stack(conv_pr), jnp.stack(pool_pr), jnp.stack(conv_sm), jnp.stack(pool_sm),
            jnp.stack(sgu_v))
```

```python
import functools

import jax
import jax.numpy as jnp
from jax import lax
from jax.experimental import pallas as pl
from jax.experimental.pallas import tpu as pltpu

D_MODEL = 1024
BATCH = 8
SEQ = 2048
DEPTH = 2
DEC_BATCH = 8
DEC_SEQ = 64
PAST_LEN = 1024
D_CONV = 384
CONV_WIDTH = 3
D_POOL = 256
POOL_HIST = 15
D_SGU = 384
SGU_HEADS = 4
SGU_HEAD_DIM = 96
SGU_LEN = 128
D_PROJ = 2176
N_GROUPS = 4
EXPERTS_PER_GROUP = 8
N_EXPERTS = 32
D_EXPERT = 512
EPS = 1e-6

T_PROMPT = BATCH * SEQ
T_SAMPLE = DEC_BATCH * DEC_SEQ
T_ALL = T_PROMPT + T_SAMPLE
TL = 512
TILES_PER_SEQ = SEQ // TL
N_PROMPT_TILES = T_PROMPT // TL
N_TILES = N_PROMPT_TILES + T_SAMPLE // TL
HIST_ROWS = 16
ROUTER_ROWS = 8 + N_EXPERTS
TM = 512
N_ASSIGN = 2 * T_ALL
N_PAGES = N_ASSIGN // TM + N_EXPERTS
PAGE_LANES = 128
PAGE_ROWS = N_PAGES * TM
N_PIECES = 2 if TL <= TM else 3
RUN_FIELDS = 3 * N_PIECES
RUNS_PER_TILE = N_EXPERTS * RUN_FIELDS
ROW_DTYPE = jnp.bfloat16
VMEM_LIMIT = 56 * 1024 * 1024

assert TM <= TL <= 2 * TM and N_PAGES <= PAGE_LANES and T_ALL == N_TILES * TL


def _pack_rows(x):
    return x.astype(ROW_DTYPE).reshape(x.shape[0], 2, D_MODEL // 2)


def _unpack_rows(rows):
    return rows.astype(jnp.float32).reshape(rows.shape[0], D_MODEL)


def _split(a):
    hi = a.astype(jnp.bfloat16)
    return hi, (a - hi.astype(jnp.float32)).astype(jnp.bfloat16)


def _dot_split(a, b_hi, b_lo):
    a_hi, a_lo = _split(a)
    dot = functools.partial(jnp.dot, preferred_element_type=jnp.float32)
    return dot(a_hi, b_hi) + dot(a_lo, b_hi) + dot(a_hi, b_lo)


def _rms(x, g):
    return x * lax.rsqrt(jnp.mean(x * x, axis=-1, keepdims=True) + EPS) * g


def _mix_rows(proj, zhist, phist, pos0, n, conv_w, pool_bd, pool_scale, wm_all, bias_full, pool_lo=None,
              wm_lo=None):
    a_b = proj[:, 0:384]
    a_c = proj[:, 384:768]
    a_h = proj[:, 768:1152]
    p_in = proj[:, 1152:1408]
    s_u = proj[:, 1408:1792]
    s_v = proj[:, 1792:2176]

    z = a_c * a_h
    zext = jnp.concatenate([zhist, z], axis=0)
    conv_y = (conv_w[0:1, :] * pltpu.roll(zext, 2, 0)[HIST_ROWS:, :]
              + conv_w[1:2, :] * pltpu.roll(zext, 1, 0)[HIST_ROWS:, :]
              + conv_w[2:3, :] * z)
    a_out = a_b * conv_y

    pext = jnp.concatenate([phist, p_in], axis=0)
    s2 = pext + pltpu.roll(pext, 1, 0)
    s4 = s2 + pltpu.roll(s2, 2, 0)
    s8 = s4 + pltpu.roll(s4, 4, 0)
    s16 = s8 + pltpu.roll(s8, 8, 0)
    lane = lax.broadcasted_iota(jnp.int32, (1, D_POOL), 1)
    wsum = jnp.where(lane < 64, s2, jnp.where(lane < 128, s4, jnp.where(lane < 192, s8, s16)))
    wsum = wsum[HIST_ROWS:, :]
    win = jnp.where(lane < 64, 2.0, jnp.where(lane < 128, 4.0, jnp.where(lane < 192, 8.0, 16.0)))
    pos = (pos0 + lax.broadcasted_iota(jnp.int32, (n, 1), 0) + 1).astype(jnp.float32)
    cnt = jnp.minimum(win, pos)
    pooled = wsum / cnt - p_in
    if pool_lo is None:
        p_out = jnp.dot(pooled.astype(jnp.bfloat16), pool_bd, preferred_element_type=jnp.float32)
    else:
        p_out = _dot_split(pooled, pool_bd, pool_lo)
    p_out = p_out * pool_scale

    lane_s = lax.broadcasted_iota(jnp.int32, (1, D_SGU), 1)
    chunk = min(n, SGU_LEN)
    def head_rows(w):
        if chunk == SGU_LEN:
            return w
        return jnp.concatenate([w[h * SGU_LEN:h * SGU_LEN + chunk, 0:chunk] for h in range(SGU_HEADS)], axis=0)
    wm = head_rows(wm_all)
    s_rows = []
    for c in range(n // chunk):
        v_c = s_v[c * chunk:(c + 1) * chunk, :]
        if wm_lo is None:
            r = jnp.dot(wm, v_c.astype(jnp.bfloat16), preferred_element_type=jnp.float32)
        else:
            v_hi, v_lo = _split(v_c)
            r = (jnp.dot(wm, v_hi, preferred_element_type=jnp.float32)
                 + jnp.dot(head_rows(wm_lo), v_hi, preferred_element_type=jnp.float32)
                 + jnp.dot(wm, v_lo, preferred_element_type=jnp.float32))
        s_c = jnp.where(lane_s < 96, r[0:chunk],
                        jnp.where(lane_s < 192, r[chunk:2 * chunk],
                                  jnp.where(lane_s < 288, r[2 * chunk:3 * chunk], r[3 * chunk:4 * chunk])))
        s_rows.append(s_c + bias_full[0:chunk, :])
    s_gate = s_rows[0] if len(s_rows) == 1 else jnp.concatenate(s_rows, axis=0)
    s_out = s_u * s_gate

    mix = jnp.concatenate([a_out, p_out, s_out], axis=-1)
    return mix, zext[n:n + HIST_ROWS, :], pext[n:n + HIST_ROWS, :]


def _route(h2, wr_t, br_col, n):
    h_hi = h2.astype(jnp.bfloat16)
    h_lo = (h2 - h_hi.astype(jnp.float32)).astype(jnp.bfloat16)
    nt = (((1,), (1,)), ((), ()))
    by_hi = lax.dot_general(wr_t, h_hi, nt, preferred_element_type=jnp.float32)
    by_lo = lax.dot_general(wr_t[0:ROUTER_ROWS, :], h_lo, nt, preferred_element_type=jnp.float32)
    logits = by_hi[0:ROUTER_ROWS, :] + by_hi[ROUTER_ROWS:, :] + by_lo + br_col
    row8 = lax.broadcasted_iota(jnp.int32, (8, n), 0)
    lc = jnp.where(row8 < N_GROUPS, logits[0:8, :], -jnp.inf)
    mc = jnp.max(lc, axis=0, keepdims=True)
    g_sel = jnp.min(jnp.where(lc == mc, row8, 8), axis=0, keepdims=True)
    p_sel = 1.0 / jnp.sum(jnp.exp(lc - mc), axis=0, keepdims=True)
    lf = logits[8 + 3 * EXPERTS_PER_GROUP:8 + 4 * EXPERTS_PER_GROUP, :]
    for g in (2, 1, 0):
        lf = jnp.where(g_sel == g, logits[8 + g * EXPERTS_PER_GROUP:8 + (g + 1) * EXPERTS_PER_GROUP, :], lf)
    m1 = jnp.max(lf, axis=0, keepdims=True)
    i1 = jnp.min(jnp.where(lf == m1, row8, 8), axis=0, keepdims=True)
    lf2 = jnp.where(row8 == i1, -jnp.inf, lf)
    m2 = jnp.max(lf2, axis=0, keepdims=True)
    i2 = jnp.min(jnp.where(lf2 == m2, row8, 8), axis=0, keepdims=True)
    t = jnp.exp(m2 - m1)
    wa = 1.0 / (1.0 + t)
    wb = t / (1.0 + t)
    e0 = g_sel * EXPERTS_PER_GROUP + i1
    e1 = g_sel * EXPERTS_PER_GROUP + i2
    return e0, e1, p_sel * wa, p_sel * wb


def _place_rows(e0, e1, triu, tril_e, fill_ref, base_ref, np_ref, pexp_ref):
    row_e = lax.broadcasted_iota(jnp.int32, (N_EXPERTS, TL), 0)
    oh0 = row_e == e0
    oh1 = row_e == e1
    oh = jnp.where(oh0 | oh1, 1.0, 0.0)
    rank = jnp.dot(oh.astype(jnp.bfloat16), triu, preferred_element_type=jnp.float32).astype(jnp.int32)
    cnt = jnp.sum(oh, axis=1, keepdims=True).astype(jnp.int32)
    lower = jnp.where(e0 < row_e, 1.0, 0.0) + jnp.where(e1 < row_e, 1.0, 0.0)
    first = jnp.sum(lower, axis=1, keepdims=True).astype(jnp.int32)
    sorted_all = first + rank
    lpos0 = jnp.sum(jnp.where(oh0, sorted_all, 0), axis=0, keepdims=True)
    lpos1 = jnp.sum(jnp.where(oh1, sorted_all, 0), axis=0, keepdims=True)

    fill = fill_ref[:, 0:1]
    base = base_ref[:, 0:1]
    npages = np_ref[0:1, 0:1]
    total = fill + cnt
    need = (total > TM).astype(jnp.int32) + (total > 2 * TM).astype(jnp.int32)
    need_b = jnp.broadcast_to(need.astype(jnp.float32), (N_EXPERTS, 128)).astype(jnp.bfloat16)
    before = jnp.dot(tril_e, need_b, preferred_element_type=jnp.float32)[:, 0:1].astype(jnp.int32)
    new_id = npages + before
    new_base = new_id * TM
    fill_ref[...] = jnp.broadcast_to(total - need * TM, (N_EXPERTS, 128))
    base_ref[...] = jnp.broadcast_to(jnp.where(need > 0, new_base + (need - 1) * TM, base), (N_EXPERTS, 128))
    np_ref[...] = jnp.broadcast_to(npages + jnp.sum(need, axis=0, keepdims=True), (8, 128))
    page_lane = lax.broadcasted_iota(jnp.int32, (N_EXPERTS, PAGE_LANES), 1)
    expert_col = lax.broadcasted_iota(jnp.int32, (N_EXPERTS, 1), 0)
    owns = ((page_lane == new_id) & (need >= 1)) | ((page_lane == new_id + 1) & (need == 2))
    pexp_ref[...] = pexp_ref[...] + jnp.sum(jnp.where(owns, expert_col, 0), axis=0, keepdims=True)

    lane = lax.broadcasted_iota(jnp.int32, (N_EXPERTS, 128), 1)
    cols = jnp.where(lane == 0, cnt, jnp.where(lane == 1, first, jnp.where(lane == 2, fill,
                     jnp.where(lane == 3, base, jnp.where(lane == 4, new_base, 0)))))
    square = jnp.concatenate([cols, jnp.zeros((128 - N_EXPERTS, 128), jnp.int32)], axis=0)
    per_expert = square.astype(jnp.float32).T.astype(jnp.int32)[0:8, :]
    return lpos0, lpos1, per_expert


def _run_pieces(cnt, first, fill, base, new_base):
    n0 = jnp.minimum(cnt, TM - fill)
    n1 = jnp.minimum(cnt - n0, TM)
    n2 = cnt - n0 - n1
    pieces = ((first, base + fill, n0), (first + n0, new_base, n1), (first + n0 + n1, new_base + TM, n2))
    return pieces[:N_PIECES]


def _start_run_gather(runs_ref, tile, pages_ref, gbuf, gsem):
    slot = tile % 2

    tile0 = tile * RUNS_PER_TILE
    for e in range(N_EXPERTS):
        for j in range(N_PIECES):
            k = tile0 + e * RUN_FIELDS + 3 * j
            src = runs_ref[k]
            dst = runs_ref[k + 1]
            n = runs_ref[k + 2]

            @pl.when(n > 0)
            def _():
                pltpu.make_async_copy(pages_ref.at[pl.ds(dst, n)], gbuf.at[pl.ds(slot * 2 * TL + src, n)],
                                      gsem.at[slot]).start()


def _wait_run_gather(tile, gbuf, gsem):
    half = gbuf.at[pl.ds(0, 2 * TL)]
    pltpu.make_async_copy(half, half, gsem.at[tile % 2]).wait()


def _unsort_results(lpos_ref, tile, gbuf, y0buf, y1buf):
    k0 = tile * 2 * TL
    slot = tile % 2
    for r in range(TL):
        y0buf[slot, r] = gbuf[lpos_ref[k0 + r]]
        y1buf[slot, r] = gbuf[lpos_ref[k0 + TL + r]]


def _first_results(lpos_ref, runs_ref, pages_ref, gbuf, y0buf, y1buf, gsem):
    _start_run_gather(runs_ref, 0, pages_ref, gbuf, gsem)
    _wait_run_gather(0, gbuf, gsem)
    _unsort_results(lpos_ref, 0, gbuf, y0buf, y1buf)
    _start_run_gather(runs_ref, 1, pages_ref, gbuf, gsem)
    _start_run_gather(runs_ref, 2, pages_ref, gbuf, gsem)


def _mixer_kernel(first_layer, *refs):
    if first_layer:
        xp_ref, xs_ref = refs[0:2]
        rest = refs[2:]
    else:
        lposp_ref, runsp_ref, xm_ref, yp_ref, gc_ref = refs[0:5]
        rest = refs[5:]
    (sconv_ref, spool_ref, g1_ref, win_ref, convw_ref, poolbd_ref, pscale_ref, wm_ref, bias_ref,
     wout_ref, g2_ref, wr_ref, br_ref, triu_ref, trile_ref, winlo_ref, woutlo_ref, poollo_ref, wmlo_ref,
     xmid_ref, hs_ref, lpos_ref, runs_ref, meta_ref, gcol_ref, cpr_ref, ppr_ref, csm_ref, psm_ref, sv_ref,
     zh_ref, ph_ref, h2buf, sbuf, posv, pexv, poss, pexs, fill_ref, base_ref, np_ref,
     pexp_ref, fill_s, base_s, np_s, pexp_s, cnt_s, rsem, msem) = rest[:49]
    if not first_layer:
        gbuf, y0buf, y1buf, gsem = rest[49:]

    i = pl.program_id(0)

    def load_x(prompt):
        if first_layer:
            return xp_ref[...] if prompt else xs_ref[...]
        g = gc_ref[...]
        return (xm_ref[...] + g[:, 0:1] * _unpack_rows(y0buf[i % 2])
                + g[:, 1:2] * _unpack_rows(y1buf[i % 2]))

    row_m = lax.broadcasted_iota(jnp.int32, (SGU_HEADS * SGU_LEN, SGU_LEN), 0) % SGU_LEN
    col_m = lax.broadcasted_iota(jnp.int32, (SGU_HEADS * SGU_LEN, SGU_LEN), 1)

    def wait_rows(slot):
        half = sbuf.at[pl.ds(0, 2 * TL)]
        pltpu.make_async_copy(half, half, rsem.at[slot]).wait()

    def wait_positions(slot):
        pltpu.make_async_copy(posv, poss, msem).wait()
        pltpu.make_async_copy(pexv, pexs.at[slot], msem).wait()

    def sort_rows(slot):
        for r in range(TL):
            row = h2buf[r]
            sbuf[poss[0, r]] = row
            sbuf[poss[1, r]] = row

    def start_runs(tile, slot):
        tile0 = tile * RUNS_PER_TILE
        for e in range(N_EXPERTS):
            pieces = _run_pieces(pexs[slot, 0, e], pexs[slot, 1, e], pexs[slot, 2, e], pexs[slot, 3, e],
                                 pexs[slot, 4, e])
            for j, (src, dst, n) in enumerate(pieces):
                k = tile0 + e * RUN_FIELDS + 3 * j
                runs_ref[k] = src
                runs_ref[k + 1] = dst
                runs_ref[k + 2] = n

                @pl.when(n > 0)
                def _():
                    pltpu.make_async_copy(sbuf.at[pl.ds(slot * 2 * TL + src, n)], hs_ref.at[pl.ds(dst, n)],
                                          rsem.at[slot]).start()

        @pl.when(tile >= 1)
        def _():
            wait_rows(1 - slot)

    def finish(x, mix, precise=False):
        if precise:
            x_mid = x + _dot_split(mix, wout_ref[...], woutlo_ref[...])
        else:
            x_mid = x + jnp.dot(mix.astype(jnp.bfloat16), wout_ref[...], preferred_element_type=jnp.float32)
        xmid_ref[...] = x_mid
        h2 = _rms(x_mid, g2_ref[...])
        h2buf[...] = _pack_rows(h2)
        e0, e1, g0, g1 = _route(h2, wr_ref[...], br_ref[...], TL)
        row128 = lax.broadcasted_iota(jnp.int32, (128, TL), 0)
        gcol_ref[...] = jnp.where(row128 == 0, g0, jnp.where(row128 == 1, g1, 0.0)).T
        lpos0, lpos1, per_expert = _place_rows(e0, e1, triu_ref[...], trile_ref[...], fill_ref, base_ref,
                                               np_ref, pexp_ref)
        row8 = lax.broadcasted_iota(jnp.int32, (8, TL), 0)
        half = (i % 2) * (2 * TL)
        posv[...] = jnp.where(row8 == 0, lpos0, jnp.where(row8 == 1, lpos1, 0)) + half
        pexv[...] = per_expert
        row2 = lax.broadcasted_iota(jnp.int32, (2, TL), 0)
        lpos_ref[0] = jnp.where(row2 == 0, lpos0, lpos1) + half
        pltpu.make_async_copy(posv, poss, msem).start()
        pltpu.make_async_copy(pexv, pexs.at[i % 2], msem).start()

    def project(x, precise=False):
        h = _rms(x, g1_ref[...])
        if precise:
            return _dot_split(h, win_ref[...], winlo_ref[...])
        return jnp.dot(h.astype(jnp.bfloat16), win_ref[...], preferred_element_type=jnp.float32)

    def masked_wm(ref=wm_ref):
        return jnp.where(col_m <= row_m, ref[...], jnp.zeros_like(ref[...]))

    @pl.when(i == 0)
    def _init():
        zh_ref[...] = jnp.zeros_like(zh_ref)
        ph_ref[...] = jnp.zeros_like(ph_ref)
        fill_ref[...] = jnp.full_like(fill_ref, TM)
        base_ref[...] = jnp.zeros_like(base_ref)
        np_ref[...] = jnp.zeros_like(np_ref)
        pexp_ref[...] = jnp.zeros_like(pexp_ref)
        h2buf[...] = jnp.zeros((TL, 2, D_MODEL // 2), ROW_DTYPE)
        row8 = lax.broadcasted_iota(jnp.int32, (8, TL), 0)
        posv[...] = 2 * TL + jnp.minimum(row8, 1) * TL + lax.broadcasted_iota(jnp.int32, (8, TL), 1)
        pexv[...] = jnp.zeros((8, 128), jnp.int32)
        pltpu.make_async_copy(posv, poss, msem).start()
        pltpu.make_async_copy(pexv, pexs.at[1], msem).start()
        if not first_layer:
            _first_results(lposp_ref, runsp_ref, yp_ref, gbuf, y0buf, y1buf, gsem)

    def _close_pages():
        copies = [pltpu.make_async_copy(fill_ref, fill_s, msem), pltpu.make_async_copy(base_ref, base_s, msem),
                  pltpu.make_async_copy(np_ref, np_s, msem), pltpu.make_async_copy(pexp_ref, pexp_s, msem)]
        for c in copies:
            c.start()
        for c in copies:
            c.wait()
        npages = np_s[0, 0]

        def zero_cnt(e, c):
            cnt_s[e] = 0
            return c
        lax.fori_loop(0, N_EXPERTS, zero_cnt, 0)

        def count(p, c):
            e = pexp_s[0, p]
            cnt_s[e] = cnt_s[e] + 1
            return c
        lax.fori_loop(0, npages, count, 0)

        def prefix(e, run):
            n = cnt_s[e]
            cnt_s[e] = run
            return run + n
        lax.fori_loop(0, N_EXPERTS, prefix, 0)

        def emit(p, c):
            e = pexp_s[0, p]
            q = cnt_s[e]
            cnt_s[e] = q + 1
            meta_ref[0, q] = p
            meta_ref[1, q] = e
            return c
        lax.fori_loop(0, npages, emit, 0)

        def pad_meta(q, c):
            meta_ref[0, q] = jnp.minimum(q, N_PAGES - 1)
            meta_ref[1, q] = meta_ref[1, npages - 1]
            return c
        lax.fori_loop(npages, PAGE_LANES, pad_meta, 0)

        def fill_row2(q, c):
            meta_ref[2, q] = npages
            return c
        lax.fori_loop(0, PAGE_LANES, fill_row2, 0)

        def mark_first(q, seen):
            is_first = (q < npages) & ((q == 0) | (meta_ref[1, q] != meta_ref[1, jnp.maximum(q - 1, 0)]))
            seen = seen + jnp.where(is_first, 1, 0)
            meta_ref[3, q] = jnp.where(is_first, 1, 0)
            meta_ref[4, q] = jnp.maximum(seen - 1, 0)
            meta_ref[5, q] = -1
            return seen
        lax.fori_loop(0, PAGE_LANES, mark_first, 0)

        def mark_next(t, carry):
            cur, nxt = carry
            q = npages - 1 - t
            e = meta_ref[1, q]
            nxt = jnp.where(e != cur, cur, nxt)
            meta_ref[5, q] = nxt
            return e, nxt
        lax.fori_loop(0, npages, mark_next, (jnp.int32(-1), jnp.int32(-1)))

        h2buf[...] = jnp.zeros((TL, 2, D_MODEL // 2), ROW_DTYPE)

        def tails(start):
            def tail(e, c):
                f = fill_s[e, 0]
                b = base_s[e, 0]

                @pl.when(f < TM)
                def _zero_tail():
                    cp = pltpu.make_async_copy(h2buf.at[pl.ds(0, TM - f)], hs_ref.at[pl.ds(b + f, TM - f)],
                                               rsem.at[0])
                    cp.start() if start else cp.wait()
                return c
            lax.fori_loop(0, N_EXPERTS, tail, 0)

            def unused(p, c):
                cp = pltpu.make_async_copy(h2buf.at[pl.ds(0, TM)], hs_ref.at[pl.ds(p * TM, TM)], rsem.at[0])
                cp.start() if start else cp.wait()
                return c
            lax.fori_loop(npages, N_PAGES, unused, 0)

        tails(True)
        tails(False)

    prev_slot = (i + 1) % 2

    @pl.when(i < N_PROMPT_TILES)
    def _prompt():
        if not first_layer:
            _wait_run_gather(i + 1, gbuf, gsem)
        s = i % TILES_PER_SEQ
        x = load_x(True)
        proj = project(x)
        zhist = jnp.where(s == 0, 0.0, zh_ref[...])
        phist = jnp.where(s == 0, 0.0, ph_ref[...])
        mix, znew, pnew = _mix_rows(proj, zhist, phist, s * TL, TL, convw_ref[...], poolbd_ref[...],
                                    pscale_ref[...], masked_wm(), bias_ref[...])
        zh_ref[...] = znew
        ph_ref[...] = pnew
        cpr_ref[0] = znew
        ppr_ref[0] = pnew
        if not first_layer:
            _unsort_results(lposp_ref, i + 1, gbuf, y0buf, y1buf)
        wait_positions(prev_slot)
        sort_rows(prev_slot)
        finish(x, mix)

    @pl.when(i <= N_PROMPT_TILES - 1)
    def _prompt_runs():
        start_runs(jnp.maximum(i - 1, 0), prev_slot)
        if not first_layer:
            @pl.when(i + 3 < N_TILES)
            def _():
                _start_run_gather(runsp_ref, i + 3, yp_ref, gbuf, gsem)

    @pl.when(i == N_PROMPT_TILES)
    def _sample():
        x = load_x(False)
        proj = project(x, precise=True)
        wm = masked_wm()
        wm_lo = masked_wm(wmlo_ref)
        mixes = []
        for b in range(DEC_BATCH):
            rows = slice(b * DEC_SEQ, (b + 1) * DEC_SEQ)
            mix, znew, pnew = _mix_rows(proj[rows, :], sconv_ref[b], spool_ref[b], PAST_LEN, DEC_SEQ,
                                        convw_ref[...], poolbd_ref[...], pscale_ref[...], wm, bias_ref[...],
                                        pool_lo=poollo_ref[...], wm_lo=wm_lo)
            csm_ref[b] = znew
            psm_ref[b] = pnew
            mixes.append(mix)
        sv_ref[...] = proj[:, 1792:2176]
        wait_positions(prev_slot)
        sort_rows(prev_slot)
        finish(x, jnp.concatenate(mixes, axis=0), precise=True)
        start_runs(i - 1, prev_slot)
        wait_positions(i % 2)
        sort_rows(i % 2)
        start_runs(i, i % 2)
        wait_rows(i % 2)
        _close_pages()


def _mixer_call(first_layer, xs, sconv_pad, spool_pad, g1, w_in, conv_w, pool_bd, pool_scale, wm_all,
                bias_full, w_out, g2, wr_t, br_col, triu, tril_e, lows):
    tile = lambda i, *_: (i, 0)
    prompt_tile = lambda i, *_: (jnp.minimum(i, N_PROMPT_TILES - 1), 0)
    const2 = lambda i, *_: (0, 0)
    const3 = lambda i, *_: (0, 0, 0)
    if first_layer:
        prefetch = ()
        x_specs = [pl.BlockSpec((TL, D_MODEL), prompt_tile), pl.BlockSpec((TL, D_MODEL), const2)]
    else:
        prefetch = xs[0:2]
        xs = xs[2:]
        x_specs = [pl.BlockSpec((TL, D_MODEL), tile), pl.BlockSpec(memory_space=pl.ANY),
                   pl.BlockSpec((TL, 128), tile)]
    full = lambda a: pl.BlockSpec(a.shape, const2 if a.ndim == 2 else const3, pipeline_mode=pl.Buffered(1))
    weights = [sconv_pad, spool_pad, g1, w_in, conv_w, pool_bd, pool_scale, wm_all, bias_full, w_out, g2,
               wr_t, br_col, triu, tril_e, *lows]
    in_specs = x_specs + [full(a) for a in weights]
    seq_of = lambda i, *_: (jnp.minimum(i // TILES_PER_SEQ, BATCH - 1), 0, 0)
    out_shape = [
        jax.ShapeDtypeStruct((T_ALL, D_MODEL), jnp.float32),
        jax.ShapeDtypeStruct((PAGE_ROWS, 2, D_MODEL // 2), ROW_DTYPE),
        jax.ShapeDtypeStruct((N_TILES, 2, TL), jnp.int32),
        jax.ShapeDtypeStruct((N_TILES * RUNS_PER_TILE,), jnp.int32),
        jax.ShapeDtypeStruct((6, PAGE_LANES), jnp.int32),
        jax.ShapeDtypeStruct((T_ALL, 128), jnp.float32),
        jax.ShapeDtypeStruct((BATCH, HIST_ROWS, D_CONV), jnp.float32),
        jax.ShapeDtypeStruct((BATCH, HIST_ROWS, D_POOL), jnp.float32),
        jax.ShapeDtypeStruct((DEC_BATCH, HIST_ROWS, D_CONV), jnp.float32),
        jax.ShapeDtypeStruct((DEC_BATCH, HIST_ROWS, D_POOL), jnp.float32),
        jax.ShapeDtypeStruct((T_SAMPLE, D_SGU), jnp.float32),
    ]
    out_specs = [
        pl.BlockSpec((TL, D_MODEL), tile),
        pl.BlockSpec(memory_space=pl.ANY),
        pl.BlockSpec((1, 2, TL), lambda i, *_: (i, 0, 0)),
        pl.BlockSpec(memory_space=pltpu.SMEM),
        pl.BlockSpec(memory_space=pltpu.SMEM),
        pl.BlockSpec((TL, 128), tile),
        pl.BlockSpec((1, HIST_ROWS, D_CONV), seq_of),
        pl.BlockSpec((1, HIST_ROWS, D_POOL), seq_of),
        pl.BlockSpec((DEC_BATCH, HIST_ROWS, D_CONV), const3),
        pl.BlockSpec((DEC_BATCH, HIST_ROWS, D_POOL), const3),
        pl.BlockSpec((T_SAMPLE, D_SGU), const2),
    ]
    scratch = [
        pltpu.VMEM((HIST_ROWS, D_CONV), jnp.float32),
        pltpu.VMEM((HIST_ROWS, D_POOL), jnp.float32),
        pltpu.VMEM((TL, 2, D_MODEL // 2), ROW_DTYPE),
        pltpu.VMEM((4 * TL, 2, D_MODEL // 2), ROW_DTYPE),
        pltpu.VMEM((8, TL), jnp.int32),
        pltpu.VMEM((8, 128), jnp.int32),
        pltpu.SMEM((8, TL), jnp.int32),
        pltpu.SMEM((2, 8, 128), jnp.int32),
        pltpu.VMEM((N_EXPERTS, 128), jnp.int32),
        pltpu.VMEM((N_EXPERTS, 128), jnp.int32),
        pltpu.VMEM((8, 128), jnp.int32),
        pltpu.VMEM((8, PAGE_LANES), jnp.int32),
        pltpu.SMEM((N_EXPERTS, 128), jnp.int32),
        pltpu.SMEM((N_EXPERTS, 128), jnp.int32),
        pltpu.SMEM((8, 128), jnp.int32),
        pltpu.SMEM((8, PAGE_LANES), jnp.int32),
        pltpu.SMEM((N_EXPERTS,), jnp.int32),
        pltpu.SemaphoreType.DMA((2,)),
        pltpu.SemaphoreType.DMA(()),
    ]
    if not first_layer:
        scratch += [
            pltpu.VMEM((4 * TL, 2, D_MODEL // 2), ROW_DTYPE),
            pltpu.VMEM((2, TL, 2, D_MODEL // 2), ROW_DTYPE),
            pltpu.VMEM((2, TL, 2, D_MODEL // 2), ROW_DTYPE),
            pltpu.SemaphoreType.DMA((2,)),
        ]
    grid_spec = pltpu.PrefetchScalarGridSpec(num_scalar_prefetch=len(prefetch), grid=(N_TILES,),
                                             in_specs=in_specs, out_specs=out_specs, scratch_shapes=scratch)
    return pl.pallas_call(
        functools.partial(_mixer_kernel, first_layer),
        grid_spec=grid_spec,
        out_shape=out_shape,
        compiler_params=pltpu.CompilerParams(dimension_semantics=("arbitrary",),
                                             vmem_limit_bytes=VMEM_LIMIT),
        name="mixer_first" if first_layer else "mixer_next",
    )(*prefetch, *xs, *weights)


def _expert_kernel(layer, meta_ref, hs_ref, wg_hbm, wu_hbm, wd_hbm, y_ref, xbuf, wg32, wu32, wd32, wg16, wu16,
                   wd16, wsem):
    s = pl.program_id(0)
    npages = meta_ref[2, 0]

    def weight_copies(expert, slot):
        return [pltpu.make_async_copy(wg_hbm.at[layer, expert], wg32.at[slot], wsem.at[slot]),
                pltpu.make_async_copy(wu_hbm.at[layer, expert], wu32.at[slot], wsem.at[slot]),
                pltpu.make_async_copy(wd_hbm.at[layer, expert], wd32.at[slot], wsem.at[slot])]

    @pl.when(s == 0)
    def _first_weights():
        for c in weight_copies(meta_ref[1, 0], 0):
            c.start()

    @pl.when((s < npages) & (meta_ref[3, s] == 1))
    def _switch_expert():
        slot = meta_ref[4, s] % 2
        for c in weight_copies(meta_ref[1, s], slot):
            c.wait()
        nxt = meta_ref[5, s]

        @pl.when(nxt >= 0)
        def _():
            for c in weight_copies(nxt, 1 - slot):
                c.start()
        wg16[...] = wg32[slot].astype(jnp.bfloat16)
        wu16[...] = wu32[slot].astype(jnp.bfloat16)
        wd16[...] = wd32[slot].astype(jnp.bfloat16)

    @pl.when(s < npages)
    def _page():
        xbuf[...] = hs_ref[...].reshape(TM, D_MODEL)
        x = xbuf[...]
        hg = jnp.dot(x, wg16[...], preferred_element_type=jnp.float32)
        hu = jnp.dot(x, wu16[...], preferred_element_type=jnp.float32)
        h = (hg * jax.nn.sigmoid(hg) * hu).astype(jnp.bfloat16)
        out = jnp.dot(h, wd16[...], preferred_element_type=jnp.float32)
        y_ref[...] = _pack_rows(out)

    @pl.when(s >= npages)
    def _unused_page():
        y_ref[...] = jnp.zeros((TM, 2, D_MODEL // 2), ROW_DTYPE)


def _expert_call(layer, meta, hs, w_gate, w_up, w_down):
    page = lambda s, meta: (meta[0, s], 0, 0)
    grid_spec = pltpu.PrefetchScalarGridSpec(
        num_scalar_prefetch=1,
        grid=(N_PAGES,),
        in_specs=[
            pl.BlockSpec((TM, 2, D_MODEL // 2), page),
            pl.BlockSpec(memory_space=pl.ANY),
            pl.BlockSpec(memory_space=pl.ANY),
            pl.BlockSpec(memory_space=pl.ANY),
        ],
        out_specs=pl.BlockSpec((TM, 2, D_MODEL // 2), page),
        scratch_shapes=[
            pltpu.VMEM((TM, D_MODEL), ROW_DTYPE),
            pltpu.VMEM((2, D_MODEL, D_EXPERT), jnp.float32),
            pltpu.VMEM((2, D_MODEL, D_EXPERT), jnp.float32),
            pltpu.VMEM((2, D_EXPERT, D_MODEL), jnp.float32),
            pltpu.VMEM((D_MODEL, D_EXPERT), jnp.bfloat16),
            pltpu.VMEM((D_MODEL, D_EXPERT), jnp.bfloat16),
            pltpu.VMEM((D_EXPERT, D_MODEL), jnp.bfloat16),
            pltpu.SemaphoreType.DMA((2,)),
        ],
    )
    return pl.pallas_call(
        functools.partial(_expert_kernel, layer),
        grid_spec=grid_spec,
        out_shape=jax.ShapeDtypeStruct((PAGE_ROWS, 2, D_MODEL // 2), ROW_DTYPE),
        compiler_params=pltpu.CompilerParams(dimension_semantics=("arbitrary",),
                                             vmem_limit_bytes=VMEM_LIMIT),
        name="experts",
    )(meta, hs, w_gate, w_up, w_down)


def _final_kernel(lposp_ref, runsp_ref, xm_ref, yp_ref, gc_ref, g_ref, yp_out, ys_out, gbuf, y0buf, y1buf, gsem):
    i = pl.program_id(0)

    @pl.when(i == 0)
    def _():
        _first_results(lposp_ref, runsp_ref, yp_ref, gbuf, y0buf, y1buf, gsem)

    def normed():
        g = gc_ref[...]
        x = xm_ref[...] + g[:, 0:1] * _unpack_rows(y0buf[i % 2]) + g[:, 1:2] * _unpack_rows(y1buf[i % 2])
        return _rms(x, g_ref[...])

    @pl.when(i < N_PROMPT_TILES)
    def _():
        _wait_run_gather(i + 1, gbuf, gsem)
        yp_out[...] = normed()
        _unsort_results(lposp_ref, i + 1, gbuf, y0buf, y1buf)

    @pl.when(i + 3 < N_TILES)
    def _():
        _start_run_gather(runsp_ref, i + 3, yp_ref, gbuf, gsem)

    @pl.when(i == N_PROMPT_TILES)
    def _():
        ys_out[...] = normed()


def _final_call(lpos, runs, x_mid, y_pages, gcol, g):
    tile = lambda i, *_: (i, 0)
    grid_spec = pltpu.PrefetchScalarGridSpec(
        num_scalar_prefetch=2,
        grid=(N_TILES,),
        in_specs=[pl.BlockSpec((TL, D_MODEL), tile), pl.BlockSpec(memory_space=pl.ANY),
                  pl.BlockSpec((TL, 128), tile), pl.BlockSpec((1, D_MODEL), lambda i, *_: (0, 0))],
        out_specs=[pl.BlockSpec((TL, D_MODEL), lambda i, *_: (jnp.minimum(i, N_PROMPT_TILES - 1), 0)),
                   pl.BlockSpec((TL, D_MODEL), lambda i, *_: (0, 0))],
        scratch_shapes=[pltpu.VMEM((4 * TL, 2, D_MODEL // 2), ROW_DTYPE), pltpu.VMEM((2, TL, 2, D_MODEL // 2), ROW_DTYPE),
                        pltpu.VMEM((2, TL, 2, D_MODEL // 2), ROW_DTYPE), pltpu.SemaphoreType.DMA((2,))],
    )
    return pl.pallas_call(
        _final_kernel,
        grid_spec=grid_spec,
        out_shape=[jax.ShapeDtypeStruct((T_PROMPT, D_MODEL), jnp.float32),
                   jax.ShapeDtypeStruct((T_SAMPLE, D_MODEL), jnp.float32)],
        compiler_params=pltpu.CompilerParams(dimension_semantics=("arbitrary",),
                                             vmem_limit_bytes=VMEM_LIMIT),
        name="final_norm",
    )(lpos, runs, x_mid, y_pages, gcol, g)


def kernel(x_prompt, x_sample, state_conv, state_pool, norm1_g, w_in, conv_w, pool_w, pool_scale, sgu_w, sgu_b, w_out, norm2_g, router_coarse_w, router_coarse_b, router_fine_w, router_fine_b, moe_w_gate, moe_w_up, moe_w_down, final_norm_g):
    bf16 = jnp.bfloat16
    xs = (x_prompt.reshape(T_PROMPT, D_MODEL), x_sample.reshape(T_SAMPLE, D_MODEL))
    sconv_pad = jnp.pad(state_conv, ((0, 0), (0, 0), (HIST_ROWS - (CONV_WIDTH - 1), 0), (0, 0)))
    spool_pad = jnp.pad(state_pool, ((0, 0), (0, 0), (HIST_ROWS - POOL_HIST, 0), (0, 0)))
    idx = jnp.arange(TL, dtype=jnp.int32)
    triu = (idx[:, None] < idx[None, :]).astype(bf16)
    ide = jnp.arange(N_EXPERTS, dtype=jnp.int32)
    tril_e = (ide[None, :] < ide[:, None]).astype(bf16)
    conv_pr, pool_pr, conv_sm, pool_sm, sgu_v = [], [], [], [], []
    x_mid = y_pages = gcol = lpos = runs = None
    for l in range(DEPTH):
        pool_bd32 = jax.scipy.linalg.block_diag(*[pool_w[l, g] for g in range(4)])
        pool_bd = pool_bd32.astype(bf16)
        wm32 = sgu_w[l].reshape(SGU_HEADS * SGU_LEN, SGU_LEN)
        wm_all = wm32.astype(bf16)
        bias_full = jnp.repeat(sgu_b[l].T, SGU_HEAD_DIM, axis=1)
        wr = jnp.zeros((ROUTER_ROWS, D_MODEL), jnp.float32)
        wr = wr.at[0:N_GROUPS].set(router_coarse_w[l].T).at[8:].set(router_fine_w[l].T)
        wr_hi = wr.astype(bf16)
        wr_lo = (wr - wr_hi.astype(jnp.float32)).astype(bf16)
        wr_t = jnp.concatenate([wr_hi, wr_lo], axis=0)
        br_col = jnp.zeros((ROUTER_ROWS, 1), jnp.float32)
        br_col = br_col.at[0:N_GROUPS, 0].set(router_coarse_b[l]).at[8:, 0].set(router_fine_b[l])
        low = lambda w: (w - w.astype(bf16).astype(jnp.float32)).astype(bf16)
        lows = (low(w_in[l]), low(w_out[l]), low(pool_bd32), low(wm32))
        outs = _mixer_call(l == 0, xs, sconv_pad[l], spool_pad[l], norm1_g[l].reshape(1, D_MODEL),
                           w_in[l].astype(bf16), conv_w[l], pool_bd, pool_scale[l].reshape(1, D_POOL),
                           wm_all, bias_full, w_out[l].astype(bf16), norm2_g[l].reshape(1, D_MODEL),
                           wr_t, br_col, triu, tril_e, lows)
        x_mid, hs, lpos, runs, meta, gcol, cpr, ppr, csm, psm, sv = outs
        lpos = lpos.reshape(N_TILES * 2 * TL)
        conv_pr.append(cpr[:, HIST_ROWS - (CONV_WIDTH - 1):, :])
        pool_pr.append(ppr[:, HIST_ROWS - POOL_HIST:, :])
        conv_sm.append(csm[:, HIST_ROWS - (CONV_WIDTH - 1):, :])
        pool_sm.append(psm[:, HIST_ROWS - POOL_HIST:, :])
        sgu_v.append(sv.reshape(DEC_BATCH, DEC_SEQ, D_SGU))
        y_pages = _expert_call(l, meta, hs, moe_w_gate, moe_w_up, moe_w_down)
        xs = (lpos, runs, x_mid, y_pages, gcol)
    y_prompt, y_sample = _final_call(lpos, runs, x_mid, y_pages, gcol, final_norm_g.reshape(1, D_MODEL))
    return (y_prompt.reshape(BATCH, SEQ, D_MODEL), y_sample.reshape(DEC_BATCH, DEC_SEQ, D_MODEL),
            jnp.stack(conv_pr), jnp.stack(pool_pr), jnp.stack(conv_sm), jnp.stack(pool_sm),
            jnp.stack(sgu_v))
```
